```python
import functools
import jax, jax.numpy as jnp
from jax import lax
import numpy as np

D_MODEL = 1024
BATCH = 8
SEQ = 2048
DEPTH = 1
DEC_BATCH = 128
DEC_SEQ = 1
PAST_LEN = 16384
PAGE_SIZE = 128

ATTN_HEADS = 8
ATTN_KV_HEADS = 2
HEAD_DIM = 64
ATTN_GROUP = ATTN_HEADS // ATTN_KV_HEADS
ATTN_WIDTH = ATTN_HEADS * HEAD_DIM
WINDOW = 128
ATTN_BLOCK = WINDOW
ROT_DIM = HEAD_DIM // 4
ROPE_THETA = 500000.0
HG_HEADS = 4
HG_DK = 128
HG_DV = 128
HG_WIDTH = HG_HEADS * HG_DV
HG_CHUNK = 64
MIX_WIDTH = ATTN_WIDTH + HG_WIDTH
IN_SIZES = (ATTN_WIDTH, ATTN_KV_HEADS * HEAD_DIM, ATTN_KV_HEADS * HEAD_DIM,
            HG_HEADS * HG_DK, HG_HEADS * HG_DK, HG_HEADS * HG_DV, HG_HEADS * HG_DV)
IN_COLS = sum(IN_SIZES)
IN_SPLITS = tuple(int(s) for s in np.cumsum(IN_SIZES)[:-1])
N_EXPERTS = 32
TOP_K = 4
D_FF = D_MODEL
SWIGLU_ALPHA = 1.702
SWIGLU_LIMIT = 7.0
EPS = 1e-5

kernel_name = 'hymba_swa_sink_hgrn2_moe_adaln_step'

F32 = jnp.float32


def rmsnorm(x, g):
    xf = x.astype(F32)
    y = xf * lax.rsqrt(jnp.mean(xf * xf, axis=-1, keepdims=True) + EPS)
    return (y * g.astype(F32)).astype(x.dtype)


def rotary(x, pos):
    half = ROT_DIM // 2
    inv = ROPE_THETA ** (-(jnp.arange(half, dtype=F32) * 2.0 / ROT_DIM))
    ang = pos.astype(F32)[:, None] * inv[None, :]
    cos = jnp.cos(ang)[:, None, :]
    sin = jnp.sin(ang)[:, None, :]
    xr = x[..., :ROT_DIM].astype(F32)
    x1, x2 = xr[..., :half], xr[..., half:]
    rot = jnp.concatenate([x1 * cos - x2 * sin, x2 * cos + x1 * sin], axis=-1)
    return jnp.concatenate([rot.astype(x.dtype), x[..., ROT_DIM:]], axis=-1)


def sink_attention(q, k, v, q_pos, k_pos, sinks):
    s = jnp.einsum('...qhgd,...shd->...hgqs', q, k).astype(F32) * (HEAD_DIM ** -0.5)
    diff = q_pos[..., :, None] - k_pos[..., None, :]
    ok = (diff >= 0) & (diff <= WINDOW) & (k_pos[..., None, :] >= 0)
    s = jnp.where(ok[..., None, None, :, :], s, -jnp.inf)
    sink = sinks.astype(F32).reshape(ATTN_KV_HEADS, ATTN_GROUP)[:, :, None, None]
    m = jnp.maximum(jnp.max(s, axis=-1, keepdims=True), sink)
    p = jnp.exp(s - m)
    p = p / (jnp.sum(p, axis=-1, keepdims=True) + jnp.exp(sink - m))
    return jnp.einsum('...hgqs,...shd->...qhgd', p.astype(v.dtype), v)


def attn_prompt(q, k, v, sinks, *, win):
    B, S = q.shape[:2]
    nb = S // ATTN_BLOCK
    qb = q.reshape(B, nb, ATTN_BLOCK, ATTN_KV_HEADS, ATTN_GROUP, HEAD_DIM)
    kb = k.reshape(B, nb, ATTN_BLOCK, ATTN_KV_HEADS, HEAD_DIM)
    vb = v.reshape(B, nb, ATTN_BLOCK, ATTN_KV_HEADS, HEAD_DIM)
    shift = ((0, 0), (1, 0), (0, 0), (0, 0), (0, 0))
    k2 = jnp.concatenate([jnp.pad(kb[:, :-1], shift), kb], axis=2)
    v2 = jnp.concatenate([jnp.pad(vb[:, :-1], shift), vb], axis=2)
    pos = jnp.arange(S, dtype=jnp.int32).reshape(nb, ATTN_BLOCK)
    kpos = jnp.concatenate([pos - ATTN_BLOCK, pos], axis=1)
    o = sink_attention(qb, k2, v2, pos, kpos, sinks).reshape(B, S, ATTN_WIDTH)
    return o, k[:, S - win:], v[:, S - win:]


def attn_sample(q, k, v, sinks, *, k_buf, v_buf, q_pos):
    B, S = q.shape[:2]
    win = k_buf.shape[1]
    k_all = jnp.concatenate([k_buf.astype(k.dtype), k], axis=1)
    v_all = jnp.concatenate([v_buf.astype(v.dtype), v], axis=1)
    kpos = PAST_LEN - win + jnp.arange(win + S, dtype=jnp.int32)
    o = sink_attention(q, k_all, v_all, q_pos, kpos, sinks).reshape(B, S, ATTN_WIDTH)
    return o, k_all[:, S:], v_all[:, S:]


def hgrn_chunk(S0, q, k, v, logf):
    C = q.shape[1]
    b = jnp.cumsum(logf, axis=1)
    causal = jnp.tril(jnp.ones((C, C), dtype=bool))
    rel = b[:, :, None] - b[:, None, :]
    decay = jnp.exp(jnp.where(causal[None, :, :, None, None], rel, -jnp.inf))
    a = jnp.einsum('bthk,bshk,btshk->bhts', q, k, decay)
    o = (jnp.einsum('bhts,bshv->bthv', a, v)
         + jnp.einsum('bthk,bhkv->bthv', q * jnp.exp(b), S0))
    b_last = b[:, -1]
    S = (jnp.exp(b_last)[..., None] * S0
         + jnp.einsum('bshk,bshv->bhkv', k * jnp.exp(b_last[:, None] - b), v))
    return S, o


def hgrn_prompt(q, k, v, logf):
    B, S = q.shape[:2]
    nc = S // HG_CHUNK

    def chunks(t):
        return t.reshape(B, nc, HG_CHUNK, *t.shape[2:]).swapaxes(0, 1)

    S0 = jnp.zeros((B, HG_HEADS, HG_DK, HG_DV), F32)
    S_fin, o = lax.scan(lambda s, xs: hgrn_chunk(s, *xs), S0,
                        (chunks(q), chunks(k), chunks(v), chunks(logf)))
    return o.swapaxes(0, 1).reshape(B, S, HG_HEADS, HG_DV), S_fin


def hgrn_sample(q, k, v, logf, *, S0):
    S, o = hgrn_chunk(S0.astype(F32), q, k, v, logf)
    return o, S


def clamped_swiglu(u):
    glu, lin = u[..., ::2], u[..., 1::2]
    glu = jnp.minimum(glu, SWIGLU_LIMIT)
    lin = jnp.clip(lin, -SWIGLU_LIMIT, SWIGLU_LIMIT)
    return glu * jax.nn.sigmoid(SWIGLU_ALPHA * glu) * (lin + 1.0)


def moe_ffn(h, w_router, b_router, w_up, b_up, w_down, b_down):
    shp = h.shape
    t = h.reshape(-1, shp[-1])
    logits = (t @ w_router + b_router).astype(F32)
    top_v, top_i = lax.top_k(logits, TOP_K)
    wts = jax.nn.softmax(top_v, axis=-1)
    combine = jnp.sum(jax.nn.one_hot(top_i, N_EXPERTS, dtype=F32) * wts[..., None], axis=1)

    def expert(acc, e):
        wu, bu, wd, bd, cw = e
        y = clamped_swiglu(t @ wu + bu) @ wd + bd
        return acc + cw[:, None].astype(t.dtype) * y, None

    out, _ = lax.scan(expert, jnp.zeros_like(t), (w_up, b_up, w_down, b_down, combine.T))
    return out.reshape(shp)


def decoder_layer(x, c, pos, attn_core, hgrn_core, lb, w_ada, b_ada, g_mix, g_ffn, w_in,
                  attn_sinks, g_attn_out, g_hg_out, w_out, w_router, b_router,
                  w_up, b_up, w_down, b_down):
    B, S, _ = x.shape
    mod = jax.nn.silu(c) @ w_ada + b_ada
    sh1, sc1, gt1, sh2, sc2, gt2 = [m[:, None, :] for m in jnp.split(mod, 6, axis=-1)]
    h = rmsnorm(x, g_mix) * (1.0 + sc1) + sh1
    z = h @ w_in
    q_a, k_a, v_a, q_h, f_h, i_h, g_h = jnp.split(z, IN_SPLITS, axis=-1)
    q_a = rotary(q_a.reshape(B, S, ATTN_HEADS, HEAD_DIM), pos).reshape(
        B, S, ATTN_KV_HEADS, ATTN_GROUP, HEAD_DIM)
    k_a = rotary(k_a.reshape(B, S, ATTN_KV_HEADS, HEAD_DIM), pos)
    v_a = v_a.reshape(B, S, ATTN_KV_HEADS, HEAD_DIM)
    o_a, k_new, v_new = attn_core(q_a, k_a, v_a, attn_sinks)
    o_a = rmsnorm(o_a, g_attn_out)
    f = lb + (1.0 - lb) * jax.nn.sigmoid(f_h.reshape(B, S, HG_HEADS, HG_DK).astype(F32))
    o_h, S_new = hgrn_core(q_h.reshape(B, S, HG_HEADS, HG_DK).astype(F32), 1.0 - f,
                           i_h.reshape(B, S, HG_HEADS, HG_DV).astype(F32), jnp.log(f))
    o_h = rmsnorm(o_h, g_hg_out) * jax.nn.silu(g_h.reshape(B, S, HG_HEADS, HG_DV).astype(F32))
    o_h = o_h.reshape(B, S, HG_WIDTH).astype(x.dtype)
    x = x + gt1 * (jnp.concatenate([o_a, o_h], axis=-1) @ w_out)
    h = rmsnorm(x, g_ffn) * (1.0 + sc2) + sh2
    x = x + gt2 * moe_ffn(h, w_router, b_router, w_up, b_up, w_down, b_down)
    return x, k_new, v_new, S_new


def setup_inputs(seed: int = 0) -> dict:
    key = jax.random.key(seed)
    ks = jax.random.split(key, 32)
    D = D_MODEL
    win = min(WINDOW, PAST_LEN)

    def nrm(k, shape, scale):
        return jax.random.normal(k, shape, F32) * scale

    return {
        'x_prompt': nrm(ks[0], (BATCH, SEQ, D), 1.0),
        'x_sample': nrm(ks[1], (DEC_BATCH, DEC_SEQ, D), 1.0),
        'c_prompt': nrm(ks[2], (BATCH, D), 1.0),
        'c_sample': nrm(ks[3], (DEC_BATCH, D), 1.0),
        'cache_k_win': nrm(ks[4], (DEPTH, DEC_BATCH, win, ATTN_KV_HEADS, HEAD_DIM), 1.0),
        'cache_v_win': nrm(ks[5], (DEPTH, DEC_BATCH, win, ATTN_KV_HEADS, HEAD_DIM), 1.0),
        'state_hgrn': nrm(ks[6], (DEPTH, DEC_BATCH, HG_HEADS, HG_DK, HG_DV), 0.5),
        'w_ada': nrm(ks[7], (DEPTH, D, 6 * D), 0.5 * D ** -0.5),
        'b_ada': nrm(ks[8], (DEPTH, 6 * D), 0.02),
        'g_mix': 1.0 + nrm(ks[9], (DEPTH, D), 0.02),
        'g_ffn': 1.0 + nrm(ks[10], (DEPTH, D), 0.02),
        'w_in': nrm(ks[11], (DEPTH, D, IN_COLS), D ** -0.5),
        'attn_sinks': nrm(ks[12], (DEPTH, ATTN_HEADS), 0.5),
        'g_attn_out': 1.0 + nrm(ks[13], (DEPTH, ATTN_WIDTH), 0.02),
        'hg_lb_logits': nrm(ks[14], (DEPTH + 1, HG_HEADS * HG_DK), 0.5),
        'g_hg_out': 1.0 + nrm(ks[15], (DEPTH, HG_HEADS, HG_DV), 0.02),
        'w_out': nrm(ks[16], (DEPTH, MIX_WIDTH, D), MIX_WIDTH ** -0.5),
        'w_router': nrm(ks[17], (DEPTH, D, N_EXPERTS), D ** -0.5),
        'b_router': nrm(ks[18], (DEPTH, N_EXPERTS), 0.01),
        'w_up': nrm(ks[19], (DEPTH, N_EXPERTS, D, 2 * D_FF), D ** -0.5),
        'b_up': nrm(ks[20], (DEPTH, N_EXPERTS, 2 * D_FF), 0.02),
        'w_down': nrm(ks[21], (DEPTH, N_EXPERTS, D_FF, D), D_FF ** -0.5),
        'b_down': nrm(ks[22], (DEPTH, N_EXPERTS, D), 0.02),
        'g_final': 1.0 + nrm(ks[23], (D,), 0.02),
    }


def reference(x_prompt, x_sample, c_prompt, c_sample, cache_k_win, cache_v_win, state_hgrn,
              w_ada, b_ada, g_mix, g_ffn, w_in, attn_sinks, g_attn_out, hg_lb_logits, g_hg_out,
              w_out, w_router, b_router, w_up, b_up, w_down, b_down, g_final):
    win = cache_k_win.shape[2]
    pos_p = jnp.arange(x_prompt.shape[1], dtype=jnp.int32)
    pos_s = PAST_LEN + jnp.arange(x_sample.shape[1], dtype=jnp.int32)
    lb_all = jnp.cumsum(jax.nn.softmax(hg_lb_logits.astype(F32), axis=0), axis=0)
    xp, xs = x_prompt, x_sample
    kp_l, vp_l, sp_l, ks_l, vs_l, ss_l = [], [], [], [], [], []
    for l in range(DEPTH):
        lb = lb_all[l].reshape(HG_HEADS, HG_DK)
        w = (w_ada[l], b_ada[l], g_mix[l], g_ffn[l], w_in[l], attn_sinks[l], g_attn_out[l],
             g_hg_out[l], w_out[l], w_router[l], b_router[l], w_up[l], b_up[l], w_down[l],
             b_down[l])
        xp, kp, vp, sp = decoder_layer(
            xp, c_prompt, pos_p, functools.partial(attn_prompt, win=win), hgrn_prompt, lb, *w)
        xs, kn, vn, sn = decoder_layer(
            xs, c_sample, pos_s,
            functools.partial(attn_sample, k_buf=cache_k_win[l], v_buf=cache_v_win[l], q_pos=pos_s),
            functools.partial(hgrn_sample, S0=state_hgrn[l]), lb, *w)
        kp_l.append(kp); vp_l.append(vp); sp_l.append(sp)
        ks_l.append(kn); vs_l.append(vn); ss_l.append(sn)
    y_prompt = rmsnorm(xp, g_final)
    y_sample = rmsnorm(xs, g_final)
    k_win_prompt = jnp.stack(kp_l)
    v_win_prompt = jnp.stack(vp_l)
    hgrn_prompt_state = jnp.stack(sp_l)
    k_win_sample = jnp.stack(ks_l)
    v_win_sample = jnp.stack(vs_l)
    hgrn_sample_state = jnp.stack(ss_l)
    return (y_prompt, y_sample, k_win_prompt, v_win_prompt, hgrn_prompt_state,
            k_win_sample, v_win_sample, hgrn_sample_state)
```

```python
import functools

import numpy as np
import jax
import jax.numpy as jnp
from jax import lax
from jax.experimental import pallas as pl
from jax.experimental.pallas import tpu as pltpu

F32 = jnp.float32
BF16 = jnp.bfloat16

D_MODEL = 1024
SEQ = 2048
PAST_LEN = 16384
ATTN_HEADS = 8
ATTN_KV_HEADS = 2
HEAD_DIM = 64
ATTN_GROUP = ATTN_HEADS // ATTN_KV_HEADS
ATTN_WIDTH = ATTN_HEADS * HEAD_DIM
KV_WIDTH = ATTN_KV_HEADS * HEAD_DIM
WINDOW = 128
ROT_DIM = HEAD_DIM // 4
ROPE_THETA = 500000.0
HG_HEADS = 4
HG_DK = 128
HG_DV = 128
HG_WIDTH = HG_HEADS * HG_DV
HG_CHUNK = 64
HG_SUB = 8
IN_COLS = ATTN_WIDTH + 2 * KV_WIDTH + 4 * HG_WIDTH
N_EXPERTS = 32
TOP_K = 4
D_FF = D_MODEL
SWIGLU_ALPHA = 1.702
SWIGLU_LIMIT = 7.0
EPS = 1e-5

V7X_VMEM_LIMIT = 56 * 1024 * 1024

NN = (((1,), (0,)), ((), ()))
NT = (((1,), (1,)), ((), ()))
TN = (((0,), (0,)), ((), ()))


def _dg(a, b, dims):
    return lax.dot_general(a, b, dims, preferred_element_type=F32)


def _split3(x):
    h = x.astype(BF16)
    r = x - h.astype(F32)
    m = r.astype(BF16)
    l = (r - m.astype(F32)).astype(BF16)
    return h, m, l


def _dot_f32(a, b, dims):
    ah, am, al = _split3(a)
    bh, bm, bl = _split3(b)
    return (_dg(ah, bh, dims) + (_dg(ah, bm, dims) + _dg(am, bh, dims))
            + (_dg(am, bm, dims) + _dg(ah, bl, dims) + _dg(al, bh, dims)))


def _dot_exact_lhs(a_bf16, b, dims):
    bh, bm, bl = _split3(b)
    return _dg(a_bf16, bh, dims) + _dg(a_bf16, bm, dims) + _dg(a_bf16, bl, dims)


def _sigmoid(x):
    return 1.0 / (1.0 + jnp.exp(-x))


def _rms(x, g):
    return x * lax.rsqrt(jnp.mean(x * x, axis=-1, keepdims=True) + EPS) * g


def _mod_kernel(c_ref, w_ref, b_ref, o_ref):
    c = c_ref[...]
    o_ref[...] = _dot_f32(c * _sigmoid(c), w_ref[...], NN) + b_ref[...]


def _modulation(c_all, w_ada, b_ada):
    n = c_all.shape[0]
    return pl.pallas_call(
        _mod_kernel,
        grid=(6,),
        in_specs=[pl.BlockSpec((n, D_MODEL), lambda j: (0, 0)),
                  pl.BlockSpec((D_MODEL, D_MODEL), lambda j: (0, j)),
                  pl.BlockSpec((1, D_MODEL), lambda j: (0, j))],
        out_specs=pl.BlockSpec((n, D_MODEL), lambda j: (0, j)),
        out_shape=jax.ShapeDtypeStruct((n, 6 * D_MODEL), F32),
        compiler_params=pltpu.CompilerParams(vmem_limit_bytes=V7X_VMEM_LIMIT),
        name="adaln_mod",
    )(c_all, w_ada, b_ada.reshape(1, -1))


def _rotate(x, cos_t, sin_t):
    width = x.shape[-1]
    d = lax.broadcasted_iota(jnp.int32, x.shape, 1) % HEAD_DIM
    half = ROT_DIM // 2
    partner = jnp.where(d < half, pltpu.roll(x, width - half, axis=1), pltpu.roll(x, half, axis=1))
    return x * cos_t + partner * sin_t


def _inproj_kernel(x_ref, sh_ref, sc_ref, g_ref, w_ref, cos_ref, sin_ref,
                   qa_ref, ka_ref, va_ref, qh_ref, fh_ref, ih_ref, gh_ref):
    x = x_ref[...]
    sh = sh_ref[...].reshape(-1, D_MODEL)
    sc = sc_ref[...].reshape(-1, D_MODEL)
    h = _rms(x, g_ref[...]) * (1.0 + sc) + sh
    z = _dg(h.astype(BF16), w_ref[...], NN)
    cos_k = cos_ref[...]
    sin_k = sin_ref[...]
    cos_q = jnp.concatenate([cos_k] * ATTN_GROUP, axis=1)
    sin_q = jnp.concatenate([sin_k] * ATTN_GROUP, axis=1)
    o = 0
    qa = _rotate(z[:, o:o + ATTN_WIDTH], cos_q, sin_q)
    qa_ref[...] = (qa * (HEAD_DIM ** -0.5)).astype(BF16)
    o += ATTN_WIDTH
    ka_ref[...] = _rotate(z[:, o:o + KV_WIDTH], cos_k, sin_k)
    o += KV_WIDTH
    va_ref[...] = z[:, o:o + KV_WIDTH]
    o += KV_WIDTH
    qh_ref[...] = z[:, o:o + HG_WIDTH]
    o += HG_WIDTH
    fh_ref[...] = z[:, o:o + HG_WIDTH]
    o += HG_WIDTH
    ih_ref[...] = z[:, o:o + HG_WIDTH].astype(BF16)
    o += HG_WIDTH
    gh_ref[...] = z[:, o:o + HG_WIDTH]


def _rope_tables(positions):
    half = ROT_DIM // 2
    inv = ROPE_THETA ** (-(np.arange(half, dtype=np.float64) * 2.0 / ROT_DIM))
    ang = np.asarray(positions, np.float64)[:, None] * inv[None, :]
    cos_h = np.ones((len(positions), HEAD_DIM))
    sin_h = np.zeros((len(positions), HEAD_DIM))
    cos_h[:, :half] = np.cos(ang)
    cos_h[:, half:ROT_DIM] = np.cos(ang)
    sin_h[:, :half] = -np.sin(ang)
    sin_h[:, half:ROT_DIM] = np.sin(ang)
    cos_t = np.tile(cos_h, (1, ATTN_KV_HEADS)).astype(np.float32)
    sin_t = np.tile(sin_h, (1, ATTN_KV_HEADS)).astype(np.float32)
    return jnp.asarray(cos_t), jnp.asarray(sin_t)


def _in_projection(x, mod3, mod2, g_mix, w_in_bf16, cos_t, sin_t, tm):
    t = x.shape[0]
    nt = t // tm
    if mod3 is not None:
        per_b = (t // mod3.shape[0]) // tm
        sh_spec = pl.BlockSpec((1, 1, D_MODEL), lambda i: (i // per_b, 0, 0))
        sc_spec = pl.BlockSpec((1, 1, D_MODEL), lambda i: (i // per_b, 0, 1))
        mod = mod3
        ncs = cos_t.shape[0] // tm
        cs_spec = pl.BlockSpec((tm, KV_WIDTH), lambda i: (i % ncs, 0))
    else:
        sh_spec = pl.BlockSpec((tm, D_MODEL), lambda i: (i, 0))
        sc_spec = pl.BlockSpec((tm, D_MODEL), lambda i: (i, 1))
        mod = mod2
        cs_spec = pl.BlockSpec((tm, KV_WIDTH), lambda i: (i, 0))
    row = lambda w: pl.BlockSpec((tm, w), lambda i: (i, 0))
    return pl.pallas_call(
        _inproj_kernel,
        grid=(nt,),
        in_specs=[row(D_MODEL), sh_spec, sc_spec,
                  pl.BlockSpec((1, D_MODEL), lambda i: (0, 0)),
                  pl.BlockSpec((D_MODEL, IN_COLS), lambda i: (0, 0)),
                  cs_spec, cs_spec],
        out_specs=[row(ATTN_WIDTH), row(KV_WIDTH), row(KV_WIDTH),
                   row(HG_WIDTH), row(HG_WIDTH), row(HG_WIDTH), row(HG_WIDTH)],
        out_shape=[jax.ShapeDtypeStruct((t, ATTN_WIDTH), BF16),
                   jax.ShapeDtypeStruct((t, KV_WIDTH), F32),
                   jax.ShapeDtypeStruct((t, KV_WIDTH), F32),
                   jax.ShapeDtypeStruct((t, HG_WIDTH), F32),
                   jax.ShapeDtypeStruct((t, HG_WIDTH), F32),
                   jax.ShapeDtypeStruct((t, HG_WIDTH), BF16),
                   jax.ShapeDtypeStruct((t, HG_WIDTH), F32)],
        compiler_params=pltpu.CompilerParams(vmem_limit_bytes=V7X_VMEM_LIMIT),
        name="in_proj",
    )(x, mod, mod, g_mix.reshape(1, -1), w_in_bf16, cos_t, sin_t)


def _attn_prompt_kernel(sink_ref, q_ref, kc_ref, kp_ref, vc_ref, vp_ref, g_ref, o_ref):
    n = pl.program_id(1)
    blk = WINDOW
    q = q_ref[...]
    qi = lax.broadcasted_iota(jnp.int32, (ATTN_GROUP * blk, 2 * blk), 0) % blk
    kj = lax.broadcasted_iota(jnp.int32, (ATTN_GROUP * blk, 2 * blk), 1)
    ok = (kj >= qi) & (kj <= qi + blk) & ((kj >= blk) | (n > 0))
    outs = []
    for h in range(ATTN_KV_HEADS):
        ls = slice(h * HEAD_DIM, (h + 1) * HEAD_DIM)
        k2 = jnp.concatenate([kp_ref[:, ls], kc_ref[:, ls]], axis=0).astype(BF16)
        v2 = jnp.concatenate([vp_ref[:, ls], vc_ref[:, ls]], axis=0).astype(BF16)
        q4 = jnp.concatenate(
            [q[:, (h * ATTN_GROUP + g) * HEAD_DIM:(h * ATTN_GROUP + g + 1) * HEAD_DIM]
             for g in range(ATTN_GROUP)], axis=0)
        sink = jnp.concatenate(
            [jnp.full((blk, 1), sink_ref[h * ATTN_GROUP + g], F32) for g in range(ATTN_GROUP)], axis=0)
        s = jnp.where(ok, _dg(q4, k2, NT), -jnp.inf)
        m = jnp.maximum(jnp.max(s, axis=-1, keepdims=True), sink)
        p = jnp.exp(s - m)
        den = jnp.sum(p, axis=-1, keepdims=True) + jnp.exp(sink - m)
        o = _dg(p.astype(BF16), v2, NN) / den
        outs += [o[g * blk:(g + 1) * blk] for g in range(ATTN_GROUP)]
    o_all = jnp.concatenate(outs, axis=1)
    o_ref[...] = _rms(o_all, g_ref[...]).astype(BF16)


def _attention_prompt(qa, ka, va, sinks, g_attn_out, batch):
    t = qa.shape[0]
    nb = t // batch // WINDOW
    cur = lambda w: pl.BlockSpec((WINDOW, w), lambda b, n: (b * nb + n, 0))
    prev = lambda w: pl.BlockSpec((WINDOW, w), lambda b, n: (b * nb + jnp.maximum(n - 1, 0), 0))
    return pl.pallas_call(
        _attn_prompt_kernel,
        grid=(batch, nb),
        in_specs=[pl.BlockSpec(memory_space=pltpu.SMEM),
                  cur(ATTN_WIDTH), cur(KV_WIDTH), prev(KV_WIDTH), cur(KV_WIDTH), prev(KV_WIDTH),
                  pl.BlockSpec((1, ATTN_WIDTH), lambda b, n: (0, 0))],
        out_specs=cur(ATTN_WIDTH),
        out_shape=jax.ShapeDtypeStruct((t, ATTN_WIDTH), BF16),
        name="attn_prompt",
    )(sinks, qa, ka, ka, va, va, g_attn_out.reshape(1, -1))


def _lower_bound(lb_logits_ref):
    lg = lb_logits_ref[...]
    e = jnp.exp(lg - jnp.max(lg, axis=0, keepdims=True))
    return e[0:1] / jnp.sum(e, axis=0, keepdims=True)


def _hgrn_prompt_kernel(q_ref, f_ref, i_ref, g_ref, lbl_ref, gn_ref, o_ref, sfin_ref, st_ref):
    c = pl.program_id(1)
    C = HG_CHUNK
    nsub = C // HG_SUB

    @pl.when(c == 0)
    def _():
        st_ref[...] = jnp.zeros_like(st_ref)

    lb_all = _lower_bound(lbl_ref)
    row = lax.broadcasted_iota(jnp.int32, (C, HG_DK), 0)
    r64 = lax.broadcasted_iota(jnp.int32, (C, C), 0)
    c64 = lax.broadcasted_iota(jnp.int32, (C, C), 1)
    tri = (r64 >= c64).astype(BF16)
    rsub = r64 % HG_SUB
    for hh in range(HG_HEADS):
        sl = slice(hh * HG_DK, (hh + 1) * HG_DK)
        lb = lb_all[:, sl]
        f = lb + (1.0 - lb) * _sigmoid(f_ref[:, sl])
        kk = 1.0 - f
        logf = jnp.log(f)
        q = q_ref[:, sl]
        v = i_ref[:, sl]
        b = _dot_exact_lhs(tri, logf, NN)
        a = jnp.where(r64 == c64, jnp.sum(q * kk, axis=-1, keepdims=True), 0.0)
        w = logf
        for d in range(1, HG_SUB):
            if d > 1:
                w = w + pltpu.roll(logf, d - 1, axis=0)
            ad = jnp.sum(q * pltpu.roll(kk, d, axis=0) * jnp.exp(w), axis=-1, keepdims=True)
            a = a + jnp.where((c64 == r64 - d) & (rsub >= d), ad, 0.0)
        bend = jnp.concatenate(
            [jnp.broadcast_to(b[j * HG_SUB + HG_SUB - 1:(j + 1) * HG_SUB], (HG_SUB, HG_DK))
             for j in range(nsub)], axis=0)
        kt = kk * jnp.exp(bend - b)
        lhs, rhs = [], []
        for j in range(nsub - 1):
            anchor = b[j * HG_SUB + HG_SUB - 1:(j + 1) * HG_SUB]
            qd = q * jnp.exp(jnp.minimum(b - anchor, 0.0))
            lhs.append(jnp.where(row >= (j + 1) * HG_SUB, qd, 0.0).astype(BF16))
            rhs.append(jnp.where(row // HG_SUB == j, kt, 0.0).astype(BF16))
        a = a + _dg(jnp.concatenate(lhs, axis=1), jnp.concatenate(rhs, axis=1), NT)
        st = st_ref[hh]
        o = _dg(a.astype(BF16), v, NN) + _dg((q * jnp.exp(b)).astype(BF16), st.astype(BF16), NT)
        blast = b[C - 1:C]
        kd = (kk * jnp.exp(blast - b)).astype(BF16)
        st_new = st * jnp.exp(blast) + _dg(v, kd, TN)
        st_ref[hh] = st_new
        g = g_ref[:, sl]
        o_ref[:, sl] = (_rms(o, gn_ref[:, sl]) * (g * _sigmoid(g))).astype(BF16)

        @pl.when(c == pl.num_programs(1) - 1)
        def _():
            sfin_ref[0, hh] = st_new.T


def _hgrn_prompt(qh, fh, ih, gh, lb_logits, g_hg_out, batch):
    t = qh.shape[0]
    nc = t // batch // HG_CHUNK
    blk = pl.BlockSpec((HG_CHUNK, HG_WIDTH), lambda b, c: (b * nc + c, 0))
    const = lambda r: pl.BlockSpec((r, HG_WIDTH), lambda b, c: (0, 0))
    return pl.pallas_call(
        _hgrn_prompt_kernel,
        grid=(batch, nc),
        in_specs=[blk, blk, blk, blk, const(lb_logits.shape[0]), const(1)],
        out_specs=[blk, pl.BlockSpec((1, HG_HEADS, HG_DK, HG_DV), lambda b, c: (b, 0, 0, 0))],
        out_shape=[jax.ShapeDtypeStruct((t, HG_WIDTH), BF16),
                   jax.ShapeDtypeStruct((batch, HG_HEADS, HG_DK, HG_DV), F32)],
        scratch_shapes=[pltpu.VMEM((HG_HEADS, HG_DV, HG_DK), F32)],
        name="hgrn_prompt",
    )(qh, fh, ih, gh, lb_logits, g_hg_out.reshape(1, -1))


def _attn_sample_kernel(sink_ref, q_ref, kn_ref, vn_ref, kc_ref, vc_ref, g_ref,
                        o_ref, ko_ref, vo_ref):
    bt = q_ref.shape[0]
    win = kc_ref.shape[1]
    kc = kc_ref[...]
    vc = vc_ref[...]
    kn = kn_ref[...]
    vn = vn_ref[...]
    outs = []
    for h in range(ATTN_KV_HEADS):
        ls = slice(h * HEAD_DIM, (h + 1) * HEAD_DIM)
        q = q_ref[:, h * ATTN_GROUP:(h + 1) * ATTN_GROUP, :]
        s = jnp.einsum('bgd,bjd->bgj', q, kc[:, :, ls].astype(BF16), preferred_element_type=F32)
        s_new = jnp.sum(q.astype(F32) * kn[:, None, ls], axis=-1, keepdims=True)
        gi = lax.broadcasted_iota(jnp.int32, (1, ATTN_GROUP, 1), 1)
        sink = jnp.zeros((1, ATTN_GROUP, 1), F32)
        for g in range(ATTN_GROUP):
            sink = jnp.where(gi == g, sink_ref[h * ATTN_GROUP + g], sink)
        m = jnp.maximum(jnp.maximum(jnp.max(s, axis=-1, keepdims=True), s_new), sink)
        p = jnp.exp(s - m)
        p_new = jnp.exp(s_new - m)
        den = jnp.sum(p, axis=-1, keepdims=True) + p_new + jnp.exp(sink - m)
        o = jnp.einsum('bgj,bjd->bgd', p.astype(BF16), vc[:, :, ls].astype(BF16),
                       preferred_element_type=F32)
        o = (o + p_new * vn[:, None, ls]) / den
        outs.append(o)
    ssq = sum(jnp.sum(jnp.sum(o * o, axis=-1, keepdims=True), axis=1, keepdims=True) for o in outs)
    scale = lax.rsqrt(ssq / ATTN_WIDTH + EPS)
    for h in range(ATTN_KV_HEADS):
        o_ref[h] = (outs[h] * scale * g_ref[h][None]).astype(BF16)
    ri = lax.broadcasted_iota(jnp.int32, (win, KV_WIDTH), 0)
    for b in range(bt):
        ko_ref[b] = jnp.where(ri == win - 1, kn[b:b + 1], pltpu.roll(kc[b], win - 1, axis=0))
        vo_ref[b] = jnp.where(ri == win - 1, vn[b:b + 1], pltpu.roll(vc[b], win - 1, axis=0))


def _attention_sample(qa, ka, va, cache_k, cache_v, sinks, g_attn_out, bt=8):
    nb = qa.shape[0]
    win = cache_k.shape[1]
    q3 = qa.reshape(nb, ATTN_HEADS, HEAD_DIM)
    g3 = g_attn_out.reshape(ATTN_KV_HEADS, ATTN_GROUP, HEAD_DIM)
    row = lambda w: pl.BlockSpec((bt, w), lambda i: (i, 0))
    cache = pl.BlockSpec((bt, win, KV_WIDTH), lambda i: (i, 0, 0))
    o4, k_new, v_new = pl.pallas_call(
        _attn_sample_kernel,
        grid=(nb // bt,),
        in_specs=[pl.BlockSpec(memory_space=pltpu.SMEM),
                  pl.BlockSpec((bt, ATTN_HEADS, HEAD_DIM), lambda i: (i, 0, 0)),
                  row(KV_WIDTH), row(KV_WIDTH), cache, cache,
                  pl.BlockSpec((ATTN_KV_HEADS, ATTN_GROUP, HEAD_DIM), lambda i: (0, 0, 0))],
        out_specs=[pl.BlockSpec((ATTN_KV_HEADS, bt, ATTN_GROUP, HEAD_DIM), lambda i: (0, i, 0, 0)),
                   cache, cache],
        out_shape=[jax.ShapeDtypeStruct((ATTN_KV_HEADS, nb, ATTN_GROUP, HEAD_DIM), BF16),
                   jax.ShapeDtypeStruct(cache_k.shape, F32),
                   jax.ShapeDtypeStruct(cache_v.shape, F32)],
        name="attn_sample",
    )(sinks, q3, ka, va, cache_k, cache_v, g3)
    oa = jnp.transpose(o4, (1, 0, 2, 3)).reshape(nb, ATTN_WIDTH)
    return oa, k_new, v_new


def _hgrn_sample_kernel(q_ref, f_ref, i_ref, g_ref, lbl_ref, gn_ref, s0_ref, o_ref, s_ref):
    bt = q_ref.shape[0]
    lb = _lower_bound(lbl_ref)
    f = lb + (1.0 - lb) * _sigmoid(f_ref[...])
    kk = 1.0 - f
    q = q_ref[...]
    v = i_ref[...].astype(F32)
    g = g_ref[...]
    gate = g * _sigmoid(g)
    r = lax.broadcasted_iota(jnp.int32, (HG_DK, HG_DK), 0)
    cc = lax.broadcasted_iota(jnp.int32, (HG_DK, HG_DK), 1)
    diag = r == cc

    def column(x_row):
        return jnp.sum(jnp.where(diag, x_row, 0.0), axis=1, keepdims=True)

    for b in range(bt):
        parts = []
        for hh in range(HG_HEADS):
            sl = slice(hh * HG_DK, (hh + 1) * HG_DK)
            s_new = column(f[b:b + 1, sl]) * s0_ref[b, hh] + column(kk[b:b + 1, sl]) * v[b:b + 1, sl]
            s_ref[b, hh] = s_new
            o = jnp.sum(s_new * column(q[b:b + 1, sl]), axis=0, keepdims=True)
            parts.append(_rms(o, gn_ref[:, sl]))
        o_ref[b:b + 1, :] = (jnp.concatenate(parts, axis=1) * gate[b:b + 1]).astype(BF16)


def _hgrn_sample(qh, fh, ih, gh, lb_logits, g_hg_out, state, bt=8):
    nb = qh.shape[0]
    row = pl.BlockSpec((bt, HG_WIDTH), lambda i: (i, 0))
    const = lambda r: pl.BlockSpec((r, HG_WIDTH), lambda i: (0, 0))
    st = pl.BlockSpec((bt, HG_HEADS, HG_DK, HG_DV), lambda i: (i, 0, 0, 0))
    return pl.pallas_call(
        _hgrn_sample_kernel,
        grid=(nb // bt,),
        in_specs=[row, row, row, row, const(lb_logits.shape[0]), const(1), st],
        out_specs=[row, st],
        out_shape=[jax.ShapeDtypeStruct((nb, HG_WIDTH), BF16),
                   jax.ShapeDtypeStruct(state.shape, F32)],
        name="hgrn_sample",
    )(qh, fh, ih, gh, lb_logits, g_hg_out.reshape(1, -1), state)


def _outproj_kernel(x_ref, oa_ref, oh_ref, gt_ref, sh_ref, sc_ref, g_ref, wo_ref, wr_ref, br_ref,
                    x1_ref, h2_ref, cw_ref):
    gt = gt_ref[...].reshape(-1, D_MODEL)
    sh = sh_ref[...].reshape(-1, D_MODEL)
    sc = sc_ref[...].reshape(-1, D_MODEL)
    mix = _dg(oa_ref[...], wo_ref[0:ATTN_WIDTH, :], NN) + _dg(oh_ref[...], wo_ref[ATTN_WIDTH:, :], NN)
    x1 = x_ref[...] + gt * mix
    x1_ref[...] = x1
    h2 = _rms(x1, g_ref[...]) * (1.0 + sc) + sh
    h2_ref[...] = h2
    logits = _dot_f32(wr_ref[...], h2, NT) + br_ref[...]
    ei = lax.broadcasted_iota(jnp.int32, logits.shape, 0)
    vals, sel = [], []
    l = logits
    for _ in range(TOP_K):
        m = jnp.max(l, axis=0, keepdims=True)
        idx = jnp.min(jnp.where(l == m, ei, N_EXPERTS), axis=0, keepdims=True)
        pick = ei == idx
        vals.append(m)
        sel.append(pick)
        l = jnp.where(pick, -jnp.inf, l)
    ex = [jnp.exp(v - vals[0]) for v in vals]
    den = ex[0] + ex[1] + ex[2] + ex[3]
    cw = jnp.zeros_like(logits)
    for k in range(TOP_K):
        cw = jnp.where(sel[k], ex[k] / den, cw)
    cw_ref[...] = cw


def _out_projection(x, oa, oh, mod3, mod2, g_ffn, w_out_bf16, w_router_t, b_router, tm):
    t = x.shape[0]
    nt = t // tm
    if mod3 is not None:
        per_b = (t // mod3.shape[0]) // tm
        mspec = lambda j: pl.BlockSpec((1, 1, D_MODEL), lambda i: (i // per_b, 0, j))
        mod = mod3
    else:
        mspec = lambda j: pl.BlockSpec((tm, D_MODEL), lambda i: (i, j))
        mod = mod2
    row = lambda w: pl.BlockSpec((tm, w), lambda i: (i, 0))
    full = lambda a: pl.BlockSpec(a.shape, lambda i: (0,) * a.ndim)
    g2 = g_ffn.reshape(1, -1)
    br = b_router.reshape(-1, 1)
    return pl.pallas_call(
        _outproj_kernel,
        grid=(nt,),
        in_specs=[row(D_MODEL), row(ATTN_WIDTH), row(HG_WIDTH), mspec(2), mspec(3), mspec(4),
                  full(g2), full(w_out_bf16), full(w_router_t), full(br)],
        out_specs=[row(D_MODEL), row(D_MODEL), pl.BlockSpec((N_EXPERTS, tm), lambda i: (0, i))],
        out_shape=[jax.ShapeDtypeStruct((t, D_MODEL), F32),
                   jax.ShapeDtypeStruct((t, D_MODEL), F32),
                   jax.ShapeDtypeStruct((N_EXPERTS, t), F32)],
        compiler_params=pltpu.CompilerParams(vmem_limit_bytes=V7X_VMEM_LIMIT),
        name="out_proj_router",
    )(x, oa, oh, mod, mod, mod, g2, w_out_bf16, w_router_t, br)


def _moe_kernel(h_ref, x1_ref, cw_ref, gt_ref, wg_ref, wl_ref, bg_ref, bl_ref, wd_ref, bd_ref, gf_ref,
                y_ref, acc_ref):
    e = pl.program_id(1)

    @pl.when(e == 0)
    def _():
        acc_ref[...] = jnp.zeros_like(acc_ref)

    hb = h_ref[...].astype(BF16)
    glu = jnp.minimum(_dg(hb, wg_ref[0], NN) + bg_ref[0], SWIGLU_LIMIT)
    lin = jnp.clip(_dg(hb, wl_ref[0], NN) + bl_ref[0], -SWIGLU_LIMIT, SWIGLU_LIMIT)
    act = glu * _sigmoid(SWIGLU_ALPHA * glu) * (lin + 1.0)
    y = _dg(act.astype(BF16), wd_ref[0], NN) + bd_ref[0]
    lane = lax.broadcasted_iota(jnp.int32, cw_ref.shape, 1)
    cw = jnp.sum(jnp.where(lane == e, cw_ref[...], 0.0), axis=1, keepdims=True)
    acc_ref[...] += cw * y

    @pl.when(e == pl.num_programs(1) - 1)
    def _():
        gt = gt_ref[...].reshape(-1, D_MODEL)
        y_ref[...] = _rms(x1_ref[...] + gt * acc_ref[...], gf_ref[...])


def _moe_dense(h2, x1, cw, mod3, mod2, wg, wl, bg, bl, wd, bd, g_final, tm):
    t = h2.shape[0]
    nt = t // tm
    if mod3 is not None:
        per_b = (t // mod3.shape[0]) // tm
        gspec = pl.BlockSpec((1, 1, D_MODEL), lambda i, e: (i // per_b, 0, 5))
        mod = mod3
    else:
        gspec = pl.BlockSpec((tm, D_MODEL), lambda i, e: (i, 5))
        mod = mod2
    row = lambda w: pl.BlockSpec((tm, w), lambda i, e: (i, 0))
    wspec = pl.BlockSpec((1, D_MODEL, D_FF), lambda i, e: (e, 0, 0))
    bspec = pl.BlockSpec((1, 1, D_FF), lambda i, e: (e, 0, 0))
    return pl.pallas_call(
        _moe_kernel,
        grid=(nt, N_EXPERTS),
        in_specs=[row(D_MODEL), row(D_MODEL), row(N_EXPERTS), gspec,
                  wspec, wspec, bspec, bspec, wspec, bspec,
                  pl.BlockSpec((1, D_MODEL), lambda i, e: (0, 0))],
        out_specs=row(D_MODEL),
        out_shape=jax.ShapeDtypeStruct((t, D_MODEL), F32),
        scratch_shapes=[pltpu.VMEM((tm, D_MODEL), F32)],
        compiler_params=pltpu.CompilerParams(vmem_limit_bytes=V7X_VMEM_LIMIT),
        name="moe_dense",
    )(h2, x1, cw, mod, wg, wl, bg, bl, wd, bd, g_final.reshape(1, -1))


def kernel(x_prompt, x_sample, c_prompt, c_sample, cache_k_win, cache_v_win, state_hgrn, w_ada, b_ada,
           g_mix, g_ffn, w_in, attn_sinks, g_attn_out, hg_lb_logits, g_hg_out, w_out, w_router, b_router,
           w_up, b_up, w_down, b_down, g_final):
    batch, seq, d = x_prompt.shape
    nsamp = x_sample.shape[0]
    win = cache_k_win.shape[2]
    layer = 0

    mod = _modulation(jnp.concatenate([c_prompt, c_sample], axis=0), w_ada[layer], b_ada[layer])
    mod_p = mod[:batch].reshape(batch, 1, 6 * d)
    mod_s = mod[batch:]

    w_in_b = w_in[layer].astype(BF16)
    w_out_b = w_out[layer].astype(BF16)
    w_router_t = w_router[layer].T
    wg = w_up[layer][:, :, 0::2].astype(BF16)
    wl = w_up[layer][:, :, 1::2].astype(BF16)
    bg = b_up[layer][:, None, 0::2]
    bl = b_up[layer][:, None, 1::2]
    wd = w_down[layer].astype(BF16)
    bd = b_down[layer][:, None, :]

    xp = x_prompt.reshape(batch * seq, d)
    cos_p, sin_p = _rope_tables(np.arange(seq))
    qa, ka, va, qh, fh, ih, gh = _in_projection(xp, mod_p, None, g_mix[layer], w_in_b, cos_p, sin_p, 512)
    oa = _attention_prompt(qa, ka, va, attn_sinks[layer], g_attn_out[layer], batch)
    oh, s_prompt = _hgrn_prompt(qh, fh, ih, gh, hg_lb_logits, g_hg_out[layer], batch)
    x1, h2, cw_t = _out_projection(xp, oa, oh, mod_p, None, g_ffn[layer], w_out_b, w_router_t,
                                   b_router[layer], 512)
    y_prompt = _moe_dense(h2, x1, cw_t.T, mod_p, None, wg, wl, bg, bl, wd, bd, g_final, 512)
    k_win_p = ka.reshape(batch, seq, ATTN_KV_HEADS, HEAD_DIM)[:, seq - win:]
    v_win_p = va.reshape(batch, seq, ATTN_KV_HEADS, HEAD_DIM)[:, seq - win:]

    xs = x_sample.reshape(nsamp, d)
    cos_s, sin_s = _rope_tables(np.full((nsamp,), PAST_LEN))
    qa, ka, va, qh, fh, ih, gh = _in_projection(xs, None, mod_s, g_mix[layer], w_in_b, cos_s, sin_s, nsamp)
    oa, k_win_s, v_win_s = _attention_sample(
        qa, ka, va, cache_k_win[layer].reshape(nsamp, win, KV_WIDTH),
        cache_v_win[layer].reshape(nsamp, win, KV_WIDTH), attn_sinks[layer], g_attn_out[layer])
    oh, s_sample = _hgrn_sample(qh, fh, ih, gh, hg_lb_logits, g_hg_out[layer], state_hgrn[layer])
    x1, h2, cw_t = _out_projection(xs, oa, oh, None, mod_s, g_ffn[layer], w_out_b, w_router_t,
                                   b_router[layer], nsamp)
    y_sample = _moe_dense(h2, x1, cw_t.T, None, mod_s, wg, wl, bg, bl, wd, bd, g_final, nsamp)

    kv_shape = (1, nsamp, win, ATTN_KV_HEADS, HEAD_DIM)
    return (y_prompt.reshape(batch, seq, d), y_sample.reshape(nsamp, 1, d),
            k_win_p[None], v_win_p[None], s_prompt[None],
            k_win_s.reshape(kv_shape), v_win_s.reshape(kv_shape), s_sample[None])
```

```python
import functools

import numpy as np
import jax
import jax.numpy as jnp
from jax import lax
from jax.experimental import pallas as pl
from jax.experimental.pallas import tpu as pltpu

F32 = jnp.float32
BF16 = jnp.bfloat16

D_MODEL = 1024
SEQ = 2048
PAST_LEN = 16384
ATTN_HEADS = 8
ATTN_KV_HEADS = 2
HEAD_DIM = 64
ATTN_GROUP = ATTN_HEADS // ATTN_KV_HEADS
ATTN_WIDTH = ATTN_HEADS * HEAD_DIM
KV_WIDTH = ATTN_KV_HEADS * HEAD_DIM
WINDOW = 128
ROT_DIM = HEAD_DIM // 4
ROPE_THETA = 500000.0
HG_HEADS = 4
HG_DK = 128
HG_DV = 128
HG_WIDTH = HG_HEADS * HG_DV
HG_CHUNK = 64
HG_SUB = 8
IN_COLS = ATTN_WIDTH + 2 * KV_WIDTH + 4 * HG_WIDTH
N_EXPERTS = 32
TOP_K = 4
D_FF = D_MODEL
SWIGLU_ALPHA = 1.702
SWIGLU_LIMIT = 7.0
EPS = 1e-5

V7X_VMEM_LIMIT = 56 * 1024 * 1024

PIECE = 16
TOK_TILE = 512
SAMPLE_TILE = 128
N_PROMPT_TILES = 32
SLOT_ROWS = {TOK_TILE: 2560, SAMPLE_TILE: 1024}
TILE_PIECE_CAP = [SLOT_ROWS[TOK_TILE] // PIECE] * N_PROMPT_TILES + [SLOT_ROWS[SAMPLE_TILE] // PIECE]
TILE_PIECE_BASE = [i * TILE_PIECE_CAP[0] for i in range(N_PROMPT_TILES + 1)]
N_PIECES = sum(TILE_PIECE_CAP)
MOE_TILE_PIECES = 16
MOE_MAX_STEPS = N_PIECES // MOE_TILE_PIECES + N_EXPERTS + 1

NN = (((1,), (0,)), ((), ()))
NT = (((1,), (1,)), ((), ()))
TN = (((0,), (0,)), ((), ()))


def _dg(a, b, dims):
    return lax.dot_general(a, b, dims, preferred_element_type=F32)


def _split3(x):
    h = x.astype(BF16)
    r = x - h.astype(F32)
    m = r.astype(BF16)
    l = (r - m.astype(F32)).astype(BF16)
    return h, m, l


def _dot_f32(a, b, dims):
    ah, am, al = _split3(a)
    bh, bm, bl = _split3(b)
    return (_dg(ah, bh, dims) + (_dg(ah, bm, dims) + _dg(am, bh, dims))
            + (_dg(am, bm, dims) + _dg(ah, bl, dims) + _dg(al, bh, dims)))


def _dot_exact_lhs(a_bf16, b, dims):
    bh, bm, bl = _split3(b)
    return _dg(a_bf16, bh, dims) + _dg(a_bf16, bm, dims) + _dg(a_bf16, bl, dims)


def _sigmoid(x):
    return 1.0 / (1.0 + jnp.exp(-x))


def _rms(x, g):
    return x * lax.rsqrt(jnp.mean(x * x, axis=-1, keepdims=True) + EPS) * g


def _mod_kernel(c_ref, w_ref, b_ref, o_ref):
    c = c_ref[...]
    o_ref[...] = _dot_f32(c * _sigmoid(c), w_ref[...], NN) + b_ref[...]


def _modulation(c_all, w_ada, b_ada):
    n = c_all.shape[0]
    return pl.pallas_call(
        _mod_kernel,
        grid=(6,),
        in_specs=[pl.BlockSpec((n, D_MODEL), lambda j: (0, 0)),
                  pl.BlockSpec((D_MODEL, D_MODEL), lambda j: (0, j)),
                  pl.BlockSpec((1, D_MODEL), lambda j: (0, j))],
        out_specs=pl.BlockSpec((n, D_MODEL), lambda j: (0, j)),
        out_shape=jax.ShapeDtypeStruct((n, 6 * D_MODEL), F32),
        compiler_params=pltpu.CompilerParams(vmem_limit_bytes=V7X_VMEM_LIMIT),
        name="adaln_mod",
    )(c_all, w_ada, b_ada.reshape(1, -1))


def _rotate(x, cos_t, sin_t):
    width = x.shape[-1]
    d = lax.broadcasted_iota(jnp.int32, x.shape, 1) % HEAD_DIM
    half = ROT_DIM // 2
    partner = jnp.where(d < half, pltpu.roll(x, width - half, axis=1), pltpu.roll(x, half, axis=1))
    return x * cos_t + partner * sin_t


def _inproj_kernel(x_ref, sh_ref, sc_ref, g_ref, w_ref, cos_ref, sin_ref,
                   qa_ref, ka_ref, va_ref, qh_ref, fh_ref, ih_ref, gh_ref):
    x = x_ref[...]
    sh = sh_ref[...].reshape(-1, D_MODEL)
    sc = sc_ref[...].reshape(-1, D_MODEL)
    h = _rms(x, g_ref[...]) * (1.0 + sc) + sh
    z = _dg(h.astype(BF16), w_ref[...], NN)
    cos_k = cos_ref[...]
    sin_k = sin_ref[...]
    cos_q = jnp.concatenate([cos_k] * ATTN_GROUP, axis=1)
    sin_q = jnp.concatenate([sin_k] * ATTN_GROUP, axis=1)
    o = 0
    qa = _rotate(z[:, o:o + ATTN_WIDTH], cos_q, sin_q)
    qa_ref[...] = (qa * (HEAD_DIM ** -0.5)).astype(BF16)
    o += ATTN_WIDTH
    ka_ref[...] = _rotate(z[:, o:o + KV_WIDTH], cos_k, sin_k)
    o += KV_WIDTH
    va_ref[...] = z[:, o:o + KV_WIDTH]
    o += KV_WIDTH
    qh_ref[...] = z[:, o:o + HG_WIDTH]
    o += HG_WIDTH
    fh_ref[...] = z[:, o:o + HG_WIDTH]
    o += HG_WIDTH
    ih_ref[...] = z[:, o:o + HG_WIDTH].astype(BF16)
    o += HG_WIDTH
    gh_ref[...] = z[:, o:o + HG_WIDTH]


def _rope_tables(positions):
    half = ROT_DIM // 2
    inv = ROPE_THETA ** (-(np.arange(half, dtype=np.float64) * 2.0 / ROT_DIM))
    ang = np.asarray(positions, np.float64)[:, None] * inv[None, :]
    cos_h = np.ones((len(positions), HEAD_DIM))
    sin_h = np.zeros((len(positions), HEAD_DIM))
    cos_h[:, :half] = np.cos(ang)
    cos_h[:, half:ROT_DIM] = np.cos(ang)
    sin_h[:, :half] = -np.sin(ang)
    sin_h[:, half:ROT_DIM] = np.sin(ang)
    cos_t = np.tile(cos_h, (1, ATTN_KV_HEADS)).astype(np.float32)
    sin_t = np.tile(sin_h, (1, ATTN_KV_HEADS)).astype(np.float32)
    return jnp.asarray(cos_t), jnp.asarray(sin_t)


def _in_projection(x, mod3, mod2, g_mix, w_in_bf16, cos_t, sin_t, tm):
    t = x.shape[0]
    nt = t // tm
    if mod3 is not None:
        per_b = (t // mod3.shape[0]) // tm
        sh_spec = pl.BlockSpec((1, 1, D_MODEL), lambda i: (i // per_b, 0, 0))
        sc_spec = pl.BlockSpec((1, 1, D_MODEL), lambda i: (i // per_b, 0, 1))
        mod = mod3
        ncs = cos_t.shape[0] // tm
        cs_spec = pl.BlockSpec((tm, KV_WIDTH), lambda i: (i % ncs, 0))
    else:
        sh_spec = pl.BlockSpec((tm, D_MODEL), lambda i: (i, 0))
        sc_spec = pl.BlockSpec((tm, D_MODEL), lambda i: (i, 1))
        mod = mod2
        cs_spec = pl.BlockSpec((tm, KV_WIDTH), lambda i: (i, 0))
    row = lambda w: pl.BlockSpec((tm, w), lambda i: (i, 0))
    return pl.pallas_call(
        _inproj_kernel,
        grid=(nt,),
        in_specs=[row(D_MODEL), sh_spec, sc_spec,
                  pl.BlockSpec((1, D_MODEL), lambda i: (0, 0)),
                  pl.BlockSpec((D_MODEL, IN_COLS), lambda i: (0, 0)),
                  cs_spec, cs_spec],
        out_specs=[row(ATTN_WIDTH), row(KV_WIDTH), row(KV_WIDTH),
                   row(HG_WIDTH), row(HG_WIDTH), row(HG_WIDTH), row(HG_WIDTH)],
        out_shape=[jax.ShapeDtypeStruct((t, ATTN_WIDTH), BF16),
                   jax.ShapeDtypeStruct((t, KV_WIDTH), F32),
                   jax.ShapeDtypeStruct((t, KV_WIDTH), F32),
                   jax.ShapeDtypeStruct((t, HG_WIDTH), F32),
                   jax.ShapeDtypeStruct((t, HG_WIDTH), F32),
                   jax.ShapeDtypeStruct((t, HG_WIDTH), BF16),
                   jax.ShapeDtypeStruct((t, HG_WIDTH), F32)],
        compiler_params=pltpu.CompilerParams(vmem_limit_bytes=V7X_VMEM_LIMIT),
        name="in_proj",
    )(x, mod, mod, g_mix.reshape(1, -1), w_in_bf16, cos_t, sin_t)


def _attn_prompt_kernel(sink_ref, q_ref, kc_ref, kp_ref, vc_ref, vp_ref, g_ref, o_ref):
    n = pl.program_id(1)
    blk = WINDOW
    q = q_ref[...]
    qi = lax.broadcasted_iota(jnp.int32, (ATTN_GROUP * blk, 2 * blk), 0) % blk
    kj = lax.broadcasted_iota(jnp.int32, (ATTN_GROUP * blk, 2 * blk), 1)
    ok = (kj >= qi) & (kj <= qi + blk) & ((kj >= blk) | (n > 0))
    outs = []
    for h in range(ATTN_KV_HEADS):
        ls = slice(h * HEAD_DIM, (h + 1) * HEAD_DIM)
        k2 = jnp.concatenate([kp_ref[:, ls], kc_ref[:, ls]], axis=0).astype(BF16)
        v2 = jnp.concatenate([vp_ref[:, ls], vc_ref[:, ls]], axis=0).astype(BF16)
        q4 = jnp.concatenate(
            [q[:, (h * ATTN_GROUP + g) * HEAD_DIM:(h * ATTN_GROUP + g + 1) * HEAD_DIM]
             for g in range(ATTN_GROUP)], axis=0)
        sink = jnp.concatenate(
            [jnp.full((blk, 1), sink_ref[h * ATTN_GROUP + g], F32) for g in range(ATTN_GROUP)], axis=0)
        s = jnp.where(ok, _dg(q4, k2, NT), -jnp.inf)
        m = jnp.maximum(jnp.max(s, axis=-1, keepdims=True), sink)
        p = jnp.exp(s - m)
        den = jnp.sum(p, axis=-1, keepdims=True) + jnp.exp(sink - m)
        o = _dg(p.astype(BF16), v2, NN) / den
        outs += [o[g * blk:(g + 1) * blk] for g in range(ATTN_GROUP)]
    o_all = jnp.concatenate(outs, axis=1)
    o_ref[...] = _rms(o_all, g_ref[...]).astype(BF16)


def _attention_prompt(qa, ka, va, sinks, g_attn_out, batch):
    t = qa.shape[0]
    nb = t // batch // WINDOW
    cur = lambda w: pl.BlockSpec((WINDOW, w), lambda b, n: (b * nb + n, 0))
    prev = lambda w: pl.BlockSpec((WINDOW, w), lambda b, n: (b * nb + jnp.maximum(n - 1, 0), 0))
    return pl.pallas_call(
        _attn_prompt_kernel,
        grid=(batch, nb),
        in_specs=[pl.BlockSpec(memory_space=pltpu.SMEM),
                  cur(ATTN_WIDTH), cur(KV_WIDTH), prev(KV_WIDTH), cur(KV_WIDTH), prev(KV_WIDTH),
                  pl.BlockSpec((1, ATTN_WIDTH), lambda b, n: (0, 0))],
        out_specs=cur(ATTN_WIDTH),
        out_shape=jax.ShapeDtypeStruct((t, ATTN_WIDTH), BF16),
        name="attn_prompt",
    )(sinks, qa, ka, ka, va, va, g_attn_out.reshape(1, -1))


def _lower_bound(lb_logits_ref):
    lg = lb_logits_ref[...]
    e = jnp.exp(lg - jnp.max(lg, axis=0, keepdims=True))
    return e[0:1] / jnp.sum(e, axis=0, keepdims=True)


def _hgrn_prompt_kernel(q_ref, f_ref, i_ref, g_ref, lbl_ref, gn_ref, o_ref, sfin_ref, st_ref):
    c = pl.program_id(1)
    C = HG_CHUNK
    nsub = C // HG_SUB

    @pl.when(c == 0)
    def _():
        st_ref[...] = jnp.zeros_like(st_ref)

    lb_all = _lower_bound(lbl_ref)
    row = lax.broadcasted_iota(jnp.int32, (C, HG_DK), 0)
    r64 = lax.broadcasted_iota(jnp.int32, (C, C), 0)
    c64 = lax.broadcasted_iota(jnp.int32, (C, C), 1)
    tri = (r64 >= c64).astype(BF16)
    rsub = r64 % HG_SUB
    for hh in range(HG_HEADS):
        sl = slice(hh * HG_DK, (hh + 1) * HG_DK)
        lb = lb_all[:, sl]
        f = lb + (1.0 - lb) * _sigmoid(f_ref[:, sl])
        kk = 1.0 - f
        logf = jnp.log(f)
        q = q_ref[:, sl]
        v = i_ref[:, sl]
        b = _dot_exact_lhs(tri, logf, NN)
        a = jnp.where(r64 == c64, jnp.sum(q * kk, axis=-1, keepdims=True), 0.0)
        w = logf
        for d in range(1, HG_SUB):
            if d > 1:
                w = w + pltpu.roll(logf, d - 1, axis=0)
            ad = jnp.sum(q * pltpu.roll(kk, d, axis=0) * jnp.exp(w), axis=-1, keepdims=True)
            a = a + jnp.where((c64 == r64 - d) & (rsub >= d), ad, 0.0)
        bend = jnp.concatenate(
            [jnp.broadcast_to(b[j * HG_SUB + HG_SUB - 1:(j + 1) * HG_SUB], (HG_SUB, HG_DK))
             for j in range(nsub)], axis=0)
        kt = kk * jnp.exp(bend - b)
        lhs, rhs = [], []
        for j in range(nsub - 1):
            anchor = b[j * HG_SUB + HG_SUB - 1:(j + 1) * HG_SUB]
            qd = q * jnp.exp(jnp.minimum(b - anchor, 0.0))
            lhs.append(jnp.where(row >= (j + 1) * HG_SUB, qd, 0.0).astype(BF16))
            rhs.append(jnp.where(row // HG_SUB == j, kt, 0.0).astype(BF16))
        a = a + _dg(jnp.concatenate(lhs, axis=1), jnp.concatenate(rhs, axis=1), NT)
        st = st_ref[hh]
        o = _dg(a.astype(BF16), v, NN) + _dg((q * jnp.exp(b)).astype(BF16), st.astype(BF16), NT)
        blast = b[C - 1:C]
        kd = (kk * jnp.exp(blast - b)).astype(BF16)
        st_new = st * jnp.exp(blast) + _dg(v, kd, TN)
        st_ref[hh] = st_new
        g = g_ref[:, sl]
        o_ref[:, sl] = (_rms(o, gn_ref[:, sl]) * (g * _sigmoid(g))).astype(BF16)

        @pl.when(c == pl.num_programs(1) - 1)
        def _():
            sfin_ref[0, hh] = st_new.T


def _hgrn_prompt(qh, fh, ih, gh, lb_logits, g_hg_out, batch):
    t = qh.shape[0]
    nc = t // batch // HG_CHUNK
    blk = pl.BlockSpec((HG_CHUNK, HG_WIDTH), lambda b, c: (b * nc + c, 0))
    const = lambda r: pl.BlockSpec((r, HG_WIDTH), lambda b, c: (0, 0))
    return pl.pallas_call(
        _hgrn_prompt_kernel,
        grid=(batch, nc),
        in_specs=[blk, blk, blk, blk, const(lb_logits.shape[0]), const(1)],
        out_specs=[blk, pl.BlockSpec((1, HG_HEADS, HG_DK, HG_DV), lambda b, c: (b, 0, 0, 0))],
        out_shape=[jax.ShapeDtypeStruct((t, HG_WIDTH), BF16),
                   jax.ShapeDtypeStruct((batch, HG_HEADS, HG_DK, HG_DV), F32)],
        scratch_shapes=[pltpu.VMEM((HG_HEADS, HG_DV, HG_DK), F32)],
        name="hgrn_prompt",
    )(qh, fh, ih, gh, lb_logits, g_hg_out.reshape(1, -1))


def _attn_sample_kernel(sink_ref, q_ref, kn_ref, vn_ref, kc_ref, vc_ref, g_ref,
                        o_ref, ko_ref, vo_ref):
    bt = q_ref.shape[0]
    win = kc_ref.shape[1]
    kc = kc_ref[...]
    vc = vc_ref[...]
    kn = kn_ref[...]
    vn = vn_ref[...]
    outs = []
    for h in range(ATTN_KV_HEADS):
        ls = slice(h * HEAD_DIM, (h + 1) * HEAD_DIM)
        q = q_ref[:, h * ATTN_GROUP:(h + 1) * ATTN_GROUP, :]
        s = jnp.einsum('bgd,bjd->bgj', q, kc[:, :, ls].astype(BF16), preferred_element_type=F32)
        s_new = jnp.sum(q.astype(F32) * kn[:, None, ls], axis=-1, keepdims=True)
        gi = lax.broadcasted_iota(jnp.int32, (1, ATTN_GROUP, 1), 1)
        sink = jnp.zeros((1, ATTN_GROUP, 1), F32)
        for g in range(ATTN_GROUP):
            sink = jnp.where(gi == g, sink_ref[h * ATTN_GROUP + g], sink)
        m = jnp.maximum(jnp.maximum(jnp.max(s, axis=-1, keepdims=True), s_new), sink)
        p = jnp.exp(s - m)
        p_new = jnp.exp(s_new - m)
        den = jnp.sum(p, axis=-1, keepdims=True) + p_new + jnp.exp(sink - m)
        o = jnp.einsum('bgj,bjd->bgd', p.astype(BF16), vc[:, :, ls].astype(BF16),
                       preferred_element_type=F32)
        o = (o + p_new * vn[:, None, ls]) / den
        outs.append(o)
    ssq = sum(jnp.sum(jnp.sum(o * o, axis=-1, keepdims=True), axis=1, keepdims=True) for o in outs)
    scale = lax.rsqrt(ssq / ATTN_WIDTH + EPS)
    for h in range(ATTN_KV_HEADS):
        o_ref[h] = (outs[h] * scale * g_ref[h][None]).astype(BF16)
    ri = lax.broadcasted_iota(jnp.int32, (win, KV_WIDTH), 0)
    for b in range(bt):
        ko_ref[b] = jnp.where(ri == win - 1, kn[b:b + 1], pltpu.roll(kc[b], win - 1, axis=0))
        vo_ref[b] = jnp.where(ri == win - 1, vn[b:b + 1], pltpu.roll(vc[b], win - 1, axis=0))


def _attention_sample(qa, ka, va, cache_k, cache_v, sinks, g_attn_out, bt=8):
    nb = qa.shape[0]
    win = cache_k.shape[1]
    q3 = qa.reshape(nb, ATTN_HEADS, HEAD_DIM)
    g3 = g_attn_out.reshape(ATTN_KV_HEADS, ATTN_GROUP, HEAD_DIM)
    row = lambda w: pl.BlockSpec((bt, w), lambda i: (i, 0))
    cache = pl.BlockSpec((bt, win, KV_WIDTH), lambda i: (i, 0, 0))
    o4, k_new, v_new = pl.pallas_call(
        _attn_sample_kernel,
        grid=(nb // bt,),
        in_specs=[pl.BlockSpec(memory_space=pltpu.SMEM),
                  pl.BlockSpec((bt, ATTN_HEADS, HEAD_DIM), lambda i: (i, 0, 0)),
                  row(KV_WIDTH), row(KV_WIDTH), cache, cache,
                  pl.BlockSpec((ATTN_KV_HEADS, ATTN_GROUP, HEAD_DIM), lambda i: (0, 0, 0))],
        out_specs=[pl.BlockSpec((ATTN_KV_HEADS, bt, ATTN_GROUP, HEAD_DIM), lambda i: (0, i, 0, 0)),
                   cache, cache],
        out_shape=[jax.ShapeDtypeStruct((ATTN_KV_HEADS, nb, ATTN_GROUP, HEAD_DIM), BF16),
                   jax.ShapeDtypeStruct(cache_k.shape, F32),
                   jax.ShapeDtypeStruct(cache_v.shape, F32)],
        name="attn_sample",
    )(sinks, q3, ka, va, cache_k, cache_v, g3)
    oa = jnp.transpose(o4, (1, 0, 2, 3)).reshape(nb, ATTN_WIDTH)
    return oa, k_new, v_new


def _hgrn_sample_kernel(q_ref, f_ref, i_ref, g_ref, lbl_ref, gn_ref, s0_ref, o_ref, s_ref):
    bt = q_ref.shape[0]
    lb = _lower_bound(lbl_ref)
    f = lb + (1.0 - lb) * _sigmoid(f_ref[...])
    kk = 1.0 - f
    q = q_ref[...]
    v = i_ref[...].astype(F32)
    g = g_ref[...]
    gate = g * _sigmoid(g)
    r = lax.broadcasted_iota(jnp.int32, (HG_DK, HG_DK), 0)
    cc = lax.broadcasted_iota(jnp.int32, (HG_DK, HG_DK), 1)
    diag = r == cc

    def column(x_row):
        return jnp.sum(jnp.where(diag, x_row, 0.0), axis=1, keepdims=True)

    for b in range(bt):
        parts = []
        for hh in range(HG_HEADS):
            sl = slice(hh * HG_DK, (hh + 1) * HG_DK)
            s_new = column(f[b:b + 1, sl]) * s0_ref[b, hh] + column(kk[b:b + 1, sl]) * v[b:b + 1, sl]
            s_ref[b, hh] = s_new
            o = jnp.sum(s_new * column(q[b:b + 1, sl]), axis=0, keepdims=True)
            parts.append(_rms(o, gn_ref[:, sl]))
        o_ref[b:b + 1, :] = (jnp.concatenate(parts, axis=1) * gate[b:b + 1]).astype(BF16)


def _hgrn_sample(qh, fh, ih, gh, lb_logits, g_hg_out, state, bt=8):
    nb = qh.shape[0]
    row = pl.BlockSpec((bt, HG_WIDTH), lambda i: (i, 0))
    const = lambda r: pl.BlockSpec((r, HG_WIDTH), lambda i: (0, 0))
    st = pl.BlockSpec((bt, HG_HEADS, HG_DK, HG_DV), lambda i: (i, 0, 0, 0))
    return pl.pallas_call(
        _hgrn_sample_kernel,
        grid=(nb // bt,),
        in_specs=[row, row, row, row, const(lb_logits.shape[0]), const(1), st],
        out_specs=[row, st],
        out_shape=[jax.ShapeDtypeStruct((nb, HG_WIDTH), BF16),
                   jax.ShapeDtypeStruct(state.shape, F32)],
        name="hgrn_sample",
    )(qh, fh, ih, gh, lb_logits, g_hg_out.reshape(1, -1), state)


def _outproj_kernel(*refs, n_tiles):
    xs_ref = refs[-4]

    @pl.when(pl.program_id(0) < n_tiles)
    def _():
        _outproj_tile(*refs)

    @pl.when(pl.program_id(0) >= n_tiles)
    def _():
        xs_ref[...] = jnp.zeros_like(xs_ref)


def _outproj_tile(x_ref, oa_ref, oh_ref, gt_ref, sh_ref, sc_ref, g_ref, wo_ref, wr_ref, br_ref, *rest):
    x1_ref, xs_ref, lp_ref, w_ref, pc_ref = rest[-5:]
    tm = x_ref.shape[0]
    rows = xs_ref.shape[0]
    gt = gt_ref[...].reshape(-1, D_MODEL)
    sh = sh_ref[...].reshape(-1, D_MODEL)
    sc = sc_ref[...].reshape(-1, D_MODEL)
    mix = _dg(oa_ref[...], wo_ref[0:ATTN_WIDTH, :], NN) + _dg(oh_ref[...], wo_ref[ATTN_WIDTH:, :], NN)
    x1 = x_ref[...] + gt * mix
    x1_ref[...] = x1
    h2 = _rms(x1, g_ref[...]) * (1.0 + sc) + sh
    logits = _dot_f32(wr_ref[...], h2, NT) + br_ref[...]
    ei = lax.broadcasted_iota(jnp.int32, logits.shape, 0)
    vals, sel = [], []
    l = logits
    for _ in range(TOP_K):
        m = jnp.max(l, axis=0, keepdims=True)
        idx = jnp.min(jnp.where(l == m, ei, N_EXPERTS), axis=0, keepdims=True)
        pick = ei == idx
        vals.append(m)
        sel.append(pick)
        l = jnp.where(pick, -jnp.inf, l)
    ex = [jnp.exp(v - vals[0]) for v in vals]
    den = ex[0] + ex[1] + ex[2] + ex[3]
    for k in range(TOP_K):
        w_ref[k:k + 1, :] = ex[k] / den

    member = jnp.where(sel[0] | sel[1] | sel[2] | sel[3], 1.0, 0.0)
    tr = lax.broadcasted_iota(jnp.int32, (tm, tm), 0)
    tc = lax.broadcasted_iota(jnp.int32, (tm, tm), 1)
    rank = _dg(member.astype(BF16), (tr < tc).astype(BF16), NN)
    pieces = jnp.floor((jnp.sum(member, axis=1, keepdims=True) + (PIECE - 1)) * (1.0 / PIECE))
    er = lax.broadcasted_iota(jnp.int32, (N_EXPERTS, N_EXPERTS), 0)
    ec = lax.broadcasted_iota(jnp.int32, (N_EXPERTS, N_EXPERTS), 1)
    pieces_b = jnp.broadcast_to(pieces, (N_EXPERTS, 128))
    pc_ref[0] = pieces_b.astype(jnp.int32)
    base = _dg((ec < er).astype(BF16), pieces_b.astype(BF16), NN)[:, 0:1] * PIECE
    slot = base + rank
    lps = [jnp.sum(jnp.where(sel[k], slot, 0.0), axis=0, keepdims=True).astype(jnp.int32)
           for k in range(TOP_K)]
    for k in range(TOP_K):
        lp_ref[k:k + 1, :] = lps[k]
    hb = h2.astype(BF16)
    chunk = 512
    for r0 in range(0, rows, chunk):
        si = lax.broadcasted_iota(jnp.int32, (chunk, tm), 0) + r0
        hit = (si == lps[0]) | (si == lps[1]) | (si == lps[2]) | (si == lps[3])
        xs_ref[r0:r0 + chunk, :] = _dg(jnp.where(hit, 1.0, 0.0).astype(BF16), hb, NN).astype(BF16)


def _out_projection(x, oa, oh, mod3, mod2, g_ffn, w_out_bf16, w_router_t, b_router, tm, slots, slot_block):
    t = x.shape[0]
    nt = t // tm
    rows = SLOT_ROWS[tm]
    real = lambda i: jnp.minimum(i, nt - 1)
    if mod3 is not None:
        per_b = (t // mod3.shape[0]) // tm
        mspec = lambda j: pl.BlockSpec((1, 1, D_MODEL), lambda i: (real(i) // per_b, 0, j))
        mod = mod3
    else:
        mspec = lambda j: pl.BlockSpec((tm, D_MODEL), lambda i: (real(i), j))
        mod = mod2
    row = lambda w: pl.BlockSpec((tm, w), lambda i: (real(i), 0))
    full = lambda a: pl.BlockSpec(a.shape, lambda i: (0,) * a.ndim)
    g2 = g_ffn.reshape(1, -1)
    br = b_router.reshape(-1, 1)
    args = [x, oa, oh, mod, mod, mod, g2, w_out_bf16, w_router_t, br]
    in_specs = [row(D_MODEL), row(ATTN_WIDTH), row(HG_WIDTH), mspec(2), mspec(3), mspec(4),
                full(g2), full(w_out_bf16), full(w_router_t), full(br)]
    aliases = {}
    n_fill = 0
    if slots is not None:
        args.append(slots)
        in_specs.append(pl.BlockSpec(memory_space=pl.ANY))
        aliases = {len(args) - 1: 1}
    else:
        n_fill = pl.cdiv(N_PIECES * PIECE - nt * rows, rows)
    return pl.pallas_call(
        functools.partial(_outproj_kernel, n_tiles=nt),
        grid=(nt + n_fill,),
        in_specs=in_specs,
        out_specs=[row(D_MODEL),
                   pl.BlockSpec((rows, D_MODEL), lambda i: (slot_block + i, 0)),
                   pl.BlockSpec((TOP_K, tm), lambda i: (0, real(i))),
                   pl.BlockSpec((TOP_K, tm), lambda i: (0, real(i))),
                   pl.BlockSpec((1, N_EXPERTS, 128), lambda i: (real(i), 0, 0))],
        out_shape=[jax.ShapeDtypeStruct((t, D_MODEL), F32),
                   jax.ShapeDtypeStruct((N_PIECES * PIECE, D_MODEL), BF16),
                   jax.ShapeDtypeStruct((TOP_K, t), jnp.int32),
                   jax.ShapeDtypeStruct((TOP_K, t), F32),
                   jax.ShapeDtypeStruct((nt, N_EXPERTS, 128), jnp.int32)],
        input_output_aliases=aliases,
        compiler_params=pltpu.CompilerParams(vmem_limit_bytes=V7X_VMEM_LIMIT),
        name="out_proj_router",
    )(*args)


def _piece_tables(pieces_ie):
    n_i = pieces_ie.shape[0]
    cap = jnp.asarray(TILE_PIECE_CAP, jnp.int32)
    gbase = jnp.asarray(TILE_PIECE_BASE, jnp.int32)
    seg_start = gbase[:, None] + jnp.cumsum(pieces_ie, axis=1) - pieces_ie
    used_i = jnp.sum(pieces_ie, axis=1)
    tail_i = cap - used_i
    tail_incl = jnp.cumsum(tail_i)
    cum_incl = jnp.cumsum(pieces_ie, axis=0)
    np_e = cum_incl[-1]
    nt_e = (np_e + MOE_TILE_PIECES - 1) // MOE_TILE_PIECES
    tile_end = jnp.cumsum(nt_e)
    tile_start = tile_end - nt_e
    n_comp = tile_end[-1]

    tt = jnp.arange(MOE_MAX_STEPS, dtype=jnp.int32)
    e_t = jnp.minimum(jnp.sum(tt[:, None] >= tile_end[None, :], axis=1), N_EXPERTS - 1).astype(jnp.int32)
    is_comp = tt < n_comp
    e_last = jnp.max(jnp.where(nt_e > 0, jnp.arange(N_EXPERTS, dtype=jnp.int32), 0))
    texp = jnp.where(is_comp, e_t, e_last).astype(jnp.int32)
    first = (is_comp & (tt == tile_start[e_t])).astype(jnp.int32)

    j = jnp.arange(MOE_MAX_STEPS * MOE_TILE_PIECES, dtype=jnp.int32)
    jt = j // MOE_TILE_PIECES
    je = e_t[jt]
    q = j - MOE_TILE_PIECES * tile_start[je]
    valid = is_comp[jt] & (q < np_e[je])
    cum_sel = cum_incl[:, je]
    i_idx = jnp.minimum(jnp.sum(q[None, :] >= cum_sel, axis=0), n_i - 1)
    cum_excl = jnp.take_along_axis(cum_sel - pieces_ie[:, je], i_idx[None, :], axis=0)[0]
    piece = seg_start[i_idx, je] + (q - cum_excl)
    z = j - MOE_TILE_PIECES * n_comp
    zvalid = (z >= 0) & (z < tail_incl[-1])
    iz = jnp.minimum(jnp.sum(z[None, :] >= tail_incl[:, None], axis=0), n_i - 1)
    piece_z = gbase[iz] + used_i[iz] + (z - (tail_incl[iz] - tail_i[iz]))
    tbl_in = jnp.where(valid, piece, 0).astype(jnp.int32)
    tbl_out = jnp.where(valid, piece, jnp.where(zvalid, piece_z, N_PIECES)).astype(jnp.int32)
    return tbl_in, tbl_out, texp, is_comp.astype(jnp.int32), first


def _moe_grouped_kernel(tin_ref, tout_ref, texp_ref, kind_ref, first_ref, *refs):
    npc = MOE_TILE_PIECES
    x_refs = refs[:npc]
    wu_ref, wd_ref, bg_ref, bl_ref, bd_ref, y_hbm, wg_s, wl_s, wd_s, ybuf, sems = refs[npc:]
    t = pl.program_id(0)
    nsteps = pl.num_programs(0)
    run = kind_ref[t] == 1
    slot = t % 2

    def piece_copy(step, buf, p):
        dst = tout_ref[step * npc + p]
        return dst, pltpu.make_async_copy(
            ybuf.at[buf, pl.ds(p * PIECE, PIECE), :],
            y_hbm.at[pl.ds(pl.multiple_of(dst * PIECE, PIECE), PIECE), :],
            sems.at[buf, p])

    def wait_step(step, buf):
        for p in range(npc):
            dst, cp = piece_copy(step, buf, p)

            @pl.when(dst < N_PIECES)
            def _():
                cp.wait()

    @pl.when(t >= 2)
    def _():
        wait_step(t - 2, slot)

    @pl.when(run & (first_ref[t] == 1))
    def _():
        cb = 256
        r = lax.broadcasted_iota(jnp.int32, (cb, cb), 0)
        c = lax.broadcasted_iota(jnp.int32, (cb, cb), 1)
        perm = jnp.where(r == jnp.where(c < cb // 2, 2 * c, 2 * (c - cb // 2) + 1), 1.0, 0.0).astype(BF16)
        for blk in range(2 * D_FF // cb):
            wp = _dg(wu_ref[0, :, blk * cb:(blk + 1) * cb].astype(BF16), perm, NN).astype(BF16)
            wg_s[:, blk * (cb // 2):(blk + 1) * (cb // 2)] = wp[:, :cb // 2]
            wl_s[:, blk * (cb // 2):(blk + 1) * (cb // 2)] = wp[:, cb // 2:]
        wd_s[...] = wd_ref[0].astype(BF16)

    @pl.when(run)
    def _():
        xb = jnp.concatenate([x_refs[p][...] for p in range(npc)], axis=0)
        glu = jnp.minimum(_dg(xb, wg_s[...], NN) + bg_ref[0], SWIGLU_LIMIT)
        lin = jnp.clip(_dg(xb, wl_s[...], NN) + bl_ref[0], -SWIGLU_LIMIT, SWIGLU_LIMIT)
        act = glu * _sigmoid(SWIGLU_ALPHA * glu) * (lin + 1.0)
        ybuf[slot] = (_dg(act.astype(BF16), wd_s[...], NN) + bd_ref[0]).astype(BF16)

    @pl.when(jnp.logical_not(run))
    def _():
        ybuf[slot] = jnp.zeros((npc * PIECE, D_MODEL), BF16)

    for p in range(npc):
        dst, cp = piece_copy(t, slot, p)

        @pl.when(dst < N_PIECES)
        def _():
            cp.start()

    @pl.when(t == nsteps - 1)
    def _():
        wait_step(t - 1, 1 - slot)
        wait_step(t, slot)


def _moe_grouped(slots, tables, w_up, w_down, bg, bl, bd):
    npc = MOE_TILE_PIECES
    piece_in = [pl.BlockSpec((PIECE, D_MODEL), functools.partial(
        lambda t, tin, tout, texp, kind, first, p: (tin[t * npc + p], 0), p=p)) for p in range(npc)]
    by_expert = lambda shape: pl.BlockSpec(
        shape, lambda t, tin, tout, texp, kind, first: (texp[t],) + (0,) * (len(shape) - 1))
    return pl.pallas_call(
        _moe_grouped_kernel,
        grid_spec=pltpu.PrefetchScalarGridSpec(
            num_scalar_prefetch=5,
            grid=(MOE_MAX_STEPS,),
            in_specs=piece_in + [by_expert((1, D_MODEL, 2 * D_FF)), by_expert((1, D_FF, D_MODEL)),
                                 by_expert((1, 1, D_FF)), by_expert((1, 1, D_FF)),
                                 by_expert((1, 1, D_MODEL))],
            out_specs=pl.BlockSpec(memory_space=pl.ANY),
            scratch_shapes=[pltpu.VMEM((D_MODEL, D_FF), BF16), pltpu.VMEM((D_MODEL, D_FF), BF16),
                            pltpu.VMEM((D_FF, D_MODEL), BF16),
                            pltpu.VMEM((2, npc * PIECE, D_MODEL), BF16),
                            pltpu.SemaphoreType.DMA((2, npc))]),
        out_shape=jax.ShapeDtypeStruct((N_PIECES * PIECE, D_MODEL), BF16),
        compiler_params=pltpu.CompilerParams(vmem_limit_bytes=V7X_VMEM_LIMIT),
        name="moe_grouped",
    )(*tables, *([slots] * npc), w_up, w_down, bg, bl, bd)


def _combine_kernel(y_ref, lp_ref, w_ref, x1_ref, gt_ref, gf_ref, o_ref):
    tm = x1_ref.shape[0]
    rows = y_ref.shape[0]
    r = lax.broadcasted_iota(jnp.int32, (tm, tm), 0)
    c = lax.broadcasted_iota(jnp.int32, (tm, tm), 1)
    diag = r == c

    def column(x_row):
        return jnp.sum(jnp.where(diag, x_row, 0.0), axis=1, keepdims=True)

    si = lax.broadcasted_iota(jnp.int32, (tm, rows), 1)
    wt = jnp.zeros((tm, rows), F32)
    for k in range(TOP_K):
        pos = column(lp_ref[k:k + 1, :].astype(F32)).astype(jnp.int32)
        wt = jnp.where(si == pos, column(w_ref[k:k + 1, :]), wt)
    hi = wt.astype(BF16)
    lo = (wt - hi.astype(F32)).astype(BF16)
    y = y_ref[...]
    moe = _dg(hi, y, NN) + _dg(lo, y, NN)
    gt = gt_ref[...].reshape(-1, D_MODEL)
    o_ref[...] = _rms(x1_ref[...] + gt * moe, gf_ref[...])


def _combine(yslots, lp, w, x1, mod3, mod2, g_final, tm, slot_block):
    t = x1.shape[0]
    nt = t // tm
    rows = SLOT_ROWS[tm]
    if mod3 is not None:
        per_b = (t // mod3.shape[0]) // tm
        gspec = pl.BlockSpec((1, 1, D_MODEL), lambda i: (i // per_b, 0, 5))
        mod = mod3
    else:
        gspec = pl.BlockSpec((tm, D_MODEL), lambda i: (i, 5))
        mod = mod2
    return pl.pallas_call(
        _combine_kernel,
        grid=(nt,),
        in_specs=[pl.BlockSpec((rows, D_MODEL), lambda i: (slot_block + i, 0)),
                  pl.BlockSpec((TOP_K, tm), lambda i: (0, i)),
                  pl.BlockSpec((TOP_K, tm), lambda i: (0, i)),
                  pl.BlockSpec((tm, D_MODEL), lambda i: (i, 0)),
                  gspec,
                  pl.BlockSpec((1, D_MODEL), lambda i: (0, 0))],
        out_specs=pl.BlockSpec((tm, D_MODEL), lambda i: (i, 0)),
        out_shape=jax.ShapeDtypeStruct((t, D_MODEL), F32),
        compiler_params=pltpu.CompilerParams(vmem_limit_bytes=V7X_VMEM_LIMIT),
        name="moe_combine",
    )(yslots, lp, w, x1, mod, g_final.reshape(1, -1))


def kernel(x_prompt, x_sample, c_prompt, c_sample, cache_k_win, cache_v_win, state_hgrn, w_ada, b_ada,
           g_mix, g_ffn, w_in, attn_sinks, g_attn_out, hg_lb_logits, g_hg_out, w_out, w_router, b_router,
           w_up, b_up, w_down, b_down, g_final):
    batch, seq, d = x_prompt.shape
    nsamp = x_sample.shape[0]
    win = cache_k_win.shape[2]
    layer = 0

    mod = _modulation(jnp.concatenate([c_prompt, c_sample], axis=0), w_ada[layer], b_ada[layer])
    mod_p = mod[:batch].reshape(batch, 1, 6 * d)
    mod_s = mod[batch:]

    assert (batch * seq, nsamp) == (N_PROMPT_TILES * TOK_TILE, SAMPLE_TILE)
    w_in_b = w_in[layer].astype(BF16)
    w_out_b = w_out[layer].astype(BF16)
    w_router_t = w_router[layer].T
    bg = b_up[layer][:, None, 0::2]
    bl = b_up[layer][:, None, 1::2]
    bd = b_down[layer][:, None, :]

    xp = x_prompt.reshape(batch * seq, d)
    cos_p, sin_p = _rope_tables(np.arange(seq))
    qa, ka, va, qh, fh, ih, gh = _in_projection(xp, mod_p, None, g_mix[layer], w_in_b, cos_p, sin_p, TOK_TILE)
    oa = _attention_prompt(qa, ka, va, attn_sinks[layer], g_attn_out[layer], batch)
    oh, s_prompt = _hgrn_prompt(qh, fh, ih, gh, hg_lb_logits, g_hg_out[layer], batch)
    x1_p, slots, lp_p, cw_p, pc_p = _out_projection(
        xp, oa, oh, mod_p, None, g_ffn[layer], w_out_b, w_router_t, b_router[layer], TOK_TILE, None, 0)
    k_win_p = ka.reshape(batch, seq, ATTN_KV_HEADS, HEAD_DIM)[:, seq - win:]
    v_win_p = va.reshape(batch, seq, ATTN_KV_HEADS, HEAD_DIM)[:, seq - win:]

    xs = x_sample.reshape(nsamp, d)
    cos_s, sin_s = _rope_tables(np.full((nsamp,), PAST_LEN))
    qa, ka, va, qh, fh, ih, gh = _in_projection(xs, None, mod_s, g_mix[layer], w_in_b, cos_s, sin_s, nsamp)
    oa, k_win_s, v_win_s = _attention_sample(
        qa, ka, va, cache_k_win[layer].reshape(nsamp, win, KV_WIDTH),
        cache_v_win[layer].reshape(nsamp, win, KV_WIDTH), attn_sinks[layer], g_attn_out[layer])
    oh, s_sample = _hgrn_sample(qh, fh, ih, gh, hg_lb_logits, g_hg_out[layer], state_hgrn[layer])
    sample_block = TILE_PIECE_BASE[-1] * PIECE // SLOT_ROWS[SAMPLE_TILE]
    x1_s, slots, lp_s, cw_s, pc_s = _out_projection(
        xs, oa, oh, None, mod_s, g_ffn[layer], w_out_b, w_router_t, b_router[layer], SAMPLE_TILE,
        slots, sample_block)

    tables = _piece_tables(jnp.concatenate([pc_p[:, :, 0], pc_s[:, :, 0]], axis=0))
    yslots = _moe_grouped(slots, tables, w_up[layer], w_down[layer], bg, bl, bd)
    y_prompt = _combine(yslots, lp_p, cw_p, x1_p, mod_p, None, g_final, TOK_TILE, 0)
    y_sample = _combine(yslots, lp_s, cw_s, x1_s, None, mod_s, g_final, SAMPLE_TILE, sample_block)

    kv_shape = (1, nsamp, win, ATTN_KV_HEADS, HEAD_DIM)
    return (y_prompt.reshape(batch, seq, d), y_sample.reshape(nsamp, 1, d),
            k_win_p[None], v_win_p[None], s_prompt[None],
            k_win_s.reshape(kv_shape), v_win_s.reshape(kv_shape), s_sample[None])
```

```python
import functools

import numpy as np
import jax
import jax.numpy as jnp
from jax import lax
from jax.experimental import pallas as pl
from jax.experimental.pallas import tpu as pltpu

F32 = jnp.float32
BF16 = jnp.bfloat16

D_MODEL = 1024
SEQ = 2048
PAST_LEN = 16384
ATTN_HEADS = 8
ATTN_KV_HEADS = 2
HEAD_DIM = 64
ATTN_GROUP = ATTN_HEADS // ATTN_KV_HEADS
ATTN_WIDTH = ATTN_HEADS * HEAD_DIM
KV_WIDTH = ATTN_KV_HEADS * HEAD_DIM
WINDOW = 128
ROT_DIM = HEAD_DIM // 4
ROPE_THETA = 500000.0
HG_HEADS = 4
HG_DK = 128
HG_DV = 128
HG_WIDTH = HG_HEADS * HG_DV
HG_CHUNK = 64
HG_SUB = 8
HG_STEP_CHUNKS = 2
IN_COLS = ATTN_WIDTH + 2 * KV_WIDTH + 4 * HG_WIDTH
N_EXPERTS = 32
TOP_K = 4
D_FF = D_MODEL
SWIGLU_ALPHA = 1.702
SWIGLU_LIMIT = 7.0
EPS = 1e-5

V7X_VMEM_LIMIT = 56 * 1024 * 1024

PIECE = 16
TOK_TILE = 512
SAMPLE_TILE = 128
N_PROMPT_TILES = 32
SLOT_ROWS = {TOK_TILE: 2560, SAMPLE_TILE: 1024}
TILE_PIECE_CAP = [SLOT_ROWS[TOK_TILE] // PIECE] * N_PROMPT_TILES + [SLOT_ROWS[SAMPLE_TILE] // PIECE]
TILE_PIECE_BASE = [i * TILE_PIECE_CAP[0] for i in range(N_PROMPT_TILES + 1)]
N_PIECES = sum(TILE_PIECE_CAP)
STEP_PIECES = 32
HALF_PIECES = 16
MOE_MAX_STEPS = N_PIECES // STEP_PIECES + N_EXPERTS + 1

NN = (((1,), (0,)), ((), ()))
NT = (((1,), (1,)), ((), ()))
TN = (((0,), (0,)), ((), ()))


def _dg(a, b, dims):
    return lax.dot_general(a, b, dims, preferred_element_type=F32)


def _split3(x):
    h = x.astype(BF16)
    r = x - h.astype(F32)
    m = r.astype(BF16)
    l = (r - m.astype(F32)).astype(BF16)
    return h, m, l


def _dot_f32(a, b, dims):
    ah, am, al = _split3(a)
    bh, bm, bl = _split3(b)
    return (_dg(ah, bh, dims) + (_dg(ah, bm, dims) + _dg(am, bh, dims))
            + (_dg(am, bm, dims) + _dg(ah, bl, dims) + _dg(al, bh, dims)))


def _dot_exact_lhs(a_bf16, b, dims):
    bh, bm, bl = _split3(b)
    return _dg(a_bf16, bh, dims) + _dg(a_bf16, bm, dims) + _dg(a_bf16, bl, dims)


def _sigmoid(x):
    return 1.0 / (1.0 + jnp.exp(-x))


def _rms(x, g):
    return x * lax.rsqrt(jnp.mean(x * x, axis=-1, keepdims=True) + EPS) * g


def _mod_kernel(c_ref, w_ref, b_ref, o_ref):
    c = c_ref[...]
    o_ref[...] = _dot_f32(c * _sigmoid(c), w_ref[...], NN) + b_ref[...]


def _modulation(c_all, w_ada, b_ada):
    n = c_all.shape[0]
    return pl.pallas_call(
        _mod_kernel,
        grid=(6,),
        in_specs=[pl.BlockSpec((n, D_MODEL), lambda j: (0, 0)),
                  pl.BlockSpec((D_MODEL, D_MODEL), lambda j: (0, j)),
                  pl.BlockSpec((1, D_MODEL), lambda j: (0, j))],
        out_specs=pl.BlockSpec((n, D_MODEL), lambda j: (0, j)),
        out_shape=jax.ShapeDtypeStruct((n, 6 * D_MODEL), F32),
        compiler_params=pltpu.CompilerParams(vmem_limit_bytes=V7X_VMEM_LIMIT),
        name="adaln_mod",
    )(c_all, w_ada, b_ada.reshape(1, -1))


def _rotate(x, cos_t, sin_t):
    width = x.shape[-1]
    d = lax.broadcasted_iota(jnp.int32, x.shape, 1) % HEAD_DIM
    half = ROT_DIM // 2
    partner = jnp.where(d < half, pltpu.roll(x, width - half, axis=1), pltpu.roll(x, half, axis=1))
    return x * cos_t + partner * sin_t


def _inproj_kernel(x_ref, sh_ref, sc_ref, g_ref, w_ref, cos_ref, sin_ref,
                   qa_ref, ka_ref, va_ref, qh_ref, fh_ref, ih_ref, gh_ref):
    x = x_ref[...]
    sh = sh_ref[...].reshape(-1, D_MODEL)
    sc = sc_ref[...].reshape(-1, D_MODEL)
    h = _rms(x, g_ref[...]) * (1.0 + sc) + sh
    z = _dg(h.astype(BF16), w_ref[...], NN)
    cos_k = cos_ref[...]
    sin_k = sin_ref[...]
    cos_q = jnp.concatenate([cos_k] * ATTN_GROUP, axis=1)
    sin_q = jnp.concatenate([sin_k] * ATTN_GROUP, axis=1)
    o = 0
    qa = _rotate(z[:, o:o + ATTN_WIDTH], cos_q, sin_q)
    qa_ref[...] = (qa * (HEAD_DIM ** -0.5)).astype(BF16)
    o += ATTN_WIDTH
    ka_ref[...] = _rotate(z[:, o:o + KV_WIDTH], cos_k, sin_k)
    o += KV_WIDTH
    va_ref[...] = z[:, o:o + KV_WIDTH]
    o += KV_WIDTH
    qh_ref[...] = z[:, o:o + HG_WIDTH]
    o += HG_WIDTH
    fh_ref[...] = z[:, o:o + HG_WIDTH]
    o += HG_WIDTH
    ih_ref[...] = z[:, o:o + HG_WIDTH].astype(BF16)
    o += HG_WIDTH
    gh_ref[...] = z[:, o:o + HG_WIDTH]


def _rope_tables(positions):
    half = ROT_DIM // 2
    inv = ROPE_THETA ** (-(np.arange(half, dtype=np.float64) * 2.0 / ROT_DIM))
    ang = np.asarray(positions, np.float64)[:, None] * inv[None, :]
    cos_h = np.ones((len(positions), HEAD_DIM))
    sin_h = np.zeros((len(positions), HEAD_DIM))
    cos_h[:, :half] = np.cos(ang)
    cos_h[:, half:ROT_DIM] = np.cos(ang)
    sin_h[:, :half] = -np.sin(ang)
    sin_h[:, half:ROT_DIM] = np.sin(ang)
    cos_t = np.tile(cos_h, (1, ATTN_KV_HEADS)).astype(np.float32)
    sin_t = np.tile(sin_h, (1, ATTN_KV_HEADS)).astype(np.float32)
    return jnp.asarray(cos_t), jnp.asarray(sin_t)


def _in_projection(x, mod3, mod2, g_mix, w_in_bf16, cos_t, sin_t, tm):
    t = x.shape[0]
    nt = t // tm
    if mod3 is not None:
        per_b = (t // mod3.shape[0]) // tm
        sh_spec = pl.BlockSpec((1, 1, D_MODEL), lambda i: (i // per_b, 0, 0))
        sc_spec = pl.BlockSpec((1, 1, D_MODEL), lambda i: (i // per_b, 0, 1))
        mod = mod3
        ncs = cos_t.shape[0] // tm
        cs_spec = pl.BlockSpec((tm, KV_WIDTH), lambda i: (i % ncs, 0))
    else:
        sh_spec = pl.BlockSpec((tm, D_MODEL), lambda i: (i, 0))
        sc_spec = pl.BlockSpec((tm, D_MODEL), lambda i: (i, 1))
        mod = mod2
        cs_spec = pl.BlockSpec((tm, KV_WIDTH), lambda i: (i, 0))
    row = lambda w: pl.BlockSpec((tm, w), lambda i: (i, 0))
    return pl.pallas_call(
        _inproj_kernel,
        grid=(nt,),
        in_specs=[row(D_MODEL), sh_spec, sc_spec,
                  pl.BlockSpec((1, D_MODEL), lambda i: (0, 0)),
                  pl.BlockSpec((D_MODEL, IN_COLS), lambda i: (0, 0)),
                  cs_spec, cs_spec],
        out_specs=[row(ATTN_WIDTH), row(KV_WIDTH), row(KV_WIDTH),
                   row(HG_WIDTH), row(HG_WIDTH), row(HG_WIDTH), row(HG_WIDTH)],
        out_shape=[jax.ShapeDtypeStruct((t, ATTN_WIDTH), BF16),
                   jax.ShapeDtypeStruct((t, KV_WIDTH), F32),
                   jax.ShapeDtypeStruct((t, KV_WIDTH), F32),
                   jax.ShapeDtypeStruct((t, HG_WIDTH), F32),
                   jax.ShapeDtypeStruct((t, HG_WIDTH), F32),
                   jax.ShapeDtypeStruct((t, HG_WIDTH), BF16),
                   jax.ShapeDtypeStruct((t, HG_WIDTH), F32)],
        compiler_params=pltpu.CompilerParams(vmem_limit_bytes=V7X_VMEM_LIMIT),
        name="in_proj",
    )(x, mod, mod, g_mix.reshape(1, -1), w_in_bf16, cos_t, sin_t)


def _attn_prompt_kernel(sink_ref, q_ref, kc_ref, kp_ref, vc_ref, vp_ref, g_ref, o_ref):
    n = pl.program_id(1)
    blk = WINDOW
    q = q_ref[...]
    qi = lax.broadcasted_iota(jnp.int32, (blk, 2 * blk), 0)
    kj = lax.broadcasted_iota(jnp.int32, (blk, 2 * blk), 1)
    ok = (kj >= qi) & (kj <= qi + blk) & ((kj >= blk) | (n > 0))
    k2 = jnp.concatenate([kp_ref[...], kc_ref[...]], axis=0).astype(BF16)
    v2 = jnp.concatenate([vp_ref[...], vc_ref[...]], axis=0).astype(BF16)
    outs = []
    for head in range(ATTN_HEADS):
        h = head // ATTN_GROUP
        ls = slice(h * HEAD_DIM, (h + 1) * HEAD_DIM)
        sink = sink_ref[head]
        s = jnp.where(ok, _dg(q[:, head * HEAD_DIM:(head + 1) * HEAD_DIM], k2[:, ls], NT), -jnp.inf)
        m = jnp.maximum(jnp.max(s, axis=-1, keepdims=True), sink)
        p = jnp.exp(s - m)
        den = jnp.sum(p, axis=-1, keepdims=True) + jnp.exp(sink - m)
        outs.append(_dg(p.astype(BF16), v2[:, ls], NN) / den)
    o_all = jnp.concatenate(outs, axis=1)
    o_ref[...] = _rms(o_all, g_ref[...]).astype(BF16)


def _attention_prompt(qa, ka, va, sinks, g_attn_out, batch):
    t = qa.shape[0]
    nb = t // batch // WINDOW
    cur = lambda w: pl.BlockSpec((WINDOW, w), lambda b, n: (b * nb + n, 0))
    prev = lambda w: pl.BlockSpec((WINDOW, w), lambda b, n: (b * nb + jnp.maximum(n - 1, 0), 0))
    return pl.pallas_call(
        _attn_prompt_kernel,
        grid=(batch, nb),
        in_specs=[pl.BlockSpec(memory_space=pltpu.SMEM),
                  cur(ATTN_WIDTH), cur(KV_WIDTH), prev(KV_WIDTH), cur(KV_WIDTH), prev(KV_WIDTH),
                  pl.BlockSpec((1, ATTN_WIDTH), lambda b, n: (0, 0))],
        out_specs=cur(ATTN_WIDTH),
        out_shape=jax.ShapeDtypeStruct((t, ATTN_WIDTH), BF16),
        name="attn_prompt",
    )(sinks, qa, ka, ka, va, va, g_attn_out.reshape(1, -1))


def _lower_bound(lb_logits_ref):
    lg = lb_logits_ref[...]
    e = jnp.exp(lg - jnp.max(lg, axis=0, keepdims=True))
    return e[0:1] / jnp.sum(e, axis=0, keepdims=True)


def _hgrn_prompt_kernel(q_ref, f_ref, i_ref, g_ref, lbl_ref, gn_ref, o_ref, sfin_ref, st_ref):
    step = pl.program_id(1)
    C = HG_CHUNK
    W = HG_WIDTH
    nsub = C // HG_SUB
    n_chunks = q_ref.shape[0] // C
    heads = [slice(hh * HG_DK, (hh + 1) * HG_DK) for hh in range(HG_HEADS)]

    @pl.when(step == 0)
    def _():
        st_ref[...] = jnp.zeros_like(st_ref)

    lb = _lower_bound(lbl_ref)
    r64 = lax.broadcasted_iota(jnp.int32, (C, C), 0)
    c64 = lax.broadcasted_iota(jnp.int32, (C, C), 1)
    tri = (r64 >= c64).astype(BF16)
    rsub = r64 % HG_SUB
    o_intra, q_state, upd, decay_last = {}, {}, {}, {}
    for ci in range(n_chunks):
        rows = slice(ci * C, (ci + 1) * C)
        f = lb + (1.0 - lb) * _sigmoid(f_ref[rows, :])
        kk = 1.0 - f
        logf = jnp.log(f)
        q = q_ref[rows, :]
        v = i_ref[rows, :]
        b = _dot_exact_lhs(tri, logf, NN)
        prods = [q * kk]
        w = logf
        for d in range(1, HG_SUB):
            if d > 1:
                w = w + pltpu.roll(logf, d - 1, axis=0)
            prods.append(q * pltpu.roll(kk, d, axis=0) * jnp.exp(w))
        ends = [b[j * HG_SUB + HG_SUB - 1:(j + 1) * HG_SUB] for j in range(nsub)]
        kt = kk * jnp.exp(jnp.concatenate([jnp.broadcast_to(e, (HG_SUB, W)) for e in ends], axis=0) - b)
        lhs, rhs = [], []
        for j in range(nsub - 1):
            lo = (j + 1) * HG_SUB
            lhs.append(jnp.concatenate(
                [jnp.zeros((lo, W), F32), q[lo:] * jnp.exp(b[lo:] - ends[j])], axis=0).astype(BF16))
            pieces = [kt[j * HG_SUB:lo]]
            if j > 0:
                pieces.insert(0, jnp.zeros((j * HG_SUB, W), F32))
            pieces.append(jnp.zeros((C - lo, W), F32))
            rhs.append(jnp.concatenate(pieces, axis=0).astype(BF16))
        q_state[ci] = (q * jnp.exp(b)).astype(BF16)
        kd = (kk * jnp.exp(ends[-1] - b)).astype(BF16)
        decay_last[ci] = jnp.exp(ends[-1])
        for hh, sl in enumerate(heads):
            a = jnp.zeros((C, C), F32)
            for d in range(HG_SUB):
                ad = jnp.sum(prods[d][:, sl], axis=-1, keepdims=True)
                a = a + jnp.where((c64 == r64 - d) & (rsub >= d), ad, 0.0)
            a = a + _dg(jnp.concatenate([x[:, sl] for x in lhs], axis=1),
                        jnp.concatenate([x[:, sl] for x in rhs], axis=1), NT)
            o_intra[ci, hh] = _dg(a.astype(BF16), v[:, sl], NN)
            upd[ci, hh] = _dg(v[:, sl], kd[:, sl], TN)
    finals = []
    for hh, sl in enumerate(heads):
        st = st_ref[hh]
        for ci in range(n_chunks):
            rows = slice(ci * C, (ci + 1) * C)
            o = o_intra[ci, hh] + _dg(q_state[ci][:, sl], st.astype(BF16), NT)
            st = st * decay_last[ci][:, sl] + upd[ci, hh]
            g = g_ref[rows, sl]
            o_ref[rows, sl] = (_rms(o, gn_ref[:, sl]) * (g * _sigmoid(g))).astype(BF16)
        st_ref[hh] = st
        finals.append(st)

    @pl.when(step == pl.num_programs(1) - 1)
    def _():
        for hh in range(HG_HEADS):
            sfin_ref[0, hh] = finals[hh].T


def _hgrn_prompt(qh, fh, ih, gh, lb_logits, g_hg_out, batch):
    t = qh.shape[0]
    nc = t // batch // (HG_CHUNK * HG_STEP_CHUNKS)
    blk = pl.BlockSpec((HG_CHUNK * HG_STEP_CHUNKS, HG_WIDTH), lambda b, c: (b * nc + c, 0))
    const = lambda r: pl.BlockSpec((r, HG_WIDTH), lambda b, c: (0, 0))
    return pl.pallas_call(
        _hgrn_prompt_kernel,
        grid=(batch, nc),
        in_specs=[blk, blk, blk, blk, const(lb_logits.shape[0]), const(1)],
        out_specs=[blk, pl.BlockSpec((1, HG_HEADS, HG_DK, HG_DV), lambda b, c: (b, 0, 0, 0))],
        out_shape=[jax.ShapeDtypeStruct((t, HG_WIDTH), BF16),
                   jax.ShapeDtypeStruct((batch, HG_HEADS, HG_DK, HG_DV), F32)],
        scratch_shapes=[pltpu.VMEM((HG_HEADS, HG_DV, HG_DK), F32)],
        name="hgrn_prompt",
    )(qh, fh, ih, gh, lb_logits, g_hg_out.reshape(1, -1))


def _attn_sample_kernel(sink_ref, q_ref, kn_ref, vn_ref, kc_ref, vc_ref, g_ref,
                        o_ref, ko_ref, vo_ref):
    bt = q_ref.shape[0]
    win = kc_ref.shape[1]
    kc = kc_ref[...]
    vc = vc_ref[...]
    kn = kn_ref[...]
    vn = vn_ref[...]
    outs = []
    for h in range(ATTN_KV_HEADS):
        ls = slice(h * HEAD_DIM, (h + 1) * HEAD_DIM)
        q = q_ref[:, h * ATTN_GROUP:(h + 1) * ATTN_GROUP, :]
        s = jnp.einsum('bgd,bjd->bgj', q, kc[:, :, ls].astype(BF16), preferred_element_type=F32)
        s_new = jnp.sum(q.astype(F32) * kn[:, None, ls], axis=-1, keepdims=True)
        gi = lax.broadcasted_iota(jnp.int32, (1, ATTN_GROUP, 1), 1)
        sink = jnp.zeros((1, ATTN_GROUP, 1), F32)
        for g in range(ATTN_GROUP):
            sink = jnp.where(gi == g, sink_ref[h * ATTN_GROUP + g], sink)
        m = jnp.maximum(jnp.maximum(jnp.max(s, axis=-1, keepdims=True), s_new), sink)
        p = jnp.exp(s - m)
        p_new = jnp.exp(s_new - m)
        den = jnp.sum(p, axis=-1, keepdims=True) + p_new + jnp.exp(sink - m)
        o = jnp.einsum('bgj,bjd->bgd', p.astype(BF16), vc[:, :, ls].astype(BF16),
                       preferred_element_type=F32)
        o = (o + p_new * vn[:, None, ls]) / den
        outs.append(o)
    ssq = sum(jnp.sum(jnp.sum(o * o, axis=-1, keepdims=True), axis=1, keepdims=True) for o in outs)
    scale = lax.rsqrt(ssq / ATTN_WIDTH + EPS)
    for h in range(ATTN_KV_HEADS):
        o_ref[h] = (outs[h] * scale * g_ref[h][None]).astype(BF16)
    ri = lax.broadcasted_iota(jnp.int32, (win, KV_WIDTH), 0)
    for b in range(bt):
        ko_ref[b] = jnp.where(ri == win - 1, kn[b:b + 1], pltpu.roll(kc[b], win - 1, axis=0))
        vo_ref[b] = jnp.where(ri == win - 1, vn[b:b + 1], pltpu.roll(vc[b], win - 1, axis=0))


def _attention_sample(qa, ka, va, cache_k, cache_v, sinks, g_attn_out, bt=8):
    nb = qa.shape[0]
    win = cache_k.shape[1]
    q3 = qa.reshape(nb, ATTN_HEADS, HEAD_DIM)
    g3 = g_attn_out.reshape(ATTN_KV_HEADS, ATTN_GROUP, HEAD_DIM)
    row = lambda w: pl.BlockSpec((bt, w), lambda i: (i, 0))
    cache = pl.BlockSpec((bt, win, KV_WIDTH), lambda i: (i, 0, 0))
    o4, k_new, v_new = pl.pallas_call(
        _attn_sample_kernel,
        grid=(nb // bt,),
        in_specs=[pl.BlockSpec(memory_space=pltpu.SMEM),
                  pl.BlockSpec((bt, ATTN_HEADS, HEAD_DIM), lambda i: (i, 0, 0)),
                  row(KV_WIDTH), row(KV_WIDTH), cache, cache,
                  pl.BlockSpec((ATTN_KV_HEADS, ATTN_GROUP, HEAD_DIM), lambda i: (0, 0, 0))],
        out_specs=[pl.BlockSpec((ATTN_KV_HEADS, bt, ATTN_GROUP, HEAD_DIM), lambda i: (0, i, 0, 0)),
                   cache, cache],
        out_shape=[jax.ShapeDtypeStruct((ATTN_KV_HEADS, nb, ATTN_GROUP, HEAD_DIM), BF16),
                   jax.ShapeDtypeStruct(cache_k.shape, F32),
                   jax.ShapeDtypeStruct(cache_v.shape, F32)],
        name="attn_sample",
    )(sinks, q3, ka, va, cache_k, cache_v, g3)
    oa = jnp.transpose(o4, (1, 0, 2, 3)).reshape(nb, ATTN_WIDTH)
    return oa, k_new, v_new


def _hgrn_sample_kernel(q_ref, f_ref, i_ref, g_ref, lbl_ref, gn_ref, s0_ref, o_ref, s_ref):
    bt = q_ref.shape[0]
    lb = _lower_bound(lbl_ref)
    f = lb + (1.0 - lb) * _sigmoid(f_ref[...])
    kk = 1.0 - f
    q = q_ref[...]
    v = i_ref[...].astype(F32)
    g = g_ref[...]
    gate = g * _sigmoid(g)
    r = lax.broadcasted_iota(jnp.int32, (HG_DK, HG_DK), 0)
    cc = lax.broadcasted_iota(jnp.int32, (HG_DK, HG_DK), 1)
    diag = r == cc

    def column(x_row):
        return jnp.sum(jnp.where(diag, x_row, 0.0), axis=1, keepdims=True)

    for b in range(bt):
        parts = []
        for hh in range(HG_HEADS):
            sl = slice(hh * HG_DK, (hh + 1) * HG_DK)
            s_new = column(f[b:b + 1, sl]) * s0_ref[b, hh] + column(kk[b:b + 1, sl]) * v[b:b + 1, sl]
            s_ref[b, hh] = s_new
            o = jnp.sum(s_new * column(q[b:b + 1, sl]), axis=0, keepdims=True)
            parts.append(_rms(o, gn_ref[:, sl]))
        o_ref[b:b + 1, :] = (jnp.concatenate(parts, axis=1) * gate[b:b + 1]).astype(BF16)


def _hgrn_sample(qh, fh, ih, gh, lb_logits, g_hg_out, state, bt=8):
    nb = qh.shape[0]
    row = pl.BlockSpec((bt, HG_WIDTH), lambda i: (i, 0))
    const = lambda r: pl.BlockSpec((r, HG_WIDTH), lambda i: (0, 0))
    st = pl.BlockSpec((bt, HG_HEADS, HG_DK, HG_DV), lambda i: (i, 0, 0, 0))
    return pl.pallas_call(
        _hgrn_sample_kernel,
        grid=(nb // bt,),
        in_specs=[row, row, row, row, const(lb_logits.shape[0]), const(1), st],
        out_specs=[row, st],
        out_shape=[jax.ShapeDtypeStruct((nb, HG_WIDTH), BF16),
                   jax.ShapeDtypeStruct(state.shape, F32)],
        name="hgrn_sample",
    )(qh, fh, ih, gh, lb_logits, g_hg_out.reshape(1, -1), state)


def _outproj_kernel(*refs, n_tiles):
    xs_ref = refs[-4]

    @pl.when(pl.program_id(0) < n_tiles)
    def _():
        _outproj_tile(*refs)

    @pl.when(pl.program_id(0) >= n_tiles)
    def _():
        xs_ref[...] = jnp.zeros_like(xs_ref)


def _outproj_tile(x_ref, oa_ref, oh_ref, gt_ref, sh_ref, sc_ref, g_ref, wo_ref, wr_ref, br_ref, *rest):
    x1_ref, xs_ref, lp_ref, w_ref, pc_ref = rest[-5:]
    tm = x_ref.shape[0]
    rows = xs_ref.shape[0]
    gt = gt_ref[...].reshape(-1, D_MODEL)
    sh = sh_ref[...].reshape(-1, D_MODEL)
    sc = sc_ref[...].reshape(-1, D_MODEL)
    mix = _dg(oa_ref[...], wo_ref[0:ATTN_WIDTH, :], NN) + _dg(oh_ref[...], wo_ref[ATTN_WIDTH:, :], NN)
    x1 = x_ref[...] + gt * mix
    x1_ref[...] = x1
    h2 = _rms(x1, g_ref[...]) * (1.0 + sc) + sh
    logits = _dot_f32(wr_ref[...], h2, NT) + br_ref[...]
    ei = lax.broadcasted_iota(jnp.int32, logits.shape, 0)
    vals, sel = [], []
    l = logits
    for _ in range(TOP_K):
        m = jnp.max(l, axis=0, keepdims=True)
        idx = jnp.min(jnp.where(l == m, ei, N_EXPERTS), axis=0, keepdims=True)
        pick = ei == idx
        vals.append(m)
        sel.append(pick)
        l = jnp.where(pick, -jnp.inf, l)
    ex = [jnp.exp(v - vals[0]) for v in vals]
    den = ex[0] + ex[1] + ex[2] + ex[3]
    for k in range(TOP_K):
        w_ref[k:k + 1, :] = ex[k] / den

    member = jnp.where(sel[0] | sel[1] | sel[2] | sel[3], 1.0, 0.0)
    tr = lax.broadcasted_iota(jnp.int32, (tm, tm), 0)
    tc = lax.broadcasted_iota(jnp.int32, (tm, tm), 1)
    rank = _dg(member.astype(BF16), (tr < tc).astype(BF16), NN)
    pieces = jnp.floor((jnp.sum(member, axis=1, keepdims=True) + (PIECE - 1)) * (1.0 / PIECE))
    er = lax.broadcasted_iota(jnp.int32, (N_EXPERTS, N_EXPERTS), 0)
    ec = lax.broadcasted_iota(jnp.int32, (N_EXPERTS, N_EXPERTS), 1)
    pieces_b = jnp.broadcast_to(pieces, (N_EXPERTS, 128))
    pc_ref[0] = pieces_b.astype(jnp.int32)
    base = _dg((ec < er).astype(BF16), pieces_b.astype(BF16), NN)[:, 0:1] * PIECE
    slot = base + rank
    lps = [jnp.sum(jnp.where(sel[k], slot, 0.0), axis=0, keepdims=True).astype(jnp.int32)
           for k in range(TOP_K)]
    for k in range(TOP_K):
        lp_ref[k:k + 1, :] = lps[k]
    hb = h2.astype(BF16)
    chunk = 512
    for r0 in range(0, rows, chunk):
        si = lax.broadcasted_iota(jnp.int32, (chunk, tm), 0) + r0
        hit = (si == lps[0]) | (si == lps[1]) | (si == lps[2]) | (si == lps[3])
        xs_ref[r0:r0 + chunk, :] = _dg(jnp.where(hit, 1.0, 0.0).astype(BF16), hb, NN).astype(BF16)


def _out_projection(x, oa, oh, mod3, mod2, g_ffn, w_out_bf16, w_router_t, b_router, tm, slots, slot_block):
    t = x.shape[0]
    nt = t // tm
    rows = SLOT_ROWS[tm]
    real = lambda i: jnp.minimum(i, nt - 1)
    if mod3 is not None:
        per_b = (t // mod3.shape[0]) // tm
        mspec = lambda j: pl.BlockSpec((1, 1, D_MODEL), lambda i: (real(i) // per_b, 0, j))
        mod = mod3
    else:
        mspec = lambda j: pl.BlockSpec((tm, D_MODEL), lambda i: (real(i), j))
        mod = mod2
    row = lambda w: pl.BlockSpec((tm, w), lambda i: (real(i), 0))
    full = lambda a: pl.BlockSpec(a.shape, lambda i: (0,) * a.ndim)
    g2 = g_ffn.reshape(1, -1)
    br = b_router.reshape(-1, 1)
    args = [x, oa, oh, mod, mod, mod, g2, w_out_bf16, w_router_t, br]
    in_specs = [row(D_MODEL), row(ATTN_WIDTH), row(HG_WIDTH), mspec(2), mspec(3), mspec(4),
                full(g2), full(w_out_bf16), full(w_router_t), full(br)]
    aliases = {}
    n_fill = 0
    if slots is not None:
        args.append(slots)
        in_specs.append(pl.BlockSpec(memory_space=pl.ANY))
        aliases = {len(args) - 1: 1}
    else:
        n_fill = pl.cdiv(N_PIECES * PIECE - nt * rows, rows)
    return pl.pallas_call(
        functools.partial(_outproj_kernel, n_tiles=nt),
        grid=(nt + n_fill,),
        in_specs=in_specs,
        out_specs=[row(D_MODEL),
                   pl.BlockSpec((rows, D_MODEL), lambda i: (slot_block + i, 0)),
                   pl.BlockSpec((TOP_K, tm), lambda i: (0, real(i))),
                   pl.BlockSpec((TOP_K, tm), lambda i: (0, real(i))),
                   pl.BlockSpec((1, N_EXPERTS, 128), lambda i: (real(i), 0, 0))],
        out_shape=[jax.ShapeDtypeStruct((t, D_MODEL), F32),
                   jax.ShapeDtypeStruct((N_PIECES * PIECE, D_MODEL), BF16),
                   jax.ShapeDtypeStruct((TOP_K, t), jnp.int32),
                   jax.ShapeDtypeStruct((TOP_K, t), F32),
                   jax.ShapeDtypeStruct((nt, N_EXPERTS, 128), jnp.int32)],
        input_output_aliases=aliases,
        compiler_params=pltpu.CompilerParams(vmem_limit_bytes=V7X_VMEM_LIMIT),
        name="out_proj_router",
    )(*args)


def _piece_tables(pieces_ie):
    cap = jnp.asarray(TILE_PIECE_CAP, jnp.int32)
    gbase = jnp.asarray(TILE_PIECE_BASE, jnp.int32)
    seg_src = gbase[:, None] + jnp.cumsum(pieces_ie, axis=1) - pieces_ie
    used_i = jnp.sum(pieces_ie, axis=1)
    tail_i = cap - used_i
    cum_i = jnp.cumsum(pieces_ie, axis=0) - pieces_ie
    np_e = jnp.sum(pieces_ie, axis=0)
    rem = np_e % STEP_PIECES
    nt_e = np_e // STEP_PIECES + (rem > 0)
    tile_end = jnp.cumsum(nt_e)
    tile_start = tile_end - nt_e
    n_comp = tile_end[-1]
    dest = jnp.concatenate([(STEP_PIECES * tile_start[None, :] + cum_i).T.reshape(-1),
                            STEP_PIECES * n_comp + jnp.cumsum(tail_i) - tail_i])
    src = jnp.concatenate([seg_src.T.reshape(-1), gbase + used_i])
    length = jnp.concatenate([pieces_ie.T.reshape(-1), tail_i])
    delta = src - dest
    end = dest + length
    zero = jnp.zeros((1,), jnp.int32)
    d_delta = delta - jnp.concatenate([zero, delta[:-1]])
    d_end = end - jnp.concatenate([zero, end[:-1]])
    j = jnp.arange(MOE_MAX_STEPS * STEP_PIECES, dtype=jnp.int32)
    reached = dest[None, :] <= j[:, None]
    piece = j + jnp.sum(jnp.where(reached, d_delta[None, :], 0), axis=1)
    valid = j < jnp.sum(jnp.where(reached, d_end[None, :], 0), axis=1)
    tbl_out = jnp.where(valid, piece, N_PIECES).astype(jnp.int32)
    tbl_in = jnp.where(valid & (j < STEP_PIECES * n_comp), piece, 0).astype(jnp.int32)
    tt = jnp.arange(MOE_MAX_STEPS, dtype=jnp.int32)
    e_t = jnp.minimum(jnp.sum(tt[:, None] >= tile_end[None, :], axis=1), N_EXPERTS - 1).astype(jnp.int32)
    is_comp = tt < n_comp
    e_last = jnp.max(jnp.where(nt_e > 0, jnp.arange(N_EXPERTS, dtype=jnp.int32), 0))
    texp = jnp.where(is_comp, e_t, e_last).astype(jnp.int32)
    first = (is_comp & (tt == tile_start[e_t])).astype(jnp.int32)
    short = (tt == tile_end[e_t] - 1) & (rem[e_t] > 0) & (rem[e_t] <= HALF_PIECES)
    kind = jnp.where(is_comp, jnp.where(short, 1, 2), 0).astype(jnp.int32)
    return tbl_in, tbl_out, texp, kind, first


def _moe_grouped_kernel(tin_ref, tout_ref, texp_ref, kind_ref, first_ref, *refs):
    npc = STEP_PIECES
    x_refs = refs[:npc]
    wu_ref, wd_ref, bg_ref, bl_ref, bd_ref, y_hbm, wg_s, wl_s, wd_s, ybuf, sems = refs[npc:]
    t = pl.program_id(0)
    nsteps = pl.num_programs(0)
    kind = kind_ref[t]
    slot = t % 2

    def piece_copy(step, buf, p):
        dst = tout_ref[step * npc + p]
        return dst, pltpu.make_async_copy(
            ybuf.at[buf, pl.ds(p * PIECE, PIECE), :],
            y_hbm.at[pl.ds(pl.multiple_of(dst * PIECE, PIECE), PIECE), :],
            sems.at[buf, p])

    def wait_step(step, buf):
        for p in range(npc):
            dst, cp = piece_copy(step, buf, p)

            @pl.when(dst < N_PIECES)
            def _():
                cp.wait()

    @pl.when(t >= 2)
    def _():
        wait_step(t - 2, slot)

    @pl.when((kind > 0) & (first_ref[t] == 1))
    def _():
        cb = 256
        r = lax.broadcasted_iota(jnp.int32, (cb, cb), 0)
        c = lax.broadcasted_iota(jnp.int32, (cb, cb), 1)
        perm = jnp.where(r == jnp.where(c < cb // 2, 2 * c, 2 * (c - cb // 2) + 1), 1.0, 0.0).astype(BF16)
        for blk in range(2 * D_FF // cb):
            wp = _dg(wu_ref[0, :, blk * cb:(blk + 1) * cb].astype(BF16), perm, NN).astype(BF16)
            wg_s[:, blk * (cb // 2):(blk + 1) * (cb // 2)] = wp[:, :cb // 2]
            wl_s[:, blk * (cb // 2):(blk + 1) * (cb // 2)] = wp[:, cb // 2:]
        wd_s[...] = wd_ref[0].astype(BF16)

    def expert(n_pieces):
        xb = jnp.concatenate([x_refs[p][...] for p in range(n_pieces)], axis=0)
        glu = jnp.minimum(_dg(xb, wg_s[...], NN) + bg_ref[0], SWIGLU_LIMIT)
        lin = jnp.clip(_dg(xb, wl_s[...], NN) + bl_ref[0], -SWIGLU_LIMIT, SWIGLU_LIMIT)
        act = glu * _sigmoid(SWIGLU_ALPHA * glu) * (lin + 1.0)
        y = (_dg(act.astype(BF16), wd_s[...], NN) + bd_ref[0]).astype(BF16)
        ybuf[slot, 0:n_pieces * PIECE, :] = y

    @pl.when(kind == 2)
    def _():
        expert(STEP_PIECES)

    @pl.when(kind == 1)
    def _():
        expert(HALF_PIECES)

    @pl.when(kind == 0)
    def _():
        ybuf[slot] = jnp.zeros((npc * PIECE, D_MODEL), BF16)

    for p in range(npc):
        dst, cp = piece_copy(t, slot, p)

        @pl.when(dst < N_PIECES)
        def _():
            cp.start()

    @pl.when(t == nsteps - 1)
    def _():
        wait_step(t - 1, 1 - slot)
        wait_step(t, slot)


def _moe_grouped(slots, tables, w_up, w_down, bg, bl, bd):
    npc = STEP_PIECES
    piece_in = [pl.BlockSpec((PIECE, D_MODEL), functools.partial(
        lambda t, tin, tout, texp, kind, first, p: (tin[t * npc + p], 0), p=p)) for p in range(npc)]
    by_expert = lambda shape: pl.BlockSpec(
        shape, lambda t, tin, tout, texp, kind, first: (texp[t],) + (0,) * (len(shape) - 1))
    return pl.pallas_call(
        _moe_grouped_kernel,
        grid_spec=pltpu.PrefetchScalarGridSpec(
            num_scalar_prefetch=5,
            grid=(MOE_MAX_STEPS,),
            in_specs=piece_in + [by_expert((1, D_MODEL, 2 * D_FF)), by_expert((1, D_FF, D_MODEL)),
                                 by_expert((1, 1, D_FF)), by_expert((1, 1, D_FF)),
                                 by_expert((1, 1, D_MODEL))],
            out_specs=pl.BlockSpec(memory_space=pl.ANY),
            scratch_shapes=[pltpu.VMEM((D_MODEL, D_FF), BF16), pltpu.VMEM((D_MODEL, D_FF), BF16),
                            pltpu.VMEM((D_FF, D_MODEL), BF16),
                            pltpu.VMEM((2, npc * PIECE, D_MODEL), BF16),
                            pltpu.SemaphoreType.DMA((2, npc))]),
        out_shape=jax.ShapeDtypeStruct((N_PIECES * PIECE, D_MODEL), BF16),
        compiler_params=pltpu.CompilerParams(vmem_limit_bytes=V7X_VMEM_LIMIT),
        name="moe_grouped",
    )(*tables, *([slots] * npc), w_up, w_down, bg, bl, bd)


def _combine_kernel(y_ref, lp_ref, w_ref, x1_ref, gt_ref, gf_ref, o_ref):
    tm = x1_ref.shape[0]
    rows = y_ref.shape[0]
    r = lax.broadcasted_iota(jnp.int32, (tm, tm), 0)
    c = lax.broadcasted_iota(jnp.int32, (tm, tm), 1)
    diag = r == c

    def column(x_row):
        return jnp.sum(jnp.where(diag, x_row, 0.0), axis=1, keepdims=True)

    si = lax.broadcasted_iota(jnp.int32, (tm, rows), 1)
    wt = jnp.zeros((tm, rows), F32)
    for k in range(TOP_K):
        pos = column(lp_ref[k:k + 1, :].astype(F32)).astype(jnp.int32)
        wt = jnp.where(si == pos, column(w_ref[k:k + 1, :]), wt)
    hi = wt.astype(BF16)
    lo = (wt - hi.astype(F32)).astype(BF16)
    y = y_ref[...]
    moe = _dg(hi, y, NN) + _dg(lo, y, NN)
    gt = gt_ref[...].reshape(-1, D_MODEL)
    o_ref[...] = _rms(x1_ref[...] + gt * moe, gf_ref[...])


def _combine(yslots, lp, w, x1, mod3, mod2, g_final, tm, slot_block):
    t = x1.shape[0]
    nt = t // tm
    rows = SLOT_ROWS[tm]
    if mod3 is not None:
        per_b = (t // mod3.shape[0]) // tm
        gspec = pl.BlockSpec((1, 1, D_MODEL), lambda i: (i // per_b, 0, 5))
        mod = mod3
    else:
        gspec = pl.BlockSpec((tm, D_MODEL), lambda i: (i, 5))
        mod = mod2
    return pl.pallas_call(
        _combine_kernel,
        grid=(nt,),
        in_specs=[pl.BlockSpec((rows, D_MODEL), lambda i: (slot_block + i, 0)),
                  pl.BlockSpec((TOP_K, tm), lambda i: (0, i)),
                  pl.BlockSpec((TOP_K, tm), lambda i: (0, i)),
                  pl.BlockSpec((tm, D_MODEL), lambda i: (i, 0)),
                  gspec,
                  pl.BlockSpec((1, D_MODEL), lambda i: (0, 0))],
        out_specs=pl.BlockSpec((tm, D_MODEL), lambda i: (i, 0)),
        out_shape=jax.ShapeDtypeStruct((t, D_MODEL), F32),
        compiler_params=pltpu.CompilerParams(vmem_limit_bytes=V7X_VMEM_LIMIT),
        name="moe_combine",
    )(yslots, lp, w, x1, mod, g_final.reshape(1, -1))


def kernel(x_prompt, x_sample, c_prompt, c_sample, cache_k_win, cache_v_win, state_hgrn, w_ada, b_ada,
           g_mix, g_ffn, w_in, attn_sinks, g_attn_out, hg_lb_logits, g_hg_out, w_out, w_router, b_router,
           w_up, b_up, w_down, b_down, g_final):
    batch, seq, d = x_prompt.shape
    nsamp = x_sample.shape[0]
    win = cache_k_win.shape[2]
    layer = 0

    mod = _modulation(jnp.concatenate([c_prompt, c_sample], axis=0), w_ada[layer], b_ada[layer])
    mod_p = mod[:batch].reshape(batch, 1, 6 * d)
    mod_s = mod[batch:]

    assert (batch * seq, nsamp) == (N_PROMPT_TILES * TOK_TILE, SAMPLE_TILE)
    w_in_b = w_in[layer].astype(BF16)
    w_out_b = w_out[layer].astype(BF16)
    w_router_t = w_router[layer].T
    bg = b_up[layer][:, None, 0::2]
    bl = b_up[layer][:, None, 1::2]
    bd = b_down[layer][:, None, :]

    xp = x_prompt.reshape(batch * seq, d)
    cos_p, sin_p = _rope_tables(np.arange(seq))
    qa, ka, va, qh, fh, ih, gh = _in_projection(xp, mod_p, None, g_mix[layer], w_in_b, cos_p, sin_p, TOK_TILE)
    oa = _attention_prompt(qa, ka, va, attn_sinks[layer], g_attn_out[layer], batch)
    oh, s_prompt = _hgrn_prompt(qh, fh, ih, gh, hg_lb_logits, g_hg_out[layer], batch)
    x1_p, slots, lp_p, cw_p, pc_p = _out_projection(
        xp, oa, oh, mod_p, None, g_ffn[layer], w_out_b, w_router_t, b_router[layer], TOK_TILE, None, 0)
    k_win_p = ka.reshape(batch, seq, ATTN_KV_HEADS, HEAD_DIM)[:, seq - win:]
    v_win_p = va.reshape(batch, seq, ATTN_KV_HEADS, HEAD_DIM)[:, seq - win:]

    xs = x_sample.reshape(nsamp, d)
    cos_s, sin_s = _rope_tables(np.full((nsamp,), PAST_LEN))
    qa, ka, va, qh, fh, ih, gh = _in_projection(xs, None, mod_s, g_mix[layer], w_in_b, cos_s, sin_s, nsamp)
    oa, k_win_s, v_win_s = _attention_sample(
        qa, ka, va, cache_k_win[layer].reshape(nsamp, win, KV_WIDTH),
        cache_v_win[layer].reshape(nsamp, win, KV_WIDTH), attn_sinks[layer], g_attn_out[layer])
    oh, s_sample = _hgrn_sample(qh, fh, ih, gh, hg_lb_logits, g_hg_out[layer], state_hgrn[layer])
    sample_block = TILE_PIECE_BASE[-1] * PIECE // SLOT_ROWS[SAMPLE_TILE]
    x1_s, slots, lp_s, cw_s, pc_s = _out_projection(
        xs, oa, oh, None, mod_s, g_ffn[layer], w_out_b, w_router_t, b_router[layer], SAMPLE_TILE,
        slots, sample_block)

    tables = _piece_tables(jnp.concatenate([pc_p[:, :, 0], pc_s[:, :, 0]], axis=0))
    yslots = _moe_grouped(slots, tables, w_up[layer], w_down[layer], bg, bl, bd)
    y_prompt = _combine(yslots, lp_p, cw_p, x1_p, mod_p, None, g_final, TOK_TILE, 0)
    y_sample = _combine(yslots, lp_s, cw_s, x1_s, None, mod_s, g_final, SAMPLE_TILE, sample_block)

    kv_shape = (1, nsamp, win, ATTN_KV_HEADS, HEAD_DIM)
    return (y_prompt.reshape(batch, seq, d), y_sample.reshape(nsamp, 1, d),
            k_win_p[None], v_win_p[None], s_prompt[None],
            k_win_s.reshape(kv_shape), v_win_s.reshape(kv_shape), s_sample[None])
```

```python
import functools

import numpy as np
import jax
import jax.numpy as jnp
from jax import lax
from jax.experimental import pallas as pl
from jax.experimental.pallas import tpu as pltpu

F32 = jnp.float32
BF16 = jnp.bfloat16

D_MODEL = 1024
SEQ = 2048
PAST_LEN = 16384
ATTN_HEADS = 8
ATTN_KV_HEADS = 2
HEAD_DIM = 64
ATTN_GROUP = ATTN_HEADS // ATTN_KV_HEADS
ATTN_WIDTH = ATTN_HEADS * HEAD_DIM
KV_WIDTH = ATTN_KV_HEADS * HEAD_DIM
WINDOW = 128
ATTN_STEP_BLOCKS = 2
ROT_DIM = HEAD_DIM // 4
ROPE_THETA = 500000.0
HG_HEADS = 4
HG_DK = 128
HG_DV = 128
HG_WIDTH = HG_HEADS * HG_DV
HG_CHUNK = 64
HG_SUB = 8
HG_STEP_CHUNKS = 2
IN_COLS = ATTN_WIDTH + 2 * KV_WIDTH + 4 * HG_WIDTH
N_EXPERTS = 32
TOP_K = 4
D_FF = D_MODEL
SWIGLU_ALPHA = 1.702
SWIGLU_LIMIT = 7.0
EPS = 1e-5

V7X_VMEM_LIMIT = 56 * 1024 * 1024

PIECE = 16
TOK_TILE = 512
SAMPLE_TILE = 128
N_PROMPT_TILES = 32
SLOT_ROWS = {TOK_TILE: 2560, SAMPLE_TILE: 1024}
TILE_PIECE_CAP = [SLOT_ROWS[TOK_TILE] // PIECE] * N_PROMPT_TILES + [SLOT_ROWS[SAMPLE_TILE] // PIECE]
TILE_PIECE_BASE = [i * TILE_PIECE_CAP[0] for i in range(N_PROMPT_TILES + 1)]
N_PIECES = sum(TILE_PIECE_CAP)
STEP_PIECES = 32
HALF_PIECES = 16
MOE_MAX_STEPS = N_PIECES // STEP_PIECES + N_EXPERTS + 1 + 2

NN = (((1,), (0,)), ((), ()))
NT = (((1,), (1,)), ((), ()))
TN = (((0,), (0,)), ((), ()))


def _dg(a, b, dims):
    return lax.dot_general(a, b, dims, preferred_element_type=F32)


def _split3(x):
    h = x.astype(BF16)
    r = x - h.astype(F32)
    m = r.astype(BF16)
    l = (r - m.astype(F32)).astype(BF16)
    return h, m, l


def _dot_f32(a, b, dims):
    ah, am, al = _split3(a)
    bh, bm, bl = _split3(b)
    return (_dg(ah, bh, dims) + (_dg(ah, bm, dims) + _dg(am, bh, dims))
            + (_dg(am, bm, dims) + _dg(ah, bl, dims) + _dg(al, bh, dims)))


def _dot_exact_lhs(a_bf16, b, dims):
    bh, bm, bl = _split3(b)
    return _dg(a_bf16, bh, dims) + _dg(a_bf16, bm, dims) + _dg(a_bf16, bl, dims)


def _sigmoid(x):
    return 1.0 / (1.0 + jnp.exp(-x))


def _rms(x, g):
    return x * lax.rsqrt(jnp.mean(x * x, axis=-1, keepdims=True) + EPS) * g


def _mod_kernel(c_ref, w_ref, b_ref, o_ref):
    c = c_ref[...]
    o_ref[...] = _dot_f32(c * _sigmoid(c), w_ref[...], NN) + b_ref[...]


def _modulation(c_all, w_ada, b_ada):
    n = c_all.shape[0]
    return pl.pallas_call(
        _mod_kernel,
        grid=(6,),
        in_specs=[pl.BlockSpec((n, D_MODEL), lambda j: (0, 0)),
                  pl.BlockSpec((D_MODEL, D_MODEL), lambda j: (0, j)),
                  pl.BlockSpec((1, D_MODEL), lambda j: (0, j))],
        out_specs=pl.BlockSpec((n, D_MODEL), lambda j: (0, j)),
        out_shape=jax.ShapeDtypeStruct((n, 6 * D_MODEL), F32),
        compiler_params=pltpu.CompilerParams(vmem_limit_bytes=V7X_VMEM_LIMIT),
        name="adaln_mod",
    )(c_all, w_ada, b_ada.reshape(1, -1))


def _rotate(x, cos_t, sin_t):
    width = x.shape[-1]
    d = lax.broadcasted_iota(jnp.int32, x.shape, 1) % HEAD_DIM
    half = ROT_DIM // 2
    partner = jnp.where(d < half, pltpu.roll(x, width - half, axis=1), pltpu.roll(x, half, axis=1))
    return x * cos_t + partner * sin_t


def _inproj_kernel(x_ref, sh_ref, sc_ref, g_ref, w_ref, cos_ref, sin_ref,
                   qa_ref, ka_ref, va_ref, qh_ref, fh_ref, ih_ref, gh_ref):
    x = x_ref[...]
    sh = sh_ref[...].reshape(-1, D_MODEL)
    sc = sc_ref[...].reshape(-1, D_MODEL)
    h = _rms(x, g_ref[...]) * (1.0 + sc) + sh
    z = _dg(h.astype(BF16), w_ref[...], NN)
    cos_k = cos_ref[...]
    sin_k = sin_ref[...]
    cos_q = jnp.concatenate([cos_k] * ATTN_GROUP, axis=1)
    sin_q = jnp.concatenate([sin_k] * ATTN_GROUP, axis=1)
    o = 0
    qa = _rotate(z[:, o:o + ATTN_WIDTH], cos_q, sin_q)
    qa_ref[...] = (qa * (HEAD_DIM ** -0.5)).astype(BF16)
    o += ATTN_WIDTH
    ka_ref[...] = _rotate(z[:, o:o + KV_WIDTH], cos_k, sin_k)
    o += KV_WIDTH
    va_ref[...] = z[:, o:o + KV_WIDTH]
    o += KV_WIDTH
    qh_ref[...] = z[:, o:o + HG_WIDTH]
    o += HG_WIDTH
    fh_ref[...] = z[:, o:o + HG_WIDTH]
    o += HG_WIDTH
    ih_ref[...] = z[:, o:o + HG_WIDTH].astype(BF16)
    o += HG_WIDTH
    gh_ref[...] = z[:, o:o + HG_WIDTH]


def _rope_tables(positions):
    half = ROT_DIM // 2
    inv = ROPE_THETA ** (-(np.arange(half, dtype=np.float64) * 2.0 / ROT_DIM))
    ang = np.asarray(positions, np.float64)[:, None] * inv[None, :]
    cos_h = np.ones((len(positions), HEAD_DIM))
    sin_h = np.zeros((len(positions), HEAD_DIM))
    cos_h[:, :half] = np.cos(ang)
    cos_h[:, half:ROT_DIM] = np.cos(ang)
    sin_h[:, :half] = -np.sin(ang)
    sin_h[:, half:ROT_DIM] = np.sin(ang)
    cos_t = np.tile(cos_h, (1, ATTN_KV_HEADS)).astype(np.float32)
    sin_t = np.tile(sin_h, (1, ATTN_KV_HEADS)).astype(np.float32)
    return jnp.asarray(cos_t), jnp.asarray(sin_t)


def _in_projection(x, mod3, mod2, g_mix, w_in_bf16, cos_t, sin_t, tm):
    t = x.shape[0]
    nt = t // tm
    if mod3 is not None:
        per_b = (t // mod3.shape[0]) // tm
        sh_spec = pl.BlockSpec((1, 1, D_MODEL), lambda i: (i // per_b, 0, 0))
        sc_spec = pl.BlockSpec((1, 1, D_MODEL), lambda i: (i // per_b, 0, 1))
        mod = mod3
        ncs = cos_t.shape[0] // tm
        cs_spec = pl.BlockSpec((tm, KV_WIDTH), lambda i: (i % ncs, 0))
    else:
        sh_spec = pl.BlockSpec((tm, D_MODEL), lambda i: (i, 0))
        sc_spec = pl.BlockSpec((tm, D_MODEL), lambda i: (i, 1))
        mod = mod2
        cs_spec = pl.BlockSpec((tm, KV_WIDTH), lambda i: (i, 0))
    row = lambda w: pl.BlockSpec((tm, w), lambda i: (i, 0))
    return pl.pallas_call(
        _inproj_kernel,
        grid=(nt,),
        in_specs=[row(D_MODEL), sh_spec, sc_spec,
                  pl.BlockSpec((1, D_MODEL), lambda i: (0, 0)),
                  pl.BlockSpec((D_MODEL, IN_COLS), lambda i: (0, 0)),
                  cs_spec, cs_spec],
        out_specs=[row(ATTN_WIDTH), row(KV_WIDTH), row(KV_WIDTH),
                   row(HG_WIDTH), row(HG_WIDTH), row(HG_WIDTH), row(HG_WIDTH)],
        out_shape=[jax.ShapeDtypeStruct((t, ATTN_WIDTH), BF16),
                   jax.ShapeDtypeStruct((t, KV_WIDTH), F32),
                   jax.ShapeDtypeStruct((t, KV_WIDTH), F32),
                   jax.ShapeDtypeStruct((t, HG_WIDTH), F32),
                   jax.ShapeDtypeStruct((t, HG_WIDTH), F32),
                   jax.ShapeDtypeStruct((t, HG_WIDTH), BF16),
                   jax.ShapeDtypeStruct((t, HG_WIDTH), F32)],
        compiler_params=pltpu.CompilerParams(vmem_limit_bytes=V7X_VMEM_LIMIT),
        name="in_proj",
    )(x, mod, mod, g_mix.reshape(1, -1), w_in_bf16, cos_t, sin_t)


def _attn_prompt_kernel(sink_ref, q_ref, kc_ref, kp_ref, vc_ref, vp_ref, g_ref, o_ref):
    n = pl.program_id(1)
    blk = WINDOW
    nblk = q_ref.shape[0] // blk
    pair_w = 2 * HEAD_DIM
    low = lax.broadcasted_iota(jnp.int32, (1, pair_w), 1) < HEAD_DIM
    kall = jnp.concatenate([kp_ref[...], kc_ref[...]], axis=0)
    vall = jnp.concatenate([vp_ref[...], vc_ref[...]], axis=0)
    kroll = pltpu.roll(kall, HEAD_DIM, axis=1)
    vroll = pltpu.roll(vall, HEAD_DIM, axis=1)
    kdup = [jnp.where(low, kall, kroll).astype(BF16), jnp.where(low, kroll, kall).astype(BF16)]
    v_lo = [jnp.where(low, vall, 0.0).astype(BF16), jnp.where(low, vroll, 0.0).astype(BF16)]
    v_hi = [jnp.where(low, 0.0, vroll).astype(BF16), jnp.where(low, 0.0, vall).astype(BF16)]
    qi = lax.broadcasted_iota(jnp.int32, (blk, 2 * blk), 0)
    kj = lax.broadcasted_iota(jnp.int32, (blk, 2 * blk), 1)
    band = (kj >= qi) & (kj <= qi + blk)
    zero = jnp.zeros((), BF16)
    for bi in range(nblk):
        rows = slice(bi * blk, (bi + 1) * blk)
        krows = slice(bi * blk, (bi + 2) * blk)
        ok = band & ((kj >= blk) | (n > 0)) if bi == 0 else band
        pairs = []
        for pair in range(ATTN_HEADS // 2):
            h = (2 * pair) // ATTN_GROUP
            qp = q_ref[rows, pair * pair_w:(pair + 1) * pair_w]
            acc, dens = None, []
            for half in range(2):
                sink = sink_ref[2 * pair + half]
                qm = jnp.where(low if half == 0 else jnp.logical_not(low), qp, zero)
                s = jnp.where(ok, _dg(qm, kdup[h][krows], NT), -jnp.inf)
                m = jnp.maximum(jnp.max(s, axis=-1, keepdims=True), sink)
                p = jnp.exp(s - m)
                dens.append(jnp.sum(p, axis=-1, keepdims=True) + jnp.exp(sink - m))
                pv = _dg(p.astype(BF16), (v_lo if half == 0 else v_hi)[h][krows], NN)
                acc = pv if acc is None else acc + pv
            pairs.append(acc / jnp.where(low, dens[0], dens[1]))
        o_ref[rows, :] = _rms(jnp.concatenate(pairs, axis=1), g_ref[...]).astype(BF16)


def _attention_prompt(qa, ka, va, sinks, g_attn_out, batch):
    t = qa.shape[0]
    nb = t // batch // (WINDOW * ATTN_STEP_BLOCKS)
    cur = lambda w: pl.BlockSpec((WINDOW * ATTN_STEP_BLOCKS, w), lambda b, n: (b * nb + n, 0))
    prev = lambda w: pl.BlockSpec(
        (WINDOW, w), lambda b, n: (ATTN_STEP_BLOCKS * (b * nb + n) - jnp.minimum(n, 1), 0))
    return pl.pallas_call(
        _attn_prompt_kernel,
        grid=(batch, nb),
        in_specs=[pl.BlockSpec(memory_space=pltpu.SMEM),
                  cur(ATTN_WIDTH), cur(KV_WIDTH), prev(KV_WIDTH), cur(KV_WIDTH), prev(KV_WIDTH),
                  pl.BlockSpec((1, ATTN_WIDTH), lambda b, n: (0, 0))],
        out_specs=cur(ATTN_WIDTH),
        out_shape=jax.ShapeDtypeStruct((t, ATTN_WIDTH), BF16),
        name="attn_prompt",
    )(sinks, qa, ka, ka, va, va, g_attn_out.reshape(1, -1))


def _lower_bound(lb_logits_ref):
    lg = lb_logits_ref[...]
    e = jnp.exp(lg - jnp.max(lg, axis=0, keepdims=True))
    return e[0:1] / jnp.sum(e, axis=0, keepdims=True)


def _hgrn_prompt_kernel(q_ref, f_ref, i_ref, g_ref, lbl_ref, gn_ref, o_ref, sfin_ref, st_ref):
    step = pl.program_id(1)
    C = HG_CHUNK
    W = HG_WIDTH
    nsub = C // HG_SUB
    n_chunks = q_ref.shape[0] // C
    heads = [slice(hh * HG_DK, (hh + 1) * HG_DK) for hh in range(HG_HEADS)]

    @pl.when(step == 0)
    def _():
        st_ref[...] = jnp.zeros_like(st_ref)

    lb = _lower_bound(lbl_ref)
    r64 = lax.broadcasted_iota(jnp.int32, (C, C), 0)
    c64 = lax.broadcasted_iota(jnp.int32, (C, C), 1)
    tri = (r64 >= c64).astype(BF16)
    rsub = r64 % HG_SUB
    o_intra, q_state, upd, decay_last = {}, {}, {}, {}
    for ci in range(n_chunks):
        rows = slice(ci * C, (ci + 1) * C)
        f = lb + (1.0 - lb) * _sigmoid(f_ref[rows, :])
        kk = 1.0 - f
        logf = jnp.log(f)
        q = q_ref[rows, :]
        v = i_ref[rows, :]
        b = _dot_exact_lhs(tri, logf, NN)
        prods = [q * kk]
        w = logf
        for d in range(1, HG_SUB):
            if d > 1:
                w = w + pltpu.roll(logf, d - 1, axis=0)
            prods.append(q * pltpu.roll(kk, d, axis=0) * jnp.exp(w))
        ends = [b[j * HG_SUB + HG_SUB - 1:(j + 1) * HG_SUB] for j in range(nsub)]
        kt = kk * jnp.exp(jnp.concatenate([jnp.broadcast_to(e, (HG_SUB, W)) for e in ends], axis=0) - b)
        lhs, rhs = [], []
        for j in range(nsub - 1):
            lo = (j + 1) * HG_SUB
            lhs.append(jnp.concatenate(
                [jnp.zeros((lo, W), F32), q[lo:] * jnp.exp(b[lo:] - ends[j])], axis=0).astype(BF16))
            pieces = [kt[j * HG_SUB:lo]]
            if j > 0:
                pieces.insert(0, jnp.zeros((j * HG_SUB, W), F32))
            pieces.append(jnp.zeros((C - lo, W), F32))
            rhs.append(jnp.concatenate(pieces, axis=0).astype(BF16))
        q_state[ci] = (q * jnp.exp(b)).astype(BF16)
        kd = (kk * jnp.exp(ends[-1] - b)).astype(BF16)
        decay_last[ci] = jnp.exp(ends[-1])
        for hh, sl in enumerate(heads):
            a = jnp.zeros((C, C), F32)
            for d in range(HG_SUB):
                ad = jnp.sum(prods[d][:, sl], axis=-1, keepdims=True)
                a = a + jnp.where((c64 == r64 - d) & (rsub >= d), ad, 0.0)
            a = a + _dg(jnp.concatenate([x[:, sl] for x in lhs], axis=1),
                        jnp.concatenate([x[:, sl] for x in rhs], axis=1), NT)
            o_intra[ci, hh] = _dg(a.astype(BF16), v[:, sl], NN)
            upd[ci, hh] = _dg(v[:, sl], kd[:, sl], TN)
    finals = []
    for hh, sl in enumerate(heads):
        st = st_ref[hh]
        for ci in range(n_chunks):
            rows = slice(ci * C, (ci + 1) * C)
            o = o_intra[ci, hh] + _dg(q_state[ci][:, sl], st.astype(BF16), NT)
            st = st * decay_last[ci][:, sl] + upd[ci, hh]
            g = g_ref[rows, sl]
            o_ref[rows, sl] = (_rms(o, gn_ref[:, sl]) * (g * _sigmoid(g))).astype(BF16)
        st_ref[hh] = st
        finals.append(st)

    @pl.when(step == pl.num_programs(1) - 1)
    def _():
        for hh in range(HG_HEADS):
            sfin_ref[0, hh] = finals[hh].T


def _hgrn_prompt(qh, fh, ih, gh, lb_logits, g_hg_out, batch):
    t = qh.shape[0]
    nc = t // batch // (HG_CHUNK * HG_STEP_CHUNKS)
    blk = pl.BlockSpec((HG_CHUNK * HG_STEP_CHUNKS, HG_WIDTH), lambda b, c: (b * nc + c, 0))
    const = lambda r: pl.BlockSpec((r, HG_WIDTH), lambda b, c: (0, 0))
    return pl.pallas_call(
        _hgrn_prompt_kernel,
        grid=(batch, nc),
        in_specs=[blk, blk, blk, blk, const(lb_logits.shape[0]), const(1)],
        out_specs=[blk, pl.BlockSpec((1, HG_HEADS, HG_DK, HG_DV), lambda b, c: (b, 0, 0, 0))],
        out_shape=[jax.ShapeDtypeStruct((t, HG_WIDTH), BF16),
                   jax.ShapeDtypeStruct((batch, HG_HEADS, HG_DK, HG_DV), F32)],
        scratch_shapes=[pltpu.VMEM((HG_HEADS, HG_DV, HG_DK), F32)],
        name="hgrn_prompt",
    )(qh, fh, ih, gh, lb_logits, g_hg_out.reshape(1, -1))


def _attn_sample_kernel(sink_ref, q_ref, kn_ref, vn_ref, kc_ref, vc_ref, g_ref,
                        o_ref, ko_ref, vo_ref):
    bt = q_ref.shape[0]
    win = kc_ref.shape[1]
    kc = kc_ref[...]
    vc = vc_ref[...]
    kn = kn_ref[...]
    vn = vn_ref[...]
    outs = []
    for h in range(ATTN_KV_HEADS):
        ls = slice(h * HEAD_DIM, (h + 1) * HEAD_DIM)
        q = q_ref[:, h * ATTN_GROUP:(h + 1) * ATTN_GROUP, :]
        s = jnp.einsum('bgd,bjd->bgj', q, kc[:, :, ls].astype(BF16), preferred_element_type=F32)
        s_new = jnp.sum(q.astype(F32) * kn[:, None, ls], axis=-1, keepdims=True)
        gi = lax.broadcasted_iota(jnp.int32, (1, ATTN_GROUP, 1), 1)
        sink = jnp.zeros((1, ATTN_GROUP, 1), F32)
        for g in range(ATTN_GROUP):
            sink = jnp.where(gi == g, sink_ref[h * ATTN_GROUP + g], sink)
        m = jnp.maximum(jnp.maximum(jnp.max(s, axis=-1, keepdims=True), s_new), sink)
        p = jnp.exp(s - m)
        p_new = jnp.exp(s_new - m)
        den = jnp.sum(p, axis=-1, keepdims=True) + p_new + jnp.exp(sink - m)
        o = jnp.einsum('bgj,bjd->bgd', p.astype(BF16), vc[:, :, ls].astype(BF16),
                       preferred_element_type=F32)
        o = (o + p_new * vn[:, None, ls]) / den
        outs.append(o)
    ssq = sum(jnp.sum(jnp.sum(o * o, axis=-1, keepdims=True), axis=1, keepdims=True) for o in outs)
    scale = lax.rsqrt(ssq / ATTN_WIDTH + EPS)
    for h in range(ATTN_KV_HEADS):
        o_ref[h] = (outs[h] * scale * g_ref[h][None]).astype(BF16)
    ri = lax.broadcasted_iota(jnp.int32, (win, KV_WIDTH), 0)
    for b in range(bt):
        ko_ref[b] = jnp.where(ri == win - 1, kn[b:b + 1], pltpu.roll(kc[b], win - 1, axis=0))
        vo_ref[b] = jnp.where(ri == win - 1, vn[b:b + 1], pltpu.roll(vc[b], win - 1, axis=0))


def _attention_sample(qa, ka, va, cache_k, cache_v, sinks, g_attn_out, bt=8):
    nb = qa.shape[0]
    win = cache_k.shape[1]
    q3 = qa.reshape(nb, ATTN_HEADS, HEAD_DIM)
    g3 = g_attn_out.reshape(ATTN_KV_HEADS, ATTN_GROUP, HEAD_DIM)
    row = lambda w: pl.BlockSpec((bt, w), lambda i: (i, 0))
    cache = pl.BlockSpec((bt, win, KV_WIDTH), lambda i: (i, 0, 0))
    o4, k_new, v_new = pl.pallas_call(
        _attn_sample_kernel,
        grid=(nb // bt,),
        in_specs=[pl.BlockSpec(memory_space=pltpu.SMEM),
                  pl.BlockSpec((bt, ATTN_HEADS, HEAD_DIM), lambda i: (i, 0, 0)),
                  row(KV_WIDTH), row(KV_WIDTH), cache, cache,
                  pl.BlockSpec((ATTN_KV_HEADS, ATTN_GROUP, HEAD_DIM), lambda i: (0, 0, 0))],
        out_specs=[pl.BlockSpec((ATTN_KV_HEADS, bt, ATTN_GROUP, HEAD_DIM), lambda i: (0, i, 0, 0)),
                   cache, cache],
        out_shape=[jax.ShapeDtypeStruct((ATTN_KV_HEADS, nb, ATTN_GROUP, HEAD_DIM), BF16),
                   jax.ShapeDtypeStruct(cache_k.shape, F32),
                   jax.ShapeDtypeStruct(cache_v.shape, F32)],
        name="attn_sample",
    )(sinks, q3, ka, va, cache_k, cache_v, g3)
    oa = jnp.transpose(o4, (1, 0, 2, 3)).reshape(nb, ATTN_WIDTH)
    return oa, k_new, v_new


def _hgrn_sample_kernel(q_ref, f_ref, i_ref, g_ref, lbl_ref, gn_ref, s0_ref, o_ref, s_ref):
    bt = q_ref.shape[0]
    lb = _lower_bound(lbl_ref)
    f = lb + (1.0 - lb) * _sigmoid(f_ref[...])
    kk = 1.0 - f
    q = q_ref[...]
    v = i_ref[...].astype(F32)
    g = g_ref[...]
    gate = g * _sigmoid(g)
    r = lax.broadcasted_iota(jnp.int32, (HG_DK, HG_DK), 0)
    cc = lax.broadcasted_iota(jnp.int32, (HG_DK, HG_DK), 1)
    diag = r == cc

    def column(x_row):
        return jnp.sum(jnp.where(diag, x_row, 0.0), axis=1, keepdims=True)

    for b in range(bt):
        parts = []
        for hh in range(HG_HEADS):
            sl = slice(hh * HG_DK, (hh + 1) * HG_DK)
            s_new = column(f[b:b + 1, sl]) * s0_ref[b, hh] + column(kk[b:b + 1, sl]) * v[b:b + 1, sl]
            s_ref[b, hh] = s_new
            o = jnp.sum(s_new * column(q[b:b + 1, sl]), axis=0, keepdims=True)
            parts.append(_rms(o, gn_ref[:, sl]))
        o_ref[b:b + 1, :] = (jnp.concatenate(parts, axis=1) * gate[b:b + 1]).astype(BF16)


def _hgrn_sample(qh, fh, ih, gh, lb_logits, g_hg_out, state, bt=8):
    nb = qh.shape[0]
    row = pl.BlockSpec((bt, HG_WIDTH), lambda i: (i, 0))
    const = lambda r: pl.BlockSpec((r, HG_WIDTH), lambda i: (0, 0))
    st = pl.BlockSpec((bt, HG_HEADS, HG_DK, HG_DV), lambda i: (i, 0, 0, 0))
    return pl.pallas_call(
        _hgrn_sample_kernel,
        grid=(nb // bt,),
        in_specs=[row, row, row, row, const(lb_logits.shape[0]), const(1), st],
        out_specs=[row, st],
        out_shape=[jax.ShapeDtypeStruct((nb, HG_WIDTH), BF16),
                   jax.ShapeDtypeStruct(state.shape, F32)],
        name="hgrn_sample",
    )(qh, fh, ih, gh, lb_logits, g_hg_out.reshape(1, -1), state)


def _outproj_kernel(*refs, n_tiles):
    xs_ref = refs[-4]

    @pl.when(pl.program_id(0) < n_tiles)
    def _():
        _outproj_tile(*refs)

    @pl.when(pl.program_id(0) >= n_tiles)
    def _():
        xs_ref[...] = jnp.zeros_like(xs_ref)


def _outproj_tile(x_ref, oa_ref, oh_ref, gt_ref, sh_ref, sc_ref, g_ref, wo_ref, wr_ref, br_ref, *rest):
    x1_ref, xs_ref, lp_ref, w_ref, pc_ref = rest[-5:]
    tm = x_ref.shape[0]
    rows = xs_ref.shape[0]
    gt = gt_ref[...].reshape(-1, D_MODEL)
    sh = sh_ref[...].reshape(-1, D_MODEL)
    sc = sc_ref[...].reshape(-1, D_MODEL)
    mix = _dg(oa_ref[...], wo_ref[0:ATTN_WIDTH, :], NN) + _dg(oh_ref[...], wo_ref[ATTN_WIDTH:, :], NN)
    x1 = x_ref[...] + gt * mix
    x1_ref[...] = x1
    h2 = _rms(x1, g_ref[...]) * (1.0 + sc) + sh
    logits = _dot_f32(wr_ref[...], h2, NT) + br_ref[...]
    ei = lax.broadcasted_iota(jnp.int32, logits.shape, 0)
    vals, sel = [], []
    l = logits
    for _ in range(TOP_K):
        m = jnp.max(l, axis=0, keepdims=True)
        idx = jnp.min(jnp.where(l == m, ei, N_EXPERTS), axis=0, keepdims=True)
        pick = ei == idx
        vals.append(m)
        sel.append(pick)
        l = jnp.where(pick, -jnp.inf, l)
    ex = [jnp.exp(v - vals[0]) for v in vals]
    den = ex[0] + ex[1] + ex[2] + ex[3]
    for k in range(TOP_K):
        w_ref[k:k + 1, :] = ex[k] / den

    member = jnp.where(sel[0] | sel[1] | sel[2] | sel[3], 1.0, 0.0)
    tr = lax.broadcasted_iota(jnp.int32, (tm, tm), 0)
    tc = lax.broadcasted_iota(jnp.int32, (tm, tm), 1)
    rank = _dg(member.astype(BF16), (tr < tc).astype(BF16), NN)
    pieces = jnp.floor((jnp.sum(member, axis=1, keepdims=True) + (PIECE - 1)) * (1.0 / PIECE))
    er = lax.broadcasted_iota(jnp.int32, (N_EXPERTS, N_EXPERTS), 0)
    ec = lax.broadcasted_iota(jnp.int32, (N_EXPERTS, N_EXPERTS), 1)
    pieces_b = jnp.broadcast_to(pieces, (N_EXPERTS, 128))
    pc_ref[0] = pieces_b.astype(jnp.int32)
    base = _dg((ec < er).astype(BF16), pieces_b.astype(BF16), NN)[:, 0:1] * PIECE
    slot = base + rank
    lps = [jnp.sum(jnp.where(sel[k], slot, 0.0), axis=0, keepdims=True).astype(jnp.int32)
           for k in range(TOP_K)]
    for k in range(TOP_K):
        lp_ref[k:k + 1, :] = lps[k]
    hb = h2.astype(BF16)
    chunk = 512
    for r0 in range(0, rows, chunk):
        si = lax.broadcasted_iota(jnp.int32, (chunk, tm), 0) + r0
        hit = (si == lps[0]) | (si == lps[1]) | (si == lps[2]) | (si == lps[3])
        xs_ref[r0:r0 + chunk, :] = _dg(jnp.where(hit, 1.0, 0.0).astype(BF16), hb, NN).astype(BF16)


def _out_projection(x, oa, oh, mod3, mod2, g_ffn, w_out_bf16, w_router_t, b_router, tm, slots, slot_block):
    t = x.shape[0]
    nt = t // tm
    rows = SLOT_ROWS[tm]
    real = lambda i: jnp.minimum(i, nt - 1)
    if mod3 is not None:
        per_b = (t // mod3.shape[0]) // tm
        mspec = lambda j: pl.BlockSpec((1, 1, D_MODEL), lambda i: (real(i) // per_b, 0, j))
        mod = mod3
    else:
        mspec = lambda j: pl.BlockSpec((tm, D_MODEL), lambda i: (real(i), j))
        mod = mod2
    row = lambda w: pl.BlockSpec((tm, w), lambda i: (real(i), 0))
    full = lambda a: pl.BlockSpec(a.shape, lambda i: (0,) * a.ndim)
    g2 = g_ffn.reshape(1, -1)
    br = b_router.reshape(-1, 1)
    args = [x, oa, oh, mod, mod, mod, g2, w_out_bf16, w_router_t, br]
    in_specs = [row(D_MODEL), row(ATTN_WIDTH), row(HG_WIDTH), mspec(2), mspec(3), mspec(4),
                full(g2), full(w_out_bf16), full(w_router_t), full(br)]
    aliases = {}
    n_fill = 0
    if slots is not None:
        args.append(slots)
        in_specs.append(pl.BlockSpec(memory_space=pl.ANY))
        aliases = {len(args) - 1: 1}
    else:
        n_fill = pl.cdiv(N_PIECES * PIECE - nt * rows, rows)
    return pl.pallas_call(
        functools.partial(_outproj_kernel, n_tiles=nt),
        grid=(nt + n_fill,),
        in_specs=in_specs,
        out_specs=[row(D_MODEL),
                   pl.BlockSpec((rows, D_MODEL), lambda i: (slot_block + i, 0)),
                   pl.BlockSpec((TOP_K, tm), lambda i: (0, real(i))),
                   pl.BlockSpec((TOP_K, tm), lambda i: (0, real(i))),
                   pl.BlockSpec((1, N_EXPERTS, 128), lambda i: (real(i), 0, 0))],
        out_shape=[jax.ShapeDtypeStruct((t, D_MODEL), F32),
                   jax.ShapeDtypeStruct((N_PIECES * PIECE, D_MODEL), BF16),
                   jax.ShapeDtypeStruct((TOP_K, t), jnp.int32),
                   jax.ShapeDtypeStruct((TOP_K, t), F32),
                   jax.ShapeDtypeStruct((nt, N_EXPERTS, 128), jnp.int32)],
        input_output_aliases=aliases,
        compiler_params=pltpu.CompilerParams(vmem_limit_bytes=V7X_VMEM_LIMIT),
        name="out_proj_router",
    )(*args)


def _piece_tables(pieces_ie):
    cap = jnp.asarray(TILE_PIECE_CAP, jnp.int32)
    gbase = jnp.asarray(TILE_PIECE_BASE, jnp.int32)
    seg_src = gbase[:, None] + jnp.cumsum(pieces_ie, axis=1) - pieces_ie
    used_i = jnp.sum(pieces_ie, axis=1)
    tail_i = cap - used_i
    np_e = jnp.sum(pieces_ie, axis=0)
    rem = np_e % STEP_PIECES
    head_e = jnp.where(rem > 0, rem, jnp.minimum(np_e, STEP_PIECES))
    nt_e = np_e // STEP_PIECES + (rem > 0)
    tile_end = jnp.cumsum(nt_e)
    tile_start = tile_end - nt_e
    n_comp = tile_end[-1]
    q_start_e = jnp.cumsum(np_e) - np_e
    n_used = jnp.sum(np_e)
    tt = jnp.arange(MOE_MAX_STEPS, dtype=jnp.int32)
    e_t = jnp.minimum(jnp.sum(tt[:, None] >= tile_end[None, :], axis=1), N_EXPERTS - 1).astype(jnp.int32)
    is_comp = tt < n_comp
    k = tt - tile_start[e_t]
    q0_comp = q_start_e[e_t] + jnp.where(k == 0, 0, head_e[e_t] + STEP_PIECES * (k - 1))
    live_comp = jnp.where(k == 0, head_e[e_t], STEP_PIECES)
    q0_fill = n_used + STEP_PIECES * (tt - n_comp)
    q0 = jnp.where(is_comp, q0_comp, q0_fill)
    live = jnp.where(is_comp, live_comp, jnp.clip(N_PIECES - q0_fill, 0, STEP_PIECES))
    n_busy = n_comp + (N_PIECES - n_used + STEP_PIECES - 1) // STEP_PIECES
    length = jnp.concatenate([pieces_ie.T.reshape(-1), tail_i])
    src = jnp.concatenate([seg_src.T.reshape(-1), gbase + used_i])
    dest = jnp.cumsum(length) - length
    delta = src - dest
    d_delta = delta - jnp.concatenate([jnp.zeros((1,), jnp.int32), delta[:-1]])
    lane = jnp.arange(STEP_PIECES, dtype=jnp.int32)
    qq = (q0[:, None] + lane[None, :]).reshape(-1)
    ok = (lane[None, :] < live[:, None]).reshape(-1)
    piece = qq + jnp.sum(jnp.where(dest[None, :] <= qq[:, None], d_delta[None, :], 0), axis=1)
    dump = (N_PIECES + (tt % 2)[:, None] * STEP_PIECES + lane[None, :]).reshape(-1)
    tbl_out = jnp.where(ok, piece, dump).astype(jnp.int32)
    last = jnp.maximum(n_comp - 1, 0)
    tbl_in = jnp.where(ok & jnp.repeat(is_comp, STEP_PIECES), piece, 0).reshape(MOE_MAX_STEPS, STEP_PIECES)
    tbl_in = jnp.where(is_comp[:, None], tbl_in, tbl_in[last][None, :]).reshape(-1).astype(jnp.int32)
    texp = jnp.where(is_comp, e_t, e_t[last]).astype(jnp.int32)
    first = (is_comp & (k == 0)).astype(jnp.int32)
    kind = jnp.where(is_comp, jnp.where(live <= HALF_PIECES, 1, 2), jnp.where(tt < n_busy, 0, 3))
    return tbl_in, tbl_out, texp, kind.astype(jnp.int32), first


def _moe_grouped_kernel(tin_ref, tout_ref, texp_ref, kind_ref, first_ref, *refs):
    npc = STEP_PIECES
    x_refs = refs[:npc]
    wu_ref, wd_ref, bg_ref, bl_ref, bd_ref, y_hbm, wg_s, wl_s, wd_s, ybuf, sems = refs[npc:]
    t = pl.program_id(0)
    nsteps = pl.num_programs(0)
    kind = kind_ref[t]
    slot = t % 2

    def piece_copy(step, buf, p):
        dst = tout_ref[step * npc + p]
        return pltpu.make_async_copy(
            ybuf.at[buf, pl.ds(p * PIECE, PIECE), :],
            y_hbm.at[pl.ds(pl.multiple_of(dst * PIECE, PIECE), PIECE), :],
            sems.at[buf, p])

    def wait_step(step, buf):
        for p in range(npc):
            piece_copy(step, buf, p).wait()

    @pl.when(t == 0)
    def _():
        ybuf[...] = jnp.zeros_like(ybuf)

    @pl.when(t >= 2)
    def _():
        wait_step(t - 2, slot)

    @pl.when((kind > 0) & (first_ref[t] == 1))
    def _():
        cb = 256
        r = lax.broadcasted_iota(jnp.int32, (cb, cb), 0)
        c = lax.broadcasted_iota(jnp.int32, (cb, cb), 1)
        perm = jnp.where(r == jnp.where(c < cb // 2, 2 * c, 2 * (c - cb // 2) + 1), 1.0, 0.0).astype(BF16)
        for blk in range(2 * D_FF // cb):
            wp = _dg(wu_ref[0, :, blk * cb:(blk + 1) * cb].astype(BF16), perm, NN).astype(BF16)
            wg_s[:, blk * (cb // 2):(blk + 1) * (cb // 2)] = wp[:, :cb // 2]
            wl_s[:, blk * (cb // 2):(blk + 1) * (cb // 2)] = wp[:, cb // 2:]
        wd_s[...] = wd_ref[0].astype(BF16)

    def expert(n_pieces):
        xb = jnp.concatenate([x_refs[p][...] for p in range(n_pieces)], axis=0)
        glu = jnp.minimum(_dg(xb, wg_s[...], NN) + bg_ref[0], SWIGLU_LIMIT)
        lin = jnp.clip(_dg(xb, wl_s[...], NN) + bl_ref[0], -SWIGLU_LIMIT, SWIGLU_LIMIT)
        act = glu * _sigmoid(SWIGLU_ALPHA * glu) * (lin + 1.0)
        y = (_dg(act.astype(BF16), wd_s[...], NN) + bd_ref[0]).astype(BF16)
        ybuf[slot, 0:n_pieces * PIECE, :] = y

    @pl.when(kind == 2)
    def _():
        expert(STEP_PIECES)

    @pl.when(kind == 1)
    def _():
        expert(HALF_PIECES)

    @pl.when(kind == 0)
    def _():
        ybuf[slot] = jnp.zeros((npc * PIECE, D_MODEL), BF16)

    for p in range(npc):
        piece_copy(t, slot, p).start()

    @pl.when(t == nsteps - 1)
    def _():
        wait_step(t - 1, 1 - slot)
        wait_step(t, slot)


def _moe_grouped(slots, tables, w_up, w_down, bg, bl, bd):
    npc = STEP_PIECES
    piece_in = [pl.BlockSpec((PIECE, D_MODEL), functools.partial(
        lambda t, tin, tout, texp, kind, first, p: (tin[t * npc + p], 0), p=p)) for p in range(npc)]
    by_expert = lambda shape: pl.BlockSpec(
        shape, lambda t, tin, tout, texp, kind, first: (texp[t],) + (0,) * (len(shape) - 1))
    return pl.pallas_call(
        _moe_grouped_kernel,
        grid_spec=pltpu.PrefetchScalarGridSpec(
            num_scalar_prefetch=5,
            grid=(MOE_MAX_STEPS,),
            in_specs=piece_in + [by_expert((1, D_MODEL, 2 * D_FF)), by_expert((1, D_FF, D_MODEL)),
                                 by_expert((1, 1, D_FF)), by_expert((1, 1, D_FF)),
                                 by_expert((1, 1, D_MODEL))],
            out_specs=pl.BlockSpec(memory_space=pl.ANY),
            scratch_shapes=[pltpu.VMEM((D_MODEL, D_FF), BF16), pltpu.VMEM((D_MODEL, D_FF), BF16),
                            pltpu.VMEM((D_FF, D_MODEL), BF16),
                            pltpu.VMEM((2, npc * PIECE, D_MODEL), BF16),
                            pltpu.SemaphoreType.DMA((2, npc))]),
        out_shape=jax.ShapeDtypeStruct(((N_PIECES + 2 * npc) * PIECE, D_MODEL), BF16),
        compiler_params=pltpu.CompilerParams(vmem_limit_bytes=V7X_VMEM_LIMIT),
        name="moe_grouped",
    )(*tables, *([slots] * npc), w_up, w_down, bg, bl, bd)


def _combine_kernel(y_ref, lp_ref, w_ref, x1_ref, gt_ref, gf_ref, o_ref):
    tm = x1_ref.shape[0]
    rows = y_ref.shape[0]
    r = lax.broadcasted_iota(jnp.int32, (tm, tm), 0)
    c = lax.broadcasted_iota(jnp.int32, (tm, tm), 1)
    diag = r == c

    def column(x_row):
        return jnp.sum(jnp.where(diag, x_row, 0.0), axis=1, keepdims=True)

    si = lax.broadcasted_iota(jnp.int32, (tm, rows), 1)
    wt = jnp.zeros((tm, rows), F32)
    for k in range(TOP_K):
        pos = column(lp_ref[k:k + 1, :].astype(F32)).astype(jnp.int32)
        wt = jnp.where(si == pos, column(w_ref[k:k + 1, :]), wt)
    moe = _dg(wt.astype(BF16), y_ref[...], NN)
    gt = gt_ref[...].reshape(-1, D_MODEL)
    o_ref[...] = _rms(x1_ref[...] + gt * moe, gf_ref[...])


def _combine(yslots, lp, w, x1, mod3, mod2, g_final, tm, slot_block):
    t = x1.shape[0]
    nt = t // tm
    rows = SLOT_ROWS[tm]
    if mod3 is not None:
        per_b = (t // mod3.shape[0]) // tm
        gspec = pl.BlockSpec((1, 1, D_MODEL), lambda i: (i // per_b, 0, 5))
        mod = mod3
    else:
        gspec = pl.BlockSpec((tm, D_MODEL), lambda i: (i, 5))
        mod = mod2
    return pl.pallas_call(
        _combine_kernel,
        grid=(nt,),
        in_specs=[pl.BlockSpec((rows, D_MODEL), lambda i: (slot_block + i, 0)),
                  pl.BlockSpec((TOP_K, tm), lambda i: (0, i)),
                  pl.BlockSpec((TOP_K, tm), lambda i: (0, i)),
                  pl.BlockSpec((tm, D_MODEL), lambda i: (i, 0)),
                  gspec,
                  pl.BlockSpec((1, D_MODEL), lambda i: (0, 0))],
        out_specs=pl.BlockSpec((tm, D_MODEL), lambda i: (i, 0)),
        out_shape=jax.ShapeDtypeStruct((t, D_MODEL), F32),
        compiler_params=pltpu.CompilerParams(vmem_limit_bytes=V7X_VMEM_LIMIT),
        name="moe_combine",
    )(yslots, lp, w, x1, mod, g_final.reshape(1, -1))


def kernel(x_prompt, x_sample, c_prompt, c_sample, cache_k_win, cache_v_win, state_hgrn, w_ada, b_ada,
           g_mix, g_ffn, w_in, attn_sinks, g_attn_out, hg_lb_logits, g_hg_out, w_out, w_router, b_router,
           w_up, b_up, w_down, b_down, g_final):
    batch, seq, d = x_prompt.shape
    nsamp = x_sample.shape[0]
    win = cache_k_win.shape[2]
    layer = 0

    mod = _modulation(jnp.concatenate([c_prompt, c_sample], axis=0), w_ada[layer], b_ada[layer])
    mod_p = mod[:batch].reshape(batch, 1, 6 * d)
    mod_s = mod[batch:]

    assert (batch * seq, nsamp) == (N_PROMPT_TILES * TOK_TILE, SAMPLE_TILE)
    w_in_b = w_in[layer].astype(BF16)
    w_out_b = w_out[layer].astype(BF16)
    w_router_t = w_router[layer].T
    bg = b_up[layer][:, None, 0::2]
    bl = b_up[layer][:, None, 1::2]
    bd = b_down[layer][:, None, :]

    xp = x_prompt.reshape(batch * seq, d)
    cos_p, sin_p = _rope_tables(np.arange(seq))
    qa, ka, va, qh, fh, ih, gh = _in_projection(xp, mod_p, None, g_mix[layer], w_in_b, cos_p, sin_p, TOK_TILE)
    oa = _attention_prompt(qa, ka, va, attn_sinks[layer], g_attn_out[layer], batch)
    oh, s_prompt = _hgrn_prompt(qh, fh, ih, gh, hg_lb_logits, g_hg_out[layer], batch)
    x1_p, slots, lp_p, cw_p, pc_p = _out_projection(
        xp, oa, oh, mod_p, None, g_ffn[layer], w_out_b, w_router_t, b_router[layer], TOK_TILE, None, 0)
    k_win_p = ka.reshape(batch, seq, ATTN_KV_HEADS, HEAD_DIM)[:, seq - win:]
    v_win_p = va.reshape(batch, seq, ATTN_KV_HEADS, HEAD_DIM)[:, seq - win:]

    xs = x_sample.reshape(nsamp, d)
    cos_s, sin_s = _rope_tables(np.full((nsamp,), PAST_LEN))
    qa, ka, va, qh, fh, ih, gh = _in_projection(xs, None, mod_s, g_mix[layer], w_in_b, cos_s, sin_s, nsamp)
    oa, k_win_s, v_win_s = _attention_sample(
        qa, ka, va, cache_k_win[layer].reshape(nsamp, win, KV_WIDTH),
        cache_v_win[layer].reshape(nsamp, win, KV_WIDTH), attn_sinks[layer], g_attn_out[layer])
    oh, s_sample = _hgrn_sample(qh, fh, ih, gh, hg_lb_logits, g_hg_out[layer], state_hgrn[layer])
    sample_block = TILE_PIECE_BASE[-1] * PIECE // SLOT_ROWS[SAMPLE_TILE]
    x1_s, slots, lp_s, cw_s, pc_s = _out_projection(
        xs, oa, oh, None, mod_s, g_ffn[layer], w_out_b, w_router_t, b_router[layer], SAMPLE_TILE,
        slots, sample_block)

    tables = _piece_tables(jnp.concatenate([pc_p[:, :, 0], pc_s[:, :, 0]], axis=0))
    yslots = _moe_grouped(slots, tables, w_up[layer], w_down[layer], bg, bl, bd)
    y_prompt = _combine(yslots, lp_p, cw_p, x1_p, mod_p, None, g_final, TOK_TILE, 0)
    y_sample = _combine(yslots, lp_s, cw_s, x1_s, None, mod_s, g_final, SAMPLE_TILE, sample_block)

    kv_shape = (1, nsamp, win, ATTN_KV_HEADS, HEAD_DIM)
    return (y_prompt.reshape(batch, seq, d), y_sample.reshape(nsamp, 1, d),
            k_win_p[None], v_win_p[None], s_prompt[None],
            k_win_s.reshape(kv_shape), v_win_s.reshape(kv_shape), s_sample[None])
```

```python
import functools

import numpy as np
import jax
import jax.numpy as jnp
from jax import lax
from jax.experimental import pallas as pl
from jax.experimental.pallas import tpu as pltpu

F32 = jnp.float32
BF16 = jnp.bfloat16

D_MODEL = 1024
SEQ = 2048
PAST_LEN = 16384
ATTN_HEADS = 8
ATTN_KV_HEADS = 2
HEAD_DIM = 64
ATTN_GROUP = ATTN_HEADS // ATTN_KV_HEADS
ATTN_WIDTH = ATTN_HEADS * HEAD_DIM
KV_WIDTH = ATTN_KV_HEADS * HEAD_DIM
WINDOW = 128
ATTN_STEP_BLOCKS = 2
ROT_DIM = HEAD_DIM // 4
ROPE_THETA = 500000.0
HG_HEADS = 4
HG_DK = 128
HG_DV = 128
HG_WIDTH = HG_HEADS * HG_DV
HG_CHUNK = 64
HG_SUB = 8
HG_STEP_CHUNKS = 2
IN_COLS = ATTN_WIDTH + 2 * KV_WIDTH + 4 * HG_WIDTH
N_EXPERTS = 32
TOP_K = 4
D_FF = D_MODEL
SWIGLU_ALPHA = 1.702
SWIGLU_LIMIT = 7.0
EPS = 1e-5

V7X_VMEM_LIMIT = 56 * 1024 * 1024

PIECE = 16
TOK_TILE = 512
SAMPLE_TILE = 128
N_PROMPT_TILES = 32
SLOT_ROWS = {TOK_TILE: 2560, SAMPLE_TILE: 1024}
TILE_PIECE_CAP = [SLOT_ROWS[TOK_TILE] // PIECE] * N_PROMPT_TILES + [SLOT_ROWS[SAMPLE_TILE] // PIECE]
TILE_PIECE_BASE = [i * TILE_PIECE_CAP[0] for i in range(N_PROMPT_TILES + 1)]
N_PIECES = sum(TILE_PIECE_CAP)
STEP_PIECES = 32
HALF_PIECES = 16
MOE_MAX_STEPS = N_PIECES // STEP_PIECES + N_EXPERTS + 1 + 2

NN = (((1,), (0,)), ((), ()))
NT = (((1,), (1,)), ((), ()))
TN = (((0,), (0,)), ((), ()))


def _dg(a, b, dims):
    return lax.dot_general(a, b, dims, preferred_element_type=F32)


def _split3(x):
    h = x.astype(BF16)
    r = x - h.astype(F32)
    m = r.astype(BF16)
    l = (r - m.astype(F32)).astype(BF16)
    return h, m, l


def _dot_f32(a, b, dims):
    ah, am, al = _split3(a)
    bh, bm, bl = _split3(b)
    return (_dg(ah, bh, dims) + (_dg(ah, bm, dims) + _dg(am, bh, dims))
            + (_dg(am, bm, dims) + _dg(ah, bl, dims) + _dg(al, bh, dims)))


def _dot_exact_lhs(a_bf16, b, dims):
    bh, bm, bl = _split3(b)
    return _dg(a_bf16, bh, dims) + _dg(a_bf16, bm, dims) + _dg(a_bf16, bl, dims)


def _sigmoid(x):
    return 1.0 / (1.0 + jnp.exp(-x))


def _rms(x, g):
    return x * lax.rsqrt(jnp.mean(x * x, axis=-1, keepdims=True) + EPS) * g


def _mod_kernel(c_ref, w_ref, b_ref, o_ref):
    c = c_ref[...]
    o_ref[...] = _dot_f32(c * _sigmoid(c), w_ref[...], NN) + b_ref[...]


def _modulation(c_all, w_ada, b_ada):
    n = c_all.shape[0]
    return pl.pallas_call(
        _mod_kernel,
        grid=(6,),
        in_specs=[pl.BlockSpec((n, D_MODEL), lambda j: (0, 0)),
                  pl.BlockSpec((D_MODEL, D_MODEL), lambda j: (0, j)),
                  pl.BlockSpec((1, D_MODEL), lambda j: (0, j))],
        out_specs=pl.BlockSpec((n, D_MODEL), lambda j: (0, j)),
        out_shape=jax.ShapeDtypeStruct((n, 6 * D_MODEL), F32),
        compiler_params=pltpu.CompilerParams(vmem_limit_bytes=V7X_VMEM_LIMIT),
        name="adaln_mod",
    )(c_all, w_ada, b_ada.reshape(1, -1))


def _rotate(x, cos_t, sin_t):
    width = x.shape[-1]
    d = lax.broadcasted_iota(jnp.int32, x.shape, 1) % HEAD_DIM
    half = ROT_DIM // 2
    partner = jnp.where(d < half, pltpu.roll(x, width - half, axis=1), pltpu.roll(x, half, axis=1))
    return x * cos_t + partner * sin_t


def _inproj_kernel(x_ref, sh_ref, sc_ref, g_ref, w_ref, cos_ref, sin_ref,
                   qa_ref, ka_ref, va_ref, qh_ref, fh_ref, ih_ref, gh_ref):
    x = x_ref[...]
    sh = sh_ref[...].reshape(-1, D_MODEL)
    sc = sc_ref[...].reshape(-1, D_MODEL)
    h = _rms(x, g_ref[...]) * (1.0 + sc) + sh
    z = _dg(h.astype(BF16), w_ref[...], NN)
    cos_k = cos_ref[...]
    sin_k = sin_ref[...]
    cos_q = jnp.concatenate([cos_k] * ATTN_GROUP, axis=1)
    sin_q = jnp.concatenate([sin_k] * ATTN_GROUP, axis=1)
    o = 0
    qa = _rotate(z[:, o:o + ATTN_WIDTH], cos_q, sin_q)
    qa_ref[...] = (qa * (HEAD_DIM ** -0.5)).astype(BF16)
    o += ATTN_WIDTH
    ka_ref[...] = _rotate(z[:, o:o + KV_WIDTH], cos_k, sin_k)
    o += KV_WIDTH
    va_ref[...] = z[:, o:o + KV_WIDTH]
    o += KV_WIDTH
    qh_ref[...] = z[:, o:o + HG_WIDTH]
    o += HG_WIDTH
    fh_ref[...] = z[:, o:o + HG_WIDTH]
    o += HG_WIDTH
    ih_ref[...] = z[:, o:o + HG_WIDTH].astype(BF16)
    o += HG_WIDTH
    gh_ref[...] = z[:, o:o + HG_WIDTH]


def _rope_tables(positions):
    half = ROT_DIM // 2
    inv = ROPE_THETA ** (-(np.arange(half, dtype=np.float64) * 2.0 / ROT_DIM))
    ang = np.asarray(positions, np.float64)[:, None] * inv[None, :]
    cos_h = np.ones((len(positions), HEAD_DIM))
    sin_h = np.zeros((len(positions), HEAD_DIM))
    cos_h[:, :half] = np.cos(ang)
    cos_h[:, half:ROT_DIM] = np.cos(ang)
    sin_h[:, :half] = -np.sin(ang)
    sin_h[:, half:ROT_DIM] = np.sin(ang)
    cos_t = np.tile(cos_h, (1, ATTN_KV_HEADS)).astype(np.float32)
    sin_t = np.tile(sin_h, (1, ATTN_KV_HEADS)).astype(np.float32)
    return jnp.asarray(cos_t), jnp.asarray(sin_t)


def _in_projection(x, mod3, mod2, g_mix, w_in_bf16, cos_t, sin_t, tm):
    t = x.shape[0]
    nt = t // tm
    if mod3 is not None:
        per_b = (t // mod3.shape[0]) // tm
        sh_spec = pl.BlockSpec((1, 1, D_MODEL), lambda i: (i // per_b, 0, 0))
        sc_spec = pl.BlockSpec((1, 1, D_MODEL), lambda i: (i // per_b, 0, 1))
        mod = mod3
        ncs = cos_t.shape[0] // tm
        cs_spec = pl.BlockSpec((tm, KV_WIDTH), lambda i: (i % ncs, 0))
    else:
        sh_spec = pl.BlockSpec((tm, D_MODEL), lambda i: (i, 0))
        sc_spec = pl.BlockSpec((tm, D_MODEL), lambda i: (i, 1))
        mod = mod2
        cs_spec = pl.BlockSpec((tm, KV_WIDTH), lambda i: (i, 0))
    row = lambda w: pl.BlockSpec((tm, w), lambda i: (i, 0))
    return pl.pallas_call(
        _inproj_kernel,
        grid=(nt,),
        in_specs=[row(D_MODEL), sh_spec, sc_spec,
                  pl.BlockSpec((1, D_MODEL), lambda i: (0, 0)),
                  pl.BlockSpec((D_MODEL, IN_COLS), lambda i: (0, 0)),
                  cs_spec, cs_spec],
        out_specs=[row(ATTN_WIDTH), row(KV_WIDTH), row(KV_WIDTH),
                   row(HG_WIDTH), row(HG_WIDTH), row(HG_WIDTH), row(HG_WIDTH)],
        out_shape=[jax.ShapeDtypeStruct((t, ATTN_WIDTH), BF16),
                   jax.ShapeDtypeStruct((t, KV_WIDTH), F32),
                   jax.ShapeDtypeStruct((t, KV_WIDTH), F32),
                   jax.ShapeDtypeStruct((t, HG_WIDTH), F32),
                   jax.ShapeDtypeStruct((t, HG_WIDTH), F32),
                   jax.ShapeDtypeStruct((t, HG_WIDTH), BF16),
                   jax.ShapeDtypeStruct((t, HG_WIDTH), F32)],
        compiler_params=pltpu.CompilerParams(vmem_limit_bytes=V7X_VMEM_LIMIT),
        name="in_proj",
    )(x, mod, mod, g_mix.reshape(1, -1), w_in_bf16, cos_t, sin_t)


def _attn_prompt_kernel(sink_ref, q_ref, kc_ref, kp_ref, vc_ref, vp_ref, g_ref, o_ref):
    n = pl.program_id(1)
    blk = WINDOW
    nblk = q_ref.shape[0] // blk
    pair_w = 2 * HEAD_DIM
    low = lax.broadcasted_iota(jnp.int32, (1, pair_w), 1) < HEAD_DIM
    kall = jnp.concatenate([kp_ref[...], kc_ref[...]], axis=0)
    vall = jnp.concatenate([vp_ref[...], vc_ref[...]], axis=0)
    kroll = pltpu.roll(kall, HEAD_DIM, axis=1)
    vroll = pltpu.roll(vall, HEAD_DIM, axis=1)
    kdup = [jnp.where(low, kall, kroll).astype(BF16), jnp.where(low, kroll, kall).astype(BF16)]
    v_lo = [jnp.where(low, vall, 0.0).astype(BF16), jnp.where(low, vroll, 0.0).astype(BF16)]
    v_hi = [jnp.where(low, 0.0, vroll).astype(BF16), jnp.where(low, 0.0, vall).astype(BF16)]
    qi = lax.broadcasted_iota(jnp.int32, (blk, 2 * blk), 0)
    kj = lax.broadcasted_iota(jnp.int32, (blk, 2 * blk), 1)
    band = (kj >= qi) & (kj <= qi + blk)
    zero = jnp.zeros((), BF16)
    for bi in range(nblk):
        rows = slice(bi * blk, (bi + 1) * blk)
        krows = slice(bi * blk, (bi + 2) * blk)
        ok = band & ((kj >= blk) | (n > 0)) if bi == 0 else band
        pairs = []
        for pair in range(ATTN_HEADS // 2):
            h = (2 * pair) // ATTN_GROUP
            qp = q_ref[rows, pair * pair_w:(pair + 1) * pair_w]
            acc, dens = None, []
            for half in range(2):
                sink = sink_ref[2 * pair + half]
                qm = jnp.where(low if half == 0 else jnp.logical_not(low), qp, zero)
                s = jnp.where(ok, _dg(qm, kdup[h][krows], NT), -jnp.inf)
                m = jnp.maximum(jnp.max(s, axis=-1, keepdims=True), sink)
                p = jnp.exp(s - m)
                dens.append(jnp.sum(p, axis=-1, keepdims=True) + jnp.exp(sink - m))
                pv = _dg(p.astype(BF16), (v_lo if half == 0 else v_hi)[h][krows], NN)
                acc = pv if acc is None else acc + pv
            pairs.append(acc / jnp.where(low, dens[0], dens[1]))
        o_ref[rows, :] = _rms(jnp.concatenate(pairs, axis=1), g_ref[...]).astype(BF16)


def _attention_prompt(qa, ka, va, sinks, g_attn_out, batch):
    t = qa.shape[0]
    nb = t // batch // (WINDOW * ATTN_STEP_BLOCKS)
    cur = lambda w: pl.BlockSpec((WINDOW * ATTN_STEP_BLOCKS, w), lambda b, n: (b * nb + n, 0))
    prev = lambda w: pl.BlockSpec(
        (WINDOW, w), lambda b, n: (ATTN_STEP_BLOCKS * (b * nb + n) - jnp.minimum(n, 1), 0))
    return pl.pallas_call(
        _attn_prompt_kernel,
        grid=(batch, nb),
        in_specs=[pl.BlockSpec(memory_space=pltpu.SMEM),
                  cur(ATTN_WIDTH), cur(KV_WIDTH), prev(KV_WIDTH), cur(KV_WIDTH), prev(KV_WIDTH),
                  pl.BlockSpec((1, ATTN_WIDTH), lambda b, n: (0, 0))],
        out_specs=cur(ATTN_WIDTH),
        out_shape=jax.ShapeDtypeStruct((t, ATTN_WIDTH), BF16),
        name="attn_prompt",
    )(sinks, qa, ka, ka, va, va, g_attn_out.reshape(1, -1))


def _lower_bound(lb_logits_ref):
    lg = lb_logits_ref[...]
    e = jnp.exp(lg - jnp.max(lg, axis=0, keepdims=True))
    return e[0:1] / jnp.sum(e, axis=0, keepdims=True)


def _hgrn_prompt_kernel(q_ref, f_ref, i_ref, g_ref, lbl_ref, gn_ref, o_ref, sfin_ref, st_ref):
    step = pl.program_id(1)
    C = HG_CHUNK
    W = HG_WIDTH
    nsub = C // HG_SUB
    n_chunks = q_ref.shape[0] // C
    heads = [slice(hh * HG_DK, (hh + 1) * HG_DK) for hh in range(HG_HEADS)]

    @pl.when(step == 0)
    def _():
        st_ref[...] = jnp.zeros_like(st_ref)

    lb = _lower_bound(lbl_ref)
    r64 = lax.broadcasted_iota(jnp.int32, (C, C), 0)
    c64 = lax.broadcasted_iota(jnp.int32, (C, C), 1)
    tri = (r64 >= c64).astype(BF16)
    rsub = r64 % HG_SUB
    o_intra, q_state, upd, decay_last = {}, {}, {}, {}
    for ci in range(n_chunks):
        rows = slice(ci * C, (ci + 1) * C)
        f = lb + (1.0 - lb) * _sigmoid(f_ref[rows, :])
        kk = 1.0 - f
        logf = jnp.log(f)
        q = q_ref[rows, :]
        v = i_ref[rows, :]
        b = _dot_exact_lhs(tri, logf, NN)
        prods = [q * kk]
        w = logf
        for d in range(1, HG_SUB):
            if d > 1:
                w = w + pltpu.roll(logf, d - 1, axis=0)
            prods.append(q * pltpu.roll(kk, d, axis=0) * jnp.exp(w))
        ends = [b[j * HG_SUB + HG_SUB - 1:(j + 1) * HG_SUB] for j in range(nsub)]
        kt = kk * jnp.exp(jnp.concatenate([jnp.broadcast_to(e, (HG_SUB, W)) for e in ends], axis=0) - b)
        lhs, rhs = [], []
        for j in range(nsub - 1):
            lo = (j + 1) * HG_SUB
            lhs.append(jnp.concatenate(
                [jnp.zeros((lo, W), F32), q[lo:] * jnp.exp(b[lo:] - ends[j])], axis=0).astype(BF16))
            pieces = [kt[j * HG_SUB:lo]]
            if j > 0:
                pieces.insert(0, jnp.zeros((j * HG_SUB, W), F32))
            pieces.append(jnp.zeros((C - lo, W), F32))
            rhs.append(jnp.concatenate(pieces, axis=0).astype(BF16))
        q_state[ci] = (q * jnp.exp(b)).astype(BF16)
        kd = (kk * jnp.exp(ends[-1] - b)).astype(BF16)
        decay_last[ci] = jnp.exp(ends[-1])
        for hh, sl in enumerate(heads):
            a = jnp.zeros((C, C), F32)
            for d in range(HG_SUB):
                ad = jnp.sum(prods[d][:, sl], axis=-1, keepdims=True)
                a = a + jnp.where((c64 == r64 - d) & (rsub >= d), ad, 0.0)
            a = a + _dg(jnp.concatenate([x[:, sl] for x in lhs], axis=1),
                        jnp.concatenate([x[:, sl] for x in rhs], axis=1), NT)
            o_intra[ci, hh] = _dg(a.astype(BF16), v[:, sl], NN)
            upd[ci, hh] = _dg(v[:, sl], kd[:, sl], TN)
    finals = []
    for hh, sl in enumerate(heads):
        st = st_ref[hh]
        for ci in range(n_chunks):
            rows = slice(ci * C, (ci + 1) * C)
            o = o_intra[ci, hh] + _dg(q_state[ci][:, sl], st.astype(BF16), NT)
            st = st * decay_last[ci][:, sl] + upd[ci, hh]
            g = g_ref[rows, sl]
            o_ref[rows, sl] = (_rms(o, gn_ref[:, sl]) * (g * _sigmoid(g))).astype(BF16)
        st_ref[hh] = st
        finals.append(st)

    @pl.when(step == pl.num_programs(1) - 1)
    def _():
        for hh in range(HG_HEADS):
            sfin_ref[0, hh] = finals[hh].T


def _hgrn_prompt(qh, fh, ih, gh, lb_logits, g_hg_out, batch):
    t = qh.shape[0]
    nc = t // batch // (HG_CHUNK * HG_STEP_CHUNKS)
    blk = pl.BlockSpec((HG_CHUNK * HG_STEP_CHUNKS, HG_WIDTH), lambda b, c: (b * nc + c, 0))
    const = lambda r: pl.BlockSpec((r, HG_WIDTH), lambda b, c: (0, 0))
    return pl.pallas_call(
        _hgrn_prompt_kernel,
        grid=(batch, nc),
        in_specs=[blk, blk, blk, blk, const(lb_logits.shape[0]), const(1)],
        out_specs=[blk, pl.BlockSpec((1, HG_HEADS, HG_DK, HG_DV), lambda b, c: (b, 0, 0, 0))],
        out_shape=[jax.ShapeDtypeStruct((t, HG_WIDTH), BF16),
                   jax.ShapeDtypeStruct((batch, HG_HEADS, HG_DK, HG_DV), F32)],
        scratch_shapes=[pltpu.VMEM((HG_HEADS, HG_DV, HG_DK), F32)],
        name="hgrn_prompt",
    )(qh, fh, ih, gh, lb_logits, g_hg_out.reshape(1, -1))


def _attn_sample_kernel(sink_ref, q_ref, kn_ref, vn_ref, kc_ref, vc_ref, g_ref,
                        o_ref, ko_ref, vo_ref):
    bt = q_ref.shape[0]
    win = kc_ref.shape[1]
    kc = kc_ref[...]
    vc = vc_ref[...]
    kn = kn_ref[...]
    vn = vn_ref[...]
    outs = []
    for h in range(ATTN_KV_HEADS):
        ls = slice(h * HEAD_DIM, (h + 1) * HEAD_DIM)
        q = q_ref[:, h * ATTN_GROUP:(h + 1) * ATTN_GROUP, :]
        s = jnp.einsum('bgd,bjd->bgj', q, kc[:, :, ls].astype(BF16), preferred_element_type=F32)
        s_new = jnp.sum(q.astype(F32) * kn[:, None, ls], axis=-1, keepdims=True)
        gi = lax.broadcasted_iota(jnp.int32, (1, ATTN_GROUP, 1), 1)
        sink = jnp.zeros((1, ATTN_GROUP, 1), F32)
        for g in range(ATTN_GROUP):
            sink = jnp.where(gi == g, sink_ref[h * ATTN_GROUP + g], sink)
        m = jnp.maximum(jnp.maximum(jnp.max(s, axis=-1, keepdims=True), s_new), sink)
        p = jnp.exp(s - m)
        p_new = jnp.exp(s_new - m)
        den = jnp.sum(p, axis=-1, keepdims=True) + p_new + jnp.exp(sink - m)
        o = jnp.einsum('bgj,bjd->bgd', p.astype(BF16), vc[:, :, ls].astype(BF16),
                       preferred_element_type=F32)
        o = (o + p_new * vn[:, None, ls]) / den
        outs.append(o)
    ssq = sum(jnp.sum(jnp.sum(o * o, axis=-1, keepdims=True), axis=1, keepdims=True) for o in outs)
    scale = lax.rsqrt(ssq / ATTN_WIDTH + EPS)
    for h in range(ATTN_KV_HEADS):
        o_ref[h] = (outs[h] * scale * g_ref[h][None]).astype(BF16)
    ri = lax.broadcasted_iota(jnp.int32, (win, KV_WIDTH), 0)
    for b in range(bt):
        ko_ref[b] = jnp.where(ri == win - 1, kn[b:b + 1], pltpu.roll(kc[b], win - 1, axis=0))
        vo_ref[b] = jnp.where(ri == win - 1, vn[b:b + 1], pltpu.roll(vc[b], win - 1, axis=0))


def _attention_sample(qa, ka, va, cache_k, cache_v, sinks, g_attn_out, bt=8):
    nb = qa.shape[0]
    win = cache_k.shape[1]
    q3 = qa.reshape(nb, ATTN_HEADS, HEAD_DIM)
    g3 = g_attn_out.reshape(ATTN_KV_HEADS, ATTN_GROUP, HEAD_DIM)
    row = lambda w: pl.BlockSpec((bt, w), lambda i: (i, 0))
    cache = pl.BlockSpec((bt, win, KV_WIDTH), lambda i: (i, 0, 0))
    o4, k_new, v_new = pl.pallas_call(
        _attn_sample_kernel,
        grid=(nb // bt,),
        in_specs=[pl.BlockSpec(memory_space=pltpu.SMEM),
                  pl.BlockSpec((bt, ATTN_HEADS, HEAD_DIM), lambda i: (i, 0, 0)),
                  row(KV_WIDTH), row(KV_WIDTH), cache, cache,
                  pl.BlockSpec((ATTN_KV_HEADS, ATTN_GROUP, HEAD_DIM), lambda i: (0, 0, 0))],
        out_specs=[pl.BlockSpec((ATTN_KV_HEADS, bt, ATTN_GROUP, HEAD_DIM), lambda i: (0, i, 0, 0)),
                   cache, cache],
        out_shape=[jax.ShapeDtypeStruct((ATTN_KV_HEADS, nb, ATTN_GROUP, HEAD_DIM), BF16),
                   jax.ShapeDtypeStruct(cache_k.shape, F32),
                   jax.ShapeDtypeStruct(cache_v.shape, F32)],
        name="attn_sample",
    )(sinks, q3, ka, va, cache_k, cache_v, g3)
    oa = jnp.transpose(o4, (1, 0, 2, 3)).reshape(nb, ATTN_WIDTH)
    return oa, k_new, v_new


def _hgrn_sample_kernel(q_ref, f_ref, i_ref, g_ref, lbl_ref, gn_ref, s0_ref, o_ref, s_ref):
    bt = q_ref.shape[0]
    lb = _lower_bound(lbl_ref)
    f = lb + (1.0 - lb) * _sigmoid(f_ref[...])
    kk = 1.0 - f
    q = q_ref[...]
    v = i_ref[...].astype(F32)
    g = g_ref[...]
    gate = g * _sigmoid(g)
    r = lax.broadcasted_iota(jnp.int32, (HG_DK, HG_DK), 0)
    cc = lax.broadcasted_iota(jnp.int32, (HG_DK, HG_DK), 1)
    diag = r == cc

    def column(x_row):
        return jnp.sum(jnp.where(diag, x_row, 0.0), axis=1, keepdims=True)

    for b in range(bt):
        parts = []
        for hh in range(HG_HEADS):
            sl = slice(hh * HG_DK, (hh + 1) * HG_DK)
            s_new = column(f[b:b + 1, sl]) * s0_ref[b, hh] + column(kk[b:b + 1, sl]) * v[b:b + 1, sl]
            s_ref[b, hh] = s_new
            o = jnp.sum(s_new * column(q[b:b + 1, sl]), axis=0, keepdims=True)
            parts.append(_rms(o, gn_ref[:, sl]))
        o_ref[b:b + 1, :] = (jnp.concatenate(parts, axis=1) * gate[b:b + 1]).astype(BF16)


def _hgrn_sample(qh, fh, ih, gh, lb_logits, g_hg_out, state, bt=8):
    nb = qh.shape[0]
    row = pl.BlockSpec((bt, HG_WIDTH), lambda i: (i, 0))
    const = lambda r: pl.BlockSpec((r, HG_WIDTH), lambda i: (0, 0))
    st = pl.BlockSpec((bt, HG_HEADS, HG_DK, HG_DV), lambda i: (i, 0, 0, 0))
    return pl.pallas_call(
        _hgrn_sample_kernel,
        grid=(nb // bt,),
        in_specs=[row, row, row, row, const(lb_logits.shape[0]), const(1), st],
        out_specs=[row, st],
        out_shape=[jax.ShapeDtypeStruct((nb, HG_WIDTH), BF16),
                   jax.ShapeDtypeStruct(state.shape, F32)],
        name="hgrn_sample",
    )(qh, fh, ih, gh, lb_logits, g_hg_out.reshape(1, -1), state)


def _outproj_kernel(*refs, n_tiles):
    xs_ref = refs[-4]

    @pl.when(pl.program_id(0) < n_tiles)
    def _():
        _outproj_tile(*refs)

    @pl.when(pl.program_id(0) >= n_tiles)
    def _():
        xs_ref[...] = jnp.zeros_like(xs_ref)


def _outproj_tile(x_ref, oa_ref, oh_ref, gt_ref, sh_ref, sc_ref, g_ref, wo_ref, wr_ref, br_ref, *rest):
    x1_ref, xs_ref, lp_ref, w_ref, pc_ref = rest[-5:]
    tm = x_ref.shape[0]
    rows = xs_ref.shape[0]
    gt = gt_ref[...].reshape(-1, D_MODEL)
    sh = sh_ref[...].reshape(-1, D_MODEL)
    sc = sc_ref[...].reshape(-1, D_MODEL)
    mix = _dg(oa_ref[...], wo_ref[0:ATTN_WIDTH, :], NN) + _dg(oh_ref[...], wo_ref[ATTN_WIDTH:, :], NN)
    x1 = x_ref[...] + gt * mix
    x1_ref[...] = x1
    h2 = _rms(x1, g_ref[...]) * (1.0 + sc) + sh
    logits = _dot_f32(wr_ref[...], h2, NT) + br_ref[...]
    ei = lax.broadcasted_iota(jnp.int32, logits.shape, 0)
    vals, sel = [], []
    l = logits
    for _ in range(TOP_K):
        m = jnp.max(l, axis=0, keepdims=True)
        idx = jnp.min(jnp.where(l == m, ei, N_EXPERTS), axis=0, keepdims=True)
        pick = ei == idx
        vals.append(m)
        sel.append(pick)
        l = jnp.where(pick, -jnp.inf, l)
    ex = [jnp.exp(v - vals[0]) for v in vals]
    den = ex[0] + ex[1] + ex[2] + ex[3]
    for k in range(TOP_K):
        w_ref[k:k + 1, :] = ex[k] / den

    member = jnp.where(sel[0] | sel[1] | sel[2] | sel[3], 1.0, 0.0)
    tr = lax.broadcasted_iota(jnp.int32, (tm, tm), 0)
    tc = lax.broadcasted_iota(jnp.int32, (tm, tm), 1)
    rank = _dg(member.astype(BF16), (tr < tc).astype(BF16), NN)
    pieces = jnp.floor((jnp.sum(member, axis=1, keepdims=True) + (PIECE - 1)) * (1.0 / PIECE))
    er = lax.broadcasted_iota(jnp.int32, (N_EXPERTS, N_EXPERTS), 0)
    ec = lax.broadcasted_iota(jnp.int32, (N_EXPERTS, N_EXPERTS), 1)
    pieces_b = jnp.broadcast_to(pieces, (N_EXPERTS, 128))
    pc_ref[0] = pieces_b.astype(jnp.int32)
    base = _dg((ec < er).astype(BF16), pieces_b.astype(BF16), NN)[:, 0:1] * PIECE
    slot = base + rank
    lps = [jnp.sum(jnp.where(sel[k], slot, 0.0), axis=0, keepdims=True).astype(jnp.int32)
           for k in range(TOP_K)]
    for k in range(TOP_K):
        lp_ref[k:k + 1, :] = lps[k]
    hb = h2.astype(BF16)
    chunk = 512
    for r0 in range(0, rows, chunk):
        si = lax.broadcasted_iota(jnp.int32, (chunk, tm), 0) + r0
        hit = (si == lps[0]) | (si == lps[1]) | (si == lps[2]) | (si == lps[3])
        xs_ref[r0:r0 + chunk, :] = _dg(jnp.where(hit, 1.0, 0.0).astype(BF16), hb, NN).astype(BF16)


def _out_projection(x, oa, oh, mod3, mod2, g_ffn, w_out_bf16, w_router_t, b_router, tm, slots, slot_block):
    t = x.shape[0]
    nt = t // tm
    rows = SLOT_ROWS[tm]
    real = lambda i: jnp.minimum(i, nt - 1)
    if mod3 is not None:
        per_b = (t // mod3.shape[0]) // tm
        mspec = lambda j: pl.BlockSpec((1, 1, D_MODEL), lambda i: (real(i) // per_b, 0, j))
        mod = mod3
    else:
        mspec = lambda j: pl.BlockSpec((tm, D_MODEL), lambda i: (real(i), j))
        mod = mod2
    row = lambda w: pl.BlockSpec((tm, w), lambda i: (real(i), 0))
    full = lambda a: pl.BlockSpec(a.shape, lambda i: (0,) * a.ndim)
    g2 = g_ffn.reshape(1, -1)
    br = b_router.reshape(-1, 1)
    args = [x, oa, oh, mod, mod, mod, g2, w_out_bf16, w_router_t, br]
    in_specs = [row(D_MODEL), row(ATTN_WIDTH), row(HG_WIDTH), mspec(2), mspec(3), mspec(4),
                full(g2), full(w_out_bf16), full(w_router_t), full(br)]
    aliases = {}
    n_fill = 0
    if slots is not None:
        args.append(slots)
        in_specs.append(pl.BlockSpec(memory_space=pl.ANY))
        aliases = {len(args) - 1: 1}
    else:
        n_fill = pl.cdiv(N_PIECES * PIECE - nt * rows, rows)
    return pl.pallas_call(
        functools.partial(_outproj_kernel, n_tiles=nt),
        grid=(nt + n_fill,),
        in_specs=in_specs,
        out_specs=[row(D_MODEL),
                   pl.BlockSpec((rows, D_MODEL), lambda i: (slot_block + i, 0)),
                   pl.BlockSpec((TOP_K, tm), lambda i: (0, real(i))),
                   pl.BlockSpec((TOP_K, tm), lambda i: (0, real(i))),
                   pl.BlockSpec((1, N_EXPERTS, 128), lambda i: (real(i), 0, 0))],
        out_shape=[jax.ShapeDtypeStruct((t, D_MODEL), F32),
                   jax.ShapeDtypeStruct((N_PIECES * PIECE, D_MODEL), BF16),
                   jax.ShapeDtypeStruct((TOP_K, t), jnp.int32),
                   jax.ShapeDtypeStruct((TOP_K, t), F32),
                   jax.ShapeDtypeStruct((nt, N_EXPERTS, 128), jnp.int32)],
        input_output_aliases=aliases,
        compiler_params=pltpu.CompilerParams(vmem_limit_bytes=V7X_VMEM_LIMIT),
        name="out_proj_router",
    )(*args)


def _piece_tables(pieces_ie):
    cap = jnp.asarray(TILE_PIECE_CAP, jnp.int32)
    gbase = jnp.asarray(TILE_PIECE_BASE, jnp.int32)
    seg_src = gbase[:, None] + jnp.cumsum(pieces_ie, axis=1) - pieces_ie
    used_i = jnp.sum(pieces_ie, axis=1)
    tail_i = cap - used_i
    np_e = jnp.sum(pieces_ie, axis=0)
    rem = np_e % STEP_PIECES
    head_e = jnp.where(rem > 0, rem, jnp.minimum(np_e, STEP_PIECES))
    nt_e = np_e // STEP_PIECES + (rem > 0)
    tile_end = jnp.cumsum(nt_e)
    tile_start = tile_end - nt_e
    n_comp = tile_end[-1]
    q_start_e = jnp.cumsum(np_e) - np_e
    n_used = jnp.sum(np_e)
    tt = jnp.arange(MOE_MAX_STEPS, dtype=jnp.int32)
    e_t = jnp.minimum(jnp.sum(tt[:, None] >= tile_end[None, :], axis=1), N_EXPERTS - 1).astype(jnp.int32)
    is_comp = tt < n_comp
    k = tt - tile_start[e_t]
    q0_comp = q_start_e[e_t] + jnp.where(k == 0, 0, head_e[e_t] + STEP_PIECES * (k - 1))
    live_comp = jnp.where(k == 0, head_e[e_t], STEP_PIECES)
    q0_fill = n_used + STEP_PIECES * (tt - n_comp)
    q0 = jnp.where(is_comp, q0_comp, q0_fill)
    live = jnp.where(is_comp, live_comp, jnp.clip(N_PIECES - q0_fill, 0, STEP_PIECES))
    n_busy = n_comp + (N_PIECES - n_used + STEP_PIECES - 1) // STEP_PIECES
    length = jnp.concatenate([pieces_ie.T.reshape(-1), tail_i])
    src = jnp.concatenate([seg_src.T.reshape(-1), gbase + used_i])
    dest = jnp.cumsum(length) - length
    delta = src - dest
    d_delta = delta - jnp.concatenate([jnp.zeros((1,), jnp.int32), delta[:-1]])
    lane = jnp.arange(STEP_PIECES, dtype=jnp.int32)
    qq = (q0[:, None] + lane[None, :]).reshape(-1)
    ok = (lane[None, :] < live[:, None]).reshape(-1)
    piece = qq + jnp.sum(jnp.where(dest[None, :] <= qq[:, None], d_delta[None, :], 0), axis=1)
    dump = (N_PIECES + (tt % 2)[:, None] * STEP_PIECES + lane[None, :]).reshape(-1)
    tbl_out = jnp.where(ok, piece, dump).astype(jnp.int32)
    last = jnp.maximum(n_comp - 1, 0)
    tbl_in = jnp.where(ok & jnp.repeat(is_comp, STEP_PIECES), piece, 0).reshape(MOE_MAX_STEPS, STEP_PIECES)
    tbl_in = jnp.where(is_comp[:, None], tbl_in, tbl_in[last][None, :]).reshape(-1).astype(jnp.int32)
    texp = jnp.where(is_comp, e_t, e_t[last]).astype(jnp.int32)
    first = (is_comp & (k == 0)).astype(jnp.int32)
    kind = jnp.where(is_comp, jnp.where(live <= HALF_PIECES, 1, 2), jnp.where(tt < n_busy, 0, 3))
    return tbl_in, tbl_out, texp, kind.astype(jnp.int32), first


def _moe_grouped_kernel(tin_ref, tout_ref, texp_ref, kind_ref, first_ref,
                        x_hbm, wu_ref, wd_ref, bg_ref, bl_ref, bd_ref, y_hbm,
                        wg_s, wl_s, wd_s, xbuf, ybuf, xsems, ysems):
    npc = STEP_PIECES
    t = pl.program_id(0)
    nsteps = pl.num_programs(0)
    kind = kind_ref[t]
    slot = t % 2

    def x_copy(step, buf, p):
        src = tin_ref[step * npc + p]
        return pltpu.make_async_copy(
            x_hbm.at[pl.ds(pl.multiple_of(src * PIECE, PIECE), PIECE), :],
            xbuf.at[buf, pl.ds(p * PIECE, PIECE), :],
            xsems.at[buf, p])

    def x_each(step, buf, fn):
        k = kind_ref[step]

        @pl.when(k == 2)
        def _():
            for p in range(STEP_PIECES):
                fn(x_copy(step, buf, p))

        @pl.when(k == 1)
        def _():
            for p in range(HALF_PIECES):
                fn(x_copy(step, buf, p))

    def y_copy(step, buf, p):
        dst = tout_ref[step * npc + p]
        return pltpu.make_async_copy(
            ybuf.at[buf, pl.ds(p * PIECE, PIECE), :],
            y_hbm.at[pl.ds(pl.multiple_of(dst * PIECE, PIECE), PIECE), :],
            ysems.at[buf, p])

    def y_wait(step, buf):
        for p in range(npc):
            y_copy(step, buf, p).wait()

    @pl.when(t == 0)
    def _():
        ybuf[...] = jnp.zeros_like(ybuf)
        x_each(0, 0, lambda cp: cp.start())

    @pl.when(t >= 2)
    def _():
        y_wait(t - 2, slot)

    @pl.when(t + 1 < nsteps)
    def _():
        x_each(t + 1, 1 - slot, lambda cp: cp.start())

    @pl.when((kind > 0) & (kind < 3) & (first_ref[t] == 1))
    def _():
        cb = 256
        r = lax.broadcasted_iota(jnp.int32, (cb, cb), 0)
        c = lax.broadcasted_iota(jnp.int32, (cb, cb), 1)
        perm = jnp.where(r == jnp.where(c < cb // 2, 2 * c, 2 * (c - cb // 2) + 1), 1.0, 0.0).astype(BF16)
        for blk in range(2 * D_FF // cb):
            wp = _dg(wu_ref[0, :, blk * cb:(blk + 1) * cb].astype(BF16), perm, NN).astype(BF16)
            wg_s[:, blk * (cb // 2):(blk + 1) * (cb // 2)] = wp[:, :cb // 2]
            wl_s[:, blk * (cb // 2):(blk + 1) * (cb // 2)] = wp[:, cb // 2:]
        wd_s[...] = wd_ref[0].astype(BF16)

    x_each(t, slot, lambda cp: cp.wait())

    def expert(n_pieces):
        xb = xbuf[slot, 0:n_pieces * PIECE, :]
        glu = jnp.minimum(_dg(xb, wg_s[...], NN) + bg_ref[0], SWIGLU_LIMIT)
        lin = jnp.clip(_dg(xb, wl_s[...], NN) + bl_ref[0], -SWIGLU_LIMIT, SWIGLU_LIMIT)
        act = glu * _sigmoid(SWIGLU_ALPHA * glu) * (lin + 1.0)
        y = (_dg(act.astype(BF16), wd_s[...], NN) + bd_ref[0]).astype(BF16)
        ybuf[slot, 0:n_pieces * PIECE, :] = y

    @pl.when(kind == 2)
    def _():
        expert(STEP_PIECES)

    @pl.when(kind == 1)
    def _():
        expert(HALF_PIECES)

    @pl.when(kind == 0)
    def _():
        ybuf[slot] = jnp.zeros((npc * PIECE, D_MODEL), BF16)

    for p in range(npc):
        y_copy(t, slot, p).start()

    @pl.when(t == nsteps - 1)
    def _():
        y_wait(t - 1, 1 - slot)
        y_wait(t, slot)


def _moe_grouped(slots, tables, w_up, w_down, bg, bl, bd):
    npc = STEP_PIECES
    by_expert = lambda shape: pl.BlockSpec(
        shape, lambda t, tin, tout, texp, kind, first: (texp[t],) + (0,) * (len(shape) - 1))
    return pl.pallas_call(
        _moe_grouped_kernel,
        grid_spec=pltpu.PrefetchScalarGridSpec(
            num_scalar_prefetch=5,
            grid=(MOE_MAX_STEPS,),
            in_specs=[pl.BlockSpec(memory_space=pl.ANY),
                      by_expert((1, D_MODEL, 2 * D_FF)), by_expert((1, D_FF, D_MODEL)),
                      by_expert((1, 1, D_FF)), by_expert((1, 1, D_FF)), by_expert((1, 1, D_MODEL))],
            out_specs=pl.BlockSpec(memory_space=pl.ANY),
            scratch_shapes=[pltpu.VMEM((D_MODEL, D_FF), BF16), pltpu.VMEM((D_MODEL, D_FF), BF16),
                            pltpu.VMEM((D_FF, D_MODEL), BF16),
                            pltpu.VMEM((2, npc * PIECE, D_MODEL), BF16),
                            pltpu.VMEM((2, npc * PIECE, D_MODEL), BF16),
                            pltpu.SemaphoreType.DMA((2, npc)),
                            pltpu.SemaphoreType.DMA((2, npc))]),
        out_shape=jax.ShapeDtypeStruct(((N_PIECES + 2 * npc) * PIECE, D_MODEL), BF16),
        compiler_params=pltpu.CompilerParams(vmem_limit_bytes=V7X_VMEM_LIMIT),
        name="moe_grouped",
    )(*tables, slots, w_up, w_down, bg, bl, bd)


def _combine_kernel(y_ref, lp_ref, w_ref, x1_ref, gt_ref, gf_ref, o_ref):
    tm = x1_ref.shape[0]
    rows = y_ref.shape[0]
    r = lax.broadcasted_iota(jnp.int32, (tm, tm), 0)
    c = lax.broadcasted_iota(jnp.int32, (tm, tm), 1)
    diag = r == c

    def column(x_row):
        return jnp.sum(jnp.where(diag, x_row, 0.0), axis=1, keepdims=True)

    si = lax.broadcasted_iota(jnp.int32, (tm, rows), 1)
    wt = jnp.zeros((tm, rows), F32)
    for k in range(TOP_K):
        pos = column(lp_ref[k:k + 1, :].astype(F32)).astype(jnp.int32)
        wt = jnp.where(si == pos, column(w_ref[k:k + 1, :]), wt)
    moe = _dg(wt.astype(BF16), y_ref[...], NN)
    gt = gt_ref[...].reshape(-1, D_MODEL)
    o_ref[...] = _rms(x1_ref[...] + gt * moe, gf_ref[...])


def _combine(yslots, lp, w, x1, mod3, mod2, g_final, tm, slot_block):
    t = x1.shape[0]
    nt = t // tm
    rows = SLOT_ROWS[tm]
    if mod3 is not None:
        per_b = (t // mod3.shape[0]) // tm
        gspec = pl.BlockSpec((1, 1, D_MODEL), lambda i: (i // per_b, 0, 5))
        mod = mod3
    else:
        gspec = pl.BlockSpec((tm, D_MODEL), lambda i: (i, 5))
        mod = mod2
    return pl.pallas_call(
        _combine_kernel,
        grid=(nt,),
        in_specs=[pl.BlockSpec((rows, D_MODEL), lambda i: (slot_block + i, 0)),
                  pl.BlockSpec((TOP_K, tm), lambda i: (0, i)),
                  pl.BlockSpec((TOP_K, tm), lambda i: (0, i)),
                  pl.BlockSpec((tm, D_MODEL), lambda i: (i, 0)),
                  gspec,
                  pl.BlockSpec((1, D_MODEL), lambda i: (0, 0))],
        out_specs=pl.BlockSpec((tm, D_MODEL), lambda i: (i, 0)),
        out_shape=jax.ShapeDtypeStruct((t, D_MODEL), F32),
        compiler_params=pltpu.CompilerParams(vmem_limit_bytes=V7X_VMEM_LIMIT),
        name="moe_combine",
    )(yslots, lp, w, x1, mod, g_final.reshape(1, -1))


def kernel(x_prompt, x_sample, c_prompt, c_sample, cache_k_win, cache_v_win, state_hgrn, w_ada, b_ada,
           g_mix, g_ffn, w_in, attn_sinks, g_attn_out, hg_lb_logits, g_hg_out, w_out, w_router, b_router,
           w_up, b_up, w_down, b_down, g_final):
    batch, seq, d = x_prompt.shape
    nsamp = x_sample.shape[0]
    win = cache_k_win.shape[2]
    layer = 0

    mod = _modulation(jnp.concatenate([c_prompt, c_sample], axis=0), w_ada[layer], b_ada[layer])
    mod_p = mod[:batch].reshape(batch, 1, 6 * d)
    mod_s = mod[batch:]

    assert (batch * seq, nsamp) == (N_PROMPT_TILES * TOK_TILE, SAMPLE_TILE)
    w_in_b = w_in[layer].astype(BF16)
    w_out_b = w_out[layer].astype(BF16)
    w_router_t = w_router[layer].T
    bg = b_up[layer][:, None, 0::2]
    bl = b_up[layer][:, None, 1::2]
    bd = b_down[layer][:, None, :]

    xp = x_prompt.reshape(batch * seq, d)
    cos_p, sin_p = _rope_tables(np.arange(seq))
    qa, ka, va, qh, fh, ih, gh = _in_projection(xp, mod_p, None, g_mix[layer], w_in_b, cos_p, sin_p, TOK_TILE)
    oa = _attention_prompt(qa, ka, va, attn_sinks[layer], g_attn_out[layer], batch)
    oh, s_prompt = _hgrn_prompt(qh, fh, ih, gh, hg_lb_logits, g_hg_out[layer], batch)
    x1_p, slots, lp_p, cw_p, pc_p = _out_projection(
        xp, oa, oh, mod_p, None, g_ffn[layer], w_out_b, w_router_t, b_router[layer], TOK_TILE, None, 0)
    k_win_p = ka.reshape(batch, seq, ATTN_KV_HEADS, HEAD_DIM)[:, seq - win:]
    v_win_p = va.reshape(batch, seq, ATTN_KV_HEADS, HEAD_DIM)[:, seq - win:]

    xs = x_sample.reshape(nsamp, d)
    cos_s, sin_s = _rope_tables(np.full((nsamp,), PAST_LEN))
    qa, ka, va, qh, fh, ih, gh = _in_projection(xs, None, mod_s, g_mix[layer], w_in_b, cos_s, sin_s, nsamp)
    oa, k_win_s, v_win_s = _attention_sample(
        qa, ka, va, cache_k_win[layer].reshape(nsamp, win, KV_WIDTH),
        cache_v_win[layer].reshape(nsamp, win, KV_WIDTH), attn_sinks[layer], g_attn_out[layer])
    oh, s_sample = _hgrn_sample(qh, fh, ih, gh, hg_lb_logits, g_hg_out[layer], state_hgrn[layer])
    sample_block = TILE_PIECE_BASE[-1] * PIECE // SLOT_ROWS[SAMPLE_TILE]
    x1_s, slots, lp_s, cw_s, pc_s = _out_projection(
        xs, oa, oh, None, mod_s, g_ffn[layer], w_out_b, w_router_t, b_router[layer], SAMPLE_TILE,
        slots, sample_block)

    tables = _piece_tables(jnp.concatenate([pc_p[:, :, 0], pc_s[:, :, 0]], axis=0))
    yslots = _moe_grouped(slots, tables, w_up[layer], w_down[layer], bg, bl, bd)
    y_prompt = _combine(yslots, lp_p, cw_p, x1_p, mod_p, None, g_final, TOK_TILE, 0)
    y_sample = _combine(yslots, lp_s, cw_s, x1_s, None, mod_s, g_final, SAMPLE_TILE, sample_block)

    kv_shape = (1, nsamp, win, ATTN_KV_HEADS, HEAD_DIM)
    return (y_prompt.reshape(batch, seq, d), y_sample.reshape(nsamp, 1, d),
            k_win_p[None], v_win_p[None], s_prompt[None],
            k_win_s.reshape(kv_shape), v_win_s.reshape(kv_shape), s_sample[None])
```

```python
import functools

import numpy as np
import jax
import jax.numpy as jnp
from jax import lax
from jax.experimental import pallas as pl
from jax.experimental.pallas import tpu as pltpu

F32 = jnp.float32
BF16 = jnp.bfloat16

D_MODEL = 1024
SEQ = 2048
PAST_LEN = 16384
ATTN_HEADS = 8
ATTN_KV_HEADS = 2
HEAD_DIM = 64
ATTN_GROUP = ATTN_HEADS // ATTN_KV_HEADS
ATTN_WIDTH = ATTN_HEADS * HEAD_DIM
KV_WIDTH = ATTN_KV_HEADS * HEAD_DIM
WINDOW = 128
ATTN_STEP_BLOCKS = 2
ROT_DIM = HEAD_DIM // 4
ROPE_THETA = 500000.0
HG_HEADS = 4
HG_DK = 128
HG_DV = 128
HG_WIDTH = HG_HEADS * HG_DV
HG_CHUNK = 64
HG_SUB = 8
HG_STEP_CHUNKS = 2
IN_COLS = ATTN_WIDTH + 2 * KV_WIDTH + 4 * HG_WIDTH
N_EXPERTS = 32
TOP_K = 4
D_FF = D_MODEL
SWIGLU_ALPHA = 1.702
SWIGLU_LIMIT = 7.0
EPS = 1e-5
LOG2E = float(np.log2(np.e))

V7X_VMEM_LIMIT = 56 * 1024 * 1024

PIECE = 16
TOK_TILE = 512
SAMPLE_TILE = 128
N_PROMPT_TILES = 32
SLOT_ROWS = {TOK_TILE: 2560, SAMPLE_TILE: 1024}
TILE_PIECE_CAP = [SLOT_ROWS[TOK_TILE] // PIECE] * N_PROMPT_TILES + [SLOT_ROWS[SAMPLE_TILE] // PIECE]
TILE_PIECE_BASE = [i * TILE_PIECE_CAP[0] for i in range(N_PROMPT_TILES + 1)]
N_PIECES = sum(TILE_PIECE_CAP)
STEP_PIECES = 32
HALF_PIECES = 16
MOE_MAX_STEPS = N_PIECES // STEP_PIECES + N_EXPERTS + 1 + 2

NN = (((1,), (0,)), ((), ()))
NT = (((1,), (1,)), ((), ()))
TN = (((0,), (0,)), ((), ()))


def _dg(a, b, dims):
    return lax.dot_general(a, b, dims, preferred_element_type=F32)


def _split3(x):
    h = x.astype(BF16)
    r = x - h.astype(F32)
    m = r.astype(BF16)
    l = (r - m.astype(F32)).astype(BF16)
    return h, m, l


def _dot_f32(a, b, dims):
    ah, am, al = _split3(a)
    bh, bm, bl = _split3(b)
    return (_dg(ah, bh, dims) + (_dg(ah, bm, dims) + _dg(am, bh, dims))
            + (_dg(am, bm, dims) + _dg(ah, bl, dims) + _dg(al, bh, dims)))


def _dot_exact_lhs(a_bf16, b, dims):
    bh, bm, bl = _split3(b)
    return _dg(a_bf16, bh, dims) + _dg(a_bf16, bm, dims) + _dg(a_bf16, bl, dims)


def _sigmoid(x):
    return 0.5 * jnp.tanh(0.5 * x) + 0.5


def _rms(x, g):
    return x * lax.rsqrt(jnp.mean(x * x, axis=-1, keepdims=True) + EPS) * g


def _mod_kernel(c_ref, w_ref, b_ref, o_ref):
    c = c_ref[...]
    o_ref[...] = _dot_f32(c * _sigmoid(c), w_ref[...], NN) + b_ref[...]


def _modulation(c_all, w_ada, b_ada):
    n = c_all.shape[0]
    return pl.pallas_call(
        _mod_kernel,
        grid=(6,),
        in_specs=[pl.BlockSpec((n, D_MODEL), lambda j: (0, 0)),
                  pl.BlockSpec((D_MODEL, D_MODEL), lambda j: (0, j)),
                  pl.BlockSpec((1, D_MODEL), lambda j: (0, j))],
        out_specs=pl.BlockSpec((n, D_MODEL), lambda j: (0, j)),
        out_shape=jax.ShapeDtypeStruct((n, 6 * D_MODEL), F32),
        compiler_params=pltpu.CompilerParams(vmem_limit_bytes=V7X_VMEM_LIMIT),
        name="adaln_mod",
    )(c_all, w_ada, b_ada.reshape(1, -1))


def _rotate(x, cos_t, sin_t):
    width = x.shape[-1]
    d = lax.broadcasted_iota(jnp.int32, x.shape, 1) % HEAD_DIM
    half = ROT_DIM // 2
    partner = jnp.where(d < half, pltpu.roll(x, width - half, axis=1), pltpu.roll(x, half, axis=1))
    return x * cos_t + partner * sin_t


def _inproj_kernel(x_ref, sh_ref, sc_ref, g_ref, w_ref, cos_ref, sin_ref,
                   qa_ref, ka_ref, va_ref, qh_ref, fh_ref, ih_ref, gh_ref):
    x = x_ref[...]
    sh = sh_ref[...].reshape(-1, D_MODEL)
    sc = sc_ref[...].reshape(-1, D_MODEL)
    h = _rms(x, g_ref[...]) * (1.0 + sc) + sh
    z = _dg(h.astype(BF16), w_ref[...], NN)
    cos_k = cos_ref[...]
    sin_k = sin_ref[...]
    cos_q = jnp.concatenate([cos_k] * ATTN_GROUP, axis=1)
    sin_q = jnp.concatenate([sin_k] * ATTN_GROUP, axis=1)
    o = 0
    qa = _rotate(z[:, o:o + ATTN_WIDTH], cos_q, sin_q)
    qa_ref[...] = (qa * (HEAD_DIM ** -0.5 * LOG2E)).astype(BF16)
    o += ATTN_WIDTH
    ka_ref[...] = _rotate(z[:, o:o + KV_WIDTH], cos_k, sin_k)
    o += KV_WIDTH
    va_ref[...] = z[:, o:o + KV_WIDTH]
    o += KV_WIDTH
    qh_ref[...] = z[:, o:o + HG_WIDTH]
    o += HG_WIDTH
    fh_ref[...] = z[:, o:o + HG_WIDTH]
    o += HG_WIDTH
    ih_ref[...] = z[:, o:o + HG_WIDTH].astype(BF16)
    o += HG_WIDTH
    gh_ref[...] = z[:, o:o + HG_WIDTH]


def _rope_tables(positions):
    half = ROT_DIM // 2
    inv = ROPE_THETA ** (-(np.arange(half, dtype=np.float64) * 2.0 / ROT_DIM))
    ang = np.asarray(positions, np.float64)[:, None] * inv[None, :]
    cos_h = np.ones((len(positions), HEAD_DIM))
    sin_h = np.zeros((len(positions), HEAD_DIM))
    cos_h[:, :half] = np.cos(ang)
    cos_h[:, half:ROT_DIM] = np.cos(ang)
    sin_h[:, :half] = -np.sin(ang)
    sin_h[:, half:ROT_DIM] = np.sin(ang)
    cos_t = np.tile(cos_h, (1, ATTN_KV_HEADS)).astype(np.float32)
    sin_t = np.tile(sin_h, (1, ATTN_KV_HEADS)).astype(np.float32)
    return jnp.asarray(cos_t), jnp.asarray(sin_t)


def _in_projection(x, mod3, mod2, g_mix, w_in_bf16, cos_t, sin_t, tm):
    t = x.shape[0]
    nt = t // tm
    if mod3 is not None:
        per_b = (t // mod3.shape[0]) // tm
        sh_spec = pl.BlockSpec((1, 1, D_MODEL), lambda i: (i // per_b, 0, 0))
        sc_spec = pl.BlockSpec((1, 1, D_MODEL), lambda i: (i // per_b, 0, 1))
        mod = mod3
        ncs = cos_t.shape[0] // tm
        cs_spec = pl.BlockSpec((tm, KV_WIDTH), lambda i: (i % ncs, 0))
    else:
        sh_spec = pl.BlockSpec((tm, D_MODEL), lambda i: (i, 0))
        sc_spec = pl.BlockSpec((tm, D_MODEL), lambda i: (i, 1))
        mod = mod2
        cs_spec = pl.BlockSpec((tm, KV_WIDTH), lambda i: (i, 0))
    row = lambda w: pl.BlockSpec((tm, w), lambda i: (i, 0))
    return pl.pallas_call(
        _inproj_kernel,
        grid=(nt,),
        in_specs=[row(D_MODEL), sh_spec, sc_spec,
                  pl.BlockSpec((1, D_MODEL), lambda i: (0, 0)),
                  pl.BlockSpec((D_MODEL, IN_COLS), lambda i: (0, 0)),
                  cs_spec, cs_spec],
        out_specs=[row(ATTN_WIDTH), row(KV_WIDTH), row(KV_WIDTH),
                   row(HG_WIDTH), row(HG_WIDTH), row(HG_WIDTH), row(HG_WIDTH)],
        out_shape=[jax.ShapeDtypeStruct((t, ATTN_WIDTH), BF16),
                   jax.ShapeDtypeStruct((t, KV_WIDTH), F32),
                   jax.ShapeDtypeStruct((t, KV_WIDTH), F32),
                   jax.ShapeDtypeStruct((t, HG_WIDTH), F32),
                   jax.ShapeDtypeStruct((t, HG_WIDTH), F32),
                   jax.ShapeDtypeStruct((t, HG_WIDTH), BF16),
                   jax.ShapeDtypeStruct((t, HG_WIDTH), F32)],
        compiler_params=pltpu.CompilerParams(vmem_limit_bytes=V7X_VMEM_LIMIT),
        name="in_proj",
    )(x, mod, mod, g_mix.reshape(1, -1), w_in_bf16, cos_t, sin_t)


def _attn_prompt_kernel(sink_ref, q_ref, kc_ref, kp_ref, vc_ref, vp_ref, g_ref, o_ref):
    n = pl.program_id(1)
    blk = WINDOW
    nblk = q_ref.shape[0] // blk
    pair_w = 2 * HEAD_DIM
    low = lax.broadcasted_iota(jnp.int32, (1, pair_w), 1) < HEAD_DIM
    kall = jnp.concatenate([kp_ref[...], kc_ref[...]], axis=0)
    vall = jnp.concatenate([vp_ref[...], vc_ref[...]], axis=0)
    kroll = pltpu.roll(kall, HEAD_DIM, axis=1)
    vroll = pltpu.roll(vall, HEAD_DIM, axis=1)
    kdup = [jnp.where(low, kall, kroll).astype(BF16), jnp.where(low, kroll, kall).astype(BF16)]
    v_lo = [jnp.where(low, vall, 0.0).astype(BF16), jnp.where(low, vroll, 0.0).astype(BF16)]
    v_hi = [jnp.where(low, 0.0, vroll).astype(BF16), jnp.where(low, 0.0, vall).astype(BF16)]
    qi = lax.broadcasted_iota(jnp.int32, (blk, 2 * blk), 0)
    kj = lax.broadcasted_iota(jnp.int32, (blk, 2 * blk), 1)
    band = (kj >= qi) & (kj <= qi + blk)
    zero = jnp.zeros((), BF16)
    for bi in range(nblk):
        rows = slice(bi * blk, (bi + 1) * blk)
        krows = slice(bi * blk, (bi + 2) * blk)
        ok = band & ((kj >= blk) | (n > 0)) if bi == 0 else band
        pairs = []
        for pair in range(ATTN_HEADS // 2):
            h = (2 * pair) // ATTN_GROUP
            qp = q_ref[rows, pair * pair_w:(pair + 1) * pair_w]
            acc, dens = None, []
            for half in range(2):
                sink = sink_ref[2 * pair + half] * LOG2E
                qm = jnp.where(low if half == 0 else jnp.logical_not(low), qp, zero)
                s = jnp.where(ok, _dg(qm, kdup[h][krows], NT), -jnp.inf)
                m = jnp.maximum(jnp.max(s, axis=-1, keepdims=True), sink)
                p = jnp.exp2(s - m)
                dens.append(jnp.sum(p, axis=-1, keepdims=True) + jnp.exp2(sink - m))
                pv = _dg(p.astype(BF16), (v_lo if half == 0 else v_hi)[h][krows], NN)
                acc = pv if acc is None else acc + pv
            pairs.append(acc / jnp.where(low, dens[0], dens[1]))
        o_ref[rows, :] = _rms(jnp.concatenate(pairs, axis=1), g_ref[...]).astype(BF16)


def _attention_prompt(qa, ka, va, sinks, g_attn_out, batch):
    t = qa.shape[0]
    nb = t // batch // (WINDOW * ATTN_STEP_BLOCKS)
    cur = lambda w: pl.BlockSpec((WINDOW * ATTN_STEP_BLOCKS, w), lambda b, n: (b * nb + n, 0))
    prev = lambda w: pl.BlockSpec(
        (WINDOW, w), lambda b, n: (ATTN_STEP_BLOCKS * (b * nb + n) - jnp.minimum(n, 1), 0))
    return pl.pallas_call(
        _attn_prompt_kernel,
        grid=(batch, nb),
        in_specs=[pl.BlockSpec(memory_space=pltpu.SMEM),
                  cur(ATTN_WIDTH), cur(KV_WIDTH), prev(KV_WIDTH), cur(KV_WIDTH), prev(KV_WIDTH),
                  pl.BlockSpec((1, ATTN_WIDTH), lambda b, n: (0, 0))],
        out_specs=cur(ATTN_WIDTH),
        out_shape=jax.ShapeDtypeStruct((t, ATTN_WIDTH), BF16),
        name="attn_prompt",
    )(sinks, qa, ka, ka, va, va, g_attn_out.reshape(1, -1))


def _lower_bound(lb_logits_ref):
    lg = lb_logits_ref[...]
    e = jnp.exp(lg - jnp.max(lg, axis=0, keepdims=True))
    return e[0:1] / jnp.sum(e, axis=0, keepdims=True)


def _hgrn_prompt_kernel(q_ref, f_ref, i_ref, g_ref, lbl_ref, gn_ref, o_ref, sfin_ref, st_ref):
    step = pl.program_id(1)
    C = HG_CHUNK
    W = HG_WIDTH
    nsub = C // HG_SUB
    n_chunks = q_ref.shape[0] // C
    heads = [slice(hh * HG_DK, (hh + 1) * HG_DK) for hh in range(HG_HEADS)]

    @pl.when(step == 0)
    def _():
        st_ref[...] = jnp.zeros_like(st_ref)

    lb = _lower_bound(lbl_ref)
    r64 = lax.broadcasted_iota(jnp.int32, (C, C), 0)
    c64 = lax.broadcasted_iota(jnp.int32, (C, C), 1)
    tri = (r64 >= c64).astype(BF16)
    rsub = r64 % HG_SUB
    sr = lax.broadcasted_iota(jnp.int32, ((HG_SUB - 1) * C, C), 0)
    su = lax.broadcasted_iota(jnp.int32, ((HG_SUB - 1) * C, C), 1)
    sd, st_row = sr // C + 1, sr % C
    shifts = ((su == st_row - sd) & (st_row % HG_SUB >= sd)).astype(BF16)
    o_intra, q_state, upd, decay_last = {}, {}, {}, {}
    for ci in range(n_chunks):
        rows = slice(ci * C, (ci + 1) * C)
        f = lb + (1.0 - lb) * _sigmoid(f_ref[rows, :])
        kk = 1.0 - f
        logf = jnp.log2(f)
        q = q_ref[rows, :]
        v = i_ref[rows, :]
        b = _dot_exact_lhs(tri, logf, NN)
        k_shift = _dg(shifts, kk.astype(BF16), NN)
        prods = [q * kk]
        w = logf
        for d in range(1, HG_SUB):
            if d > 1:
                w = w + pltpu.roll(logf, d - 1, axis=0)
            prods.append(q * k_shift[(d - 1) * C:d * C] * jnp.exp2(w))
        ends = [b[j * HG_SUB + HG_SUB - 1:(j + 1) * HG_SUB] for j in range(nsub)]
        kt = kk * jnp.exp2(jnp.concatenate([jnp.broadcast_to(e, (HG_SUB, W)) for e in ends], axis=0) - b)
        lhs, rhs = [], []
        for j in range(nsub - 1):
            lo = (j + 1) * HG_SUB
            lhs.append(jnp.concatenate(
                [jnp.zeros((lo, W), F32), q[lo:] * jnp.exp2(b[lo:] - ends[j])], axis=0).astype(BF16))
            pieces = [kt[j * HG_SUB:lo]]
            if j > 0:
                pieces.insert(0, jnp.zeros((j * HG_SUB, W), F32))
            pieces.append(jnp.zeros((C - lo, W), F32))
            rhs.append(jnp.concatenate(pieces, axis=0).astype(BF16))
        q_state[ci] = (q * jnp.exp2(b)).astype(BF16)
        kd = (kk * jnp.exp2(ends[-1] - b)).astype(BF16)
        decay_last[ci] = jnp.exp2(ends[-1])
        for hh, sl in enumerate(heads):
            a = jnp.zeros((C, C), F32)
            for d in range(HG_SUB):
                a = jnp.where(c64 == r64 - d, jnp.sum(prods[d][:, sl], axis=-1, keepdims=True), a)
            a = a + _dg(jnp.concatenate([x[:, sl] for x in lhs], axis=1),
                        jnp.concatenate([x[:, sl] for x in rhs], axis=1), NT)
            o_intra[ci, hh] = _dg(a.astype(BF16), v[:, sl], NN)
            upd[ci, hh] = _dg(v[:, sl], kd[:, sl], TN)
    finals = []
    for hh, sl in enumerate(heads):
        st = st_ref[hh]
        for ci in range(n_chunks):
            rows = slice(ci * C, (ci + 1) * C)
            o = o_intra[ci, hh] + _dg(q_state[ci][:, sl], st.astype(BF16), NT)
            st = st * decay_last[ci][:, sl] + upd[ci, hh]
            g = g_ref[rows, sl]
            o_ref[rows, sl] = (_rms(o, gn_ref[:, sl]) * (g * _sigmoid(g))).astype(BF16)
        st_ref[hh] = st
        finals.append(st)

    @pl.when(step == pl.num_programs(1) - 1)
    def _():
        for hh in range(HG_HEADS):
            sfin_ref[0, hh] = finals[hh].T


def _hgrn_prompt(qh, fh, ih, gh, lb_logits, g_hg_out, batch):
    t = qh.shape[0]
    nc = t // batch // (HG_CHUNK * HG_STEP_CHUNKS)
    blk = pl.BlockSpec((HG_CHUNK * HG_STEP_CHUNKS, HG_WIDTH), lambda b, c: (b * nc + c, 0))
    const = lambda r: pl.BlockSpec((r, HG_WIDTH), lambda b, c: (0, 0))
    return pl.pallas_call(
        _hgrn_prompt_kernel,
        grid=(batch, nc),
        in_specs=[blk, blk, blk, blk, const(lb_logits.shape[0]), const(1)],
        out_specs=[blk, pl.BlockSpec((1, HG_HEADS, HG_DK, HG_DV), lambda b, c: (b, 0, 0, 0))],
        out_shape=[jax.ShapeDtypeStruct((t, HG_WIDTH), BF16),
                   jax.ShapeDtypeStruct((batch, HG_HEADS, HG_DK, HG_DV), F32)],
        scratch_shapes=[pltpu.VMEM((HG_HEADS, HG_DV, HG_DK), F32)],
        name="hgrn_prompt",
    )(qh, fh, ih, gh, lb_logits, g_hg_out.reshape(1, -1))


def _attn_sample_kernel(sink_ref, q_ref, kn_ref, vn_ref, kc_ref, vc_ref, g_ref,
                        o_ref, ko_ref, vo_ref):
    bt = q_ref.shape[0]
    win = kc_ref.shape[1]
    kc = kc_ref[...]
    vc = vc_ref[...]
    kn = kn_ref[...]
    vn = vn_ref[...]
    outs = []
    for h in range(ATTN_KV_HEADS):
        ls = slice(h * HEAD_DIM, (h + 1) * HEAD_DIM)
        q = q_ref[:, h * ATTN_GROUP:(h + 1) * ATTN_GROUP, :]
        s = jnp.einsum('bgd,bjd->bgj', q, kc[:, :, ls].astype(BF16), preferred_element_type=F32)
        s_new = jnp.sum(q.astype(F32) * kn[:, None, ls], axis=-1, keepdims=True)
        gi = lax.broadcasted_iota(jnp.int32, (1, ATTN_GROUP, 1), 1)
        sink = jnp.zeros((1, ATTN_GROUP, 1), F32)
        for g in range(ATTN_GROUP):
            sink = jnp.where(gi == g, sink_ref[h * ATTN_GROUP + g] * LOG2E, sink)
        m = jnp.maximum(jnp.maximum(jnp.max(s, axis=-1, keepdims=True), s_new), sink)
        p = jnp.exp2(s - m)
        p_new = jnp.exp2(s_new - m)
        den = jnp.sum(p, axis=-1, keepdims=True) + p_new + jnp.exp2(sink - m)
        o = jnp.einsum('bgj,bjd->bgd', p.astype(BF16), vc[:, :, ls].astype(BF16),
                       preferred_element_type=F32)
        o = (o + p_new * vn[:, None, ls]) / den
        outs.append(o)
    ssq = sum(jnp.sum(jnp.sum(o * o, axis=-1, keepdims=True), axis=1, keepdims=True) for o in outs)
    scale = lax.rsqrt(ssq / ATTN_WIDTH + EPS)
    for h in range(ATTN_KV_HEADS):
        o_ref[h] = (outs[h] * scale * g_ref[h][None]).astype(BF16)
    ri = lax.broadcasted_iota(jnp.int32, (win, KV_WIDTH), 0)
    for b in range(bt):
        ko_ref[b] = jnp.where(ri == win - 1, kn[b:b + 1], pltpu.roll(kc[b], win - 1, axis=0))
        vo_ref[b] = jnp.where(ri == win - 1, vn[b:b + 1], pltpu.roll(vc[b], win - 1, axis=0))


def _attention_sample(qa, ka, va, cache_k, cache_v, sinks, g_attn_out, bt=8):
    nb = qa.shape[0]
    win = cache_k.shape[1]
    q3 = qa.reshape(nb, ATTN_HEADS, HEAD_DIM)
    g3 = g_attn_out.reshape(ATTN_KV_HEADS, ATTN_GROUP, HEAD_DIM)
    row = lambda w: pl.BlockSpec((bt, w), lambda i: (i, 0))
    cache = pl.BlockSpec((bt, win, KV_WIDTH), lambda i: (i, 0, 0))
    o4, k_new, v_new = pl.pallas_call(
        _attn_sample_kernel,
        grid=(nb // bt,),
        in_specs=[pl.BlockSpec(memory_space=pltpu.SMEM),
                  pl.BlockSpec((bt, ATTN_HEADS, HEAD_DIM), lambda i: (i, 0, 0)),
                  row(KV_WIDTH), row(KV_WIDTH), cache, cache,
                  pl.BlockSpec((ATTN_KV_HEADS, ATTN_GROUP, HEAD_DIM), lambda i: (0, 0, 0))],
        out_specs=[pl.BlockSpec((ATTN_KV_HEADS, bt, ATTN_GROUP, HEAD_DIM), lambda i: (0, i, 0, 0)),
                   cache, cache],
        out_shape=[jax.ShapeDtypeStruct((ATTN_KV_HEADS, nb, ATTN_GROUP, HEAD_DIM), BF16),
                   jax.ShapeDtypeStruct(cache_k.shape, F32),
                   jax.ShapeDtypeStruct(cache_v.shape, F32)],
        name="attn_sample",
    )(sinks, q3, ka, va, cache_k, cache_v, g3)
    oa = jnp.transpose(o4, (1, 0, 2, 3)).reshape(nb, ATTN_WIDTH)
    return oa, k_new, v_new


def _hgrn_sample_kernel(q_ref, f_ref, i_ref, g_ref, lbl_ref, gn_ref, s0_ref, o_ref, s_ref):
    bt = q_ref.shape[0]
    lb = _lower_bound(lbl_ref)
    f = lb + (1.0 - lb) * _sigmoid(f_ref[...])
    kk = 1.0 - f
    q = q_ref[...]
    v = i_ref[...].astype(F32)
    g = g_ref[...]
    gate = g * _sigmoid(g)
    r = lax.broadcasted_iota(jnp.int32, (HG_DK, HG_DK), 0)
    cc = lax.broadcasted_iota(jnp.int32, (HG_DK, HG_DK), 1)
    diag = r == cc

    def column(x_row):
        return jnp.sum(jnp.where(diag, x_row, 0.0), axis=1, keepdims=True)

    for b in range(bt):
        parts = []
        for hh in range(HG_HEADS):
            sl = slice(hh * HG_DK, (hh + 1) * HG_DK)
            s_new = column(f[b:b + 1, sl]) * s0_ref[b, hh] + column(kk[b:b + 1, sl]) * v[b:b + 1, sl]
            s_ref[b, hh] = s_new
            o = jnp.sum(s_new * column(q[b:b + 1, sl]), axis=0, keepdims=True)
            parts.append(_rms(o, gn_ref[:, sl]))
        o_ref[b:b + 1, :] = (jnp.concatenate(parts, axis=1) * gate[b:b + 1]).astype(BF16)


def _hgrn_sample(qh, fh, ih, gh, lb_logits, g_hg_out, state, bt=8):
    nb = qh.shape[0]
    row = pl.BlockSpec((bt, HG_WIDTH), lambda i: (i, 0))
    const = lambda r: pl.BlockSpec((r, HG_WIDTH), lambda i: (0, 0))
    st = pl.BlockSpec((bt, HG_HEADS, HG_DK, HG_DV), lambda i: (i, 0, 0, 0))
    return pl.pallas_call(
        _hgrn_sample_kernel,
        grid=(nb // bt,),
        in_specs=[row, row, row, row, const(lb_logits.shape[0]), const(1), st],
        out_specs=[row, st],
        out_shape=[jax.ShapeDtypeStruct((nb, HG_WIDTH), BF16),
                   jax.ShapeDtypeStruct(state.shape, F32)],
        name="hgrn_sample",
    )(qh, fh, ih, gh, lb_logits, g_hg_out.reshape(1, -1), state)


def _outproj_kernel(x_ref, oa_ref, oh_ref, gt_ref, sh_ref, sc_ref, g_ref, wo_ref, wr_ref, br_ref, *rest,
                    n_tiles):
    x1_ref, xs_ref, lp_ref, w_ref, pc_ref, hb_scr, lp_scr = rest[-7:]
    i = pl.program_id(0)
    tm = x_ref.shape[0]
    rows = xs_ref.shape[0]

    @pl.when(i == 0)
    def _():
        hb_scr[...] = jnp.zeros_like(hb_scr)
        lp_scr[...] = jnp.zeros_like(lp_scr)

    hb_prev = jnp.where(i <= n_tiles, hb_scr[...], jnp.zeros((), BF16))
    lp_prev = [lp_scr[k:k + 1, :] for k in range(TOP_K)]
    chunk = 512
    pending = list(range(0, rows, chunk))

    def place_next():
        if pending:
            r0 = pending.pop(0)
            si = lax.broadcasted_iota(jnp.int32, (chunk, tm), 0) + r0
            hit = (si == lp_prev[0]) | (si == lp_prev[1]) | (si == lp_prev[2]) | (si == lp_prev[3])
            xs_ref[r0:r0 + chunk, :] = _dg(jnp.where(hit, 1.0, 0.0).astype(BF16), hb_prev, NN).astype(BF16)

    place_next()
    gt = gt_ref[...].reshape(-1, D_MODEL)
    sh = sh_ref[...].reshape(-1, D_MODEL)
    sc = sc_ref[...].reshape(-1, D_MODEL)
    mix = _dg(oa_ref[...], wo_ref[0:ATTN_WIDTH, :], NN) + _dg(oh_ref[...], wo_ref[ATTN_WIDTH:, :], NN)
    x1 = x_ref[...] + gt * mix
    x1_ref[...] = x1
    place_next()
    h2 = _rms(x1, g_ref[...]) * (1.0 + sc) + sh
    place_next()
    logits = _dot_f32(wr_ref[...], h2, NT) + br_ref[...]
    place_next()
    ei = lax.broadcasted_iota(jnp.int32, logits.shape, 0)
    vals, sel = [], []
    l = logits
    for _ in range(TOP_K):
        m = jnp.max(l, axis=0, keepdims=True)
        idx = jnp.min(jnp.where(l == m, ei, N_EXPERTS), axis=0, keepdims=True)
        pick = ei == idx
        vals.append(m)
        sel.append(pick)
        l = jnp.where(pick, -jnp.inf, l)
    ex = [jnp.exp(v - vals[0]) for v in vals]
    den = ex[0] + ex[1] + ex[2] + ex[3]
    for k in range(TOP_K):
        w_ref[k:k + 1, :] = ex[k] / den
    place_next()

    member = jnp.where(sel[0] | sel[1] | sel[2] | sel[3], 1.0, 0.0)
    tr = lax.broadcasted_iota(jnp.int32, (tm, tm), 0)
    tc = lax.broadcasted_iota(jnp.int32, (tm, tm), 1)
    rank = _dg(member.astype(BF16), (tr < tc).astype(BF16), NN)
    pieces = jnp.floor((jnp.sum(member, axis=1, keepdims=True) + (PIECE - 1)) * (1.0 / PIECE))
    er = lax.broadcasted_iota(jnp.int32, (N_EXPERTS, N_EXPERTS), 0)
    ec = lax.broadcasted_iota(jnp.int32, (N_EXPERTS, N_EXPERTS), 1)
    pieces_b = jnp.broadcast_to(pieces, (N_EXPERTS, 128))
    pc_ref[0] = pieces_b.astype(jnp.int32)
    base = _dg((ec < er).astype(BF16), pieces_b.astype(BF16), NN)[:, 0:1] * PIECE
    slot = base + rank
    for k in range(TOP_K):
        lp = jnp.sum(jnp.where(sel[k], slot, 0.0), axis=0, keepdims=True).astype(jnp.int32)
        lp_ref[k:k + 1, :] = lp
        lp_scr[k:k + 1, :] = lp
    while pending:
        place_next()
    hb_scr[...] = h2.astype(BF16)


def _out_projection(x, oa, oh, mod3, mod2, g_ffn, w_out_bf16, w_router_t, b_router, tm, slots, slot_block):
    t = x.shape[0]
    nt = t // tm
    rows = SLOT_ROWS[tm]
    real = lambda i: jnp.minimum(i, nt - 1)
    if mod3 is not None:
        per_b = (t // mod3.shape[0]) // tm
        mspec = lambda j: pl.BlockSpec((1, 1, D_MODEL), lambda i: (real(i) // per_b, 0, j))
        mod = mod3
    else:
        mspec = lambda j: pl.BlockSpec((tm, D_MODEL), lambda i: (real(i), j))
        mod = mod2
    row = lambda w: pl.BlockSpec((tm, w), lambda i: (real(i), 0))
    full = lambda a: pl.BlockSpec(a.shape, lambda i: (0,) * a.ndim)
    g2 = g_ffn.reshape(1, -1)
    br = b_router.reshape(-1, 1)
    args = [x, oa, oh, mod, mod, mod, g2, w_out_bf16, w_router_t, br]
    in_specs = [row(D_MODEL), row(ATTN_WIDTH), row(HG_WIDTH), mspec(2), mspec(3), mspec(4),
                full(g2), full(w_out_bf16), full(w_router_t), full(br)]
    aliases = {}
    n_fill = 0
    if slots is not None:
        args.append(slots)
        in_specs.append(pl.BlockSpec(memory_space=pl.ANY))
        aliases = {len(args) - 1: 1}
    else:
        n_fill = pl.cdiv(N_PIECES * PIECE - nt * rows, rows)
    return pl.pallas_call(
        functools.partial(_outproj_kernel, n_tiles=nt),
        grid=(nt + 1 + n_fill,),
        in_specs=in_specs,
        out_specs=[row(D_MODEL),
                   pl.BlockSpec((rows, D_MODEL), lambda i: (slot_block + jnp.maximum(i - 1, 0), 0)),
                   pl.BlockSpec((TOP_K, tm), lambda i: (0, real(i))),
                   pl.BlockSpec((TOP_K, tm), lambda i: (0, real(i))),
                   pl.BlockSpec((1, N_EXPERTS, 128), lambda i: (real(i), 0, 0))],
        out_shape=[jax.ShapeDtypeStruct((t, D_MODEL), F32),
                   jax.ShapeDtypeStruct((N_PIECES * PIECE, D_MODEL), BF16),
                   jax.ShapeDtypeStruct((TOP_K, t), jnp.int32),
                   jax.ShapeDtypeStruct((TOP_K, t), F32),
                   jax.ShapeDtypeStruct((nt, N_EXPERTS, 128), jnp.int32)],
        scratch_shapes=[pltpu.VMEM((tm, D_MODEL), BF16), pltpu.VMEM((8, tm), jnp.int32)],
        input_output_aliases=aliases,
        compiler_params=pltpu.CompilerParams(vmem_limit_bytes=V7X_VMEM_LIMIT),
        name="out_proj_router",
    )(*args)


def _piece_tables(pieces_ie):
    cap = jnp.asarray(TILE_PIECE_CAP, jnp.int32)
    gbase = jnp.asarray(TILE_PIECE_BASE, jnp.int32)
    seg_src = gbase[:, None] + jnp.cumsum(pieces_ie, axis=1) - pieces_ie
    used_i = jnp.sum(pieces_ie, axis=1)
    tail_i = cap - used_i
    np_e = jnp.sum(pieces_ie, axis=0)
    rem = np_e % STEP_PIECES
    head_e = jnp.where(rem > 0, rem, jnp.minimum(np_e, STEP_PIECES))
    nt_e = np_e // STEP_PIECES + (rem > 0)
    tile_end = jnp.cumsum(nt_e)
    tile_start = tile_end - nt_e
    n_comp = tile_end[-1]
    q_start_e = jnp.cumsum(np_e) - np_e
    n_used = jnp.sum(np_e)
    tt = jnp.arange(MOE_MAX_STEPS, dtype=jnp.int32)
    e_t = jnp.minimum(jnp.sum(tt[:, None] >= tile_end[None, :], axis=1), N_EXPERTS - 1).astype(jnp.int32)
    is_comp = tt < n_comp
    k = tt - tile_start[e_t]
    q0_comp = q_start_e[e_t] + jnp.where(k == 0, 0, head_e[e_t] + STEP_PIECES * (k - 1))
    live_comp = jnp.where(k == 0, head_e[e_t], STEP_PIECES)
    q0_fill = n_used + STEP_PIECES * (tt - n_comp)
    q0 = jnp.where(is_comp, q0_comp, q0_fill)
    live = jnp.where(is_comp, live_comp, jnp.clip(N_PIECES - q0_fill, 0, STEP_PIECES))
    n_busy = n_comp + (N_PIECES - n_used + STEP_PIECES - 1) // STEP_PIECES
    length = jnp.concatenate([pieces_ie.T.reshape(-1), tail_i])
    src = jnp.concatenate([seg_src.T.reshape(-1), gbase + used_i])
    dest = jnp.cumsum(length) - length
    delta = src - dest
    d_delta = delta - jnp.concatenate([jnp.zeros((1,), jnp.int32), delta[:-1]])
    lane = jnp.arange(STEP_PIECES, dtype=jnp.int32)
    qq = (q0[:, None] + lane[None, :]).reshape(-1)
    ok = (lane[None, :] < live[:, None]).reshape(-1)
    piece = qq + jnp.sum(jnp.where(dest[None, :] <= qq[:, None], d_delta[None, :], 0), axis=1)
    dump = (N_PIECES + (tt % 2)[:, None] * STEP_PIECES + lane[None, :]).reshape(-1)
    tbl_out = jnp.where(ok, piece, dump).astype(jnp.int32)
    last = jnp.maximum(n_comp - 1, 0)
    tbl_in = jnp.where(ok & jnp.repeat(is_comp, STEP_PIECES), piece, 0).reshape(MOE_MAX_STEPS, STEP_PIECES)
    tbl_in = jnp.where(is_comp[:, None], tbl_in, tbl_in[last][None, :]).reshape(-1).astype(jnp.int32)
    texp = jnp.where(is_comp, e_t, e_t[last]).astype(jnp.int32)
    first = (is_comp & (k == 0)).astype(jnp.int32)
    kind = jnp.where(is_comp, jnp.where(live <= HALF_PIECES, 1, 2), jnp.where(tt < n_busy, 0, 3))
    return tbl_in, tbl_out, texp, kind.astype(jnp.int32), first


def _moe_grouped_kernel(tin_ref, tout_ref, texp_ref, kind_ref, first_ref,
                        x_hbm, wu_ref, wd_ref, bg_ref, bl_ref, bd_ref, y_hbm,
                        wg_s, wl_s, wd_s, xbuf, ybuf, xsems, ysems):
    npc = STEP_PIECES
    t = pl.program_id(0)
    nsteps = pl.num_programs(0)
    kind = kind_ref[t]
    slot = t % 2

    def x_copy(step, buf, p):
        src = tin_ref[step * npc + p]
        return pltpu.make_async_copy(
            x_hbm.at[pl.ds(pl.multiple_of(src * PIECE, PIECE), PIECE), :],
            xbuf.at[buf, pl.ds(p * PIECE, PIECE), :],
            xsems.at[buf, p])

    def x_each(step, buf, fn):
        k = kind_ref[step]

        @pl.when(k == 2)
        def _():
            for p in range(STEP_PIECES):
                fn(x_copy(step, buf, p))

        @pl.when(k == 1)
        def _():
            for p in range(HALF_PIECES):
                fn(x_copy(step, buf, p))

    def y_copy(step, buf, p):
        dst = tout_ref[step * npc + p]
        return pltpu.make_async_copy(
            ybuf.at[buf, pl.ds(p * PIECE, PIECE), :],
            y_hbm.at[pl.ds(pl.multiple_of(dst * PIECE, PIECE), PIECE), :],
            ysems.at[buf, p])

    def y_wait(step, buf):
        for p in range(npc):
            y_copy(step, buf, p).wait()

    @pl.when(t == 0)
    def _():
        ybuf[...] = jnp.zeros_like(ybuf)
        x_each(0, 0, lambda cp: cp.start())

    @pl.when(t >= 2)
    def _():
        y_wait(t - 2, slot)

    @pl.when(t + 1 < nsteps)
    def _():
        x_each(t + 1, 1 - slot, lambda cp: cp.start())

    @pl.when((kind > 0) & (kind < 3) & (first_ref[t] == 1))
    def _():
        cb = 256
        r = lax.broadcasted_iota(jnp.int32, (cb, cb), 0)
        c = lax.broadcasted_iota(jnp.int32, (cb, cb), 1)
        perm = jnp.where(r == jnp.where(c < cb // 2, 2 * c, 2 * (c - cb // 2) + 1), 1.0, 0.0).astype(BF16)
        for blk in range(2 * D_FF // cb):
            wp = _dg(wu_ref[0, :, blk * cb:(blk + 1) * cb].astype(BF16), perm, NN).astype(BF16)
            wg_s[:, blk * (cb // 2):(blk + 1) * (cb // 2)] = wp[:, :cb // 2]
            wl_s[:, blk * (cb // 2):(blk + 1) * (cb // 2)] = wp[:, cb // 2:]
        wd_s[...] = wd_ref[0].astype(BF16)

    x_each(t, slot, lambda cp: cp.wait())

    def expert(n_pieces):
        xb = xbuf[slot, 0:n_pieces * PIECE, :]
        glu = jnp.minimum(_dg(xb, wg_s[...], NN) + bg_ref[0], SWIGLU_LIMIT)
        lin = jnp.clip(_dg(xb, wl_s[...], NN) + bl_ref[0], -SWIGLU_LIMIT, SWIGLU_LIMIT)
        act = glu * _sigmoid(SWIGLU_ALPHA * glu) * (lin + 1.0)
        y = (_dg(act.astype(BF16), wd_s[...], NN) + bd_ref[0]).astype(BF16)
        ybuf[slot, 0:n_pieces * PIECE, :] = y

    @pl.when(kind == 2)
    def _():
        expert(STEP_PIECES)

    @pl.when(kind == 1)
    def _():
        expert(HALF_PIECES)

    @pl.when(kind == 0)
    def _():
        ybuf[slot] = jnp.zeros((npc * PIECE, D_MODEL), BF16)

    for p in range(npc):
        y_copy(t, slot, p).start()

    @pl.when(t == nsteps - 1)
    def _():
        y_wait(t - 1, 1 - slot)
        y_wait(t, slot)


def _moe_grouped(slots, tables, w_up, w_down, bg, bl, bd):
    npc = STEP_PIECES
    by_expert = lambda shape: pl.BlockSpec(
        shape, lambda t, tin, tout, texp, kind, first: (texp[t],) + (0,) * (len(shape) - 1))
    return pl.pallas_call(
        _moe_grouped_kernel,
        grid_spec=pltpu.PrefetchScalarGridSpec(
            num_scalar_prefetch=5,
            grid=(MOE_MAX_STEPS,),
            in_specs=[pl.BlockSpec(memory_space=pl.ANY),
                      by_expert((1, D_MODEL, 2 * D_FF)), by_expert((1, D_FF, D_MODEL)),
                      by_expert((1, 1, D_FF)), by_expert((1, 1, D_FF)), by_expert((1, 1, D_MODEL))],
            out_specs=pl.BlockSpec(memory_space=pl.ANY),
            scratch_shapes=[pltpu.VMEM((D_MODEL, D_FF), BF16), pltpu.VMEM((D_MODEL, D_FF), BF16),
                            pltpu.VMEM((D_FF, D_MODEL), BF16),
                            pltpu.VMEM((2, npc * PIECE, D_MODEL), BF16),
                            pltpu.VMEM((2, npc * PIECE, D_MODEL), BF16),
                            pltpu.SemaphoreType.DMA((2, npc)),
                            pltpu.SemaphoreType.DMA((2, npc))]),
        out_shape=jax.ShapeDtypeStruct(((N_PIECES + 2 * npc) * PIECE, D_MODEL), BF16),
        compiler_params=pltpu.CompilerParams(vmem_limit_bytes=V7X_VMEM_LIMIT),
        name="moe_grouped",
    )(*tables, slots, w_up, w_down, bg, bl, bd)


def _combine_kernel(y_ref, lp_ref, w_ref, x1_ref, gt_ref, gf_ref, o_ref):
    tm = x1_ref.shape[0]
    rows = y_ref.shape[0]
    r = lax.broadcasted_iota(jnp.int32, (tm, tm), 0)
    c = lax.broadcasted_iota(jnp.int32, (tm, tm), 1)
    diag = r == c

    def column(x_row):
        return jnp.sum(jnp.where(diag, x_row, 0.0), axis=1, keepdims=True)

    pos = [column(lp_ref[k:k + 1, :].astype(F32)).astype(jnp.int32) for k in range(TOP_K)]
    wgt = [column(w_ref[k:k + 1, :]) for k in range(TOP_K)]
    chunk = 512
    moe = jnp.zeros((tm, D_MODEL), F32)
    for r0 in range(0, rows, chunk):
        si = lax.broadcasted_iota(jnp.int32, (tm, chunk), 1) + r0
        wt = jnp.zeros((tm, chunk), F32)
        for k in range(TOP_K):
            wt = jnp.where(si == pos[k], wgt[k], wt)
        moe = moe + _dg(wt.astype(BF16), y_ref[r0:r0 + chunk, :], NN)
    gt = gt_ref[...].reshape(-1, D_MODEL)
    o_ref[...] = _rms(x1_ref[...] + gt * moe, gf_ref[...])


def _combine(yslots, lp, w, x1, mod3, mod2, g_final, tm, slot_block):
    t = x1.shape[0]
    nt = t // tm
    rows = SLOT_ROWS[tm]
    if mod3 is not None:
        per_b = (t // mod3.shape[0]) // tm
        gspec = pl.BlockSpec((1, 1, D_MODEL), lambda i: (i // per_b, 0, 5))
        mod = mod3
    else:
        gspec = pl.BlockSpec((tm, D_MODEL), lambda i: (i, 5))
        mod = mod2
    return pl.pallas_call(
        _combine_kernel,
        grid=(nt,),
        in_specs=[pl.BlockSpec((rows, D_MODEL), lambda i: (slot_block + i, 0)),
                  pl.BlockSpec((TOP_K, tm), lambda i: (0, i)),
                  pl.BlockSpec((TOP_K, tm), lambda i: (0, i)),
                  pl.BlockSpec((tm, D_MODEL), lambda i: (i, 0)),
                  gspec,
                  pl.BlockSpec((1, D_MODEL), lambda i: (0, 0))],
        out_specs=pl.BlockSpec((tm, D_MODEL), lambda i: (i, 0)),
        out_shape=jax.ShapeDtypeStruct((t, D_MODEL), F32),
        compiler_params=pltpu.CompilerParams(vmem_limit_bytes=V7X_VMEM_LIMIT),
        name="moe_combine",
    )(yslots, lp, w, x1, mod, g_final.reshape(1, -1))


def kernel(x_prompt, x_sample, c_prompt, c_sample, cache_k_win, cache_v_win, state_hgrn, w_ada, b_ada,
           g_mix, g_ffn, w_in, attn_sinks, g_attn_out, hg_lb_logits, g_hg_out, w_out, w_router, b_router,
           w_up, b_up, w_down, b_down, g_final):
    batch, seq, d = x_prompt.shape
    nsamp = x_sample.shape[0]
    win = cache_k_win.shape[2]
    layer = 0

    mod = _modulation(jnp.concatenate([c_prompt, c_sample], axis=0), w_ada[layer], b_ada[layer])
    mod_p = mod[:batch].reshape(batch, 1, 6 * d)
    mod_s = mod[batch:]

    assert (batch * seq, nsamp) == (N_PROMPT_TILES * TOK_TILE, SAMPLE_TILE)
    w_in_b = w_in[layer].astype(BF16)
    w_out_b = w_out[layer].astype(BF16)
    w_router_t = w_router[layer].T
    bg = b_up[layer][:, None, 0::2]
    bl = b_up[layer][:, None, 1::2]
    bd = b_down[layer][:, None, :]

    xp = x_prompt.reshape(batch * seq, d)
    cos_p, sin_p = _rope_tables(np.arange(seq))
    qa, ka, va, qh, fh, ih, gh = _in_projection(xp, mod_p, None, g_mix[layer], w_in_b, cos_p, sin_p, TOK_TILE)
    oa = _attention_prompt(qa, ka, va, attn_sinks[layer], g_attn_out[layer], batch)
    oh, s_prompt = _hgrn_prompt(qh, fh, ih, gh, hg_lb_logits, g_hg_out[layer], batch)
    x1_p, slots, lp_p, cw_p, pc_p = _out_projection(
        xp, oa, oh, mod_p, None, g_ffn[layer], w_out_b, w_router_t, b_router[layer], TOK_TILE, None, 0)
    k_win_p = ka.reshape(batch, seq, ATTN_KV_HEADS, HEAD_DIM)[:, seq - win:]
    v_win_p = va.reshape(batch, seq, ATTN_KV_HEADS, HEAD_DIM)[:, seq - win:]

    xs = x_sample.reshape(nsamp, d)
    cos_s, sin_s = _rope_tables(np.full((nsamp,), PAST_LEN))
    qa, ka, va, qh, fh, ih, gh = _in_projection(xs, None, mod_s, g_mix[layer], w_in_b, cos_s, sin_s, nsamp)
    oa, k_win_s, v_win_s = _attention_sample(
        qa, ka, va, cache_k_win[layer].reshape(nsamp, win, KV_WIDTH),
        cache_v_win[layer].reshape(nsamp, win, KV_WIDTH), attn_sinks[layer], g_attn_out[layer])
    oh, s_sample = _hgrn_sample(qh, fh, ih, gh, hg_lb_logits, g_hg_out[layer], state_hgrn[layer])
    sample_block = TILE_PIECE_BASE[-1] * PIECE // SLOT_ROWS[SAMPLE_TILE]
    x1_s, slots, lp_s, cw_s, pc_s = _out_projection(
        xs, oa, oh, None, mod_s, g_ffn[layer], w_out_b, w_router_t, b_router[layer], SAMPLE_TILE,
        slots, sample_block)

    tables = _piece_tables(jnp.concatenate([pc_p[:, :, 0], pc_s[:, :, 0]], axis=0))
    yslots = _moe_grouped(slots, tables, w_up[layer], w_down[layer], bg, bl, bd)
    y_prompt = _combine(yslots, lp_p, cw_p, x1_p, mod_p, None, g_final, TOK_TILE, 0)
    y_sample = _combine(yslots, lp_s, cw_s, x1_s, None, mod_s, g_final, SAMPLE_TILE, sample_block)

    kv_shape = (1, nsamp, win, ATTN_KV_HEADS, HEAD_DIM)
    return (y_prompt.reshape(batch, seq, d), y_sample.reshape(nsamp, 1, d),
            k_win_p[None], v_win_p[None], s_prompt[None],
            k_win_s.reshape(kv_shape), v_win_s.reshape(kv_shape), s_sample[None])
```

```python
import functools

import numpy as np
import jax
import jax.numpy as jnp
from jax import lax
from jax.experimental import pallas as pl
from jax.experimental.pallas import tpu as pltpu

F32 = jnp.float32
BF16 = jnp.bfloat16

D_MODEL = 1024
SEQ = 2048
PAST_LEN = 16384
ATTN_HEADS = 8
ATTN_KV_HEADS = 2
HEAD_DIM = 64
ATTN_GROUP = ATTN_HEADS // ATTN_KV_HEADS
ATTN_WIDTH = ATTN_HEADS * HEAD_DIM
KV_WIDTH = ATTN_KV_HEADS * HEAD_DIM
WINDOW = 128
ATTN_STEP_BLOCKS = 2
ROT_DIM = HEAD_DIM // 4
ROPE_THETA = 500000.0
HG_HEADS = 4
HG_DK = 128
HG_DV = 128
HG_WIDTH = HG_HEADS * HG_DV
HG_CHUNK = 64
HG_SUB = 8
HG_STEP_CHUNKS = 2
IN_COLS = ATTN_WIDTH + 2 * KV_WIDTH + 4 * HG_WIDTH
N_EXPERTS = 32
TOP_K = 4
D_FF = D_MODEL
SWIGLU_ALPHA = 1.702
SWIGLU_LIMIT = 7.0
EPS = 1e-5
LOG2E = float(np.log2(np.e))

V7X_VMEM_LIMIT = 56 * 1024 * 1024

PIECE = 16
TOK_TILE = 512
SAMPLE_TILE = 128
N_PROMPT_TILES = 32
SLOT_ROWS = {TOK_TILE: 2560, SAMPLE_TILE: 1024}
TILE_PIECE_CAP = [SLOT_ROWS[TOK_TILE] // PIECE] * N_PROMPT_TILES + [SLOT_ROWS[SAMPLE_TILE] // PIECE]
TILE_PIECE_BASE = [i * TILE_PIECE_CAP[0] for i in range(N_PROMPT_TILES + 1)]
N_PIECES = sum(TILE_PIECE_CAP)
STEP_PIECES = 32
FF_GROUP = 512
HALF_PIECES = 16
MOE_MAX_STEPS = N_PIECES // STEP_PIECES + N_EXPERTS + 1 + 2

NN = (((1,), (0,)), ((), ()))
NT = (((1,), (1,)), ((), ()))
TN = (((0,), (0,)), ((), ()))


def _dg(a, b, dims):
    return lax.dot_general(a, b, dims, preferred_element_type=F32)


def _split3(x):
    h = x.astype(BF16)
    r = x - h.astype(F32)
    m = r.astype(BF16)
    l = (r - m.astype(F32)).astype(BF16)
    return h, m, l


def _dot_f32(a, b, dims):
    ah, am, al = _split3(a)
    bh, bm, bl = _split3(b)
    return (_dg(ah, bh, dims) + (_dg(ah, bm, dims) + _dg(am, bh, dims))
            + (_dg(am, bm, dims) + _dg(ah, bl, dims) + _dg(al, bh, dims)))


def _dot_exact_lhs(a_bf16, b, dims):
    bh, bm, bl = _split3(b)
    return _dg(a_bf16, bh, dims) + _dg(a_bf16, bm, dims) + _dg(a_bf16, bl, dims)


def _sigmoid(x):
    return 0.5 * jnp.tanh(0.5 * x) + 0.5


def _rms(x, g):
    return x * lax.rsqrt(jnp.mean(x * x, axis=-1, keepdims=True) + EPS) * g


def _mod_kernel(c_ref, w_ref, b_ref, o_ref):
    c = c_ref[...]
    o_ref[...] = _dot_f32(c * _sigmoid(c), w_ref[...], NN) + b_ref[...]


def _modulation(c_all, w_ada, b_ada):
    n = c_all.shape[0]
    return pl.pallas_call(
        _mod_kernel,
        grid=(6,),
        in_specs=[pl.BlockSpec((n, D_MODEL), lambda j: (0, 0)),
                  pl.BlockSpec((D_MODEL, D_MODEL), lambda j: (0, j)),
                  pl.BlockSpec((1, D_MODEL), lambda j: (0, j))],
        out_specs=pl.BlockSpec((n, D_MODEL), lambda j: (0, j)),
        out_shape=jax.ShapeDtypeStruct((n, 6 * D_MODEL), F32),
        compiler_params=pltpu.CompilerParams(vmem_limit_bytes=V7X_VMEM_LIMIT),
        name="adaln_mod",
    )(c_all, w_ada, b_ada.reshape(1, -1))


def _rotate(x, cos_t, sin_t):
    width = x.shape[-1]
    d = lax.broadcasted_iota(jnp.int32, x.shape, 1) % HEAD_DIM
    half = ROT_DIM // 2
    partner = jnp.where(d < half, pltpu.roll(x, width - half, axis=1), pltpu.roll(x, half, axis=1))
    return x * cos_t + partner * sin_t


def _inproj_kernel(x_ref, sh_ref, sc_ref, g_ref, w_ref, cos_ref, sin_ref,
                   qa_ref, ka_ref, va_ref, qh_ref, fh_ref, ih_ref, gh_ref):
    x = x_ref[...]
    sh = sh_ref[...].reshape(-1, D_MODEL)
    sc = sc_ref[...].reshape(-1, D_MODEL)
    hb = (_rms(x, g_ref[...]) * (1.0 + sc) + sh).astype(BF16)
    cos_k = cos_ref[...]
    sin_k = sin_ref[...]
    cos_q = jnp.concatenate([cos_k] * ATTN_GROUP, axis=1)
    sin_q = jnp.concatenate([sin_k] * ATTN_GROUP, axis=1)
    o = 0

    def project(width):
        nonlocal o
        z = _dg(hb, w_ref[:, o:o + width], NN)
        o += width
        return z

    qa = _rotate(project(ATTN_WIDTH), cos_q, sin_q)
    qa_ref[...] = (qa * (HEAD_DIM ** -0.5 * LOG2E)).astype(BF16)
    kv = project(2 * KV_WIDTH)
    ka_ref[...] = _rotate(kv[:, :KV_WIDTH], cos_k, sin_k)
    va_ref[...] = kv[:, KV_WIDTH:]
    qh_ref[...] = project(HG_WIDTH)
    fh_ref[...] = project(HG_WIDTH)
    ih_ref[...] = project(HG_WIDTH).astype(BF16)
    gh_ref[...] = project(HG_WIDTH)


def _rope_tables(positions):
    half = ROT_DIM // 2
    inv = ROPE_THETA ** (-(np.arange(half, dtype=np.float64) * 2.0 / ROT_DIM))
    ang = np.asarray(positions, np.float64)[:, None] * inv[None, :]
    cos_h = np.ones((len(positions), HEAD_DIM))
    sin_h = np.zeros((len(positions), HEAD_DIM))
    cos_h[:, :half] = np.cos(ang)
    cos_h[:, half:ROT_DIM] = np.cos(ang)
    sin_h[:, :half] = -np.sin(ang)
    sin_h[:, half:ROT_DIM] = np.sin(ang)
    cos_t = np.tile(cos_h, (1, ATTN_KV_HEADS)).astype(np.float32)
    sin_t = np.tile(sin_h, (1, ATTN_KV_HEADS)).astype(np.float32)
    return jnp.asarray(cos_t), jnp.asarray(sin_t)


def _in_projection(x, mod3, mod2, g_mix, w_in_bf16, cos_t, sin_t, tm):
    t = x.shape[0]
    nt = t // tm
    if mod3 is not None:
        per_b = (t // mod3.shape[0]) // tm
        sh_spec = pl.BlockSpec((1, 1, D_MODEL), lambda i: (i // per_b, 0, 0))
        sc_spec = pl.BlockSpec((1, 1, D_MODEL), lambda i: (i // per_b, 0, 1))
        mod = mod3
        ncs = cos_t.shape[0] // tm
        cs_spec = pl.BlockSpec((tm, KV_WIDTH), lambda i: (i % ncs, 0))
    else:
        sh_spec = pl.BlockSpec((tm, D_MODEL), lambda i: (i, 0))
        sc_spec = pl.BlockSpec((tm, D_MODEL), lambda i: (i, 1))
        mod = mod2
        cs_spec = pl.BlockSpec((tm, KV_WIDTH), lambda i: (i, 0))
    row = lambda w: pl.BlockSpec((tm, w), lambda i: (i, 0))
    return pl.pallas_call(
        _inproj_kernel,
        grid=(nt,),
        in_specs=[row(D_MODEL), sh_spec, sc_spec,
                  pl.BlockSpec((1, D_MODEL), lambda i: (0, 0)),
                  pl.BlockSpec((D_MODEL, IN_COLS), lambda i: (0, 0)),
                  cs_spec, cs_spec],
        out_specs=[row(ATTN_WIDTH), row(KV_WIDTH), row(KV_WIDTH),
                   row(HG_WIDTH), row(HG_WIDTH), row(HG_WIDTH), row(HG_WIDTH)],
        out_shape=[jax.ShapeDtypeStruct((t, ATTN_WIDTH), BF16),
                   jax.ShapeDtypeStruct((t, KV_WIDTH), F32),
                   jax.ShapeDtypeStruct((t, KV_WIDTH), F32),
                   jax.ShapeDtypeStruct((t, HG_WIDTH), F32),
                   jax.ShapeDtypeStruct((t, HG_WIDTH), F32),
                   jax.ShapeDtypeStruct((t, HG_WIDTH), BF16),
                   jax.ShapeDtypeStruct((t, HG_WIDTH), F32)],
        compiler_params=pltpu.CompilerParams(vmem_limit_bytes=V7X_VMEM_LIMIT),
        name="in_proj",
    )(x, mod, mod, g_mix.reshape(1, -1), w_in_bf16, cos_t, sin_t)


def _attn_prompt_kernel(sink_ref, q_ref, kc_ref, kp_ref, vc_ref, vp_ref, g_ref, o_ref):
    n = pl.program_id(1)
    blk = WINDOW
    nblk = q_ref.shape[0] // blk
    pair_w = 2 * HEAD_DIM
    low = lax.broadcasted_iota(jnp.int32, (1, pair_w), 1) < HEAD_DIM
    kall = jnp.concatenate([kp_ref[...], kc_ref[...]], axis=0)
    vall = jnp.concatenate([vp_ref[...], vc_ref[...]], axis=0)
    kroll = pltpu.roll(kall, HEAD_DIM, axis=1)
    vroll = pltpu.roll(vall, HEAD_DIM, axis=1)
    kdup = [jnp.where(low, kall, kroll).astype(BF16), jnp.where(low, kroll, kall).astype(BF16)]
    v_lo = [jnp.where(low, vall, 0.0).astype(BF16), jnp.where(low, vroll, 0.0).astype(BF16)]
    v_hi = [jnp.where(low, 0.0, vroll).astype(BF16), jnp.where(low, 0.0, vall).astype(BF16)]
    qi = lax.broadcasted_iota(jnp.int32, (blk, 2 * blk), 0)
    kj = lax.broadcasted_iota(jnp.int32, (blk, 2 * blk), 1)
    band = (kj >= qi) & (kj <= qi + blk)
    zero = jnp.zeros((), BF16)
    for bi in range(nblk):
        rows = slice(bi * blk, (bi + 1) * blk)
        krows = slice(bi * blk, (bi + 2) * blk)
        ok = band & ((kj >= blk) | (n > 0)) if bi == 0 else band
        heads = range(ATTN_HEADS)
        kv = [head // ATTN_GROUP for head in heads]
        sinks = [sink_ref[head] * LOG2E for head in heads]
        s = []
        for head in heads:
            qp = q_ref[rows, (head // 2) * pair_w:(head // 2 + 1) * pair_w]
            qm = jnp.where(low if head % 2 == 0 else jnp.logical_not(low), qp, zero)
            s.append(jnp.where(ok, _dg(qm, kdup[kv[head]][krows], NT), -jnp.inf))
        m = [jnp.maximum(jnp.max(s[head], axis=-1, keepdims=True), sinks[head]) for head in heads]
        p = [jnp.exp2(s[head] - m[head]) for head in heads]
        den = [jnp.sum(p[head], axis=-1, keepdims=True) + jnp.exp2(sinks[head] - m[head]) for head in heads]
        pv = [_dg(p[head].astype(BF16), (v_lo if head % 2 == 0 else v_hi)[kv[head]][krows], NN)
              for head in heads]
        pairs = [(pv[2 * j] + pv[2 * j + 1]) / jnp.where(low, den[2 * j], den[2 * j + 1])
                 for j in range(ATTN_HEADS // 2)]
        o_ref[rows, :] = _rms(jnp.concatenate(pairs, axis=1), g_ref[...]).astype(BF16)


def _attention_prompt(qa, ka, va, sinks, g_attn_out, batch):
    t = qa.shape[0]
    nb = t // batch // (WINDOW * ATTN_STEP_BLOCKS)
    cur = lambda w: pl.BlockSpec((WINDOW * ATTN_STEP_BLOCKS, w), lambda b, n: (b * nb + n, 0))
    prev = lambda w: pl.BlockSpec(
        (WINDOW, w), lambda b, n: (ATTN_STEP_BLOCKS * (b * nb + n) - jnp.minimum(n, 1), 0))
    return pl.pallas_call(
        _attn_prompt_kernel,
        grid=(batch, nb),
        in_specs=[pl.BlockSpec(memory_space=pltpu.SMEM),
                  cur(ATTN_WIDTH), cur(KV_WIDTH), prev(KV_WIDTH), cur(KV_WIDTH), prev(KV_WIDTH),
                  pl.BlockSpec((1, ATTN_WIDTH), lambda b, n: (0, 0))],
        out_specs=cur(ATTN_WIDTH),
        out_shape=jax.ShapeDtypeStruct((t, ATTN_WIDTH), BF16),
        name="attn_prompt",
    )(sinks, qa, ka, ka, va, va, g_attn_out.reshape(1, -1))


def _lower_bound(lb_logits_ref):
    lg = lb_logits_ref[...]
    e = jnp.exp(lg - jnp.max(lg, axis=0, keepdims=True))
    return e[0:1] / jnp.sum(e, axis=0, keepdims=True)


def _hgrn_prompt_kernel(q_ref, f_ref, i_ref, g_ref, lbl_ref, gn_ref, o_ref, sfin_ref, st_ref):
    step = pl.program_id(1)
    C = HG_CHUNK
    W = HG_WIDTH
    nsub = C // HG_SUB
    n_chunks = q_ref.shape[0] // C
    heads = [slice(hh * HG_DK, (hh + 1) * HG_DK) for hh in range(HG_HEADS)]

    @pl.when(step == 0)
    def _():
        st_ref[...] = jnp.zeros_like(st_ref)

    lb = _lower_bound(lbl_ref)
    r64 = lax.broadcasted_iota(jnp.int32, (C, C), 0)
    c64 = lax.broadcasted_iota(jnp.int32, (C, C), 1)
    tri = (r64 >= c64).astype(BF16)
    rsub = r64 % HG_SUB
    sr = lax.broadcasted_iota(jnp.int32, ((HG_SUB - 1) * C, C), 0)
    su = lax.broadcasted_iota(jnp.int32, ((HG_SUB - 1) * C, C), 1)
    sd, st_row = sr // C + 1, sr % C
    shifts = ((su == st_row - sd) & (st_row % HG_SUB >= sd)).astype(BF16)
    o_intra, q_state, upd, decay_last = {}, {}, {}, {}
    for ci in range(n_chunks):
        rows = slice(ci * C, (ci + 1) * C)
        f = lb + (1.0 - lb) * _sigmoid(f_ref[rows, :])
        kk = 1.0 - f
        logf = jnp.log2(f)
        q = q_ref[rows, :]
        v = i_ref[rows, :]
        b = _dot_exact_lhs(tri, logf, NN)
        k_shift = _dg(shifts, kk.astype(BF16), NN)
        prods = [q * kk]
        w = logf
        for d in range(1, HG_SUB):
            if d > 1:
                w = w + pltpu.roll(logf, d - 1, axis=0)
            prods.append(q * k_shift[(d - 1) * C:d * C] * jnp.exp2(w))
        ends = [b[j * HG_SUB + HG_SUB - 1:(j + 1) * HG_SUB] for j in range(nsub)]
        kt = kk * jnp.exp2(jnp.concatenate([jnp.broadcast_to(e, (HG_SUB, W)) for e in ends], axis=0) - b)
        lhs, rhs = [], []
        for j in range(nsub - 1):
            lo = (j + 1) * HG_SUB
            lhs.append(jnp.concatenate(
                [jnp.zeros((lo, W), F32), q[lo:] * jnp.exp2(b[lo:] - ends[j])], axis=0).astype(BF16))
            pieces = [kt[j * HG_SUB:lo]]
            if j > 0:
                pieces.insert(0, jnp.zeros((j * HG_SUB, W), F32))
            pieces.append(jnp.zeros((C - lo, W), F32))
            rhs.append(jnp.concatenate(pieces, axis=0).astype(BF16))
        q_state[ci] = (q * jnp.exp2(b)).astype(BF16)
        kd = (kk * jnp.exp2(ends[-1] - b)).astype(BF16)
        decay_last[ci] = jnp.exp2(ends[-1])
        a_off = [_dg(jnp.concatenate([x[:, sl] for x in lhs], axis=1),
                     jnp.concatenate([x[:, sl] for x in rhs], axis=1), NT) for sl in heads]
        for hh, sl in enumerate(heads):
            upd[ci, hh] = _dg(v[:, sl], kd[:, sl], TN)
        sums = [[jnp.sum(prods[d][:, sl], axis=-1, keepdims=True) for d in range(HG_SUB)] for sl in heads]
        for hh, sl in enumerate(heads):
            a = a_off[hh]
            for d in range(HG_SUB):
                a = jnp.where((c64 == r64 - d) & (rsub >= d), sums[hh][d], a)
            o_intra[ci, hh] = _dg(a.astype(BF16), v[:, sl], NN)
    finals = []
    outs = {}
    for hh, sl in enumerate(heads):
        st = st_ref[hh]
        for ci in range(n_chunks):
            outs[ci, hh] = o_intra[ci, hh] + _dg(q_state[ci][:, sl], st.astype(BF16), NT)
            st = st * decay_last[ci][:, sl] + upd[ci, hh]
        st_ref[hh] = st
        finals.append(st)
    for ci in range(n_chunks):
        rows = slice(ci * C, (ci + 1) * C)
        for hh, sl in enumerate(heads):
            g = g_ref[rows, sl]
            o_ref[rows, sl] = (_rms(outs[ci, hh], gn_ref[:, sl]) * (g * _sigmoid(g))).astype(BF16)

    @pl.when(step == pl.num_programs(1) - 1)
    def _():
        for hh in range(HG_HEADS):
            sfin_ref[0, hh] = finals[hh].T


def _hgrn_prompt(qh, fh, ih, gh, lb_logits, g_hg_out, batch):
    t = qh.shape[0]
    nc = t // batch // (HG_CHUNK * HG_STEP_CHUNKS)
    blk = pl.BlockSpec((HG_CHUNK * HG_STEP_CHUNKS, HG_WIDTH), lambda b, c: (b * nc + c, 0))
    const = lambda r: pl.BlockSpec((r, HG_WIDTH), lambda b, c: (0, 0))
    return pl.pallas_call(
        _hgrn_prompt_kernel,
        grid=(batch, nc),
        in_specs=[blk, blk, blk, blk, const(lb_logits.shape[0]), const(1)],
        out_specs=[blk, pl.BlockSpec((1, HG_HEADS, HG_DK, HG_DV), lambda b, c: (b, 0, 0, 0))],
        out_shape=[jax.ShapeDtypeStruct((t, HG_WIDTH), BF16),
                   jax.ShapeDtypeStruct((batch, HG_HEADS, HG_DK, HG_DV), F32)],
        scratch_shapes=[pltpu.VMEM((HG_HEADS, HG_DV, HG_DK), F32)],
        name="hgrn_prompt",
    )(qh, fh, ih, gh, lb_logits, g_hg_out.reshape(1, -1))


def _attn_sample_kernel(sink_ref, q_ref, kn_ref, vn_ref, kc_ref, vc_ref, g_ref,
                        o_ref, ko_ref, vo_ref):
    bt = q_ref.shape[0]
    win = kc_ref.shape[1]
    kc = kc_ref[...]
    vc = vc_ref[...]
    kn = kn_ref[...]
    vn = vn_ref[...]
    outs = []
    for h in range(ATTN_KV_HEADS):
        ls = slice(h * HEAD_DIM, (h + 1) * HEAD_DIM)
        q = q_ref[:, h * ATTN_GROUP:(h + 1) * ATTN_GROUP, :]
        s = jnp.einsum('bgd,bjd->bgj', q, kc[:, :, ls].astype(BF16), preferred_element_type=F32)
        s_new = jnp.sum(q.astype(F32) * kn[:, None, ls], axis=-1, keepdims=True)
        gi = lax.broadcasted_iota(jnp.int32, (1, ATTN_GROUP, 1), 1)
        sink = jnp.zeros((1, ATTN_GROUP, 1), F32)
        for g in range(ATTN_GROUP):
            sink = jnp.where(gi == g, sink_ref[h * ATTN_GROUP + g] * LOG2E, sink)
        m = jnp.maximum(jnp.maximum(jnp.max(s, axis=-1, keepdims=True), s_new), sink)
        p = jnp.exp2(s - m)
        p_new = jnp.exp2(s_new - m)
        den = jnp.sum(p, axis=-1, keepdims=True) + p_new + jnp.exp2(sink - m)
        o = jnp.einsum('bgj,bjd->bgd', p.astype(BF16), vc[:, :, ls].astype(BF16),
                       preferred_element_type=F32)
        o = (o + p_new * vn[:, None, ls]) / den
        outs.append(o)
    ssq = sum(jnp.sum(jnp.sum(o * o, axis=-1, keepdims=True), axis=1, keepdims=True) for o in outs)
    scale = lax.rsqrt(ssq / ATTN_WIDTH + EPS)
    for h in range(ATTN_KV_HEADS):
        o_ref[h] = (outs[h] * scale * g_ref[h][None]).astype(BF16)
    ri = lax.broadcasted_iota(jnp.int32, (win, KV_WIDTH), 0)
    for b in range(bt):
        ko_ref[b] = jnp.where(ri == win - 1, kn[b:b + 1], pltpu.roll(kc[b], win - 1, axis=0))
        vo_ref[b] = jnp.where(ri == win - 1, vn[b:b + 1], pltpu.roll(vc[b], win - 1, axis=0))


def _attention_sample(qa, ka, va, cache_k, cache_v, sinks, g_attn_out, bt=8):
    nb = qa.shape[0]
    win = cache_k.shape[1]
    q3 = qa.reshape(nb, ATTN_HEADS, HEAD_DIM)
    g3 = g_attn_out.reshape(ATTN_KV_HEADS, ATTN_GROUP, HEAD_DIM)
    row = lambda w: pl.BlockSpec((bt, w), lambda i: (i, 0))
    cache = pl.BlockSpec((bt, win, KV_WIDTH), lambda i: (i, 0, 0))
    o4, k_new, v_new = pl.pallas_call(
        _attn_sample_kernel,
        grid=(nb // bt,),
        in_specs=[pl.BlockSpec(memory_space=pltpu.SMEM),
                  pl.BlockSpec((bt, ATTN_HEADS, HEAD_DIM), lambda i: (i, 0, 0)),
                  row(KV_WIDTH), row(KV_WIDTH), cache, cache,
                  pl.BlockSpec((ATTN_KV_HEADS, ATTN_GROUP, HEAD_DIM), lambda i: (0, 0, 0))],
        out_specs=[pl.BlockSpec((ATTN_KV_HEADS, bt, ATTN_GROUP, HEAD_DIM), lambda i: (0, i, 0, 0)),
                   cache, cache],
        out_shape=[jax.ShapeDtypeStruct((ATTN_KV_HEADS, nb, ATTN_GROUP, HEAD_DIM), BF16),
                   jax.ShapeDtypeStruct(cache_k.shape, F32),
                   jax.ShapeDtypeStruct(cache_v.shape, F32)],
        name="attn_sample",
    )(sinks, q3, ka, va, cache_k, cache_v, g3)
    oa = jnp.transpose(o4, (1, 0, 2, 3)).reshape(nb, ATTN_WIDTH)
    return oa, k_new, v_new


def _hgrn_sample_kernel(q_ref, f_ref, i_ref, g_ref, lbl_ref, gn_ref, s0_ref, o_ref, s_ref):
    bt = q_ref.shape[0]
    lb = _lower_bound(lbl_ref)
    f = lb + (1.0 - lb) * _sigmoid(f_ref[...])
    kk = 1.0 - f
    q = q_ref[...]
    v = i_ref[...].astype(F32)
    g = g_ref[...]
    gate = g * _sigmoid(g)
    r = lax.broadcasted_iota(jnp.int32, (HG_DK, HG_DK), 0)
    cc = lax.broadcasted_iota(jnp.int32, (HG_DK, HG_DK), 1)
    diag = r == cc

    def column(x_row):
        return jnp.sum(jnp.where(diag, x_row, 0.0), axis=1, keepdims=True)

    units = [(b, hh, slice(hh * HG_DK, (hh + 1) * HG_DK)) for b in range(bt) for hh in range(HG_HEADS)]
    f_col = [column(f[b:b + 1, sl]) for b, hh, sl in units]
    k_col = [column(kk[b:b + 1, sl]) for b, hh, sl in units]
    q_col = [column(q[b:b + 1, sl]) for b, hh, sl in units]
    s_new = [f_col[u] * s0_ref[b, hh] + k_col[u] * v[b:b + 1, sl] for u, (b, hh, sl) in enumerate(units)]
    for u, (b, hh, sl) in enumerate(units):
        s_ref[b, hh] = s_new[u]
    outs = [_rms(jnp.sum(s_new[u] * q_col[u], axis=0, keepdims=True), gn_ref[:, sl])
            for u, (b, hh, sl) in enumerate(units)]
    for b in range(bt):
        row = jnp.concatenate(outs[b * HG_HEADS:(b + 1) * HG_HEADS], axis=1)
        o_ref[b:b + 1, :] = (row * gate[b:b + 1]).astype(BF16)


def _hgrn_sample(qh, fh, ih, gh, lb_logits, g_hg_out, state, bt=8):
    nb = qh.shape[0]
    row = pl.BlockSpec((bt, HG_WIDTH), lambda i: (i, 0))
    const = lambda r: pl.BlockSpec((r, HG_WIDTH), lambda i: (0, 0))
    st = pl.BlockSpec((bt, HG_HEADS, HG_DK, HG_DV), lambda i: (i, 0, 0, 0))
    return pl.pallas_call(
        _hgrn_sample_kernel,
        grid=(nb // bt,),
        in_specs=[row, row, row, row, const(lb_logits.shape[0]), const(1), st],
        out_specs=[row, st],
        out_shape=[jax.ShapeDtypeStruct((nb, HG_WIDTH), BF16),
                   jax.ShapeDtypeStruct(state.shape, F32)],
        name="hgrn_sample",
    )(qh, fh, ih, gh, lb_logits, g_hg_out.reshape(1, -1), state)


def _outproj_kernel(x_ref, oa_ref, oh_ref, gt_ref, sh_ref, sc_ref, g_ref, wo_ref, wr_ref, br_ref, *rest,
                    n_tiles):
    x1_ref, xs_ref, lp_ref, w_ref, pc_ref, hb_scr, lp_scr = rest[-7:]
    i = pl.program_id(0)
    tm = x_ref.shape[0]
    rows = xs_ref.shape[0]

    @pl.when(i == 0)
    def _():
        hb_scr[...] = jnp.zeros_like(hb_scr)
        lp_scr[...] = jnp.zeros_like(lp_scr)

    hb_prev = jnp.where(i <= n_tiles, hb_scr[...], jnp.zeros((), BF16))
    lp_prev = [lp_scr[k:k + 1, :] for k in range(TOP_K)]
    chunk = 512
    pending = list(range(0, rows, chunk))

    def place_next():
        if pending:
            r0 = pending.pop(0)
            si = lax.broadcasted_iota(jnp.int32, (chunk, tm), 0) + r0
            hit = (si == lp_prev[0]) | (si == lp_prev[1]) | (si == lp_prev[2]) | (si == lp_prev[3])
            xs_ref[r0:r0 + chunk, :] = _dg(jnp.where(hit, 1.0, 0.0).astype(BF16), hb_prev, NN).astype(BF16)

    place_next()
    gt = gt_ref[...].reshape(-1, D_MODEL)
    sh = sh_ref[...].reshape(-1, D_MODEL)
    sc = sc_ref[...].reshape(-1, D_MODEL)
    mix = _dg(oa_ref[...], wo_ref[0:ATTN_WIDTH, :], NN) + _dg(oh_ref[...], wo_ref[ATTN_WIDTH:, :], NN)
    x1 = x_ref[...] + gt * mix
    x1_ref[...] = x1
    place_next()
    h2 = _rms(x1, g_ref[...]) * (1.0 + sc) + sh
    place_next()
    logits = _dot_f32(wr_ref[...], h2, NT) + br_ref[...]
    place_next()
    ei = lax.broadcasted_iota(jnp.int32, logits.shape, 0)
    vals, sel = [], []
    l = logits
    for _ in range(TOP_K):
        m = jnp.max(l, axis=0, keepdims=True)
        idx = jnp.min(jnp.where(l == m, ei, N_EXPERTS), axis=0, keepdims=True)
        pick = ei == idx
        vals.append(m)
        sel.append(pick)
        l = jnp.where(pick, -jnp.inf, l)
    ex = [jnp.exp(v - vals[0]) for v in vals]
    den = ex[0] + ex[1] + ex[2] + ex[3]
    for k in range(TOP_K):
        w_ref[k:k + 1, :] = ex[k] / den
    place_next()

    member = jnp.where(sel[0] | sel[1] | sel[2] | sel[3], 1.0, 0.0)
    tr = lax.broadcasted_iota(jnp.int32, (tm, tm), 0)
    tc = lax.broadcasted_iota(jnp.int32, (tm, tm), 1)
    rank = _dg(member.astype(BF16), (tr < tc).astype(BF16), NN)
    pieces = jnp.floor((jnp.sum(member, axis=1, keepdims=True) + (PIECE - 1)) * (1.0 / PIECE))
    er = lax.broadcasted_iota(jnp.int32, (N_EXPERTS, N_EXPERTS), 0)
    ec = lax.broadcasted_iota(jnp.int32, (N_EXPERTS, N_EXPERTS), 1)
    pieces_b = jnp.broadcast_to(pieces, (N_EXPERTS, 128))
    pc_ref[0] = pieces_b.astype(jnp.int32)
    base = _dg((ec < er).astype(BF16), pieces_b.astype(BF16), NN)[:, 0:1] * PIECE
    slot = base + rank
    for k in range(TOP_K):
        lp = jnp.sum(jnp.where(sel[k], slot, 0.0), axis=0, keepdims=True).astype(jnp.int32)
        lp_ref[k:k + 1, :] = lp
        lp_scr[k:k + 1, :] = lp
    while pending:
        place_next()
    hb_scr[...] = h2.astype(BF16)


def _out_projection(x, oa, oh, mod3, mod2, g_ffn, w_out_bf16, w_router_t, b_router, tm, slots, slot_block):
    t = x.shape[0]
    nt = t // tm
    rows = SLOT_ROWS[tm]
    real = lambda i: jnp.minimum(i, nt - 1)
    if mod3 is not None:
        per_b = (t // mod3.shape[0]) // tm
        mspec = lambda j: pl.BlockSpec((1, 1, D_MODEL), lambda i: (real(i) // per_b, 0, j))
        mod = mod3
    else:
        mspec = lambda j: pl.BlockSpec((tm, D_MODEL), lambda i: (real(i), j))
        mod = mod2
    row = lambda w: pl.BlockSpec((tm, w), lambda i: (real(i), 0))
    full = lambda a: pl.BlockSpec(a.shape, lambda i: (0,) * a.ndim)
    g2 = g_ffn.reshape(1, -1)
    br = b_router.reshape(-1, 1)
    args = [x, oa, oh, mod, mod, mod, g2, w_out_bf16, w_router_t, br]
    in_specs = [row(D_MODEL), row(ATTN_WIDTH), row(HG_WIDTH), mspec(2), mspec(3), mspec(4),
                full(g2), full(w_out_bf16), full(w_router_t), full(br)]
    aliases = {}
    n_fill = 0
    if slots is not None:
        args.append(slots)
        in_specs.append(pl.BlockSpec(memory_space=pl.ANY))
        aliases = {len(args) - 1: 1}
    else:
        n_fill = pl.cdiv(N_PIECES * PIECE - nt * rows, rows)
    return pl.pallas_call(
        functools.partial(_outproj_kernel, n_tiles=nt),
        grid=(nt + 1 + n_fill,),
        in_specs=in_specs,
        out_specs=[row(D_MODEL),
                   pl.BlockSpec((rows, D_MODEL), lambda i: (slot_block + jnp.maximum(i - 1, 0), 0)),
                   pl.BlockSpec((TOP_K, tm), lambda i: (0, real(i))),
                   pl.BlockSpec((TOP_K, tm), lambda i: (0, real(i))),
                   pl.BlockSpec((1, N_EXPERTS, 128), lambda i: (real(i), 0, 0))],
        out_shape=[jax.ShapeDtypeStruct((t, D_MODEL), F32),
                   jax.ShapeDtypeStruct((N_PIECES * PIECE, D_MODEL), BF16),
                   jax.ShapeDtypeStruct((TOP_K, t), jnp.int32),
                   jax.ShapeDtypeStruct((TOP_K, t), F32),
                   jax.ShapeDtypeStruct((nt, N_EXPERTS, 128), jnp.int32)],
        scratch_shapes=[pltpu.VMEM((tm, D_MODEL), BF16), pltpu.VMEM((8, tm), jnp.int32)],
        input_output_aliases=aliases,
        compiler_params=pltpu.CompilerParams(vmem_limit_bytes=V7X_VMEM_LIMIT),
        name="out_proj_router",
    )(*args)


def _piece_tables(pieces_ie):
    cap = jnp.asarray(TILE_PIECE_CAP, jnp.int32)
    gbase = jnp.asarray(TILE_PIECE_BASE, jnp.int32)
    seg_src = gbase[:, None] + jnp.cumsum(pieces_ie, axis=1) - pieces_ie
    used_i = jnp.sum(pieces_ie, axis=1)
    tail_i = cap - used_i
    np_e = jnp.sum(pieces_ie, axis=0)
    rem = np_e % STEP_PIECES
    head_e = jnp.where(rem > 0, rem, jnp.minimum(np_e, STEP_PIECES))
    nt_e = np_e // STEP_PIECES + (rem > 0)
    tile_end = jnp.cumsum(nt_e)
    tile_start = tile_end - nt_e
    n_comp = tile_end[-1]
    q_start_e = jnp.cumsum(np_e) - np_e
    n_used = jnp.sum(np_e)
    tt = jnp.arange(MOE_MAX_STEPS, dtype=jnp.int32)
    e_t = jnp.minimum(jnp.sum(tt[:, None] >= tile_end[None, :], axis=1), N_EXPERTS - 1).astype(jnp.int32)
    is_comp = tt < n_comp
    k = tt - tile_start[e_t]
    q0_comp = q_start_e[e_t] + jnp.where(k == 0, 0, head_e[e_t] + STEP_PIECES * (k - 1))
    live_comp = jnp.where(k == 0, head_e[e_t], STEP_PIECES)
    q0_fill = n_used + STEP_PIECES * (tt - n_comp)
    q0 = jnp.where(is_comp, q0_comp, q0_fill)
    live = jnp.where(is_comp, live_comp, jnp.clip(N_PIECES - q0_fill, 0, STEP_PIECES))
    n_busy = n_comp + (N_PIECES - n_used + STEP_PIECES - 1) // STEP_PIECES
    length = jnp.concatenate([pieces_ie.T.reshape(-1), tail_i])
    src = jnp.concatenate([seg_src.T.reshape(-1), gbase + used_i])
    dest = jnp.cumsum(length) - length
    delta = src - dest
    d_delta = delta - jnp.concatenate([jnp.zeros((1,), jnp.int32), delta[:-1]])
    lane = jnp.arange(STEP_PIECES, dtype=jnp.int32)
    qq = (q0[:, None] + lane[None, :]).reshape(-1)
    ok = (lane[None, :] < live[:, None]).reshape(-1)
    piece = qq + jnp.sum(jnp.where(dest[None, :] <= qq[:, None], d_delta[None, :], 0), axis=1)
    dump = (N_PIECES + (tt % 2)[:, None] * STEP_PIECES + lane[None, :]).reshape(-1)
    tbl_out = jnp.where(ok, piece, dump).astype(jnp.int32)
    last = jnp.maximum(n_comp - 1, 0)
    tbl_in = jnp.where(ok & jnp.repeat(is_comp, STEP_PIECES), piece, 0).reshape(MOE_MAX_STEPS, STEP_PIECES)
    tbl_in = jnp.where(is_comp[:, None], tbl_in, tbl_in[last][None, :]).reshape(-1).astype(jnp.int32)
    texp = jnp.where(is_comp, e_t, e_t[last]).astype(jnp.int32)
    first = (is_comp & (k == 0)).astype(jnp.int32)
    kind = jnp.where(is_comp, jnp.where(live <= HALF_PIECES, 1, 2), jnp.where(tt < n_busy, 0, 3))
    return tbl_in, tbl_out, texp, kind.astype(jnp.int32), first


def _moe_grouped_kernel(tin_ref, tout_ref, texp_ref, kind_ref, first_ref,
                        x_hbm, wu_ref, wd_ref, bg_ref, bl_ref, bd_ref, y_hbm,
                        wg_s, wl_s, wd_s, xbuf, ybuf, xsems, ysems):
    npc = STEP_PIECES
    t = pl.program_id(0)
    nsteps = pl.num_programs(0)
    kind = kind_ref[t]
    slot = t % 2

    def x_copy(step, buf, p):
        src = tin_ref[step * npc + p]
        return pltpu.make_async_copy(
            x_hbm.at[pl.ds(pl.multiple_of(src * PIECE, PIECE), PIECE), :],
            xbuf.at[buf, pl.ds(p * PIECE, PIECE), :],
            xsems.at[buf, p])

    def x_each(step, buf, fn):
        k = kind_ref[step]

        @pl.when(k == 2)
        def _():
            for p in range(STEP_PIECES):
                fn(x_copy(step, buf, p))

        @pl.when(k == 1)
        def _():
            for p in range(HALF_PIECES):
                fn(x_copy(step, buf, p))

    def y_copy(step, buf, p):
        dst = tout_ref[step * npc + p]
        return pltpu.make_async_copy(
            ybuf.at[buf, pl.ds(p * PIECE, PIECE), :],
            y_hbm.at[pl.ds(pl.multiple_of(dst * PIECE, PIECE), PIECE), :],
            ysems.at[buf, p])

    def y_wait(step, buf):
        for p in range(npc):
            y_copy(step, buf, p).wait()

    @pl.when(t == 0)
    def _():
        ybuf[...] = jnp.zeros_like(ybuf)
        x_each(0, 0, lambda cp: cp.start())

    @pl.when(t >= 2)
    def _():
        y_wait(t - 2, slot)

    @pl.when(t + 1 < nsteps)
    def _():
        x_each(t + 1, 1 - slot, lambda cp: cp.start())

    @pl.when((kind > 0) & (kind < 3) & (first_ref[t] == 1))
    def _():
        cb = 256
        r = lax.broadcasted_iota(jnp.int32, (cb, cb), 0)
        c = lax.broadcasted_iota(jnp.int32, (cb, cb), 1)
        perm = jnp.where(r == jnp.where(c < cb // 2, 2 * c, 2 * (c - cb // 2) + 1), 1.0, 0.0).astype(BF16)
        for blk in range(2 * D_FF // cb):
            wp = _dg(wu_ref[0, :, blk * cb:(blk + 1) * cb].astype(BF16), perm, NN).astype(BF16)
            wg_s[:, blk * (cb // 2):(blk + 1) * (cb // 2)] = wp[:, :cb // 2]
            wl_s[:, blk * (cb // 2):(blk + 1) * (cb // 2)] = wp[:, cb // 2:]
        wd_s[...] = wd_ref[0].astype(BF16)

    x_each(t, slot, lambda cp: cp.wait())

    def expert(n_pieces):
        xb = xbuf[slot, 0:n_pieces * PIECE, :]
        groups = [slice(j * FF_GROUP, (j + 1) * FF_GROUP) for j in range(D_FF // FF_GROUP)]
        glu = [jnp.minimum(_dg(xb, wg_s[:, c], NN) + bg_ref[0, :, c], SWIGLU_LIMIT) for c in groups]
        lin = [jnp.clip(_dg(xb, wl_s[:, c], NN) + bl_ref[0, :, c], -SWIGLU_LIMIT, SWIGLU_LIMIT)
               for c in groups]
        act = [(g * _sigmoid(SWIGLU_ALPHA * g) * (l + 1.0)).astype(BF16) for g, l in zip(glu, lin)]
        y = bd_ref[0] + _dg(act[0], wd_s[groups[0], :], NN)
        for c, a in zip(groups[1:], act[1:]):
            y = y + _dg(a, wd_s[c, :], NN)
        ybuf[slot, 0:n_pieces * PIECE, :] = y.astype(BF16)

    @pl.when(kind == 2)
    def _():
        expert(STEP_PIECES)

    @pl.when(kind == 1)
    def _():
        expert(HALF_PIECES)

    @pl.when(kind == 0)
    def _():
        ybuf[slot] = jnp.zeros((npc * PIECE, D_MODEL), BF16)

    for p in range(npc):
        y_copy(t, slot, p).start()

    @pl.when(t == nsteps - 1)
    def _():
        y_wait(t - 1, 1 - slot)
        y_wait(t, slot)


def _moe_grouped(slots, tables, w_up, w_down, bg, bl, bd):
    npc = STEP_PIECES
    by_expert = lambda shape: pl.BlockSpec(
        shape, lambda t, tin, tout, texp, kind, first: (texp[t],) + (0,) * (len(shape) - 1))
    return pl.pallas_call(
        _moe_grouped_kernel,
        grid_spec=pltpu.PrefetchScalarGridSpec(
            num_scalar_prefetch=5,
            grid=(MOE_MAX_STEPS,),
            in_specs=[pl.BlockSpec(memory_space=pl.ANY),
                      by_expert((1, D_MODEL, 2 * D_FF)), by_expert((1, D_FF, D_MODEL)),
                      by_expert((1, 1, D_FF)), by_expert((1, 1, D_FF)), by_expert((1, 1, D_MODEL))],
            out_specs=pl.BlockSpec(memory_space=pl.ANY),
            scratch_shapes=[pltpu.VMEM((D_MODEL, D_FF), BF16), pltpu.VMEM((D_MODEL, D_FF), BF16),
                            pltpu.VMEM((D_FF, D_MODEL), BF16),
                            pltpu.VMEM((2, npc * PIECE, D_MODEL), BF16),
                            pltpu.VMEM((2, npc * PIECE, D_MODEL), BF16),
                            pltpu.SemaphoreType.DMA((2, npc)),
                            pltpu.SemaphoreType.DMA((2, npc))]),
        out_shape=jax.ShapeDtypeStruct(((N_PIECES + 2 * npc) * PIECE, D_MODEL), BF16),
        compiler_params=pltpu.CompilerParams(vmem_limit_bytes=V7X_VMEM_LIMIT),
        name="moe_grouped",
    )(*tables, slots, w_up, w_down, bg, bl, bd)


def _combine_kernel(y_ref, lp_ref, w_ref, x1_ref, gt_ref, gf_ref, o_ref):
    tm = x1_ref.shape[0]
    rows = y_ref.shape[0]
    r = lax.broadcasted_iota(jnp.int32, (tm, tm), 0)
    c = lax.broadcasted_iota(jnp.int32, (tm, tm), 1)
    diag = r == c

    def column(x_row):
        return jnp.sum(jnp.where(diag, x_row, 0.0), axis=1, keepdims=True)

    pos = [column(lp_ref[k:k + 1, :].astype(F32)).astype(jnp.int32) for k in range(TOP_K)]
    wgt = [column(w_ref[k:k + 1, :]) for k in range(TOP_K)]
    chunk = 512
    moe = jnp.zeros((tm, D_MODEL), F32)
    for r0 in range(0, rows, chunk):
        si = lax.broadcasted_iota(jnp.int32, (tm, chunk), 1) + r0
        wt = jnp.zeros((tm, chunk), F32)
        for k in range(TOP_K):
            wt = jnp.where(si == pos[k], wgt[k], wt)
        moe = moe + _dg(wt.astype(BF16), y_ref[r0:r0 + chunk, :], NN)
    gt = gt_ref[...].reshape(-1, D_MODEL)
    o_ref[...] = _rms(x1_ref[...] + gt * moe, gf_ref[...])


def _combine(yslots, lp, w, x1, mod3, mod2, g_final, tm, slot_block):
    t = x1.shape[0]
    nt = t // tm
    rows = SLOT_ROWS[tm]
    if mod3 is not None:
        per_b = (t // mod3.shape[0]) // tm
        gspec = pl.BlockSpec((1, 1, D_MODEL), lambda i: (i // per_b, 0, 5))
        mod = mod3
    else:
        gspec = pl.BlockSpec((tm, D_MODEL), lambda i: (i, 5))
        mod = mod2
    return pl.pallas_call(
        _combine_kernel,
        grid=(nt,),
        in_specs=[pl.BlockSpec((rows, D_MODEL), lambda i: (slot_block + i, 0)),
                  pl.BlockSpec((TOP_K, tm), lambda i: (0, i)),
                  pl.BlockSpec((TOP_K, tm), lambda i: (0, i)),
                  pl.BlockSpec((tm, D_MODEL), lambda i: (i, 0)),
                  gspec,
                  pl.BlockSpec((1, D_MODEL), lambda i: (0, 0))],
        out_specs=pl.BlockSpec((tm, D_MODEL), lambda i: (i, 0)),
        out_shape=jax.ShapeDtypeStruct((t, D_MODEL), F32),
        compiler_params=pltpu.CompilerParams(vmem_limit_bytes=V7X_VMEM_LIMIT),
        name="moe_combine",
    )(yslots, lp, w, x1, mod, g_final.reshape(1, -1))


def kernel(x_prompt, x_sample, c_prompt, c_sample, cache_k_win, cache_v_win, state_hgrn, w_ada, b_ada,
           g_mix, g_ffn, w_in, attn_sinks, g_attn_out, hg_lb_logits, g_hg_out, w_out, w_router, b_router,
           w_up, b_up, w_down, b_down, g_final):
    batch, seq, d = x_prompt.shape
    nsamp = x_sample.shape[0]
    win = cache_k_win.shape[2]
    layer = 0

    mod = _modulation(jnp.concatenate([c_prompt, c_sample], axis=0), w_ada[layer], b_ada[layer])
    mod_p = mod[:batch].reshape(batch, 1, 6 * d)
    mod_s = mod[batch:]

    assert (batch * seq, nsamp) == (N_PROMPT_TILES * TOK_TILE, SAMPLE_TILE)
    w_in_b = w_in[layer].astype(BF16)
    w_out_b = w_out[layer].astype(BF16)
    w_router_t = w_router[layer].T
    bg = b_up[layer][:, None, 0::2]
    bl = b_up[layer][:, None, 1::2]
    bd = b_down[layer][:, None, :]

    xp = x_prompt.reshape(batch * seq, d)
    cos_p, sin_p = _rope_tables(np.arange(seq))
    qa, ka, va, qh, fh, ih, gh = _in_projection(xp, mod_p, None, g_mix[layer], w_in_b, cos_p, sin_p, TOK_TILE)
    oa = _attention_prompt(qa, ka, va, attn_sinks[layer], g_attn_out[layer], batch)
    oh, s_prompt = _hgrn_prompt(qh, fh, ih, gh, hg_lb_logits, g_hg_out[layer], batch)
    x1_p, slots, lp_p, cw_p, pc_p = _out_projection(
        xp, oa, oh, mod_p, None, g_ffn[layer], w_out_b, w_router_t, b_router[layer], TOK_TILE, None, 0)
    k_win_p = ka.reshape(batch, seq, ATTN_KV_HEADS, HEAD_DIM)[:, seq - win:]
    v_win_p = va.reshape(batch, seq, ATTN_KV_HEADS, HEAD_DIM)[:, seq - win:]

    xs = x_sample.reshape(nsamp, d)
    cos_s, sin_s = _rope_tables(np.full((nsamp,), PAST_LEN))
    qa, ka, va, qh, fh, ih, gh = _in_projection(xs, None, mod_s, g_mix[layer], w_in_b, cos_s, sin_s, nsamp)
    oa, k_win_s, v_win_s = _attention_sample(
        qa, ka, va, cache_k_win[layer].reshape(nsamp, win, KV_WIDTH),
        cache_v_win[layer].reshape(nsamp, win, KV_WIDTH), attn_sinks[layer], g_attn_out[layer])
    oh, s_sample = _hgrn_sample(qh, fh, ih, gh, hg_lb_logits, g_hg_out[layer], state_hgrn[layer])
    sample_block = TILE_PIECE_BASE[-1] * PIECE // SLOT_ROWS[SAMPLE_TILE]
    x1_s, slots, lp_s, cw_s, pc_s = _out_projection(
        xs, oa, oh, None, mod_s, g_ffn[layer], w_out_b, w_router_t, b_router[layer], SAMPLE_TILE,
        slots, sample_block)

    tables = _piece_tables(jnp.concatenate([pc_p[:, :, 0], pc_s[:, :, 0]], axis=0))
    yslots = _moe_grouped(slots, tables, w_up[layer], w_down[layer], bg, bl, bd)
    y_prompt = _combine(yslots, lp_p, cw_p, x1_p, mod_p, None, g_final, TOK_TILE, 0)
    y_sample = _combine(yslots, lp_s, cw_s, x1_s, None, mod_s, g_final, SAMPLE_TILE, sample_block)

    kv_shape = (1, nsamp, win, ATTN_KV_HEADS, HEAD_DIM)
    return (y_prompt.reshape(batch, seq, d), y_sample.reshape(nsamp, 1, d),
            k_win_p[None], v_win_p[None], s_prompt[None],
            k_win_s.reshape(kv_shape), v_win_s.reshape(kv_shape), s_sample[None])
```

```python
import functools

import numpy as np
import jax
import jax.numpy as jnp
from jax import lax
from jax.experimental import pallas as pl
from jax.experimental.pallas import tpu as pltpu

F32 = jnp.float32
BF16 = jnp.bfloat16

D_MODEL = 1024
SEQ = 2048
PAST_LEN = 16384
ATTN_HEADS = 8
ATTN_KV_HEADS = 2
HEAD_DIM = 64
ATTN_GROUP = ATTN_HEADS // ATTN_KV_HEADS
ATTN_WIDTH = ATTN_HEADS * HEAD_DIM
KV_WIDTH = ATTN_KV_HEADS * HEAD_DIM
WINDOW = 128
ATTN_STEP_BLOCKS = 8
ROT_DIM = HEAD_DIM // 4
ROPE_THETA = 500000.0
HG_HEADS = 4
HG_DK = 128
HG_DV = 128
HG_WIDTH = HG_HEADS * HG_DV
HG_CHUNK = 64
HG_SUB = 8
HG_STEP_CHUNKS = 16
IN_COLS = ATTN_WIDTH + 2 * KV_WIDTH + 4 * HG_WIDTH
N_EXPERTS = 32
TOP_K = 4
D_FF = D_MODEL
SWIGLU_ALPHA = 1.702
SWIGLU_LIMIT = 7.0
EPS = 1e-5
LOG2E = float(np.log2(np.e))

V7X_VMEM_LIMIT = 56 * 1024 * 1024

PIECE = 16
TOK_TILE = 512
SAMPLE_TILE = 128
N_PROMPT_TILES = 32
SLOT_ROWS = {TOK_TILE: 2560, SAMPLE_TILE: 1024}
TILE_PIECE_CAP = [SLOT_ROWS[TOK_TILE] // PIECE] * N_PROMPT_TILES + [SLOT_ROWS[SAMPLE_TILE] // PIECE]
TILE_PIECE_BASE = [i * TILE_PIECE_CAP[0] for i in range(N_PROMPT_TILES + 1)]
N_PIECES = sum(TILE_PIECE_CAP)
STEP_PIECES = 32
FF_GROUP = 512
HALF_PIECES = 16
MOE_MAX_STEPS = N_PIECES // STEP_PIECES + N_EXPERTS + 1 + 2

NN = (((1,), (0,)), ((), ()))
NT = (((1,), (1,)), ((), ()))
TN = (((0,), (0,)), ((), ()))


def _dg(a, b, dims):
    return lax.dot_general(a, b, dims, preferred_element_type=F32)


def _split3(x):
    h = x.astype(BF16)
    r = x - h.astype(F32)
    m = r.astype(BF16)
    l = (r - m.astype(F32)).astype(BF16)
    return h, m, l


def _dot_f32(a, b, dims):
    ah, am, al = _split3(a)
    bh, bm, bl = _split3(b)
    return (_dg(ah, bh, dims) + (_dg(ah, bm, dims) + _dg(am, bh, dims))
            + (_dg(am, bm, dims) + _dg(ah, bl, dims) + _dg(al, bh, dims)))


def _dot_exact_lhs(a_bf16, b, dims):
    bh, bm, bl = _split3(b)
    return _dg(a_bf16, bh, dims) + _dg(a_bf16, bm, dims) + _dg(a_bf16, bl, dims)


def _sigmoid(x):
    return 0.5 * jnp.tanh(0.5 * x) + 0.5


def _rms(x, g):
    return x * lax.rsqrt(jnp.mean(x * x, axis=-1, keepdims=True) + EPS) * g


def _mod_kernel(c_ref, w_ref, b_ref, o_ref):
    c = c_ref[...]
    o_ref[...] = _dot_f32(c * _sigmoid(c), w_ref[...], NN) + b_ref[...]


def _modulation(c_all, w_ada, b_ada):
    n = c_all.shape[0]
    return pl.pallas_call(
        _mod_kernel,
        grid=(6,),
        in_specs=[pl.BlockSpec((n, D_MODEL), lambda j: (0, 0)),
                  pl.BlockSpec((D_MODEL, D_MODEL), lambda j: (0, j)),
                  pl.BlockSpec((1, D_MODEL), lambda j: (0, j))],
        out_specs=pl.BlockSpec((n, D_MODEL), lambda j: (0, j)),
        out_shape=jax.ShapeDtypeStruct((n, 6 * D_MODEL), F32),
        compiler_params=pltpu.CompilerParams(vmem_limit_bytes=V7X_VMEM_LIMIT),
        name="adaln_mod",
    )(c_all, w_ada, b_ada.reshape(1, -1))


def _rotate(x, cos_t, sin_t):
    width = x.shape[-1]
    d = lax.broadcasted_iota(jnp.int32, x.shape, 1) % HEAD_DIM
    half = ROT_DIM // 2
    partner = jnp.where(d < half, pltpu.roll(x, width - half, axis=1), pltpu.roll(x, half, axis=1))
    return x * cos_t + partner * sin_t


def _inproj_kernel(x_ref, sh_ref, sc_ref, g_ref, w_ref, cos_ref, sin_ref,
                   qa_ref, ka_ref, va_ref, qh_ref, fh_ref, ih_ref, gh_ref):
    x = x_ref[...]
    sh = sh_ref[...].reshape(-1, D_MODEL)
    sc = sc_ref[...].reshape(-1, D_MODEL)
    hb = (_rms(x, g_ref[...]) * (1.0 + sc) + sh).astype(BF16)
    cos_k = cos_ref[...]
    sin_k = sin_ref[...]
    cos_q = jnp.concatenate([cos_k] * ATTN_GROUP, axis=1)
    sin_q = jnp.concatenate([sin_k] * ATTN_GROUP, axis=1)
    o = 0

    def project(width):
        nonlocal o
        z = _dg(hb, w_ref[:, o:o + width], NN)
        o += width
        return z

    qa = _rotate(project(ATTN_WIDTH), cos_q, sin_q)
    qa_ref[...] = (qa * (HEAD_DIM ** -0.5 * LOG2E)).astype(BF16)
    kv = project(2 * KV_WIDTH)
    ka_ref[...] = _rotate(kv[:, :KV_WIDTH], cos_k, sin_k)
    va_ref[...] = kv[:, KV_WIDTH:]
    qh_ref[...] = project(HG_WIDTH)
    fh_ref[...] = project(HG_WIDTH)
    ih_ref[...] = project(HG_WIDTH).astype(BF16)
    gh_ref[...] = project(HG_WIDTH)


def _rope_tables(positions):
    half = ROT_DIM // 2
    inv = ROPE_THETA ** (-(np.arange(half, dtype=np.float64) * 2.0 / ROT_DIM))
    ang = np.asarray(positions, np.float64)[:, None] * inv[None, :]
    cos_h = np.ones((len(positions), HEAD_DIM))
    sin_h = np.zeros((len(positions), HEAD_DIM))
    cos_h[:, :half] = np.cos(ang)
    cos_h[:, half:ROT_DIM] = np.cos(ang)
    sin_h[:, :half] = -np.sin(ang)
    sin_h[:, half:ROT_DIM] = np.sin(ang)
    cos_t = np.tile(cos_h, (1, ATTN_KV_HEADS)).astype(np.float32)
    sin_t = np.tile(sin_h, (1, ATTN_KV_HEADS)).astype(np.float32)
    return jnp.asarray(cos_t), jnp.asarray(sin_t)


def _in_projection(x, mod3, mod2, g_mix, w_in_bf16, cos_t, sin_t, tm):
    t = x.shape[0]
    nt = t // tm
    if mod3 is not None:
        per_b = (t // mod3.shape[0]) // tm
        sh_spec = pl.BlockSpec((1, 1, D_MODEL), lambda i: (i // per_b, 0, 0))
        sc_spec = pl.BlockSpec((1, 1, D_MODEL), lambda i: (i // per_b, 0, 1))
        mod = mod3
        ncs = cos_t.shape[0] // tm
        cs_spec = pl.BlockSpec((tm, KV_WIDTH), lambda i: (i % ncs, 0))
    else:
        sh_spec = pl.BlockSpec((tm, D_MODEL), lambda i: (i, 0))
        sc_spec = pl.BlockSpec((tm, D_MODEL), lambda i: (i, 1))
        mod = mod2
        cs_spec = pl.BlockSpec((tm, KV_WIDTH), lambda i: (i, 0))
    row = lambda w: pl.BlockSpec((tm, w), lambda i: (i, 0))
    return pl.pallas_call(
        _inproj_kernel,
        grid=(nt,),
        in_specs=[row(D_MODEL), sh_spec, sc_spec,
                  pl.BlockSpec((1, D_MODEL), lambda i: (0, 0)),
                  pl.BlockSpec((D_MODEL, IN_COLS), lambda i: (0, 0)),
                  cs_spec, cs_spec],
        out_specs=[row(ATTN_WIDTH), row(KV_WIDTH), row(KV_WIDTH),
                   row(HG_WIDTH), row(HG_WIDTH), row(HG_WIDTH), row(HG_WIDTH)],
        out_shape=[jax.ShapeDtypeStruct((t, ATTN_WIDTH), BF16),
                   jax.ShapeDtypeStruct((t, KV_WIDTH), F32),
                   jax.ShapeDtypeStruct((t, KV_WIDTH), F32),
                   jax.ShapeDtypeStruct((t, HG_WIDTH), F32),
                   jax.ShapeDtypeStruct((t, HG_WIDTH), F32),
                   jax.ShapeDtypeStruct((t, HG_WIDTH), BF16),
                   jax.ShapeDtypeStruct((t, HG_WIDTH), F32)],
        compiler_params=pltpu.CompilerParams(vmem_limit_bytes=V7X_VMEM_LIMIT),
        name="in_proj",
    )(x, mod, mod, g_mix.reshape(1, -1), w_in_bf16, cos_t, sin_t)


def _attn_prompt_kernel(sink_ref, q_ref, kc_ref, kp_ref, vc_ref, vp_ref, g_ref, o_ref):
    n = pl.program_id(1)
    blk = WINDOW
    nblk = q_ref.shape[0] // blk
    pair_w = 2 * HEAD_DIM
    low = lax.broadcasted_iota(jnp.int32, (1, pair_w), 1) < HEAD_DIM
    kall = jnp.concatenate([kp_ref[...], kc_ref[...]], axis=0)
    vall = jnp.concatenate([vp_ref[...], vc_ref[...]], axis=0)
    kroll = pltpu.roll(kall, HEAD_DIM, axis=1)
    vroll = pltpu.roll(vall, HEAD_DIM, axis=1)
    kdup = [jnp.where(low, kall, kroll).astype(BF16), jnp.where(low, kroll, kall).astype(BF16)]
    v_lo = [jnp.where(low, vall, 0.0).astype(BF16), jnp.where(low, vroll, 0.0).astype(BF16)]
    v_hi = [jnp.where(low, 0.0, vroll).astype(BF16), jnp.where(low, 0.0, vall).astype(BF16)]
    qi = lax.broadcasted_iota(jnp.int32, (blk, 2 * blk), 0)
    kj = lax.broadcasted_iota(jnp.int32, (blk, 2 * blk), 1)
    band = (kj >= qi) & (kj <= qi + blk)
    zero = jnp.zeros((), BF16)
    for bi in range(nblk):
        rows = slice(bi * blk, (bi + 1) * blk)
        krows = slice(bi * blk, (bi + 2) * blk)
        ok = band & ((kj >= blk) | (n > 0)) if bi == 0 else band
        heads = range(ATTN_HEADS)
        kv = [head // ATTN_GROUP for head in heads]
        sinks = [sink_ref[head] * LOG2E for head in heads]
        s = []
        for head in heads:
            qp = q_ref[rows, (head // 2) * pair_w:(head // 2 + 1) * pair_w]
            qm = jnp.where(low if head % 2 == 0 else jnp.logical_not(low), qp, zero)
            s.append(jnp.where(ok, _dg(qm, kdup[kv[head]][krows], NT), -jnp.inf))
        m = [jnp.maximum(jnp.max(s[head], axis=-1, keepdims=True), sinks[head]) for head in heads]
        p = [jnp.exp2(s[head] - m[head]) for head in heads]
        den = [jnp.sum(p[head], axis=-1, keepdims=True) + jnp.exp2(sinks[head] - m[head]) for head in heads]
        pv = [_dg(p[head].astype(BF16), (v_lo if head % 2 == 0 else v_hi)[kv[head]][krows], NN)
              for head in heads]
        pairs = [(pv[2 * j] + pv[2 * j + 1]) / jnp.where(low, den[2 * j], den[2 * j + 1])
                 for j in range(ATTN_HEADS // 2)]
        o_ref[rows, :] = _rms(jnp.concatenate(pairs, axis=1), g_ref[...]).astype(BF16)


def _attention_prompt(qa, ka, va, sinks, g_attn_out, batch):
    t = qa.shape[0]
    nb = t // batch // (WINDOW * ATTN_STEP_BLOCKS)
    cur = lambda w: pl.BlockSpec((WINDOW * ATTN_STEP_BLOCKS, w), lambda b, n: (b * nb + n, 0))
    prev = lambda w: pl.BlockSpec(
        (WINDOW, w), lambda b, n: (ATTN_STEP_BLOCKS * (b * nb + n) - jnp.minimum(n, 1), 0))
    return pl.pallas_call(
        _attn_prompt_kernel,
        grid=(batch, nb),
        in_specs=[pl.BlockSpec(memory_space=pltpu.SMEM),
                  cur(ATTN_WIDTH), cur(KV_WIDTH), prev(KV_WIDTH), cur(KV_WIDTH), prev(KV_WIDTH),
                  pl.BlockSpec((1, ATTN_WIDTH), lambda b, n: (0, 0))],
        out_specs=cur(ATTN_WIDTH),
        out_shape=jax.ShapeDtypeStruct((t, ATTN_WIDTH), BF16),
        name="attn_prompt",
    )(sinks, qa, ka, ka, va, va, g_attn_out.reshape(1, -1))


def _lower_bound(lb_logits_ref):
    lg = lb_logits_ref[...]
    e = jnp.exp(lg - jnp.max(lg, axis=0, keepdims=True))
    return e[0:1] / jnp.sum(e, axis=0, keepdims=True)


def _hgrn_prompt_kernel(q_ref, f_ref, i_ref, g_ref, lbl_ref, gn_ref, o_ref, sfin_ref, st_ref):
    step = pl.program_id(1)
    C = HG_CHUNK
    W = HG_WIDTH
    nsub = C // HG_SUB
    n_chunks = q_ref.shape[0] // C
    heads = [slice(hh * HG_DK, (hh + 1) * HG_DK) for hh in range(HG_HEADS)]

    @pl.when(step == 0)
    def _():
        st_ref[...] = jnp.zeros_like(st_ref)

    lb = _lower_bound(lbl_ref)
    r64 = lax.broadcasted_iota(jnp.int32, (C, C), 0)
    c64 = lax.broadcasted_iota(jnp.int32, (C, C), 1)
    tri = (r64 >= c64).astype(BF16)
    rsub = r64 % HG_SUB
    sr = lax.broadcasted_iota(jnp.int32, ((HG_SUB - 1) * C, C), 0)
    su = lax.broadcasted_iota(jnp.int32, ((HG_SUB - 1) * C, C), 1)
    sd, st_row = sr // C + 1, sr % C
    shifts = ((su == st_row - sd) & (st_row % HG_SUB >= sd)).astype(BF16)
    o_intra, q_state, upd, decay_last = {}, {}, {}, {}
    for ci in range(n_chunks):
        rows = slice(ci * C, (ci + 1) * C)
        f = lb + (1.0 - lb) * _sigmoid(f_ref[rows, :])
        kk = 1.0 - f
        logf = jnp.log2(f)
        q = q_ref[rows, :]
        v = i_ref[rows, :]
        b = _dot_exact_lhs(tri, logf, NN)
        k_shift = _dg(shifts, kk.astype(BF16), NN)
        prods = [q * kk]
        w = logf
        for d in range(1, HG_SUB):
            if d > 1:
                w = w + pltpu.roll(logf, d - 1, axis=0)
            prods.append(q * k_shift[(d - 1) * C:d * C] * jnp.exp2(w))
        ends = [b[j * HG_SUB + HG_SUB - 1:(j + 1) * HG_SUB] for j in range(nsub)]
        kt = kk * jnp.exp2(jnp.concatenate([jnp.broadcast_to(e, (HG_SUB, W)) for e in ends], axis=0) - b)
        lhs, rhs = [], []
        for j in range(nsub - 1):
            lo = (j + 1) * HG_SUB
            lhs.append(jnp.concatenate(
                [jnp.zeros((lo, W), F32), q[lo:] * jnp.exp2(b[lo:] - ends[j])], axis=0).astype(BF16))
            pieces = [kt[j * HG_SUB:lo]]
            if j > 0:
                pieces.insert(0, jnp.zeros((j * HG_SUB, W), F32))
            pieces.append(jnp.zeros((C - lo, W), F32))
            rhs.append(jnp.concatenate(pieces, axis=0).astype(BF16))
        q_state[ci] = (q * jnp.exp2(b)).astype(BF16)
        kd = (kk * jnp.exp2(ends[-1] - b)).astype(BF16)
        decay_last[ci] = jnp.exp2(ends[-1])
        a_off = [_dg(jnp.concatenate([x[:, sl] for x in lhs], axis=1),
                     jnp.concatenate([x[:, sl] for x in rhs], axis=1), NT) for sl in heads]
        for hh, sl in enumerate(heads):
            upd[ci, hh] = _dg(v[:, sl], kd[:, sl], TN)
        sums = [[jnp.sum(prods[d][:, sl], axis=-1, keepdims=True) for d in range(HG_SUB)] for sl in heads]
        for hh, sl in enumerate(heads):
            a = a_off[hh]
            for d in range(HG_SUB):
                a = jnp.where((c64 == r64 - d) & (rsub >= d), sums[hh][d], a)
            o_intra[ci, hh] = _dg(a.astype(BF16), v[:, sl], NN)
    finals = []
    outs = {}
    for hh, sl in enumerate(heads):
        st = st_ref[hh]
        for ci in range(n_chunks):
            outs[ci, hh] = o_intra[ci, hh] + _dg(q_state[ci][:, sl], st.astype(BF16), NT)
            st = st * decay_last[ci][:, sl] + upd[ci, hh]
        st_ref[hh] = st
        finals.append(st)
    for ci in range(n_chunks):
        rows = slice(ci * C, (ci + 1) * C)
        for hh, sl in enumerate(heads):
            g = g_ref[rows, sl]
            o_ref[rows, sl] = (_rms(outs[ci, hh], gn_ref[:, sl]) * (g * _sigmoid(g))).astype(BF16)

    @pl.when(step == pl.num_programs(1) - 1)
    def _():
        for hh in range(HG_HEADS):
            sfin_ref[0, hh] = finals[hh].T


def _hgrn_prompt(qh, fh, ih, gh, lb_logits, g_hg_out, batch):
    t = qh.shape[0]
    nc = t // batch // (HG_CHUNK * HG_STEP_CHUNKS)
    blk = pl.BlockSpec((HG_CHUNK * HG_STEP_CHUNKS, HG_WIDTH), lambda b, c: (b * nc + c, 0))
    const = lambda r: pl.BlockSpec((r, HG_WIDTH), lambda b, c: (0, 0))
    return pl.pallas_call(
        _hgrn_prompt_kernel,
        grid=(batch, nc),
        in_specs=[blk, blk, blk, blk, const(lb_logits.shape[0]), const(1)],
        out_specs=[blk, pl.BlockSpec((1, HG_HEADS, HG_DK, HG_DV), lambda b, c: (b, 0, 0, 0))],
        out_shape=[jax.ShapeDtypeStruct((t, HG_WIDTH), BF16),
                   jax.ShapeDtypeStruct((batch, HG_HEADS, HG_DK, HG_DV), F32)],
        scratch_shapes=[pltpu.VMEM((HG_HEADS, HG_DV, HG_DK), F32)],
        name="hgrn_prompt",
    )(qh, fh, ih, gh, lb_logits, g_hg_out.reshape(1, -1))


def _attn_sample_kernel(sink_ref, q_ref, kn_ref, vn_ref, kc_ref, vc_ref, g_ref,
                        o_ref, ko_ref, vo_ref):
    bt = q_ref.shape[0]
    win = kc_ref.shape[1]
    kc = kc_ref[...]
    vc = vc_ref[...]
    kn = kn_ref[...]
    vn = vn_ref[...]
    outs = []
    for h in range(ATTN_KV_HEADS):
        ls = slice(h * HEAD_DIM, (h + 1) * HEAD_DIM)
        q = q_ref[:, h * ATTN_GROUP:(h + 1) * ATTN_GROUP, :]
        s = jnp.einsum('bgd,bjd->bgj', q, kc[:, :, ls].astype(BF16), preferred_element_type=F32)
        s_new = jnp.sum(q.astype(F32) * kn[:, None, ls], axis=-1, keepdims=True)
        gi = lax.broadcasted_iota(jnp.int32, (1, ATTN_GROUP, 1), 1)
        sink = jnp.zeros((1, ATTN_GROUP, 1), F32)
        for g in range(ATTN_GROUP):
            sink = jnp.where(gi == g, sink_ref[h * ATTN_GROUP + g] * LOG2E, sink)
        m = jnp.maximum(jnp.maximum(jnp.max(s, axis=-1, keepdims=True), s_new), sink)
        p = jnp.exp2(s - m)
        p_new = jnp.exp2(s_new - m)
        den = jnp.sum(p, axis=-1, keepdims=True) + p_new + jnp.exp2(sink - m)
        o = jnp.einsum('bgj,bjd->bgd', p.astype(BF16), vc[:, :, ls].astype(BF16),
                       preferred_element_type=F32)
        o = (o + p_new * vn[:, None, ls]) / den
        outs.append(o)
    ssq = sum(jnp.sum(jnp.sum(o * o, axis=-1, keepdims=True), axis=1, keepdims=True) for o in outs)
    scale = lax.rsqrt(ssq / ATTN_WIDTH + EPS)
    for h in range(ATTN_KV_HEADS):
        o_ref[h] = (outs[h] * scale * g_ref[h][None]).astype(BF16)
    ri = lax.broadcasted_iota(jnp.int32, (win, KV_WIDTH), 0)
    for b in range(bt):
        ko_ref[b] = jnp.where(ri == win - 1, kn[b:b + 1], pltpu.roll(kc[b], win - 1, axis=0))
        vo_ref[b] = jnp.where(ri == win - 1, vn[b:b + 1], pltpu.roll(vc[b], win - 1, axis=0))


def _attention_sample(qa, ka, va, cache_k, cache_v, sinks, g_attn_out, bt=8):
    nb = qa.shape[0]
    win = cache_k.shape[1]
    q3 = qa.reshape(nb, ATTN_HEADS, HEAD_DIM)
    g3 = g_attn_out.reshape(ATTN_KV_HEADS, ATTN_GROUP, HEAD_DIM)
    row = lambda w: pl.BlockSpec((bt, w), lambda i: (i, 0))
    cache = pl.BlockSpec((bt, win, KV_WIDTH), lambda i: (i, 0, 0))
    o4, k_new, v_new = pl.pallas_call(
        _attn_sample_kernel,
        grid=(nb // bt,),
        in_specs=[pl.BlockSpec(memory_space=pltpu.SMEM),
                  pl.BlockSpec((bt, ATTN_HEADS, HEAD_DIM), lambda i: (i, 0, 0)),
                  row(KV_WIDTH), row(KV_WIDTH), cache, cache,
                  pl.BlockSpec((ATTN_KV_HEADS, ATTN_GROUP, HEAD_DIM), lambda i: (0, 0, 0))],
        out_specs=[pl.BlockSpec((ATTN_KV_HEADS, bt, ATTN_GROUP, HEAD_DIM), lambda i: (0, i, 0, 0)),
                   cache, cache],
        out_shape=[jax.ShapeDtypeStruct((ATTN_KV_HEADS, nb, ATTN_GROUP, HEAD_DIM), BF16),
                   jax.ShapeDtypeStruct(cache_k.shape, F32),
                   jax.ShapeDtypeStruct(cache_v.shape, F32)],
        name="attn_sample",
    )(sinks, q3, ka, va, cache_k, cache_v, g3)
    oa = jnp.transpose(o4, (1, 0, 2, 3)).reshape(nb, ATTN_WIDTH)
    return oa, k_new, v_new


def _hgrn_sample_kernel(q_ref, f_ref, i_ref, g_ref, lbl_ref, gn_ref, s0_ref, o_ref, s_ref):
    bt = q_ref.shape[0]
    lb = _lower_bound(lbl_ref)
    f = lb + (1.0 - lb) * _sigmoid(f_ref[...])
    kk = 1.0 - f
    q = q_ref[...]
    v = i_ref[...].astype(F32)
    g = g_ref[...]
    gate = g * _sigmoid(g)
    r = lax.broadcasted_iota(jnp.int32, (HG_DK, HG_DK), 0)
    cc = lax.broadcasted_iota(jnp.int32, (HG_DK, HG_DK), 1)
    diag = r == cc

    def column(x_row):
        return jnp.sum(jnp.where(diag, x_row, 0.0), axis=1, keepdims=True)

    units = [(b, hh, slice(hh * HG_DK, (hh + 1) * HG_DK)) for b in range(bt) for hh in range(HG_HEADS)]
    f_col = [column(f[b:b + 1, sl]) for b, hh, sl in units]
    k_col = [column(kk[b:b + 1, sl]) for b, hh, sl in units]
    q_col = [column(q[b:b + 1, sl]) for b, hh, sl in units]
    s_new = [f_col[u] * s0_ref[b, hh] + k_col[u] * v[b:b + 1, sl] for u, (b, hh, sl) in enumerate(units)]
    for u, (b, hh, sl) in enumerate(units):
        s_ref[b, hh] = s_new[u]
    outs = [_rms(jnp.sum(s_new[u] * q_col[u], axis=0, keepdims=True), gn_ref[:, sl])
            for u, (b, hh, sl) in enumerate(units)]
    for b in range(bt):
        row = jnp.concatenate(outs[b * HG_HEADS:(b + 1) * HG_HEADS], axis=1)
        o_ref[b:b + 1, :] = (row * gate[b:b + 1]).astype(BF16)


def _hgrn_sample(qh, fh, ih, gh, lb_logits, g_hg_out, state, bt=8):
    nb = qh.shape[0]
    row = pl.BlockSpec((bt, HG_WIDTH), lambda i: (i, 0))
    const = lambda r: pl.BlockSpec((r, HG_WIDTH), lambda i: (0, 0))
    st = pl.BlockSpec((bt, HG_HEADS, HG_DK, HG_DV), lambda i: (i, 0, 0, 0))
    return pl.pallas_call(
        _hgrn_sample_kernel,
        grid=(nb // bt,),
        in_specs=[row, row, row, row, const(lb_logits.shape[0]), const(1), st],
        out_specs=[row, st],
        out_shape=[jax.ShapeDtypeStruct((nb, HG_WIDTH), BF16),
                   jax.ShapeDtypeStruct(state.shape, F32)],
        name="hgrn_sample",
    )(qh, fh, ih, gh, lb_logits, g_hg_out.reshape(1, -1), state)


def _outproj_kernel(x_ref, oa_ref, oh_ref, gt_ref, sh_ref, sc_ref, g_ref, wo_ref, wr_ref, br_ref, *rest,
                    n_tiles):
    x1_ref, xs_ref, lp_ref, w_ref, pc_ref, hb_scr, lp_scr = rest[-7:]
    i = pl.program_id(0)
    tm = x_ref.shape[0]
    rows = xs_ref.shape[0]

    @pl.when(i == 0)
    def _():
        hb_scr[...] = jnp.zeros_like(hb_scr)
        lp_scr[...] = jnp.zeros_like(lp_scr)

    hb_prev = jnp.where(i <= n_tiles, hb_scr[...], jnp.zeros((), BF16))
    lp_prev = [lp_scr[k:k + 1, :] for k in range(TOP_K)]
    chunk = 512
    pending = list(range(0, rows, chunk))

    def place_next():
        if pending:
            r0 = pending.pop(0)
            si = lax.broadcasted_iota(jnp.int32, (chunk, tm), 0) + r0
            hit = (si == lp_prev[0]) | (si == lp_prev[1]) | (si == lp_prev[2]) | (si == lp_prev[3])
            xs_ref[r0:r0 + chunk, :] = _dg(jnp.where(hit, 1.0, 0.0).astype(BF16), hb_prev, NN).astype(BF16)

    place_next()
    gt = gt_ref[...].reshape(-1, D_MODEL)
    sh = sh_ref[...].reshape(-1, D_MODEL)
    sc = sc_ref[...].reshape(-1, D_MODEL)
    mix = _dg(oa_ref[...], wo_ref[0:ATTN_WIDTH, :], NN) + _dg(oh_ref[...], wo_ref[ATTN_WIDTH:, :], NN)
    x1 = x_ref[...] + gt * mix
    x1_ref[...] = x1
    place_next()
    h2 = _rms(x1, g_ref[...]) * (1.0 + sc) + sh
    place_next()
    logits = _dot_f32(wr_ref[...], h2, NT) + br_ref[...]
    place_next()
    ei = lax.broadcasted_iota(jnp.int32, logits.shape, 0)
    vals, sel = [], []
    l = logits
    for _ in range(TOP_K):
        m = jnp.max(l, axis=0, keepdims=True)
        idx = jnp.min(jnp.where(l == m, ei, N_EXPERTS), axis=0, keepdims=True)
        pick = ei == idx
        vals.append(m)
        sel.append(pick)
        l = jnp.where(pick, -jnp.inf, l)
    ex = [jnp.exp(v - vals[0]) for v in vals]
    den = ex[0] + ex[1] + ex[2] + ex[3]
    for k in range(TOP_K):
        w_ref[k:k + 1, :] = ex[k] / den
    place_next()

    member = jnp.where(sel[0] | sel[1] | sel[2] | sel[3], 1.0, 0.0)
    tr = lax.broadcasted_iota(jnp.int32, (tm, tm), 0)
    tc = lax.broadcasted_iota(jnp.int32, (tm, tm), 1)
    rank = _dg(member.astype(BF16), (tr < tc).astype(BF16), NN)
    pieces = jnp.floor((jnp.sum(member, axis=1, keepdims=True) + (PIECE - 1)) * (1.0 / PIECE))
    er = lax.broadcasted_iota(jnp.int32, (N_EXPERTS, N_EXPERTS), 0)
    ec = lax.broadcasted_iota(jnp.int32, (N_EXPERTS, N_EXPERTS), 1)
    pieces_b = jnp.broadcast_to(pieces, (N_EXPERTS, 128))
    pc_ref[0] = pieces_b.astype(jnp.int32)
    base = _dg((ec < er).astype(BF16), pieces_b.astype(BF16), NN)[:, 0:1] * PIECE
    slot = base + rank
    for k in range(TOP_K):
        lp = jnp.sum(jnp.where(sel[k], slot, 0.0), axis=0, keepdims=True).astype(jnp.int32)
        lp_ref[k:k + 1, :] = lp
        lp_scr[k:k + 1, :] = lp
    while pending:
        place_next()
    hb_scr[...] = h2.astype(BF16)


def _out_projection(x, oa, oh, mod3, mod2, g_ffn, w_out_bf16, w_router_t, b_router, tm, slots, slot_block):
    t = x.shape[0]
    nt = t // tm
    rows = SLOT_ROWS[tm]
    real = lambda i: jnp.minimum(i, nt - 1)
    if mod3 is not None:
        per_b = (t // mod3.shape[0]) // tm
        mspec = lambda j: pl.BlockSpec((1, 1, D_MODEL), lambda i: (real(i) // per_b, 0, j))
        mod = mod3
    else:
        mspec = lambda j: pl.BlockSpec((tm, D_MODEL), lambda i: (real(i), j))
        mod = mod2
    row = lambda w: pl.BlockSpec((tm, w), lambda i: (real(i), 0))
    full = lambda a: pl.BlockSpec(a.shape, lambda i: (0,) * a.ndim)
    g2 = g_ffn.reshape(1, -1)
    br = b_router.reshape(-1, 1)
    args = [x, oa, oh, mod, mod, mod, g2, w_out_bf16, w_router_t, br]
    in_specs = [row(D_MODEL), row(ATTN_WIDTH), row(HG_WIDTH), mspec(2), mspec(3), mspec(4),
                full(g2), full(w_out_bf16), full(w_router_t), full(br)]
    aliases = {}
    n_fill = 0
    if slots is not None:
        args.append(slots)
        in_specs.append(pl.BlockSpec(memory_space=pl.ANY))
        aliases = {len(args) - 1: 1}
    else:
        n_fill = pl.cdiv(N_PIECES * PIECE - nt * rows, rows)
    return pl.pallas_call(
        functools.partial(_outproj_kernel, n_tiles=nt),
        grid=(nt + 1 + n_fill,),
        in_specs=in_specs,
        out_specs=[row(D_MODEL),
                   pl.BlockSpec((rows, D_MODEL), lambda i: (slot_block + jnp.maximum(i - 1, 0), 0)),
                   pl.BlockSpec((TOP_K, tm), lambda i: (0, real(i))),
                   pl.BlockSpec((TOP_K, tm), lambda i: (0, real(i))),
                   pl.BlockSpec((1, N_EXPERTS, 128), lambda i: (real(i), 0, 0))],
        out_shape=[jax.ShapeDtypeStruct((t, D_MODEL), F32),
                   jax.ShapeDtypeStruct((N_PIECES * PIECE, D_MODEL), BF16),
                   jax.ShapeDtypeStruct((TOP_K, t), jnp.int32),
                   jax.ShapeDtypeStruct((TOP_K, t), F32),
                   jax.ShapeDtypeStruct((nt, N_EXPERTS, 128), jnp.int32)],
        scratch_shapes=[pltpu.VMEM((tm, D_MODEL), BF16), pltpu.VMEM((8, tm), jnp.int32)],
        input_output_aliases=aliases,
        compiler_params=pltpu.CompilerParams(vmem_limit_bytes=V7X_VMEM_LIMIT),
        name="out_proj_router",
    )(*args)


def _segment_offsets_kernel(dest_ref, dd_ref, q_ref, o_ref):
    reached = dest_ref[...] <= q_ref[...]
    o_ref[...] = jnp.sum(jnp.where(reached, dd_ref[...], 0), axis=0, keepdims=True)


def _segment_offsets(dest, d_delta, queries):
    lanes = 512
    n_seg = -(-dest.shape[0] // 8) * 8
    n_q = -(-queries.shape[0] // lanes) * lanes
    never = jnp.iinfo(jnp.int32).max
    dest_c = jnp.pad(dest, (0, n_seg - dest.shape[0]), constant_values=never).reshape(n_seg, 1)
    dd_c = jnp.pad(d_delta, (0, n_seg - d_delta.shape[0])).reshape(n_seg, 1)
    q_r = jnp.pad(queries, (0, n_q - queries.shape[0])).reshape(1, n_q)
    seg = pl.BlockSpec((n_seg, 1), lambda i: (0, 0))
    out = pl.pallas_call(
        _segment_offsets_kernel,
        grid=(n_q // lanes,),
        in_specs=[seg, seg, pl.BlockSpec((1, lanes), lambda i: (0, i))],
        out_specs=pl.BlockSpec((1, lanes), lambda i: (0, i)),
        out_shape=jax.ShapeDtypeStruct((1, n_q), jnp.int32),
        name="segment_offsets",
    )(dest_c, dd_c, q_r)
    return out[0, :queries.shape[0]]


def _piece_tables(pieces_ie):
    cap = jnp.asarray(TILE_PIECE_CAP, jnp.int32)
    gbase = jnp.asarray(TILE_PIECE_BASE, jnp.int32)
    seg_src = gbase[:, None] + jnp.cumsum(pieces_ie, axis=1) - pieces_ie
    used_i = jnp.sum(pieces_ie, axis=1)
    tail_i = cap - used_i
    np_e = jnp.sum(pieces_ie, axis=0)
    rem = np_e % STEP_PIECES
    head_e = jnp.where(rem > 0, rem, jnp.minimum(np_e, STEP_PIECES))
    nt_e = np_e // STEP_PIECES + (rem > 0)
    tile_end = jnp.cumsum(nt_e)
    tile_start = tile_end - nt_e
    n_comp = tile_end[-1]
    q_start_e = jnp.cumsum(np_e) - np_e
    n_used = jnp.sum(np_e)
    tt = jnp.arange(MOE_MAX_STEPS, dtype=jnp.int32)
    e_t = jnp.minimum(jnp.sum(tt[:, None] >= tile_end[None, :], axis=1), N_EXPERTS - 1).astype(jnp.int32)
    is_comp = tt < n_comp
    k = tt - tile_start[e_t]
    q0_comp = q_start_e[e_t] + jnp.where(k == 0, 0, head_e[e_t] + STEP_PIECES * (k - 1))
    live_comp = jnp.where(k == 0, head_e[e_t], STEP_PIECES)
    q0_fill = n_used + STEP_PIECES * (tt - n_comp)
    q0 = jnp.where(is_comp, q0_comp, q0_fill)
    live = jnp.where(is_comp, live_comp, jnp.clip(N_PIECES - q0_fill, 0, STEP_PIECES))
    n_busy = n_comp + (N_PIECES - n_used + STEP_PIECES - 1) // STEP_PIECES
    length = jnp.concatenate([pieces_ie.T.reshape(-1), tail_i])
    src = jnp.concatenate([seg_src.T.reshape(-1), gbase + used_i])
    dest = jnp.cumsum(length) - length
    delta = src - dest
    d_delta = delta - jnp.concatenate([jnp.zeros((1,), jnp.int32), delta[:-1]])
    lane = jnp.arange(STEP_PIECES, dtype=jnp.int32)
    qq = (q0[:, None] + lane[None, :]).reshape(-1)
    ok = (lane[None, :] < live[:, None]).reshape(-1)
    piece = qq + _segment_offsets(dest, d_delta, qq)
    dump =(N_PIECES + (tt % 2)[:, None] * STEP_PIECES + lane[None, :]).reshape(-1)
    tbl_out = jnp.where(ok, piece, dump).astype(jnp.int32)
    last = jnp.maximum(n_comp - 1, 0)
    tbl_in = jnp.where(ok & jnp.repeat(is_comp, STEP_PIECES), piece, 0).reshape(MOE_MAX_STEPS, STEP_PIECES)
    tbl_in = jnp.where(is_comp[:, None], tbl_in, tbl_in[last][None, :]).reshape(-1).astype(jnp.int32)
    texp = jnp.where(is_comp, e_t, e_t[last]).astype(jnp.int32)
    first = (is_comp & (k == 0)).astype(jnp.int32)
    kind = jnp.where(is_comp, jnp.where(live <= HALF_PIECES, 1, 2), jnp.where(tt < n_busy, 0, 3))
    return tbl_in, tbl_out, texp, kind.astype(jnp.int32), first


def _moe_grouped_kernel(tin_ref, tout_ref, texp_ref, kind_ref, first_ref,
                        x_hbm, wu_ref, wd_ref, bg_ref, bl_ref, bd_ref, y_hbm,
                        wg_s, wl_s, wd_s, xbuf, ybuf, xsems, ysems):
    npc = STEP_PIECES
    t = pl.program_id(0)
    nsteps = pl.num_programs(0)
    kind = kind_ref[t]
    slot = t % 2

    def x_copy(step, buf, p):
        src = tin_ref[step * npc + p]
        return pltpu.make_async_copy(
            x_hbm.at[pl.ds(pl.multiple_of(src * PIECE, PIECE), PIECE), :],
            xbuf.at[buf, pl.ds(p * PIECE, PIECE), :],
            xsems.at[buf, p])

    def x_each(step, buf, fn):
        k = kind_ref[step]

        @pl.when(k == 2)
        def _():
            for p in range(STEP_PIECES):
                fn(x_copy(step, buf, p))

        @pl.when(k == 1)
        def _():
            for p in range(HALF_PIECES):
                fn(x_copy(step, buf, p))

    def y_copy(step, buf, p):
        dst = tout_ref[step * npc + p]
        return pltpu.make_async_copy(
            ybuf.at[buf, pl.ds(p * PIECE, PIECE), :],
            y_hbm.at[pl.ds(pl.multiple_of(dst * PIECE, PIECE), PIECE), :],
            ysems.at[buf, p])

    def y_wait(step, buf):
        for p in range(npc):
            y_copy(step, buf, p).wait()

    @pl.when(t == 0)
    def _():
        ybuf[...] = jnp.zeros_like(ybuf)
        x_each(0, 0, lambda cp: cp.start())

    @pl.when(t >= 2)
    def _():
        y_wait(t - 2, slot)

    @pl.when(t + 1 < nsteps)
    def _():
        x_each(t + 1, 1 - slot, lambda cp: cp.start())

    @pl.when((kind > 0) & (kind < 3) & (first_ref[t] == 1))
    def _():
        cb = 256
        r = lax.broadcasted_iota(jnp.int32, (cb, cb), 0)
        c = lax.broadcasted_iota(jnp.int32, (cb, cb), 1)
        perm = jnp.where(r == jnp.where(c < cb // 2, 2 * c, 2 * (c - cb // 2) + 1), 1.0, 0.0).astype(BF16)
        for blk in range(2 * D_FF // cb):
            wp = _dg(wu_ref[0, :, blk * cb:(blk + 1) * cb].astype(BF16), perm, NN).astype(BF16)
            wg_s[:, blk * (cb // 2):(blk + 1) * (cb // 2)] = wp[:, :cb // 2]
            wl_s[:, blk * (cb // 2):(blk + 1) * (cb // 2)] = wp[:, cb // 2:]
        wd_s[...] = wd_ref[0].astype(BF16)

    x_each(t, slot, lambda cp: cp.wait())

    def expert(n_pieces):
        xb = xbuf[slot, 0:n_pieces * PIECE, :]
        groups = [slice(j * FF_GROUP, (j + 1) * FF_GROUP) for j in range(D_FF // FF_GROUP)]
        glu = [jnp.minimum(_dg(xb, wg_s[:, c], NN) + bg_ref[0, :, c], SWIGLU_LIMIT) for c in groups]
        lin = [jnp.clip(_dg(xb, wl_s[:, c], NN) + bl_ref[0, :, c], -SWIGLU_LIMIT, SWIGLU_LIMIT)
               for c in groups]
        act = [(g * _sigmoid(SWIGLU_ALPHA * g) * (l + 1.0)).astype(BF16) for g, l in zip(glu, lin)]
        y = bd_ref[0] + _dg(act[0], wd_s[groups[0], :], NN)
        for c, a in zip(groups[1:], act[1:]):
            y = y + _dg(a, wd_s[c, :], NN)
        ybuf[slot, 0:n_pieces * PIECE, :] = y.astype(BF16)

    @pl.when(kind == 2)
    def _():
        expert(STEP_PIECES)

    @pl.when(kind == 1)
    def _():
        expert(HALF_PIECES)

    @pl.when(kind == 0)
    def _():
        ybuf[slot] = jnp.zeros((npc * PIECE, D_MODEL), BF16)

    for p in range(npc):
        y_copy(t, slot, p).start()

    @pl.when(t == nsteps - 1)
    def _():
        y_wait(t - 1, 1 - slot)
        y_wait(t, slot)


def _moe_grouped(slots, tables, w_up, w_down, bg, bl, bd):
    npc = STEP_PIECES
    by_expert = lambda shape: pl.BlockSpec(
        shape, lambda t, tin, tout, texp, kind, first: (texp[t],) + (0,) * (len(shape) - 1))
    return pl.pallas_call(
        _moe_grouped_kernel,
        grid_spec=pltpu.PrefetchScalarGridSpec(
            num_scalar_prefetch=5,
            grid=(MOE_MAX_STEPS,),
            in_specs=[pl.BlockSpec(memory_space=pl.ANY),
                      by_expert((1, D_MODEL, 2 * D_FF)), by_expert((1, D_FF, D_MODEL)),
                      by_expert((1, 1, D_FF)), by_expert((1, 1, D_FF)), by_expert((1, 1, D_MODEL))],
            out_specs=pl.BlockSpec(memory_space=pl.ANY),
            scratch_shapes=[pltpu.VMEM((D_MODEL, D_FF), BF16), pltpu.VMEM((D_MODEL, D_FF), BF16),
                            pltpu.VMEM((D_FF, D_MODEL), BF16),
                            pltpu.VMEM((2, npc * PIECE, D_MODEL), BF16),
                            pltpu.VMEM((2, npc * PIECE, D_MODEL), BF16),
                            pltpu.SemaphoreType.DMA((2, npc)),
                            pltpu.SemaphoreType.DMA((2, npc))]),
        out_shape=jax.ShapeDtypeStruct(((N_PIECES + 2 * npc) * PIECE, D_MODEL), BF16),
        compiler_params=pltpu.CompilerParams(vmem_limit_bytes=V7X_VMEM_LIMIT),
        name="moe_grouped",
    )(*tables, slots, w_up, w_down, bg, bl, bd)


def _combine_kernel(y_ref, lp_ref, w_ref, x1_ref, gt_ref, gf_ref, o_ref):
    tm = x1_ref.shape[0]
    rows = y_ref.shape[0]
    r = lax.broadcasted_iota(jnp.int32, (tm, tm), 0)
    c = lax.broadcasted_iota(jnp.int32, (tm, tm), 1)
    diag = r == c

    def column(x_row):
        return jnp.sum(jnp.where(diag, x_row, 0.0), axis=1, keepdims=True)

    pos = [column(lp_ref[k:k + 1, :].astype(F32)).astype(jnp.int32) for k in range(TOP_K)]
    wgt = [column(w_ref[k:k + 1, :]) for k in range(TOP_K)]
    chunk = 512
    moe = jnp.zeros((tm, D_MODEL), F32)
    for r0 in range(0, rows, chunk):
        si = lax.broadcasted_iota(jnp.int32, (tm, chunk), 1) + r0
        wt = jnp.zeros((tm, chunk), F32)
        for k in range(TOP_K):
            wt = jnp.where(si == pos[k], wgt[k], wt)
        moe = moe + _dg(wt.astype(BF16), y_ref[r0:r0 + chunk, :], NN)
    gt = gt_ref[...].reshape(-1, D_MODEL)
    o_ref[...] = _rms(x1_ref[...] + gt * moe, gf_ref[...])


def _combine(yslots, lp, w, x1, mod3, mod2, g_final, tm, slot_block):
    t = x1.shape[0]
    nt = t // tm
    rows = SLOT_ROWS[tm]
    if mod3 is not None:
        per_b = (t // mod3.shape[0]) // tm
        gspec = pl.BlockSpec((1, 1, D_MODEL), lambda i: (i // per_b, 0, 5))
        mod = mod3
    else:
        gspec = pl.BlockSpec((tm, D_MODEL), lambda i: (i, 5))
        mod = mod2
    return pl.pallas_call(
        _combine_kernel,
        grid=(nt,),
        in_specs=[pl.BlockSpec((rows, D_MODEL), lambda i: (slot_block + i, 0)),
                  pl.BlockSpec((TOP_K, tm), lambda i: (0, i)),
                  pl.BlockSpec((TOP_K, tm), lambda i: (0, i)),
                  pl.BlockSpec((tm, D_MODEL), lambda i: (i, 0)),
                  gspec,
                  pl.BlockSpec((1, D_MODEL), lambda i: (0, 0))],
        out_specs=pl.BlockSpec((tm, D_MODEL), lambda i: (i, 0)),
        out_shape=jax.ShapeDtypeStruct((t, D_MODEL), F32),
        compiler_params=pltpu.CompilerParams(vmem_limit_bytes=V7X_VMEM_LIMIT),
        name="moe_combine",
    )(yslots, lp, w, x1, mod, g_final.reshape(1, -1))


def kernel(x_prompt, x_sample, c_prompt, c_sample, cache_k_win, cache_v_win, state_hgrn, w_ada, b_ada,
           g_mix, g_ffn, w_in, attn_sinks, g_attn_out, hg_lb_logits, g_hg_out, w_out, w_router, b_router,
           w_up, b_up, w_down, b_down, g_final):
    batch, seq, d = x_prompt.shape
    nsamp = x_sample.shape[0]
    win = cache_k_win.shape[2]
    layer = 0

    mod = _modulation(jnp.concatenate([c_prompt, c_sample], axis=0), w_ada[layer], b_ada[layer])
    mod_p = mod[:batch].reshape(batch, 1, 6 * d)
    mod_s = mod[batch:]

    assert (batch * seq, nsamp) == (N_PROMPT_TILES * TOK_TILE, SAMPLE_TILE)
    w_in_b = w_in[layer].astype(BF16)
    w_out_b = w_out[layer].astype(BF16)
    w_router_t = w_router[layer].T
    bg = b_up[layer][:, None, 0::2]
    bl = b_up[layer][:, None, 1::2]
    bd = b_down[layer][:, None, :]

    xp = x_prompt.reshape(batch * seq, d)
    cos_p, sin_p = _rope_tables(np.arange(seq))
    qa, ka, va, qh, fh, ih, gh = _in_projection(xp, mod_p, None, g_mix[layer], w_in_b, cos_p, sin_p, TOK_TILE)
    oa = _attention_prompt(qa, ka, va, attn_sinks[layer], g_attn_out[layer], batch)
    oh, s_prompt = _hgrn_prompt(qh, fh, ih, gh, hg_lb_logits, g_hg_out[layer], batch)
    x1_p, slots, lp_p, cw_p, pc_p = _out_projection(
        xp, oa, oh, mod_p, None, g_ffn[layer], w_out_b, w_router_t, b_router[layer], TOK_TILE, None, 0)
    k_win_p = ka.reshape(batch, seq, ATTN_KV_HEADS, HEAD_DIM)[:, seq - win:]
    v_win_p = va.reshape(batch, seq, ATTN_KV_HEADS, HEAD_DIM)[:, seq - win:]

    xs = x_sample.reshape(nsamp, d)
    cos_s, sin_s = _rope_tables(np.full((nsamp,), PAST_LEN))
    qa, ka, va, qh, fh, ih, gh = _in_projection(xs, None, mod_s, g_mix[layer], w_in_b, cos_s, sin_s, nsamp)
    oa, k_win_s, v_win_s = _attention_sample(
        qa, ka, va, cache_k_win[layer].reshape(nsamp, win, KV_WIDTH),
        cache_v_win[layer].reshape(nsamp, win, KV_WIDTH), attn_sinks[layer], g_attn_out[layer])
    oh, s_sample = _hgrn_sample(qh, fh, ih, gh, hg_lb_logits, g_hg_out[layer], state_hgrn[layer])
    sample_block = TILE_PIECE_BASE[-1] * PIECE // SLOT_ROWS[SAMPLE_TILE]
    x1_s, slots, lp_s, cw_s, pc_s = _out_projection(
        xs, oa, oh, None, mod_s, g_ffn[layer], w_out_b, w_router_t, b_router[layer], SAMPLE_TILE,
        slots, sample_block)

    tables = _piece_tables(jnp.concatenate([pc_p[:, :, 0], pc_s[:, :, 0]], axis=0))
    yslots = _moe_grouped(slots, tables, w_up[layer], w_down[layer], bg, bl, bd)
    y_prompt = _combine(yslots, lp_p, cw_p, x1_p, mod_p, None, g_final, TOK_TILE, 0)
    y_sample = _combine(yslots, lp_s, cw_s, x1_s, None, mod_s, g_final, SAMPLE_TILE, sample_block)

    kv_shape = (1, nsamp, win, ATTN_KV_HEADS, HEAD_DIM)
    return (y_prompt.reshape(batch, seq, d), y_sample.reshape(nsamp, 1, d),
            k_win_p[None], v_win_p[None], s_prompt[None],
            k_win_s.reshape(kv_shape), v_win_s.reshape(kv_shape), s_sample[None])
```

```python
import functools

import numpy as np
import jax
import jax.numpy as jnp
from jax import lax
from jax.experimental import pallas as pl
from jax.experimental.pallas import tpu as pltpu

F32 = jnp.float32
BF16 = jnp.bfloat16

D_MODEL = 1024
SEQ = 2048
PAST_LEN = 16384
ATTN_HEADS = 8
ATTN_KV_HEADS = 2
HEAD_DIM = 64
ATTN_GROUP = ATTN_HEADS // ATTN_KV_HEADS
ATTN_WIDTH = ATTN_HEADS * HEAD_DIM
KV_WIDTH = ATTN_KV_HEADS * HEAD_DIM
WINDOW = 128
ATTN_STEP_BLOCKS = 8
ROT_DIM = HEAD_DIM // 4
ROPE_THETA = 500000.0
HG_HEADS = 4
HG_DK = 128
HG_DV = 128
HG_WIDTH = HG_HEADS * HG_DV
HG_CHUNK = 64
HG_SUB = 8
HG_STEP_CHUNKS = 16
IN_COLS = ATTN_WIDTH + 2 * KV_WIDTH + 4 * HG_WIDTH
N_EXPERTS = 32
TOP_K = 4
D_FF = D_MODEL
SWIGLU_ALPHA = 1.702
SWIGLU_LIMIT = 7.0
EPS = 1e-5
LOG2E = float(np.log2(np.e))

V7X_VMEM_LIMIT = 56 * 1024 * 1024

PIECE = 16
TOK_TILE = 512
IN_PROJ_TILE = 1024
SAMPLE_TILE = 128
N_PROMPT_TILES = 32
SLOT_ROWS = {TOK_TILE: 2560, SAMPLE_TILE: 1024}
TILE_PIECE_CAP = [SLOT_ROWS[TOK_TILE] // PIECE] * N_PROMPT_TILES + [SLOT_ROWS[SAMPLE_TILE] // PIECE]
TILE_PIECE_BASE = [i * TILE_PIECE_CAP[0] for i in range(N_PROMPT_TILES + 1)]
N_PIECES = sum(TILE_PIECE_CAP)
STEP_PIECES = 32
FF_GROUP = 512
HALF_PIECES = 16
MOE_MAX_STEPS = N_PIECES // STEP_PIECES + N_EXPERTS + 1 + 2

NN = (((1,), (0,)), ((), ()))
NT = (((1,), (1,)), ((), ()))
TN = (((0,), (0,)), ((), ()))


def _dg(a, b, dims):
    return lax.dot_general(a, b, dims, preferred_element_type=F32)


def _split3(x):
    h = x.astype(BF16)
    r = x - h.astype(F32)
    m = r.astype(BF16)
    l = (r - m.astype(F32)).astype(BF16)
    return h, m, l


def _dot_f32(a, b, dims):
    ah, am, al = _split3(a)
    bh, bm, bl = _split3(b)
    return (_dg(ah, bh, dims) + (_dg(ah, bm, dims) + _dg(am, bh, dims))
            + (_dg(am, bm, dims) + _dg(ah, bl, dims) + _dg(al, bh, dims)))


def _dot_exact_lhs(a_bf16, b, dims):
    bh, bm, bl = _split3(b)
    return _dg(a_bf16, bh, dims) + _dg(a_bf16, bm, dims) + _dg(a_bf16, bl, dims)


def _sigmoid(x):
    return 0.5 * jnp.tanh(0.5 * x) + 0.5


def _rms(x, g):
    return x * lax.rsqrt(jnp.mean(x * x, axis=-1, keepdims=True) + EPS) * g


def _mod_kernel(c_ref, w_ref, b_ref, o_ref):
    c = c_ref[...]
    o_ref[...] = _dot_f32(c * _sigmoid(c), w_ref[...], NN) + b_ref[...]


def _modulation(c_all, w_ada, b_ada):
    n = c_all.shape[0]
    return pl.pallas_call(
        _mod_kernel,
        grid=(6,),
        in_specs=[pl.BlockSpec((n, D_MODEL), lambda j: (0, 0)),
                  pl.BlockSpec((D_MODEL, D_MODEL), lambda j: (0, j)),
                  pl.BlockSpec((1, D_MODEL), lambda j: (0, j))],
        out_specs=pl.BlockSpec((n, D_MODEL), lambda j: (0, j)),
        out_shape=jax.ShapeDtypeStruct((n, 6 * D_MODEL), F32),
        compiler_params=pltpu.CompilerParams(vmem_limit_bytes=V7X_VMEM_LIMIT),
        name="adaln_mod",
    )(c_all, w_ada, b_ada.reshape(1, -1))


def _rotate(x, cos_t, sin_t):
    width = x.shape[-1]
    d = lax.broadcasted_iota(jnp.int32, x.shape, 1) % HEAD_DIM
    half = ROT_DIM // 2
    partner = jnp.where(d < half, pltpu.roll(x, width - half, axis=1), pltpu.roll(x, half, axis=1))
    return x * cos_t + partner * sin_t


def _inproj_kernel(x_ref, sh_ref, sc_ref, g_ref, w_ref, cos_ref, sin_ref,
                   qa_ref, ka_ref, va_ref, qh_ref, fh_ref, ih_ref, gh_ref):
    x = x_ref[...]
    sh = sh_ref[...].reshape(-1, D_MODEL)
    sc = sc_ref[...].reshape(-1, D_MODEL)
    hb = (_rms(x, g_ref[...]) * (1.0 + sc) + sh).astype(BF16)
    cos_k = cos_ref[...]
    sin_k = sin_ref[...]
    cos_q = jnp.concatenate([cos_k] * ATTN_GROUP, axis=1)
    sin_q = jnp.concatenate([sin_k] * ATTN_GROUP, axis=1)
    o = 0

    def project(width):
        nonlocal o
        z = _dg(hb, w_ref[:, o:o + width], NN)
        o += width
        return z

    qa = _rotate(project(ATTN_WIDTH), cos_q, sin_q)
    qa_ref[...] = (qa * (HEAD_DIM ** -0.5 * LOG2E)).astype(BF16)
    kv = project(2 * KV_WIDTH)
    ka_ref[...] = _rotate(kv[:, :KV_WIDTH], cos_k, sin_k)
    va_ref[...] = kv[:, KV_WIDTH:]
    qh_ref[...] = project(HG_WIDTH)
    fh_ref[...] = project(HG_WIDTH)
    ih_ref[...] = project(HG_WIDTH).astype(BF16)
    gh_ref[...] = project(HG_WIDTH)


def _rope_tables(positions):
    half = ROT_DIM // 2
    inv = ROPE_THETA ** (-(np.arange(half, dtype=np.float64) * 2.0 / ROT_DIM))
    ang = np.asarray(positions, np.float64)[:, None] * inv[None, :]
    cos_h = np.ones((len(positions), HEAD_DIM))
    sin_h = np.zeros((len(positions), HEAD_DIM))
    cos_h[:, :half] = np.cos(ang)
    cos_h[:, half:ROT_DIM] = np.cos(ang)
    sin_h[:, :half] = -np.sin(ang)
    sin_h[:, half:ROT_DIM] = np.sin(ang)
    cos_t = np.tile(cos_h, (1, ATTN_KV_HEADS)).astype(np.float32)
    sin_t = np.tile(sin_h, (1, ATTN_KV_HEADS)).astype(np.float32)
    return jnp.asarray(cos_t), jnp.asarray(sin_t)


def _in_projection(x, mod3, mod2, g_mix, w_in_bf16, cos_t, sin_t, tm):
    t = x.shape[0]
    nt = t // tm
    if mod3 is not None:
        per_b = (t // mod3.shape[0]) // tm
        sh_spec = pl.BlockSpec((1, 1, D_MODEL), lambda i: (i // per_b, 0, 0))
        sc_spec = pl.BlockSpec((1, 1, D_MODEL), lambda i: (i // per_b, 0, 1))
        mod = mod3
        ncs = cos_t.shape[0] // tm
        cs_spec = pl.BlockSpec((tm, KV_WIDTH), lambda i: (i % ncs, 0))
    else:
        sh_spec = pl.BlockSpec((tm, D_MODEL), lambda i: (i, 0))
        sc_spec = pl.BlockSpec((tm, D_MODEL), lambda i: (i, 1))
        mod = mod2
        cs_spec = pl.BlockSpec((tm, KV_WIDTH), lambda i: (i, 0))
    row = lambda w: pl.BlockSpec((tm, w), lambda i: (i, 0))
    return pl.pallas_call(
        _inproj_kernel,
        grid=(nt,),
        in_specs=[row(D_MODEL), sh_spec, sc_spec,
                  pl.BlockSpec((1, D_MODEL), lambda i: (0, 0)),
                  pl.BlockSpec((D_MODEL, IN_COLS), lambda i: (0, 0)),
                  cs_spec, cs_spec],
        out_specs=[row(ATTN_WIDTH), row(KV_WIDTH), row(KV_WIDTH),
                   row(HG_WIDTH), row(HG_WIDTH), row(HG_WIDTH), row(HG_WIDTH)],
        out_shape=[jax.ShapeDtypeStruct((t, ATTN_WIDTH), BF16),
                   jax.ShapeDtypeStruct((t, KV_WIDTH), F32),
                   jax.ShapeDtypeStruct((t, KV_WIDTH), F32),
                   jax.ShapeDtypeStruct((t, HG_WIDTH), F32),
                   jax.ShapeDtypeStruct((t, HG_WIDTH), F32),
                   jax.ShapeDtypeStruct((t, HG_WIDTH), BF16),
                   jax.ShapeDtypeStruct((t, HG_WIDTH), F32)],
        compiler_params=pltpu.CompilerParams(vmem_limit_bytes=V7X_VMEM_LIMIT),
        name="in_proj",
    )(x, mod, mod, g_mix.reshape(1, -1), w_in_bf16, cos_t, sin_t)


def _attn_prompt_kernel(sink_ref, q_ref, kc_ref, kp_ref, vc_ref, vp_ref, g_ref, o_ref):
    n = pl.program_id(1)
    blk = WINDOW
    nblk = q_ref.shape[0] // blk
    pair_w = 2 * HEAD_DIM
    low = lax.broadcasted_iota(jnp.int32, (1, pair_w), 1) < HEAD_DIM
    kall = jnp.concatenate([kp_ref[...], kc_ref[...]], axis=0)
    vall = jnp.concatenate([vp_ref[...], vc_ref[...]], axis=0)
    kroll = pltpu.roll(kall, HEAD_DIM, axis=1)
    vroll = pltpu.roll(vall, HEAD_DIM, axis=1)
    kdup = [jnp.where(low, kall, kroll).astype(BF16), jnp.where(low, kroll, kall).astype(BF16)]
    v_lo = [jnp.where(low, vall, 0.0).astype(BF16), jnp.where(low, vroll, 0.0).astype(BF16)]
    v_hi = [jnp.where(low, 0.0, vroll).astype(BF16), jnp.where(low, 0.0, vall).astype(BF16)]
    qi = lax.broadcasted_iota(jnp.int32, (blk, 2 * blk), 0)
    kj = lax.broadcasted_iota(jnp.int32, (blk, 2 * blk), 1)
    band = (kj >= qi) & (kj <= qi + blk)
    zero = jnp.zeros((), BF16)
    for bi in range(nblk):
        rows = slice(bi * blk, (bi + 1) * blk)
        krows = slice(bi * blk, (bi + 2) * blk)
        ok = band & ((kj >= blk) | (n > 0)) if bi == 0 else band
        heads = range(ATTN_HEADS)
        kv = [head // ATTN_GROUP for head in heads]
        sinks = [sink_ref[head] * LOG2E for head in heads]
        s = []
        for head in heads:
            qp = q_ref[rows, (head // 2) * pair_w:(head // 2 + 1) * pair_w]
            qm = jnp.where(low if head % 2 == 0 else jnp.logical_not(low), qp, zero)
            s.append(jnp.where(ok, _dg(qm, kdup[kv[head]][krows], NT), -jnp.inf))
        m = [jnp.maximum(jnp.max(s[head], axis=-1, keepdims=True), sinks[head]) for head in heads]
        p = [jnp.exp2(s[head] - m[head]) for head in heads]
        den = [jnp.sum(p[head], axis=-1, keepdims=True) + jnp.exp2(sinks[head] - m[head]) for head in heads]
        pv = [_dg(p[head].astype(BF16), (v_lo if head % 2 == 0 else v_hi)[kv[head]][krows], NN)
              for head in heads]
        pairs = [(pv[2 * j] + pv[2 * j + 1]) / jnp.where(low, den[2 * j], den[2 * j + 1])
                 for j in range(ATTN_HEADS // 2)]
        o_ref[rows, :] = _rms(jnp.concatenate(pairs, axis=1), g_ref[...]).astype(BF16)


def _attention_prompt(qa, ka, va, sinks, g_attn_out, batch):
    t = qa.shape[0]
    nb = t // batch // (WINDOW * ATTN_STEP_BLOCKS)
    cur = lambda w: pl.BlockSpec((WINDOW * ATTN_STEP_BLOCKS, w), lambda b, n: (b * nb + n, 0))
    prev = lambda w: pl.BlockSpec(
        (WINDOW, w), lambda b, n: (ATTN_STEP_BLOCKS * (b * nb + n) - jnp.minimum(n, 1), 0))
    return pl.pallas_call(
        _attn_prompt_kernel,
        grid=(batch, nb),
        in_specs=[pl.BlockSpec(memory_space=pltpu.SMEM),
                  cur(ATTN_WIDTH), cur(KV_WIDTH), prev(KV_WIDTH), cur(KV_WIDTH), prev(KV_WIDTH),
                  pl.BlockSpec((1, ATTN_WIDTH), lambda b, n: (0, 0))],
        out_specs=cur(ATTN_WIDTH),
        out_shape=jax.ShapeDtypeStruct((t, ATTN_WIDTH), BF16),
        name="attn_prompt",
    )(sinks, qa, ka, ka, va, va, g_attn_out.reshape(1, -1))


def _lower_bound(lb_logits_ref):
    lg = lb_logits_ref[...]
    e = jnp.exp(lg - jnp.max(lg, axis=0, keepdims=True))
    return e[0:1] / jnp.sum(e, axis=0, keepdims=True)


def _hgrn_prompt_kernel(q_ref, f_ref, i_ref, g_ref, lbl_ref, gn_ref, o_ref, sfin_ref, st_ref):
    step = pl.program_id(1)
    C = HG_CHUNK
    W = HG_WIDTH
    nsub = C // HG_SUB
    n_chunks = q_ref.shape[0] // C
    heads = [slice(hh * HG_DK, (hh + 1) * HG_DK) for hh in range(HG_HEADS)]

    @pl.when(step == 0)
    def _():
        st_ref[...] = jnp.zeros_like(st_ref)

    lb = _lower_bound(lbl_ref)
    r64 = lax.broadcasted_iota(jnp.int32, (C, C), 0)
    c64 = lax.broadcasted_iota(jnp.int32, (C, C), 1)
    tri = (r64 >= c64).astype(BF16)
    rsub = r64 % HG_SUB
    sr = lax.broadcasted_iota(jnp.int32, ((HG_SUB - 1) * C, C), 0)
    su = lax.broadcasted_iota(jnp.int32, ((HG_SUB - 1) * C, C), 1)
    sd, st_row = sr // C + 1, sr % C
    shifts = ((su == st_row - sd) & (st_row % HG_SUB >= sd)).astype(BF16)
    o_intra, q_state, upd, decay_last = {}, {}, {}, {}
    for ci in range(n_chunks):
        rows = slice(ci * C, (ci + 1) * C)
        f = lb + (1.0 - lb) * _sigmoid(f_ref[rows, :])
        kk = 1.0 - f
        logf = jnp.log2(f)
        q = q_ref[rows, :]
        v = i_ref[rows, :]
        b = _dot_exact_lhs(tri, logf, NN)
        k_shift = _dg(shifts, kk.astype(BF16), NN)
        prods = [q * kk]
        w = logf
        for d in range(1, HG_SUB):
            if d > 1:
                w = w + pltpu.roll(logf, d - 1, axis=0)
            prods.append(q * k_shift[(d - 1) * C:d * C] * jnp.exp2(w))
        ends = [b[j * HG_SUB + HG_SUB - 1:(j + 1) * HG_SUB] for j in range(nsub)]
        kt = kk * jnp.exp2(jnp.concatenate([jnp.broadcast_to(e, (HG_SUB, W)) for e in ends], axis=0) - b)
        lhs, rhs = [], []
        for j in range(nsub - 1):
            lo = (j + 1) * HG_SUB
            lhs.append(jnp.concatenate(
                [jnp.zeros((lo, W), F32), q[lo:] * jnp.exp2(b[lo:] - ends[j])], axis=0).astype(BF16))
            pieces = [kt[j * HG_SUB:lo]]
            if j > 0:
                pieces.insert(0, jnp.zeros((j * HG_SUB, W), F32))
            pieces.append(jnp.zeros((C - lo, W), F32))
            rhs.append(jnp.concatenate(pieces, axis=0).astype(BF16))
        q_state[ci] = (q * jnp.exp2(b)).astype(BF16)
        kd = (kk * jnp.exp2(ends[-1] - b)).astype(BF16)
        decay_last[ci] = jnp.exp2(ends[-1])
        a_off = [_dg(jnp.concatenate([x[:, sl] for x in lhs], axis=1),
                     jnp.concatenate([x[:, sl] for x in rhs], axis=1), NT) for sl in heads]
        for hh, sl in enumerate(heads):
            upd[ci, hh] = _dg(v[:, sl], kd[:, sl], TN)
        sums = [[jnp.sum(prods[d][:, sl], axis=-1, keepdims=True) for d in range(HG_SUB)] for sl in heads]
        for hh, sl in enumerate(heads):
            a = a_off[hh]
            for d in range(HG_SUB):
                a = jnp.where((c64 == r64 - d) & (rsub >= d), sums[hh][d], a)
            o_intra[ci, hh] = _dg(a.astype(BF16), v[:, sl], NN)
    finals = []
    outs = {}
    for hh, sl in enumerate(heads):
        st = st_ref[hh]
        for ci in range(n_chunks):
            outs[ci, hh] = o_intra[ci, hh] + _dg(q_state[ci][:, sl], st.astype(BF16), NT)
            st = st * decay_last[ci][:, sl] + upd[ci, hh]
        st_ref[hh] = st
        finals.append(st)
    for ci in range(n_chunks):
        rows = slice(ci * C, (ci + 1) * C)
        for hh, sl in enumerate(heads):
            g = g_ref[rows, sl]
            o_ref[rows, sl] = (_rms(outs[ci, hh], gn_ref[:, sl]) * (g * _sigmoid(g))).astype(BF16)

    @pl.when(step == pl.num_programs(1) - 1)
    def _():
        for hh in range(HG_HEADS):
            sfin_ref[0, hh] = finals[hh].T


def _hgrn_prompt(qh, fh, ih, gh, lb_logits, g_hg_out, batch):
    t = qh.shape[0]
    nc = t // batch // (HG_CHUNK * HG_STEP_CHUNKS)
    blk = pl.BlockSpec((HG_CHUNK * HG_STEP_CHUNKS, HG_WIDTH), lambda b, c: (b * nc + c, 0))
    const = lambda r: pl.BlockSpec((r, HG_WIDTH), lambda b, c: (0, 0))
    return pl.pallas_call(
        _hgrn_prompt_kernel,
        grid=(batch, nc),
        in_specs=[blk, blk, blk, blk, const(lb_logits.shape[0]), const(1)],
        out_specs=[blk, pl.BlockSpec((1, HG_HEADS, HG_DK, HG_DV), lambda b, c: (b, 0, 0, 0))],
        out_shape=[jax.ShapeDtypeStruct((t, HG_WIDTH), BF16),
                   jax.ShapeDtypeStruct((batch, HG_HEADS, HG_DK, HG_DV), F32)],
        scratch_shapes=[pltpu.VMEM((HG_HEADS, HG_DV, HG_DK), F32)],
        name="hgrn_prompt",
    )(qh, fh, ih, gh, lb_logits, g_hg_out.reshape(1, -1))


def _attn_sample_kernel(sink_ref, q_ref, kn_ref, vn_ref, kc_ref, vc_ref, g_ref,
                        o_ref, ko_ref, vo_ref):
    bt = q_ref.shape[0]
    win = kc_ref.shape[1]
    kc = kc_ref[...]
    vc = vc_ref[...]
    kn = kn_ref[...]
    vn = vn_ref[...]
    outs = []
    for h in range(ATTN_KV_HEADS):
        ls = slice(h * HEAD_DIM, (h + 1) * HEAD_DIM)
        q = q_ref[:, h * ATTN_GROUP:(h + 1) * ATTN_GROUP, :]
        s = jnp.einsum('bgd,bjd->bgj', q, kc[:, :, ls].astype(BF16), preferred_element_type=F32)
        s_new = jnp.sum(q.astype(F32) * kn[:, None, ls], axis=-1, keepdims=True)
        gi = lax.broadcasted_iota(jnp.int32, (1, ATTN_GROUP, 1), 1)
        sink = jnp.zeros((1, ATTN_GROUP, 1), F32)
        for g in range(ATTN_GROUP):
            sink = jnp.where(gi == g, sink_ref[h * ATTN_GROUP + g] * LOG2E, sink)
        m = jnp.maximum(jnp.maximum(jnp.max(s, axis=-1, keepdims=True), s_new), sink)
        p = jnp.exp2(s - m)
        p_new = jnp.exp2(s_new - m)
        den = jnp.sum(p, axis=-1, keepdims=True) + p_new + jnp.exp2(sink - m)
        o = jnp.einsum('bgj,bjd->bgd', p.astype(BF16), vc[:, :, ls].astype(BF16),
                       preferred_element_type=F32)
        o = (o + p_new * vn[:, None, ls]) / den
        outs.append(o)
    ssq = sum(jnp.sum(jnp.sum(o * o, axis=-1, keepdims=True), axis=1, keepdims=True) for o in outs)
    scale = lax.rsqrt(ssq / ATTN_WIDTH + EPS)
    for h in range(ATTN_KV_HEADS):
        o_ref[h] = (outs[h] * scale * g_ref[h][None]).astype(BF16)
    ri = lax.broadcasted_iota(jnp.int32, (win, KV_WIDTH), 0)
    for b in range(bt):
        ko_ref[b] = jnp.where(ri == win - 1, kn[b:b + 1], pltpu.roll(kc[b], win - 1, axis=0))
        vo_ref[b] = jnp.where(ri == win - 1, vn[b:b + 1], pltpu.roll(vc[b], win - 1, axis=0))


def _attention_sample(qa, ka, va, cache_k, cache_v, sinks, g_attn_out, bt=8):
    nb = qa.shape[0]
    win = cache_k.shape[1]
    q3 = qa.reshape(nb, ATTN_HEADS, HEAD_DIM)
    g3 = g_attn_out.reshape(ATTN_KV_HEADS, ATTN_GROUP, HEAD_DIM)
    row = lambda w: pl.BlockSpec((bt, w), lambda i: (i, 0))
    cache = pl.BlockSpec((bt, win, KV_WIDTH), lambda i: (i, 0, 0))
    o4, k_new, v_new = pl.pallas_call(
        _attn_sample_kernel,
        grid=(nb // bt,),
        in_specs=[pl.BlockSpec(memory_space=pltpu.SMEM),
                  pl.BlockSpec((bt, ATTN_HEADS, HEAD_DIM), lambda i: (i, 0, 0)),
                  row(KV_WIDTH), row(KV_WIDTH), cache, cache,
                  pl.BlockSpec((ATTN_KV_HEADS, ATTN_GROUP, HEAD_DIM), lambda i: (0, 0, 0))],
        out_specs=[pl.BlockSpec((ATTN_KV_HEADS, bt, ATTN_GROUP, HEAD_DIM), lambda i: (0, i, 0, 0)),
                   cache, cache],
        out_shape=[jax.ShapeDtypeStruct((ATTN_KV_HEADS, nb, ATTN_GROUP, HEAD_DIM), BF16),
                   jax.ShapeDtypeStruct(cache_k.shape, F32),
                   jax.ShapeDtypeStruct(cache_v.shape, F32)],
        name="attn_sample",
    )(sinks, q3, ka, va, cache_k, cache_v, g3)
    oa = jnp.transpose(o4, (1, 0, 2, 3)).reshape(nb, ATTN_WIDTH)
    return oa, k_new, v_new


def _hgrn_sample_kernel(q_ref, f_ref, i_ref, g_ref, lbl_ref, gn_ref, s0_ref, o_ref, s_ref):
    bt = q_ref.shape[0]
    lb = _lower_bound(lbl_ref)
    f = lb + (1.0 - lb) * _sigmoid(f_ref[...])
    kk = 1.0 - f
    q = q_ref[...]
    v = i_ref[...].astype(F32)
    g = g_ref[...]
    gate = g * _sigmoid(g)
    r = lax.broadcasted_iota(jnp.int32, (HG_DK, HG_DK), 0)
    cc = lax.broadcasted_iota(jnp.int32, (HG_DK, HG_DK), 1)
    diag = r == cc

    def column(x_row):
        return jnp.sum(jnp.where(diag, x_row, 0.0), axis=1, keepdims=True)

    units = [(b, hh, slice(hh * HG_DK, (hh + 1) * HG_DK)) for b in range(bt) for hh in range(HG_HEADS)]
    f_col = [column(f[b:b + 1, sl]) for b, hh, sl in units]
    k_col = [column(kk[b:b + 1, sl]) for b, hh, sl in units]
    q_col = [column(q[b:b + 1, sl]) for b, hh, sl in units]
    s_new = [f_col[u] * s0_ref[b, hh] + k_col[u] * v[b:b + 1, sl] for u, (b, hh, sl) in enumerate(units)]
    for u, (b, hh, sl) in enumerate(units):
        s_ref[b, hh] = s_new[u]
    outs = [_rms(jnp.sum(s_new[u] * q_col[u], axis=0, keepdims=True), gn_ref[:, sl])
            for u, (b, hh, sl) in enumerate(units)]
    for b in range(bt):
        row = jnp.concatenate(outs[b * HG_HEADS:(b + 1) * HG_HEADS], axis=1)
        o_ref[b:b + 1, :] = (row * gate[b:b + 1]).astype(BF16)


def _hgrn_sample(qh, fh, ih, gh, lb_logits, g_hg_out, state, bt=8):
    nb = qh.shape[0]
    row = pl.BlockSpec((bt, HG_WIDTH), lambda i: (i, 0))
    const = lambda r: pl.BlockSpec((r, HG_WIDTH), lambda i: (0, 0))
    st = pl.BlockSpec((bt, HG_HEADS, HG_DK, HG_DV), lambda i: (i, 0, 0, 0))
    return pl.pallas_call(
        _hgrn_sample_kernel,
        grid=(nb // bt,),
        in_specs=[row, row, row, row, const(lb_logits.shape[0]), const(1), st],
        out_specs=[row, st],
        out_shape=[jax.ShapeDtypeStruct((nb, HG_WIDTH), BF16),
                   jax.ShapeDtypeStruct(state.shape, F32)],
        name="hgrn_sample",
    )(qh, fh, ih, gh, lb_logits, g_hg_out.reshape(1, -1), state)


def _outproj_kernel(x_ref, oa_ref, oh_ref, gt_ref, sh_ref, sc_ref, g_ref, wo_ref, wr_ref, br_ref, *rest,
                    n_tiles):
    x1_ref, xs_ref, lp_ref, w_ref, pc_ref, hb_scr, lp_scr = rest[-7:]
    i = pl.program_id(0)
    tm = x_ref.shape[0]
    rows = xs_ref.shape[0]

    @pl.when(i == 0)
    def _():
        hb_scr[...] = jnp.zeros_like(hb_scr)
        lp_scr[...] = jnp.zeros_like(lp_scr)

    hb_prev = jnp.where(i <= n_tiles, hb_scr[...], jnp.zeros((), BF16))
    lp_prev = [lp_scr[k:k + 1, :] for k in range(TOP_K)]
    chunk = 512
    pending = list(range(0, rows, chunk))

    def place_next():
        if pending:
            r0 = pending.pop(0)
            si = lax.broadcasted_iota(jnp.int32, (chunk, tm), 0) + r0
            hit = (si == lp_prev[0]) | (si == lp_prev[1]) | (si == lp_prev[2]) | (si == lp_prev[3])
            xs_ref[r0:r0 + chunk, :] = _dg(jnp.where(hit, 1.0, 0.0).astype(BF16), hb_prev, NN).astype(BF16)

    place_next()
    gt = gt_ref[...].reshape(-1, D_MODEL)
    sh = sh_ref[...].reshape(-1, D_MODEL)
    sc = sc_ref[...].reshape(-1, D_MODEL)
    mix = _dg(oa_ref[...], wo_ref[0:ATTN_WIDTH, :], NN) + _dg(oh_ref[...], wo_ref[ATTN_WIDTH:, :], NN)
    x1 = x_ref[...] + gt * mix
    x1_ref[...] = x1
    place_next()
    h2 = _rms(x1, g_ref[...]) * (1.0 + sc) + sh
    place_next()
    logits = _dot_f32(wr_ref[...], h2, NT) + br_ref[...]
    place_next()
    ei = lax.broadcasted_iota(jnp.int32, logits.shape, 0)
    vals, sel = [], []
    l = logits
    for _ in range(TOP_K):
        m = jnp.max(l, axis=0, keepdims=True)
        idx = jnp.min(jnp.where(l == m, ei, N_EXPERTS), axis=0, keepdims=True)
        pick = ei == idx
        vals.append(m)
        sel.append(pick)
        l = jnp.where(pick, -jnp.inf, l)
    ex = [jnp.exp(v - vals[0]) for v in vals]
    den = ex[0] + ex[1] + ex[2] + ex[3]
    for k in range(TOP_K):
        w_ref[k:k + 1, :] = ex[k] / den
    place_next()

    member = jnp.where(sel[0] | sel[1] | sel[2] | sel[3], 1.0, 0.0)
    tr = lax.broadcasted_iota(jnp.int32, (tm, tm), 0)
    tc = lax.broadcasted_iota(jnp.int32, (tm, tm), 1)
    rank = _dg(member.astype(BF16), (tr < tc).astype(BF16), NN)
    pieces = jnp.floor((jnp.sum(member, axis=1, keepdims=True) + (PIECE - 1)) * (1.0 / PIECE))
    er = lax.broadcasted_iota(jnp.int32, (N_EXPERTS, N_EXPERTS), 0)
    ec = lax.broadcasted_iota(jnp.int32, (N_EXPERTS, N_EXPERTS), 1)
    pieces_b = jnp.broadcast_to(pieces, (N_EXPERTS, 128))
    pc_ref[0] = pieces_b.astype(jnp.int32)
    base = _dg((ec < er).astype(BF16), pieces_b.astype(BF16), NN)[:, 0:1] * PIECE
    slot = base + rank
    for k in range(TOP_K):
        lp = jnp.sum(jnp.where(sel[k], slot, 0.0), axis=0, keepdims=True).astype(jnp.int32)
        lp_ref[k:k + 1, :] = lp
        lp_scr[k:k + 1, :] = lp
    while pending:
        place_next()
    hb_scr[...] = h2.astype(BF16)


def _out_projection(x, oa, oh, mod3, mod2, g_ffn, w_out_bf16, w_router_t, b_router, tm, slots, slot_block):
    t = x.shape[0]
    nt = t // tm
    rows = SLOT_ROWS[tm]
    real = lambda i: jnp.minimum(i, nt - 1)
    if mod3 is not None:
        per_b = (t // mod3.shape[0]) // tm
        mspec = lambda j: pl.BlockSpec((1, 1, D_MODEL), lambda i: (real(i) // per_b, 0, j))
        mod = mod3
    else:
        mspec = lambda j: pl.BlockSpec((tm, D_MODEL), lambda i: (real(i), j))
        mod = mod2
    row = lambda w: pl.BlockSpec((tm, w), lambda i: (real(i), 0))
    full = lambda a: pl.BlockSpec(a.shape, lambda i: (0,) * a.ndim)
    g2 = g_ffn.reshape(1, -1)
    br = b_router.reshape(-1, 1)
    args = [x, oa, oh, mod, mod, mod, g2, w_out_bf16, w_router_t, br]
    in_specs = [row(D_MODEL), row(ATTN_WIDTH), row(HG_WIDTH), mspec(2), mspec(3), mspec(4),
                full(g2), full(w_out_bf16), full(w_router_t), full(br)]
    aliases = {}
    n_fill = 0
    if slots is not None:
        args.append(slots)
        in_specs.append(pl.BlockSpec(memory_space=pl.ANY))
        aliases = {len(args) - 1: 1}
    else:
        n_fill = pl.cdiv(N_PIECES * PIECE - nt * rows, rows)
    return pl.pallas_call(
        functools.partial(_outproj_kernel, n_tiles=nt),
        grid=(nt + 1 + n_fill,),
        in_specs=in_specs,
        out_specs=[row(D_MODEL),
                   pl.BlockSpec((rows, D_MODEL), lambda i: (slot_block + jnp.maximum(i - 1, 0), 0)),
                   pl.BlockSpec((TOP_K, tm), lambda i: (0, real(i))),
                   pl.BlockSpec((TOP_K, tm), lambda i: (0, real(i))),
                   pl.BlockSpec((1, N_EXPERTS, 128), lambda i: (real(i), 0, 0))],
        out_shape=[jax.ShapeDtypeStruct((t, D_MODEL), F32),
                   jax.ShapeDtypeStruct((N_PIECES * PIECE, D_MODEL), BF16),
                   jax.ShapeDtypeStruct((TOP_K, t), jnp.int32),
                   jax.ShapeDtypeStruct((TOP_K, t), F32),
                   jax.ShapeDtypeStruct((nt, N_EXPERTS, 128), jnp.int32)],
        scratch_shapes=[pltpu.VMEM((tm, D_MODEL), BF16), pltpu.VMEM((8, tm), jnp.int32)],
        input_output_aliases=aliases,
        compiler_params=pltpu.CompilerParams(vmem_limit_bytes=V7X_VMEM_LIMIT),
        name="out_proj_router",
    )(*args)


def _segment_offsets_kernel(dest_ref, dd_ref, q_ref, o_ref):
    reached = dest_ref[...] <= q_ref[...]
    o_ref[...] = jnp.sum(jnp.where(reached, dd_ref[...], 0), axis=0, keepdims=True)


def _segment_offsets(dest, d_delta, queries):
    lanes = 512
    n_seg = -(-dest.shape[0] // 8) * 8
    n_q = -(-queries.shape[0] // lanes) * lanes
    never = jnp.iinfo(jnp.int32).max
    dest_c = jnp.pad(dest, (0, n_seg - dest.shape[0]), constant_values=never).reshape(n_seg, 1)
    dd_c = jnp.pad(d_delta, (0, n_seg - d_delta.shape[0])).reshape(n_seg, 1)
    q_r = jnp.pad(queries, (0, n_q - queries.shape[0])).reshape(1, n_q)
    seg = pl.BlockSpec((n_seg, 1), lambda i: (0, 0))
    out = pl.pallas_call(
        _segment_offsets_kernel,
        grid=(n_q // lanes,),
        in_specs=[seg, seg, pl.BlockSpec((1, lanes), lambda i: (0, i))],
        out_specs=pl.BlockSpec((1, lanes), lambda i: (0, i)),
        out_shape=jax.ShapeDtypeStruct((1, n_q), jnp.int32),
        name="segment_offsets",
    )(dest_c, dd_c, q_r)
    return out[0, :queries.shape[0]]


def _prefix_sums(x):
    i = jnp.arange(x.shape[-1])
    return jnp.sum(jnp.where(i[None, :] <= i[:, None], x[..., None, :], 0), axis=-1)


def _piece_tables(pieces_ie):
    cap = jnp.asarray(TILE_PIECE_CAP, jnp.int32)
    gbase = jnp.asarray(TILE_PIECE_BASE, jnp.int32)
    experts = jnp.arange(N_EXPERTS, dtype=jnp.int32)
    seg_src = gbase[:, None] + _prefix_sums(pieces_ie) - pieces_ie
    used_i = jnp.sum(pieces_ie, axis=1)
    tail_i = cap - used_i
    np_e = jnp.sum(pieces_ie, axis=0)
    rem = np_e % STEP_PIECES
    head_e = jnp.where(rem > 0, rem, jnp.minimum(np_e, STEP_PIECES))
    nt_e = np_e // STEP_PIECES + (rem > 0)
    tile_end = _prefix_sums(nt_e)
    tile_start = tile_end - nt_e
    n_comp = jnp.sum(nt_e)
    q_start_e = _prefix_sums(np_e) - np_e
    n_used = jnp.sum(np_e)
    tt = jnp.arange(MOE_MAX_STEPS, dtype=jnp.int32)
    e_t = jnp.minimum(jnp.sum(tt[:, None] >= tile_end[None, :], axis=1), N_EXPERTS - 1).astype(jnp.int32)
    of_step = lambda v: jnp.sum(jnp.where(e_t[:, None] == experts[None, :], v[None, :], 0), axis=1)
    is_comp = tt < n_comp
    k = tt - of_step(tile_start)
    head_t = of_step(head_e)
    q0_comp = of_step(q_start_e) + jnp.where(k == 0, 0, head_t + STEP_PIECES * (k - 1))
    live_comp = jnp.where(k == 0, head_t, STEP_PIECES)
    q0_fill = n_used + STEP_PIECES * (tt - n_comp)
    q0 = jnp.where(is_comp, q0_comp, q0_fill)
    live = jnp.where(is_comp, live_comp, jnp.clip(N_PIECES - q0_fill, 0, STEP_PIECES))
    n_busy = n_comp + (N_PIECES - n_used + STEP_PIECES - 1) // STEP_PIECES
    pieces_ei = pieces_ie.T
    dest_seg = q_start_e[:, None] + _prefix_sums(pieces_ei) - pieces_ei
    dest = jnp.concatenate([dest_seg.reshape(-1), n_used + _prefix_sums(tail_i) - tail_i])
    src = jnp.concatenate([seg_src.T.reshape(-1), gbase + used_i])
    delta = src - dest
    d_delta = delta - jnp.concatenate([jnp.zeros((1,), jnp.int32), delta[:-1]])
    lane = jnp.arange(STEP_PIECES, dtype=jnp.int32)
    qq = (q0[:, None] + lane[None, :]).reshape(-1)
    ok = lane[None, :] < live[:, None]
    piece = (qq + _segment_offsets(dest, d_delta, qq)).reshape(MOE_MAX_STEPS, STEP_PIECES)
    dump = N_PIECES + (tt % 2)[:, None] * STEP_PIECES + lane[None, :]
    tbl_out = jnp.where(ok, piece, dump)
    last = jnp.maximum(n_comp - 1, 0)
    tbl_in = jnp.where(ok & is_comp[:, None], piece, 0)
    in_last = jnp.sum(jnp.where((tt == last)[:, None], tbl_in, 0), axis=0)
    tbl_in = jnp.where(is_comp[:, None], tbl_in, in_last[None, :])
    texp = jnp.where(is_comp, e_t, jnp.sum(jnp.where(tt == last, e_t, 0)))
    first = is_comp & (k == 0)
    kind = jnp.where(is_comp, jnp.where(live <= HALF_PIECES, 1, 2), jnp.where(tt < n_busy, 0, 3))
    i32 = lambda a: a.astype(jnp.int32)
    return i32(tbl_in.reshape(-1)), i32(tbl_out.reshape(-1)), i32(texp), i32(kind), i32(first)


def _moe_grouped_kernel(tin_ref, tout_ref, texp_ref, kind_ref, first_ref,
                        x_hbm, wu_ref, wd_ref, bg_ref, bl_ref, bd_ref, y_hbm,
                        wg_s, wl_s, wd_s, xbuf, ybuf, xsems, ysems):
    npc = STEP_PIECES
    t = pl.program_id(0)
    nsteps = pl.num_programs(0)
    kind = kind_ref[t]
    slot = t % 2

    def x_copy(step, buf, p):
        src = tin_ref[step * npc + p]
        return pltpu.make_async_copy(
            x_hbm.at[pl.ds(pl.multiple_of(src * PIECE, PIECE), PIECE), :],
            xbuf.at[buf, pl.ds(p * PIECE, PIECE), :],
            xsems.at[buf, p])

    def x_each(step, buf, fn):
        k = kind_ref[step]

        @pl.when(k == 2)
        def _():
            for p in range(STEP_PIECES):
                fn(x_copy(step, buf, p))

        @pl.when(k == 1)
        def _():
            for p in range(HALF_PIECES):
                fn(x_copy(step, buf, p))

    def y_copy(step, buf, p):
        dst = tout_ref[step * npc + p]
        return pltpu.make_async_copy(
            ybuf.at[buf, pl.ds(p * PIECE, PIECE), :],
            y_hbm.at[pl.ds(pl.multiple_of(dst * PIECE, PIECE), PIECE), :],
            ysems.at[buf, p])

    def y_wait(step, buf):
        for p in range(npc):
            y_copy(step, buf, p).wait()

    @pl.when(t == 0)
    def _():
        ybuf[...] = jnp.zeros_like(ybuf)
        x_each(0, 0, lambda cp: cp.start())

    @pl.when(t >= 2)
    def _():
        y_wait(t - 2, slot)

    @pl.when(t + 1 < nsteps)
    def _():
        x_each(t + 1, 1 - slot, lambda cp: cp.start())

    @pl.when((kind > 0) & (kind < 3) & (first_ref[t] == 1))
    def _():
        cb = 256
        r = lax.broadcasted_iota(jnp.int32, (cb, cb), 0)
        c = lax.broadcasted_iota(jnp.int32, (cb, cb), 1)
        perm = jnp.where(r == jnp.where(c < cb // 2, 2 * c, 2 * (c - cb // 2) + 1), 1.0, 0.0).astype(BF16)
        for blk in range(2 * D_FF // cb):
            wp = _dg(wu_ref[0, :, blk * cb:(blk + 1) * cb].astype(BF16), perm, NN).astype(BF16)
            wg_s[:, blk * (cb // 2):(blk + 1) * (cb // 2)] = wp[:, :cb // 2]
            wl_s[:, blk * (cb // 2):(blk + 1) * (cb // 2)] = wp[:, cb // 2:]
        wd_s[...] = wd_ref[0].astype(BF16)

    x_each(t, slot, lambda cp: cp.wait())

    def expert(n_pieces):
        xb = xbuf[slot, 0:n_pieces * PIECE, :]
        groups = [slice(j * FF_GROUP, (j + 1) * FF_GROUP) for j in range(D_FF // FF_GROUP)]
        glu = [jnp.minimum(_dg(xb, wg_s[:, c], NN) + bg_ref[0, :, c], SWIGLU_LIMIT) for c in groups]
        lin = [jnp.clip(_dg(xb, wl_s[:, c], NN) + bl_ref[0, :, c], -SWIGLU_LIMIT, SWIGLU_LIMIT)
               for c in groups]
        act = [(g * _sigmoid(SWIGLU_ALPHA * g) * (l + 1.0)).astype(BF16) for g, l in zip(glu, lin)]
        y = bd_ref[0] + _dg(act[0], wd_s[groups[0], :], NN)
        for c, a in zip(groups[1:], act[1:]):
            y = y + _dg(a, wd_s[c, :], NN)
        ybuf[slot, 0:n_pieces * PIECE, :] = y.astype(BF16)

    @pl.when(kind == 2)
    def _():
        expert(STEP_PIECES)

    @pl.when(kind == 1)
    def _():
        expert(HALF_PIECES)

    @pl.when(kind == 0)
    def _():
        ybuf[slot] = jnp.zeros((npc * PIECE, D_MODEL), BF16)

    for p in range(npc):
        y_copy(t, slot, p).start()

    @pl.when(t == nsteps - 1)
    def _():
        y_wait(t - 1, 1 - slot)
        y_wait(t, slot)


def _moe_grouped(slots, tables, w_up, w_down, bg, bl, bd):
    npc = STEP_PIECES
    by_expert = lambda shape: pl.BlockSpec(
        shape, lambda t, tin, tout, texp, kind, first: (texp[t],) + (0,) * (len(shape) - 1))
    return pl.pallas_call(
        _moe_grouped_kernel,
        grid_spec=pltpu.PrefetchScalarGridSpec(
            num_scalar_prefetch=5,
            grid=(MOE_MAX_STEPS,),
            in_specs=[pl.BlockSpec(memory_space=pl.ANY),
                      by_expert((1, D_MODEL, 2 * D_FF)), by_expert((1, D_FF, D_MODEL)),
                      by_expert((1, 1, D_FF)), by_expert((1, 1, D_FF)), by_expert((1, 1, D_MODEL))],
            out_specs=pl.BlockSpec(memory_space=pl.ANY),
            scratch_shapes=[pltpu.VMEM((D_MODEL, D_FF), BF16), pltpu.VMEM((D_MODEL, D_FF), BF16),
                            pltpu.VMEM((D_FF, D_MODEL), BF16),
                            pltpu.VMEM((2, npc * PIECE, D_MODEL), BF16),
                            pltpu.VMEM((2, npc * PIECE, D_MODEL), BF16),
                            pltpu.SemaphoreType.DMA((2, npc)),
                            pltpu.SemaphoreType.DMA((2, npc))]),
        out_shape=jax.ShapeDtypeStruct(((N_PIECES + 2 * npc) * PIECE, D_MODEL), BF16),
        compiler_params=pltpu.CompilerParams(vmem_limit_bytes=V7X_VMEM_LIMIT),
        name="moe_grouped",
    )(*tables, slots, w_up, w_down, bg, bl, bd)


def _combine_kernel(y_ref, lp_ref, w_ref, x1_ref, gt_ref, gf_ref, o_ref):
    tm = x1_ref.shape[0]
    rows = y_ref.shape[0]
    r = lax.broadcasted_iota(jnp.int32, (tm, tm), 0)
    c = lax.broadcasted_iota(jnp.int32, (tm, tm), 1)
    diag = r == c

    def column(x_row):
        return jnp.sum(jnp.where(diag, x_row, 0.0), axis=1, keepdims=True)

    pos = [column(lp_ref[k:k + 1, :].astype(F32)).astype(jnp.int32) for k in range(TOP_K)]
    wgt = [column(w_ref[k:k + 1, :]) for k in range(TOP_K)]
    chunk = 512
    moe = jnp.zeros((tm, D_MODEL), F32)
    for r0 in range(0, rows, chunk):
        si = lax.broadcasted_iota(jnp.int32, (tm, chunk), 1) + r0
        wt = jnp.zeros((tm, chunk), F32)
        for k in range(TOP_K):
            wt = jnp.where(si == pos[k], wgt[k], wt)
        moe = moe + _dg(wt.astype(BF16), y_ref[r0:r0 + chunk, :], NN)
    gt = gt_ref[...].reshape(-1, D_MODEL)
    o_ref[...] = _rms(x1_ref[...] + gt * moe, gf_ref[...])


def _combine(yslots, lp, w, x1, mod3, mod2, g_final, tm, slot_block):
    t = x1.shape[0]
    nt = t // tm
    rows = SLOT_ROWS[tm]
    if mod3 is not None:
        per_b = (t // mod3.shape[0]) // tm
        gspec = pl.BlockSpec((1, 1, D_MODEL), lambda i: (i // per_b, 0, 5))
        mod = mod3
    else:
        gspec = pl.BlockSpec((tm, D_MODEL), lambda i: (i, 5))
        mod = mod2
    return pl.pallas_call(
        _combine_kernel,
        grid=(nt,),
        in_specs=[pl.BlockSpec((rows, D_MODEL), lambda i: (slot_block + i, 0)),
                  pl.BlockSpec((TOP_K, tm), lambda i: (0, i)),
                  pl.BlockSpec((TOP_K, tm), lambda i: (0, i)),
                  pl.BlockSpec((tm, D_MODEL), lambda i: (i, 0)),
                  gspec,
                  pl.BlockSpec((1, D_MODEL), lambda i: (0, 0))],
        out_specs=pl.BlockSpec((tm, D_MODEL), lambda i: (i, 0)),
        out_shape=jax.ShapeDtypeStruct((t, D_MODEL), F32),
        compiler_params=pltpu.CompilerParams(vmem_limit_bytes=V7X_VMEM_LIMIT),
        name="moe_combine",
    )(yslots, lp, w, x1, mod, g_final.reshape(1, -1))


def kernel(x_prompt, x_sample, c_prompt, c_sample, cache_k_win, cache_v_win, state_hgrn, w_ada, b_ada,
           g_mix, g_ffn, w_in, attn_sinks, g_attn_out, hg_lb_logits, g_hg_out, w_out, w_router, b_router,
           w_up, b_up, w_down, b_down, g_final):
    batch, seq, d = x_prompt.shape
    nsamp = x_sample.shape[0]
    win = cache_k_win.shape[2]
    layer = 0

    mod = _modulation(jnp.concatenate([c_prompt, c_sample], axis=0), w_ada[layer], b_ada[layer])
    mod_p = mod[:batch].reshape(batch, 1, 6 * d)
    mod_s = mod[batch:]

    assert (batch * seq, nsamp) == (N_PROMPT_TILES * TOK_TILE, SAMPLE_TILE)
    w_in_b = w_in[layer].astype(BF16)
    w_out_b = w_out[layer].astype(BF16)
    w_router_t = w_router[layer].T
    bg = b_up[layer][:, None, 0::2]
    bl = b_up[layer][:, None, 1::2]
    bd = b_down[layer][:, None, :]

    xp = x_prompt.reshape(batch * seq, d)
    cos_p, sin_p = _rope_tables(np.arange(seq))
    qa, ka, va, qh, fh, ih, gh = _in_projection(xp, mod_p, None, g_mix[layer], w_in_b, cos_p, sin_p,
                                                IN_PROJ_TILE)
    oa = _attention_prompt(qa, ka, va, attn_sinks[layer], g_attn_out[layer], batch)
    oh, s_prompt = _hgrn_prompt(qh, fh, ih, gh, hg_lb_logits, g_hg_out[layer], batch)
    x1_p, slots, lp_p, cw_p, pc_p = _out_projection(
        xp, oa, oh, mod_p, None, g_ffn[layer], w_out_b, w_router_t, b_router[layer], TOK_TILE, None, 0)
    k_win_p = ka.reshape(batch, seq, ATTN_KV_HEADS, HEAD_DIM)[:, seq - win:]
    v_win_p = va.reshape(batch, seq, ATTN_KV_HEADS, HEAD_DIM)[:, seq - win:]

    xs = x_sample.reshape(nsamp, d)
    cos_s, sin_s = _rope_tables(np.full((nsamp,), PAST_LEN))
    qa, ka, va, qh, fh, ih, gh = _in_projection(xs, None, mod_s, g_mix[layer], w_in_b, cos_s, sin_s, nsamp)
    oa, k_win_s, v_win_s = _attention_sample(
        qa, ka, va, cache_k_win[layer].reshape(nsamp, win, KV_WIDTH),
        cache_v_win[layer].reshape(nsamp, win, KV_WIDTH), attn_sinks[layer], g_attn_out[layer])
    oh, s_sample = _hgrn_sample(qh, fh, ih, gh, hg_lb_logits, g_hg_out[layer], state_hgrn[layer])
    sample_block = TILE_PIECE_BASE[-1] * PIECE // SLOT_ROWS[SAMPLE_TILE]
    x1_s, slots, lp_s, cw_s, pc_s = _out_projection(
        xs, oa, oh, None, mod_s, g_ffn[layer], w_out_b, w_router_t, b_router[layer], SAMPLE_TILE,
        slots, sample_block)

    tables = _piece_tables(jnp.concatenate([pc_p[:, :, 0], pc_s[:, :, 0]], axis=0))
    yslots = _moe_grouped(slots, tables, w_up[layer], w_down[layer], bg, bl, bd)
    y_prompt = _combine(yslots, lp_p, cw_p, x1_p, mod_p, None, g_final, TOK_TILE, 0)
    y_sample = _combine(yslots, lp_s, cw_s, x1_s, None, mod_s, g_final, SAMPLE_TILE, sample_block)

    kv_shape = (1, nsamp, win, ATTN_KV_HEADS, HEAD_DIM)
    return (y_prompt.reshape(batch, seq, d), y_sample.reshape(nsamp, 1, d),
            k_win_p[None], v_win_p[None], s_prompt[None],
            k_win_s.reshape(kv_shape), v_win_s.reshape(kv_shape), s_sample[None])
```

```python
import functools

import numpy as np
import jax
import jax.numpy as jnp
from jax import lax
from jax.experimental import pallas as pl
from jax.experimental.pallas import tpu as pltpu

F32 = jnp.float32
BF16 = jnp.bfloat16

D_MODEL = 1024
SEQ = 2048
PAST_LEN = 16384
ATTN_HEADS = 8
ATTN_KV_HEADS = 2
HEAD_DIM = 64
ATTN_GROUP = ATTN_HEADS // ATTN_KV_HEADS
ATTN_WIDTH = ATTN_HEADS * HEAD_DIM
KV_WIDTH = ATTN_KV_HEADS * HEAD_DIM
WINDOW = 128
ATTN_STEP_BLOCKS = 8
ROT_DIM = HEAD_DIM // 4
ROPE_THETA = 500000.0
HG_HEADS = 4
HG_DK = 128
HG_DV = 128
HG_WIDTH = HG_HEADS * HG_DV
HG_CHUNK = 64
HG_SUB = 8
HG_STEP_CHUNKS = 16
IN_COLS = ATTN_WIDTH + 2 * KV_WIDTH + 4 * HG_WIDTH
N_EXPERTS = 32
TOP_K = 4
D_FF = D_MODEL
SWIGLU_ALPHA = 1.702
SWIGLU_LIMIT = 7.0
EPS = 1e-5
LOG2E = float(np.log2(np.e))

V7X_VMEM_LIMIT = 56 * 1024 * 1024

PIECE = 16
TOK_TILE = 512
IN_PROJ_TILE = 1024
SAMPLE_TILE = 128
N_PROMPT_TILES = 32
SLOT_ROWS = {TOK_TILE: 2560, SAMPLE_TILE: 1024}
TILE_PIECE_CAP = [SLOT_ROWS[TOK_TILE] // PIECE] * N_PROMPT_TILES + [SLOT_ROWS[SAMPLE_TILE] // PIECE]
TILE_PIECE_BASE = [i * TILE_PIECE_CAP[0] for i in range(N_PROMPT_TILES + 1)]
N_PIECES = sum(TILE_PIECE_CAP)
STEP_PIECES = 32
FF_GROUP = 512
HALF_PIECES = 16
MOE_MAX_STEPS = N_PIECES // STEP_PIECES + N_EXPERTS + 1 + 2

NN = (((1,), (0,)), ((), ()))
NT = (((1,), (1,)), ((), ()))
TN = (((0,), (0,)), ((), ()))


def _dg(a, b, dims):
    return lax.dot_general(a, b, dims, preferred_element_type=F32)


def _split3(x):
    h = x.astype(BF16)
    r = x - h.astype(F32)
    m = r.astype(BF16)
    l = (r - m.astype(F32)).astype(BF16)
    return h, m, l


def _dot_f32(a, b, dims):
    ah, am, al = _split3(a)
    bh, bm, bl = _split3(b)
    return (_dg(ah, bh, dims) + (_dg(ah, bm, dims) + _dg(am, bh, dims))
            + (_dg(am, bm, dims) + _dg(ah, bl, dims) + _dg(al, bh, dims)))


def _split2(x):
    h = x.astype(BF16)
    return h, (x - h.astype(F32)).astype(BF16)


def _dot_hilo(a, b, dims):
    ah, al = _split2(a)
    bh, bl = _split2(b)
    return _dg(ah, bh, dims) + (_dg(ah, bl, dims) + _dg(al, bh, dims))


def _dot_exact_lhs(a_bf16, b, dims):
    bh, bm, bl = _split3(b)
    return _dg(a_bf16, bh, dims) + _dg(a_bf16, bm, dims) + _dg(a_bf16, bl, dims)


def _sigmoid(x):
    return 0.5 * jnp.tanh(0.5 * x) + 0.5


def _rms(x, g):
    return x * lax.rsqrt(jnp.mean(x * x, axis=-1, keepdims=True) + EPS) * g


def _mod_kernel(c_ref, w_ref, b_ref, o_ref):
    c = c_ref[...]
    o_ref[...] = _dot_f32(c * _sigmoid(c), w_ref[...], NN) + b_ref[...]


def _modulation(c_all, w_ada, b_ada):
    n = c_all.shape[0]
    return pl.pallas_call(
        _mod_kernel,
        grid=(6,),
        in_specs=[pl.BlockSpec((n, D_MODEL), lambda j: (0, 0)),
                  pl.BlockSpec((D_MODEL, D_MODEL), lambda j: (0, j)),
                  pl.BlockSpec((1, D_MODEL), lambda j: (0, j))],
        out_specs=pl.BlockSpec((n, D_MODEL), lambda j: (0, j)),
        out_shape=jax.ShapeDtypeStruct((n, 6 * D_MODEL), F32),
        compiler_params=pltpu.CompilerParams(vmem_limit_bytes=V7X_VMEM_LIMIT),
        name="adaln_mod",
    )(c_all, w_ada, b_ada.reshape(1, -1))


def _rotate(x, cos_t, sin_t):
    width = x.shape[-1]
    d = lax.broadcasted_iota(jnp.int32, x.shape, 1) % HEAD_DIM
    half = ROT_DIM // 2
    partner = jnp.where(d < half, pltpu.roll(x, width - half, axis=1), pltpu.roll(x, half, axis=1))
    return x * cos_t + partner * sin_t


def _inproj_kernel(x_ref, sh_ref, sc_ref, g_ref, w_ref, cos_ref, sin_ref,
                   qa_ref, ka_ref, va_ref, qh_ref, fh_ref, ih_ref, gh_ref):
    x = x_ref[...]
    sh = sh_ref[...].reshape(-1, D_MODEL)
    sc = sc_ref[...].reshape(-1, D_MODEL)
    hb = (_rms(x, g_ref[...]) * (1.0 + sc) + sh).astype(BF16)
    cos_k = cos_ref[...]
    sin_k = sin_ref[...]
    cos_q = jnp.concatenate([cos_k] * ATTN_GROUP, axis=1)
    sin_q = jnp.concatenate([sin_k] * ATTN_GROUP, axis=1)
    o = 0

    def project(width):
        nonlocal o
        z = _dg(hb, w_ref[:, o:o + width], NN)
        o += width
        return z

    qa = _rotate(project(ATTN_WIDTH), cos_q, sin_q)
    qa_ref[...] = (qa * (HEAD_DIM ** -0.5 * LOG2E)).astype(BF16)
    kv = project(2 * KV_WIDTH)
    ka_ref[...] = _rotate(kv[:, :KV_WIDTH], cos_k, sin_k)
    va_ref[...] = kv[:, KV_WIDTH:]
    qh_ref[...] = project(HG_WIDTH)
    fh_ref[...] = project(HG_WIDTH)
    ih_ref[...] = project(HG_WIDTH).astype(BF16)
    gh_ref[...] = project(HG_WIDTH)


def _rope_tables(positions):
    half = ROT_DIM // 2
    inv = ROPE_THETA ** (-(np.arange(half, dtype=np.float64) * 2.0 / ROT_DIM))
    ang = np.asarray(positions, np.float64)[:, None] * inv[None, :]
    cos_h = np.ones((len(positions), HEAD_DIM))
    sin_h = np.zeros((len(positions), HEAD_DIM))
    cos_h[:, :half] = np.cos(ang)
    cos_h[:, half:ROT_DIM] = np.cos(ang)
    sin_h[:, :half] = -np.sin(ang)
    sin_h[:, half:ROT_DIM] = np.sin(ang)
    cos_t = np.tile(cos_h, (1, ATTN_KV_HEADS)).astype(np.float32)
    sin_t = np.tile(sin_h, (1, ATTN_KV_HEADS)).astype(np.float32)
    return jnp.asarray(cos_t), jnp.asarray(sin_t)


def _in_projection(x, mod3, mod2, g_mix, w_in_bf16, cos_t, sin_t, tm):
    t = x.shape[0]
    nt = t // tm
    if mod3 is not None:
        per_b = (t // mod3.shape[0]) // tm
        sh_spec = pl.BlockSpec((1, 1, D_MODEL), lambda i: (i // per_b, 0, 0))
        sc_spec = pl.BlockSpec((1, 1, D_MODEL), lambda i: (i // per_b, 0, 1))
        mod = mod3
        ncs = cos_t.shape[0] // tm
        cs_spec = pl.BlockSpec((tm, KV_WIDTH), lambda i: (i % ncs, 0))
    else:
        sh_spec = pl.BlockSpec((tm, D_MODEL), lambda i: (i, 0))
        sc_spec = pl.BlockSpec((tm, D_MODEL), lambda i: (i, 1))
        mod = mod2
        cs_spec = pl.BlockSpec((tm, KV_WIDTH), lambda i: (i, 0))
    row = lambda w: pl.BlockSpec((tm, w), lambda i: (i, 0))
    return pl.pallas_call(
        _inproj_kernel,
        grid=(nt,),
        in_specs=[row(D_MODEL), sh_spec, sc_spec,
                  pl.BlockSpec((1, D_MODEL), lambda i: (0, 0)),
                  pl.BlockSpec((D_MODEL, IN_COLS), lambda i: (0, 0)),
                  cs_spec, cs_spec],
        out_specs=[row(ATTN_WIDTH), row(KV_WIDTH), row(KV_WIDTH),
                   row(HG_WIDTH), row(HG_WIDTH), row(HG_WIDTH), row(HG_WIDTH)],
        out_shape=[jax.ShapeDtypeStruct((t, ATTN_WIDTH), BF16),
                   jax.ShapeDtypeStruct((t, KV_WIDTH), F32),
                   jax.ShapeDtypeStruct((t, KV_WIDTH), F32),
                   jax.ShapeDtypeStruct((t, HG_WIDTH), F32),
                   jax.ShapeDtypeStruct((t, HG_WIDTH), F32),
                   jax.ShapeDtypeStruct((t, HG_WIDTH), BF16),
                   jax.ShapeDtypeStruct((t, HG_WIDTH), F32)],
        compiler_params=pltpu.CompilerParams(vmem_limit_bytes=V7X_VMEM_LIMIT),
        name="in_proj",
    )(x, mod, mod, g_mix.reshape(1, -1), w_in_bf16, cos_t, sin_t)


def _attn_prompt_kernel(sink_ref, q_ref, kc_ref, kp_ref, vc_ref, vp_ref, g_ref, o_ref):
    n = pl.program_id(1)
    blk = WINDOW
    nblk = q_ref.shape[0] // blk
    pair_w = 2 * HEAD_DIM
    low = lax.broadcasted_iota(jnp.int32, (1, pair_w), 1) < HEAD_DIM
    kall = jnp.concatenate([kp_ref[...], kc_ref[...]], axis=0)
    vall = jnp.concatenate([vp_ref[...], vc_ref[...]], axis=0)
    kroll = pltpu.roll(kall, HEAD_DIM, axis=1)
    vroll = pltpu.roll(vall, HEAD_DIM, axis=1)
    kdup = [jnp.where(low, kall, kroll).astype(BF16), jnp.where(low, kroll, kall).astype(BF16)]
    v_lo = [jnp.where(low, vall, 0.0).astype(BF16), jnp.where(low, vroll, 0.0).astype(BF16)]
    v_hi = [jnp.where(low, 0.0, vroll).astype(BF16), jnp.where(low, 0.0, vall).astype(BF16)]
    qi = lax.broadcasted_iota(jnp.int32, (blk, 2 * blk), 0)
    kj = lax.broadcasted_iota(jnp.int32, (blk, 2 * blk), 1)
    band = (kj >= qi) & (kj <= qi + blk)
    zero = jnp.zeros((), BF16)
    for bi in range(nblk):
        rows = slice(bi * blk, (bi + 1) * blk)
        krows = slice(bi * blk, (bi + 2) * blk)
        ok = band & ((kj >= blk) | (n > 0)) if bi == 0 else band
        heads = range(ATTN_HEADS)
        kv = [head // ATTN_GROUP for head in heads]
        sinks = [sink_ref[head] * LOG2E for head in heads]
        s = []
        for head in heads:
            qp = q_ref[rows, (head // 2) * pair_w:(head // 2 + 1) * pair_w]
            qm = jnp.where(low if head % 2 == 0 else jnp.logical_not(low), qp, zero)
            s.append(jnp.where(ok, _dg(qm, kdup[kv[head]][krows], NT), -jnp.inf))
        m = [jnp.maximum(jnp.max(s[head], axis=-1, keepdims=True), sinks[head]) for head in heads]
        p = [jnp.exp2(s[head] - m[head]) for head in heads]
        den = [jnp.sum(p[head], axis=-1, keepdims=True) + jnp.exp2(sinks[head] - m[head]) for head in heads]
        pv = [_dg(p[head].astype(BF16), (v_lo if head % 2 == 0 else v_hi)[kv[head]][krows], NN)
              for head in heads]
        pairs = [(pv[2 * j] + pv[2 * j + 1]) / jnp.where(low, den[2 * j], den[2 * j + 1])
                 for j in range(ATTN_HEADS // 2)]
        o_ref[rows, :] = _rms(jnp.concatenate(pairs, axis=1), g_ref[...]).astype(BF16)


def _attention_prompt(qa, ka, va, sinks, g_attn_out, batch):
    t = qa.shape[0]
    nb = t // batch // (WINDOW * ATTN_STEP_BLOCKS)
    cur = lambda w: pl.BlockSpec((WINDOW * ATTN_STEP_BLOCKS, w), lambda b, n: (b * nb + n, 0))
    prev = lambda w: pl.BlockSpec(
        (WINDOW, w), lambda b, n: (ATTN_STEP_BLOCKS * (b * nb + n) - jnp.minimum(n, 1), 0))
    return pl.pallas_call(
        _attn_prompt_kernel,
        grid=(batch, nb),
        in_specs=[pl.BlockSpec(memory_space=pltpu.SMEM),
                  cur(ATTN_WIDTH), cur(KV_WIDTH), prev(KV_WIDTH), cur(KV_WIDTH), prev(KV_WIDTH),
                  pl.BlockSpec((1, ATTN_WIDTH), lambda b, n: (0, 0))],
        out_specs=cur(ATTN_WIDTH),
        out_shape=jax.ShapeDtypeStruct((t, ATTN_WIDTH), BF16),
        name="attn_prompt",
    )(sinks, qa, ka, ka, va, va, g_attn_out.reshape(1, -1))


def _lower_bound(lb_logits_ref):
    lg = lb_logits_ref[...]
    e = jnp.exp(lg - jnp.max(lg, axis=0, keepdims=True))
    return e[0:1] / jnp.sum(e, axis=0, keepdims=True)


def _hgrn_prompt_kernel(q_ref, f_ref, i_ref, g_ref, lbl_ref, gn_ref, o_ref, sfin_ref, st_ref):
    step = pl.program_id(1)
    C = HG_CHUNK
    W = HG_WIDTH
    nsub = C // HG_SUB
    n_chunks = q_ref.shape[0] // C
    heads = [slice(hh * HG_DK, (hh + 1) * HG_DK) for hh in range(HG_HEADS)]

    @pl.when(step == 0)
    def _():
        st_ref[...] = jnp.zeros_like(st_ref)

    lb = _lower_bound(lbl_ref)
    r64 = lax.broadcasted_iota(jnp.int32, (C, C), 0)
    c64 = lax.broadcasted_iota(jnp.int32, (C, C), 1)
    tri = (r64 >= c64).astype(BF16)
    rsub = r64 % HG_SUB
    sr = lax.broadcasted_iota(jnp.int32, ((HG_SUB - 1) * C, C), 0)
    su = lax.broadcasted_iota(jnp.int32, ((HG_SUB - 1) * C, C), 1)
    sd, st_row = sr // C + 1, sr % C
    shifts = ((su == st_row - sd) & (st_row % HG_SUB >= sd)).astype(BF16)
    o_intra, q_state, upd, decay_last = {}, {}, {}, {}
    for ci in range(n_chunks):
        rows = slice(ci * C, (ci + 1) * C)
        f = lb + (1.0 - lb) * _sigmoid(f_ref[rows, :])
        kk = 1.0 - f
        logf = jnp.log2(f)
        q = q_ref[rows, :]
        v = i_ref[rows, :]
        b = _dot_exact_lhs(tri, logf, NN)
        k_shift = _dg(shifts, kk.astype(BF16), NN)
        prods = [q * kk]
        w = logf
        for d in range(1, HG_SUB):
            if d > 1:
                w = w + pltpu.roll(logf, d - 1, axis=0)
            prods.append(q * k_shift[(d - 1) * C:d * C] * jnp.exp2(w))
        ends = [b[j * HG_SUB + HG_SUB - 1:(j + 1) * HG_SUB] for j in range(nsub)]
        kt = kk * jnp.exp2(jnp.concatenate([jnp.broadcast_to(e, (HG_SUB, W)) for e in ends], axis=0) - b)
        lhs, rhs = [], []
        for j in range(nsub - 1):
            lo = (j + 1) * HG_SUB
            lhs.append(jnp.concatenate(
                [jnp.zeros((lo, W), F32), q[lo:] * jnp.exp2(b[lo:] - ends[j])], axis=0).astype(BF16))
            pieces = [kt[j * HG_SUB:lo]]
            if j > 0:
                pieces.insert(0, jnp.zeros((j * HG_SUB, W), F32))
            pieces.append(jnp.zeros((C - lo, W), F32))
            rhs.append(jnp.concatenate(pieces, axis=0).astype(BF16))
        q_state[ci] = (q * jnp.exp2(b)).astype(BF16)
        kd = (kk * jnp.exp2(ends[-1] - b)).astype(BF16)
        decay_last[ci] = jnp.exp2(ends[-1])
        a_off = [_dg(jnp.concatenate([x[:, sl] for x in lhs], axis=1),
                     jnp.concatenate([x[:, sl] for x in rhs], axis=1), NT) for sl in heads]
        for hh, sl in enumerate(heads):
            upd[ci, hh] = _dg(v[:, sl], kd[:, sl], TN)
        sums = [[jnp.sum(prods[d][:, sl], axis=-1, keepdims=True) for d in range(HG_SUB)] for sl in heads]
        for hh, sl in enumerate(heads):
            a = a_off[hh]
            for d in range(HG_SUB):
                a = jnp.where((c64 == r64 - d) & (rsub >= d), sums[hh][d], a)
            o_intra[ci, hh] = _dg(a.astype(BF16), v[:, sl], NN)
    finals = []
    outs = {}
    for hh, sl in enumerate(heads):
        st = st_ref[hh]
        for ci in range(n_chunks):
            outs[ci, hh] = o_intra[ci, hh] + _dg(q_state[ci][:, sl], st.astype(BF16), NT)
            st = st * decay_last[ci][:, sl] + upd[ci, hh]
        st_ref[hh] = st
        finals.append(st)
    for ci in range(n_chunks):
        rows = slice(ci * C, (ci + 1) * C)
        for hh, sl in enumerate(heads):
            g = g_ref[rows, sl]
            o_ref[rows, sl] = (_rms(outs[ci, hh], gn_ref[:, sl]) * (g * _sigmoid(g))).astype(BF16)

    @pl.when(step == pl.num_programs(1) - 1)
    def _():
        for hh in range(HG_HEADS):
            sfin_ref[0, hh] = finals[hh].T


def _hgrn_prompt(qh, fh, ih, gh, lb_logits, g_hg_out, batch):
    t = qh.shape[0]
    nc = t // batch // (HG_CHUNK * HG_STEP_CHUNKS)
    blk = pl.BlockSpec((HG_CHUNK * HG_STEP_CHUNKS, HG_WIDTH), lambda b, c: (b * nc + c, 0))
    const = lambda r: pl.BlockSpec((r, HG_WIDTH), lambda b, c: (0, 0))
    return pl.pallas_call(
        _hgrn_prompt_kernel,
        grid=(batch, nc),
        in_specs=[blk, blk, blk, blk, const(lb_logits.shape[0]), const(1)],
        out_specs=[blk, pl.BlockSpec((1, HG_HEADS, HG_DK, HG_DV), lambda b, c: (b, 0, 0, 0))],
        out_shape=[jax.ShapeDtypeStruct((t, HG_WIDTH), BF16),
                   jax.ShapeDtypeStruct((batch, HG_HEADS, HG_DK, HG_DV), F32)],
        scratch_shapes=[pltpu.VMEM((HG_HEADS, HG_DV, HG_DK), F32)],
        name="hgrn_prompt",
    )(qh, fh, ih, gh, lb_logits, g_hg_out.reshape(1, -1))


def _attn_sample_kernel(sink_ref, q_ref, kn_ref, vn_ref, kc_ref, vc_ref, g_ref,
                        o_ref, ko_ref, vo_ref):
    bt = q_ref.shape[0]
    win = kc_ref.shape[1]
    kc = kc_ref[...]
    vc = vc_ref[...]
    kn = kn_ref[...]
    vn = vn_ref[...]
    outs = []
    for h in range(ATTN_KV_HEADS):
        ls = slice(h * HEAD_DIM, (h + 1) * HEAD_DIM)
        q = q_ref[:, h * ATTN_GROUP:(h + 1) * ATTN_GROUP, :]
        s = jnp.einsum('bgd,bjd->bgj', q, kc[:, :, ls].astype(BF16), preferred_element_type=F32)
        s_new = jnp.sum(q.astype(F32) * kn[:, None, ls], axis=-1, keepdims=True)
        gi = lax.broadcasted_iota(jnp.int32, (1, ATTN_GROUP, 1), 1)
        sink = jnp.zeros((1, ATTN_GROUP, 1), F32)
        for g in range(ATTN_GROUP):
            sink = jnp.where(gi == g, sink_ref[h * ATTN_GROUP + g] * LOG2E, sink)
        m = jnp.maximum(jnp.maximum(jnp.max(s, axis=-1, keepdims=True), s_new), sink)
        p = jnp.exp2(s - m)
        p_new = jnp.exp2(s_new - m)
        den = jnp.sum(p, axis=-1, keepdims=True) + p_new + jnp.exp2(sink - m)
        o = jnp.einsum('bgj,bjd->bgd', p.astype(BF16), vc[:, :, ls].astype(BF16),
                       preferred_element_type=F32)
        o = (o + p_new * vn[:, None, ls]) / den
        outs.append(o)
    ssq = sum(jnp.sum(jnp.sum(o * o, axis=-1, keepdims=True), axis=1, keepdims=True) for o in outs)
    scale = lax.rsqrt(ssq / ATTN_WIDTH + EPS)
    for h in range(ATTN_KV_HEADS):
        o_ref[h] = (outs[h] * scale * g_ref[h][None]).astype(BF16)
    ri = lax.broadcasted_iota(jnp.int32, (win, KV_WIDTH), 0)
    for b in range(bt):
        ko_ref[b] = jnp.where(ri == win - 1, kn[b:b + 1], pltpu.roll(kc[b], win - 1, axis=0))
        vo_ref[b] = jnp.where(ri == win - 1, vn[b:b + 1], pltpu.roll(vc[b], win - 1, axis=0))


def _attention_sample(qa, ka, va, cache_k, cache_v, sinks, g_attn_out, bt=16):
    nb = qa.shape[0]
    win = cache_k.shape[1]
    q3 = qa.reshape(nb, ATTN_HEADS, HEAD_DIM)
    g3 = g_attn_out.reshape(ATTN_KV_HEADS, ATTN_GROUP, HEAD_DIM)
    row = lambda w: pl.BlockSpec((bt, w), lambda i: (i, 0))
    cache = pl.BlockSpec((bt, win, KV_WIDTH), lambda i: (i, 0, 0))
    o4, k_new, v_new = pl.pallas_call(
        _attn_sample_kernel,
        grid=(nb // bt,),
        in_specs=[pl.BlockSpec(memory_space=pltpu.SMEM),
                  pl.BlockSpec((bt, ATTN_HEADS, HEAD_DIM), lambda i: (i, 0, 0)),
                  row(KV_WIDTH), row(KV_WIDTH), cache, cache,
                  pl.BlockSpec((ATTN_KV_HEADS, ATTN_GROUP, HEAD_DIM), lambda i: (0, 0, 0))],
        out_specs=[pl.BlockSpec((ATTN_KV_HEADS, bt, ATTN_GROUP, HEAD_DIM), lambda i: (0, i, 0, 0)),
                   cache, cache],
        out_shape=[jax.ShapeDtypeStruct((ATTN_KV_HEADS, nb, ATTN_GROUP, HEAD_DIM), BF16),
                   jax.ShapeDtypeStruct(cache_k.shape, F32),
                   jax.ShapeDtypeStruct(cache_v.shape, F32)],
        name="attn_sample",
    )(sinks, q3, ka, va, cache_k, cache_v, g3)
    oa = jnp.transpose(o4, (1, 0, 2, 3)).reshape(nb, ATTN_WIDTH)
    return oa, k_new, v_new


def _hgrn_sample_kernel(q_ref, f_ref, i_ref, g_ref, lbl_ref, gn_ref, s0_ref, o_ref, s_ref):
    bt = q_ref.shape[0]
    lb = _lower_bound(lbl_ref)
    f = lb + (1.0 - lb) * _sigmoid(f_ref[...])
    kk = 1.0 - f
    q = q_ref[...]
    v = i_ref[...].astype(F32)
    g = g_ref[...]
    gate = g * _sigmoid(g)
    r = lax.broadcasted_iota(jnp.int32, (HG_DK, HG_DK), 0)
    cc = lax.broadcasted_iota(jnp.int32, (HG_DK, HG_DK), 1)
    diag = r == cc

    def column(x_row):
        return jnp.sum(jnp.where(diag, x_row, 0.0), axis=1, keepdims=True)

    units = [(b, hh, slice(hh * HG_DK, (hh + 1) * HG_DK)) for b in range(bt) for hh in range(HG_HEADS)]
    f_col = [column(f[b:b + 1, sl]) for b, hh, sl in units]
    k_col = [column(kk[b:b + 1, sl]) for b, hh, sl in units]
    q_col = [column(q[b:b + 1, sl]) for b, hh, sl in units]
    s_new = [f_col[u] * s0_ref[b, hh] + k_col[u] * v[b:b + 1, sl] for u, (b, hh, sl) in enumerate(units)]
    for u, (b, hh, sl) in enumerate(units):
        s_ref[b, hh] = s_new[u]
    outs = [_rms(jnp.sum(s_new[u] * q_col[u], axis=0, keepdims=True), gn_ref[:, sl])
            for u, (b, hh, sl) in enumerate(units)]
    for b in range(bt):
        row = jnp.concatenate(outs[b * HG_HEADS:(b + 1) * HG_HEADS], axis=1)
        o_ref[b:b + 1, :] = (row * gate[b:b + 1]).astype(BF16)


def _hgrn_sample(qh, fh, ih, gh, lb_logits, g_hg_out, state, bt=16):
    nb = qh.shape[0]
    row = pl.BlockSpec((bt, HG_WIDTH), lambda i: (i, 0))
    const = lambda r: pl.BlockSpec((r, HG_WIDTH), lambda i: (0, 0))
    st = pl.BlockSpec((bt, HG_HEADS, HG_DK, HG_DV), lambda i: (i, 0, 0, 0))
    return pl.pallas_call(
        _hgrn_sample_kernel,
        grid=(nb // bt,),
        in_specs=[row, row, row, row, const(lb_logits.shape[0]), const(1), st],
        out_specs=[row, st],
        out_shape=[jax.ShapeDtypeStruct((nb, HG_WIDTH), BF16),
                   jax.ShapeDtypeStruct(state.shape, F32)],
        name="hgrn_sample",
    )(qh, fh, ih, gh, lb_logits, g_hg_out.reshape(1, -1), state)


def _outproj_kernel(x_ref, oa_ref, oh_ref, gt_ref, sh_ref, sc_ref, g_ref, wo_ref, wr_ref, br_ref, *rest,
                    n_tiles):
    x1_ref, xs_ref, lp_ref, w_ref, pc_ref, hb_scr, lp_scr = rest[-7:]
    i = pl.program_id(0)
    tm = x_ref.shape[0]
    rows = xs_ref.shape[0]

    @pl.when(i == 0)
    def _():
        hb_scr[...] = jnp.zeros_like(hb_scr)
        lp_scr[...] = jnp.zeros_like(lp_scr)

    hb_prev = jnp.where(i <= n_tiles, hb_scr[...], jnp.zeros((), BF16))
    lp_prev = [lp_scr[k:k + 1, :] for k in range(TOP_K)]
    chunk = 256
    pending = list(range(0, rows, chunk))

    def place_next():
        if pending:
            r0 = pending.pop(0)
            si = lax.broadcasted_iota(jnp.int32, (chunk, tm), 0) + r0
            hit = (si == lp_prev[0]) | (si == lp_prev[1]) | (si == lp_prev[2]) | (si == lp_prev[3])
            xs_ref[r0:r0 + chunk, :] = _dg(jnp.where(hit, 1.0, 0.0).astype(BF16), hb_prev, NN).astype(BF16)

    place_next()
    gt = gt_ref[...].reshape(-1, D_MODEL)
    sh = sh_ref[...].reshape(-1, D_MODEL)
    sc = sc_ref[...].reshape(-1, D_MODEL)
    mix = _dg(oa_ref[...], wo_ref[0:ATTN_WIDTH, :], NN) + _dg(oh_ref[...], wo_ref[ATTN_WIDTH:, :], NN)
    x1 = x_ref[...] + gt * mix
    x1_ref[...] = x1
    place_next()
    h2 = _rms(x1, g_ref[...]) * (1.0 + sc) + sh
    place_next()
    logits = _dot_hilo(wr_ref[...], h2, NT) + br_ref[...]
    place_next()
    ei = lax.broadcasted_iota(jnp.int32, logits.shape, 0)
    vals, sel = [], []
    l = logits
    for _ in range(TOP_K):
        m = jnp.max(l, axis=0, keepdims=True)
        idx = jnp.min(jnp.where(l == m, ei, N_EXPERTS), axis=0, keepdims=True)
        pick = ei == idx
        vals.append(m)
        sel.append(pick)
        l = jnp.where(pick, -jnp.inf, l)
        place_next()
    ex = [jnp.exp(v - vals[0]) for v in vals]
    den = ex[0] + ex[1] + ex[2] + ex[3]
    for k in range(TOP_K):
        w_ref[k:k + 1, :] = ex[k] / den
    place_next()

    member = jnp.where(sel[0] | sel[1] | sel[2] | sel[3], 1.0, 0.0)
    tr = lax.broadcasted_iota(jnp.int32, (tm, tm), 0)
    tc = lax.broadcasted_iota(jnp.int32, (tm, tm), 1)
    rank = _dg(member.astype(BF16), (tr < tc).astype(BF16), NN)
    pieces = jnp.floor((jnp.sum(member, axis=1, keepdims=True) + (PIECE - 1)) * (1.0 / PIECE))
    er = lax.broadcasted_iota(jnp.int32, (N_EXPERTS, N_EXPERTS), 0)
    ec = lax.broadcasted_iota(jnp.int32, (N_EXPERTS, N_EXPERTS), 1)
    pieces_b = jnp.broadcast_to(pieces, (N_EXPERTS, 128))
    pc_ref[0] = pieces_b.astype(jnp.int32)
    base = _dg((ec < er).astype(BF16), pieces_b.astype(BF16), NN)[:, 0:1] * PIECE
    slot = base + rank
    for k in range(TOP_K):
        lp = jnp.sum(jnp.where(sel[k], slot, 0.0), axis=0, keepdims=True).astype(jnp.int32)
        lp_ref[k:k + 1, :] = lp
        lp_scr[k:k + 1, :] = lp
    while pending:
        place_next()
    hb_scr[...] = h2.astype(BF16)


def _out_projection(x, oa, oh, mod3, mod2, g_ffn, w_out_bf16, w_router_t, b_router, tm, slots, slot_block):
    t = x.shape[0]
    nt = t // tm
    rows = SLOT_ROWS[tm]
    real = lambda i: jnp.minimum(i, nt - 1)
    if mod3 is not None:
        per_b = (t // mod3.shape[0]) // tm
        mspec = lambda j: pl.BlockSpec((1, 1, D_MODEL), lambda i: (real(i) // per_b, 0, j))
        mod = mod3
    else:
        mspec = lambda j: pl.BlockSpec((tm, D_MODEL), lambda i: (real(i), j))
        mod = mod2
    row = lambda w: pl.BlockSpec((tm, w), lambda i: (real(i), 0))
    full = lambda a: pl.BlockSpec(a.shape, lambda i: (0,) * a.ndim)
    g2 = g_ffn.reshape(1, -1)
    br = b_router.reshape(-1, 1)
    args = [x, oa, oh, mod, mod, mod, g2, w_out_bf16, w_router_t, br]
    in_specs = [row(D_MODEL), row(ATTN_WIDTH), row(HG_WIDTH), mspec(2), mspec(3), mspec(4),
                full(g2), full(w_out_bf16), full(w_router_t), full(br)]
    aliases = {}
    n_fill = 0
    if slots is not None:
        args.append(slots)
        in_specs.append(pl.BlockSpec(memory_space=pl.ANY))
        aliases = {len(args) - 1: 1}
    else:
        n_fill = pl.cdiv(N_PIECES * PIECE - nt * rows, rows)
    return pl.pallas_call(
        functools.partial(_outproj_kernel, n_tiles=nt),
        grid=(nt + 1 + n_fill,),
        in_specs=in_specs,
        out_specs=[row(D_MODEL),
                   pl.BlockSpec((rows, D_MODEL), lambda i: (slot_block + jnp.maximum(i - 1, 0), 0)),
                   pl.BlockSpec((TOP_K, tm), lambda i: (0, real(i))),
                   pl.BlockSpec((TOP_K, tm), lambda i: (0, real(i))),
                   pl.BlockSpec((1, N_EXPERTS, 128), lambda i: (real(i), 0, 0))],
        out_shape=[jax.ShapeDtypeStruct((t, D_MODEL), F32),
                   jax.ShapeDtypeStruct((N_PIECES * PIECE, D_MODEL), BF16),
                   jax.ShapeDtypeStruct((TOP_K, t), jnp.int32),
                   jax.ShapeDtypeStruct((TOP_K, t), F32),
                   jax.ShapeDtypeStruct((nt, N_EXPERTS, 128), jnp.int32)],
        scratch_shapes=[pltpu.VMEM((tm, D_MODEL), BF16), pltpu.VMEM((8, tm), jnp.int32)],
        input_output_aliases=aliases,
        compiler_params=pltpu.CompilerParams(vmem_limit_bytes=V7X_VMEM_LIMIT),
        name="out_proj_router",
    )(*args)


def _segment_offsets_kernel(dest_ref, dd_ref, q_ref, o_ref):
    reached = dest_ref[...] <= q_ref[...]
    o_ref[...] = jnp.sum(jnp.where(reached, dd_ref[...], 0), axis=0, keepdims=True)


def _segment_offsets(dest, d_delta, queries):
    lanes = 512
    n_seg = -(-dest.shape[0] // 8) * 8
    n_q = -(-queries.shape[0] // lanes) * lanes
    never = jnp.iinfo(jnp.int32).max
    dest_c = jnp.pad(dest, (0, n_seg - dest.shape[0]), constant_values=never).reshape(n_seg, 1)
    dd_c = jnp.pad(d_delta, (0, n_seg - d_delta.shape[0])).reshape(n_seg, 1)
    q_r = jnp.pad(queries, (0, n_q - queries.shape[0])).reshape(1, n_q)
    seg = pl.BlockSpec((n_seg, 1), lambda i: (0, 0))
    out = pl.pallas_call(
        _segment_offsets_kernel,
        grid=(n_q // lanes,),
        in_specs=[seg, seg, pl.BlockSpec((1, lanes), lambda i: (0, i))],
        out_specs=pl.BlockSpec((1, lanes), lambda i: (0, i)),
        out_shape=jax.ShapeDtypeStruct((1, n_q), jnp.int32),
        name="segment_offsets",
    )(dest_c, dd_c, q_r)
    return out[0, :queries.shape[0]]


def _prefix_sums(x):
    i = jnp.arange(x.shape[-1])
    return jnp.sum(jnp.where(i[None, :] <= i[:, None], x[..., None, :], 0), axis=-1)


def _piece_tables(pieces_ie):
    cap = jnp.asarray(TILE_PIECE_CAP, jnp.int32)
    gbase = jnp.asarray(TILE_PIECE_BASE, jnp.int32)
    experts = jnp.arange(N_EXPERTS, dtype=jnp.int32)
    seg_src = gbase[:, None] + _prefix_sums(pieces_ie) - pieces_ie
    used_i = jnp.sum(pieces_ie, axis=1)
    tail_i = cap - used_i
    np_e = jnp.sum(pieces_ie, axis=0)
    rem = np_e % STEP_PIECES
    head_e = jnp.where(rem > 0, rem, jnp.minimum(np_e, STEP_PIECES))
    nt_e = np_e // STEP_PIECES + (rem > 0)
    tile_end = _prefix_sums(nt_e)
    tile_start = tile_end - nt_e
    n_comp = jnp.sum(nt_e)
    q_start_e = _prefix_sums(np_e) - np_e
    n_used = jnp.sum(np_e)
    tt = jnp.arange(MOE_MAX_STEPS, dtype=jnp.int32)
    e_t = jnp.minimum(jnp.sum(tt[:, None] >= tile_end[None, :], axis=1), N_EXPERTS - 1).astype(jnp.int32)
    of_step = lambda v: jnp.sum(jnp.where(e_t[:, None] == experts[None, :], v[None, :], 0), axis=1)
    is_comp = tt < n_comp
    k = tt - of_step(tile_start)
    head_t = of_step(head_e)
    q0_comp = of_step(q_start_e) + jnp.where(k == 0, 0, head_t + STEP_PIECES * (k - 1))
    live_comp = jnp.where(k == 0, head_t, STEP_PIECES)
    q0_fill = n_used + STEP_PIECES * (tt - n_comp)
    q0 = jnp.where(is_comp, q0_comp, q0_fill)
    live = jnp.where(is_comp, live_comp, jnp.clip(N_PIECES - q0_fill, 0, STEP_PIECES))
    n_busy = n_comp + (N_PIECES - n_used + STEP_PIECES - 1) // STEP_PIECES
    pieces_ei = pieces_ie.T
    dest_seg = q_start_e[:, None] + _prefix_sums(pieces_ei) - pieces_ei
    dest = jnp.concatenate([dest_seg.reshape(-1), n_used + _prefix_sums(tail_i) - tail_i])
    src = jnp.concatenate([seg_src.T.reshape(-1), gbase + used_i])
    delta = src - dest
    d_delta = delta - jnp.concatenate([jnp.zeros((1,), jnp.int32), delta[:-1]])
    lane = jnp.arange(STEP_PIECES, dtype=jnp.int32)
    qq = (q0[:, None] + lane[None, :]).reshape(-1)
    ok = lane[None, :] < live[:, None]
    piece = (qq + _segment_offsets(dest, d_delta, qq)).reshape(MOE_MAX_STEPS, STEP_PIECES)
    dump = N_PIECES + (tt % 2)[:, None] * STEP_PIECES + lane[None, :]
    tbl_out = jnp.where(ok, piece, dump)
    last = jnp.maximum(n_comp - 1, 0)
    tbl_in = jnp.where(ok & is_comp[:, None], piece, 0)
    in_last = jnp.sum(jnp.where((tt == last)[:, None], tbl_in, 0), axis=0)
    tbl_in = jnp.where(is_comp[:, None], tbl_in, in_last[None, :])
    texp = jnp.where(is_comp, e_t, jnp.sum(jnp.where(tt == last, e_t, 0)))
    first = is_comp & (k == 0)
    kind = jnp.where(is_comp, jnp.where(live <= HALF_PIECES, 1, 2), jnp.where(tt < n_busy, 0, 3))
    i32 = lambda a: a.astype(jnp.int32)
    return i32(tbl_in.reshape(-1)), i32(tbl_out.reshape(-1)), i32(texp), i32(kind), i32(first)


def _moe_grouped_kernel(tin_ref, tout_ref, texp_ref, kind_ref, first_ref,
                        x_hbm, wu_ref, wd_ref, bg_ref, bl_ref, bd_ref, y_hbm,
                        wg_s, wl_s, wd_s, xbuf, ybuf, xsems, ysems):
    npc = STEP_PIECES
    t = pl.program_id(0)
    nsteps = pl.num_programs(0)
    kind = kind_ref[t]
    slot = t % 2

    def x_copy(step, buf, p):
        src = tin_ref[step * npc + p]
        return pltpu.make_async_copy(
            x_hbm.at[pl.ds(pl.multiple_of(src * PIECE, PIECE), PIECE), :],
            xbuf.at[buf, pl.ds(p * PIECE, PIECE), :],
            xsems.at[buf, p])

    def x_each(step, buf, fn):
        k = kind_ref[step]

        @pl.when(k == 2)
        def _():
            for p in range(STEP_PIECES):
                fn(x_copy(step, buf, p))

        @pl.when(k == 1)
        def _():
            for p in range(HALF_PIECES):
                fn(x_copy(step, buf, p))

    def y_copy(step, buf, p):
        dst = tout_ref[step * npc + p]
        return pltpu.make_async_copy(
            ybuf.at[buf, pl.ds(p * PIECE, PIECE), :],
            y_hbm.at[pl.ds(pl.multiple_of(dst * PIECE, PIECE), PIECE), :],
            ysems.at[buf, p])

    def y_wait(step, buf):
        for p in range(npc):
            y_copy(step, buf, p).wait()

    @pl.when(t == 0)
    def _():
        ybuf[...] = jnp.zeros_like(ybuf)
        x_each(0, 0, lambda cp: cp.start())

    @pl.when(t >= 2)
    def _():
        y_wait(t - 2, slot)

    @pl.when(t + 1 < nsteps)
    def _():
        x_each(t + 1, 1 - slot, lambda cp: cp.start())

    @pl.when((kind > 0) & (kind < 3) & (first_ref[t] == 1))
    def _():
        cb = 256
        r = lax.broadcasted_iota(jnp.int32, (cb, cb), 0)
        c = lax.broadcasted_iota(jnp.int32, (cb, cb), 1)
        perm = jnp.where(r == jnp.where(c < cb // 2, 2 * c, 2 * (c - cb // 2) + 1), 1.0, 0.0).astype(BF16)
        for blk in range(2 * D_FF // cb):
            wp = _dg(wu_ref[0, :, blk * cb:(blk + 1) * cb].astype(BF16), perm, NN).astype(BF16)
            wg_s[:, blk * (cb // 2):(blk + 1) * (cb // 2)] = wp[:, :cb // 2]
            wl_s[:, blk * (cb // 2):(blk + 1) * (cb // 2)] = wp[:, cb // 2:]
        wd_s[...] = wd_ref[0].astype(BF16)

    x_each(t, slot, lambda cp: cp.wait())

    def expert(n_pieces):
        xb = xbuf[slot, 0:n_pieces * PIECE, :]
        groups = [slice(j * FF_GROUP, (j + 1) * FF_GROUP) for j in range(D_FF // FF_GROUP)]
        glu = [jnp.minimum(_dg(xb, wg_s[:, c], NN) + bg_ref[0, :, c], SWIGLU_LIMIT) for c in groups]
        lin = [jnp.clip(_dg(xb, wl_s[:, c], NN) + bl_ref[0, :, c], -SWIGLU_LIMIT, SWIGLU_LIMIT)
               for c in groups]
        act = [(g * _sigmoid(SWIGLU_ALPHA * g) * (l + 1.0)).astype(BF16) for g, l in zip(glu, lin)]
        y = bd_ref[0] + _dg(act[0], wd_s[groups[0], :], NN)
        for c, a in zip(groups[1:], act[1:]):
            y = y + _dg(a, wd_s[c, :], NN)
        ybuf[slot, 0:n_pieces * PIECE, :] = y.astype(BF16)

    @pl.when(kind == 2)
    def _():
        expert(STEP_PIECES)

    @pl.when(kind == 1)
    def _():
        expert(HALF_PIECES)

    @pl.when(kind == 0)
    def _():
        ybuf[slot] = jnp.zeros((npc * PIECE, D_MODEL), BF16)

    for p in range(npc):
        y_copy(t, slot, p).start()

    @pl.when(t == nsteps - 1)
    def _():
        y_wait(t - 1, 1 - slot)
        y_wait(t, slot)


def _moe_grouped(slots, tables, w_up, w_down, bg, bl, bd):
    npc = STEP_PIECES
    by_expert = lambda shape: pl.BlockSpec(
        shape, lambda t, tin, tout, texp, kind, first: (texp[t],) + (0,) * (len(shape) - 1))
    return pl.pallas_call(
        _moe_grouped_kernel,
        grid_spec=pltpu.PrefetchScalarGridSpec(
            num_scalar_prefetch=5,
            grid=(MOE_MAX_STEPS,),
            in_specs=[pl.BlockSpec(memory_space=pl.ANY),
                      by_expert((1, D_MODEL, 2 * D_FF)), by_expert((1, D_FF, D_MODEL)),
                      by_expert((1, 1, D_FF)), by_expert((1, 1, D_FF)), by_expert((1, 1, D_MODEL))],
            out_specs=pl.BlockSpec(memory_space=pl.ANY),
            scratch_shapes=[pltpu.VMEM((D_MODEL, D_FF), BF16), pltpu.VMEM((D_MODEL, D_FF), BF16),
                            pltpu.VMEM((D_FF, D_MODEL), BF16),
                            pltpu.VMEM((2, npc * PIECE, D_MODEL), BF16),
                            pltpu.VMEM((2, npc * PIECE, D_MODEL), BF16),
                            pltpu.SemaphoreType.DMA((2, npc)),
                            pltpu.SemaphoreType.DMA((2, npc))]),
        out_shape=jax.ShapeDtypeStruct(((N_PIECES + 2 * npc) * PIECE, D_MODEL), BF16),
        compiler_params=pltpu.CompilerParams(vmem_limit_bytes=V7X_VMEM_LIMIT),
        name="moe_grouped",
    )(*tables, slots, w_up, w_down, bg, bl, bd)


def _combine_kernel(y_ref, lp_ref, w_ref, x1_ref, gt_ref, gf_ref, o_ref):
    tm = x1_ref.shape[0]
    rows = y_ref.shape[0]
    r = lax.broadcasted_iota(jnp.int32, (tm, tm), 0)
    c = lax.broadcasted_iota(jnp.int32, (tm, tm), 1)
    diag = r == c

    def column(x_row):
        return jnp.sum(jnp.where(diag, x_row, 0.0), axis=1, keepdims=True)

    pos = [column(lp_ref[k:k + 1, :].astype(F32)).astype(jnp.int32) for k in range(TOP_K)]
    wgt = [column(w_ref[k:k + 1, :]) for k in range(TOP_K)]
    chunk = 512
    moe = jnp.zeros((tm, D_MODEL), F32)
    for r0 in range(0, rows, chunk):
        si = lax.broadcasted_iota(jnp.int32, (tm, chunk), 1) + r0
        wt = jnp.zeros((tm, chunk), F32)
        for k in range(TOP_K):
            wt = jnp.where(si == pos[k], wgt[k], wt)
        moe = moe + _dg(wt.astype(BF16), y_ref[r0:r0 + chunk, :], NN)
    gt = gt_ref[...].reshape(-1, D_MODEL)
    o_ref[...] = _rms(x1_ref[...] + gt * moe, gf_ref[...])


def _combine(yslots, lp, w, x1, mod3, mod2, g_final, tm, slot_block):
    t = x1.shape[0]
    nt = t // tm
    rows = SLOT_ROWS[tm]
    if mod3 is not None:
        per_b = (t // mod3.shape[0]) // tm
        gspec = pl.BlockSpec((1, 1, D_MODEL), lambda i: (i // per_b, 0, 5))
        mod = mod3
    else:
        gspec = pl.BlockSpec((tm, D_MODEL), lambda i: (i, 5))
        mod = mod2
    return pl.pallas_call(
        _combine_kernel,
        grid=(nt,),
        in_specs=[pl.BlockSpec((rows, D_MODEL), lambda i: (slot_block + i, 0)),
                  pl.BlockSpec((TOP_K, tm), lambda i: (0, i)),
                  pl.BlockSpec((TOP_K, tm), lambda i: (0, i)),
                  pl.BlockSpec((tm, D_MODEL), lambda i: (i, 0)),
                  gspec,
                  pl.BlockSpec((1, D_MODEL), lambda i: (0, 0))],
        out_specs=pl.BlockSpec((tm, D_MODEL), lambda i: (i, 0)),
        out_shape=jax.ShapeDtypeStruct((t, D_MODEL), F32),
        compiler_params=pltpu.CompilerParams(vmem_limit_bytes=V7X_VMEM_LIMIT),
        name="moe_combine",
    )(yslots, lp, w, x1, mod, g_final.reshape(1, -1))


def kernel(x_prompt, x_sample, c_prompt, c_sample, cache_k_win, cache_v_win, state_hgrn, w_ada, b_ada,
           g_mix, g_ffn, w_in, attn_sinks, g_attn_out, hg_lb_logits, g_hg_out, w_out, w_router, b_router,
           w_up, b_up, w_down, b_down, g_final):
    batch, seq, d = x_prompt.shape
    nsamp = x_sample.shape[0]
    win = cache_k_win.shape[2]
    layer = 0

    mod = _modulation(jnp.concatenate([c_prompt, c_sample], axis=0), w_ada[layer], b_ada[layer])
    mod_p = mod[:batch].reshape(batch, 1, 6 * d)
    mod_s = mod[batch:]

    assert (batch * seq, nsamp) == (N_PROMPT_TILES * TOK_TILE, SAMPLE_TILE)
    w_in_b = w_in[layer].astype(BF16)
    w_out_b = w_out[layer].astype(BF16)
    w_router_t = w_router[layer].T
    bg = b_up[layer][:, None, 0::2]
    bl = b_up[layer][:, None, 1::2]
    bd = b_down[layer][:, None, :]

    xp = x_prompt.reshape(batch * seq, d)
    cos_p, sin_p = _rope_tables(np.arange(seq))
    qa, ka, va, qh, fh, ih, gh = _in_projection(xp, mod_p, None, g_mix[layer], w_in_b, cos_p, sin_p,
                                                IN_PROJ_TILE)
    oa = _attention_prompt(qa, ka, va, attn_sinks[layer], g_attn_out[layer], batch)
    oh, s_prompt = _hgrn_prompt(qh, fh, ih, gh, hg_lb_logits, g_hg_out[layer], batch)
    x1_p, slots, lp_p, cw_p, pc_p = _out_projection(
        xp, oa, oh, mod_p, None, g_ffn[layer], w_out_b, w_router_t, b_router[layer], TOK_TILE, None, 0)
    k_win_p = ka.reshape(batch, seq, ATTN_KV_HEADS, HEAD_DIM)[:, seq - win:]
    v_win_p = va.reshape(batch, seq, ATTN_KV_HEADS, HEAD_DIM)[:, seq - win:]

    xs = x_sample.reshape(nsamp, d)
    cos_s, sin_s = _rope_tables(np.full((nsamp,), PAST_LEN))
    qa, ka, va, qh, fh, ih, gh = _in_projection(xs, None, mod_s, g_mix[layer], w_in_b, cos_s, sin_s, nsamp)
    oa, k_win_s, v_win_s = _attention_sample(
        qa, ka, va, cache_k_win[layer].reshape(nsamp, win, KV_WIDTH),
        cache_v_win[layer].reshape(nsamp, win, KV_WIDTH), attn_sinks[layer], g_attn_out[layer])
    oh, s_sample = _hgrn_sample(qh, fh, ih, gh, hg_lb_logits, g_hg_out[layer], state_hgrn[layer])
    sample_block = TILE_PIECE_BASE[-1] * PIECE // SLOT_ROWS[SAMPLE_TILE]
    x1_s, slots, lp_s, cw_s, pc_s = _out_projection(
        xs, oa, oh, None, mod_s, g_ffn[layer], w_out_b, w_router_t, b_router[layer], SAMPLE_TILE,
        slots, sample_block)

    tables = _piece_tables(jnp.concatenate([pc_p[:, :, 0], pc_s[:, :, 0]], axis=0))
    yslots = _moe_grouped(slots, tables, w_up[layer], w_down[layer], bg, bl, bd)
    y_prompt = _combine(yslots, lp_p, cw_p, x1_p, mod_p, None, g_final, TOK_TILE, 0)
    y_sample = _combine(yslots, lp_s, cw_s, x1_s, None, mod_s, g_final, SAMPLE_TILE, sample_block)

    kv_shape = (1, nsamp, win, ATTN_KV_HEADS, HEAD_DIM)
    return (y_prompt.reshape(batch, seq, d), y_sample.reshape(nsamp, 1, d),
            k_win_p[None], v_win_p[None], s_prompt[None],
            k_win_s.reshape(kv_shape), v_win_s.reshape(kv_shape), s_sample[None])
```

```python
import functools

import numpy as np
import jax
import jax.numpy as jnp
from jax import lax
from jax.experimental import pallas as pl
from jax.experimental.pallas import tpu as pltpu

F32 = jnp.float32
BF16 = jnp.bfloat16

D_MODEL = 1024
SEQ = 2048
PAST_LEN = 16384
ATTN_HEADS = 8
ATTN_KV_HEADS = 2
HEAD_DIM = 64
ATTN_GROUP = ATTN_HEADS // ATTN_KV_HEADS
ATTN_WIDTH = ATTN_HEADS * HEAD_DIM
KV_WIDTH = ATTN_KV_HEADS * HEAD_DIM
WINDOW = 128
ATTN_STEP_BLOCKS = 8
ROT_DIM = HEAD_DIM // 4
ROPE_THETA = 500000.0
HG_HEADS = 4
HG_DK = 128
HG_DV = 128
HG_WIDTH = HG_HEADS * HG_DV
HG_CHUNK = 64
HG_SUB = 8
HG_STEP_CHUNKS = 16
IN_COLS = ATTN_WIDTH + 2 * KV_WIDTH + 4 * HG_WIDTH
N_EXPERTS = 32
TOP_K = 4
D_FF = D_MODEL
SWIGLU_ALPHA = 1.702
SWIGLU_LIMIT = 7.0
EPS = 1e-5
LOG2E = float(np.log2(np.e))

V7X_VMEM_LIMIT = 56 * 1024 * 1024

PIECE = 16
TOK_TILE = 512
IN_PROJ_TILE = 1024
SAMPLE_TILE = 128
N_PROMPT_TILES = 32
SLOT_ROWS = {TOK_TILE: 2560, SAMPLE_TILE: 1024}
TILE_PIECE_CAP = [SLOT_ROWS[TOK_TILE] // PIECE] * N_PROMPT_TILES + [SLOT_ROWS[SAMPLE_TILE] // PIECE]
TILE_PIECE_BASE = [i * TILE_PIECE_CAP[0] for i in range(N_PROMPT_TILES + 1)]
N_PIECES = sum(TILE_PIECE_CAP)
STEP_PIECES = 32
FF_GROUP = 512
HALF_PIECES = 16
MOE_MAX_STEPS = N_PIECES // STEP_PIECES + N_EXPERTS + 1 + 2

NN = (((1,), (0,)), ((), ()))
NT = (((1,), (1,)), ((), ()))
TN = (((0,), (0,)), ((), ()))


def _dg(a, b, dims):
    return lax.dot_general(a, b, dims, preferred_element_type=F32)


def _split3(x):
    h = x.astype(BF16)
    r = x - h.astype(F32)
    m = r.astype(BF16)
    l = (r - m.astype(F32)).astype(BF16)
    return h, m, l


def _dot_f32(a, b, dims):
    ah, am, al = _split3(a)
    bh, bm, bl = _split3(b)
    return (_dg(ah, bh, dims) + (_dg(ah, bm, dims) + _dg(am, bh, dims))
            + (_dg(am, bm, dims) + _dg(ah, bl, dims) + _dg(al, bh, dims)))


def _split2(x):
    h = x.astype(BF16)
    return h, (x - h.astype(F32)).astype(BF16)


def _dot_hilo(a, b, dims):
    ah, al = _split2(a)
    bh, bl = _split2(b)
    return _dg(ah, bh, dims) + (_dg(ah, bl, dims) + _dg(al, bh, dims))


def _dot_exact_lhs(a_bf16, b, dims):
    bh, bm, bl = _split3(b)
    return _dg(a_bf16, bh, dims) + _dg(a_bf16, bm, dims) + _dg(a_bf16, bl, dims)


def _sigmoid(x):
    return 0.5 * jnp.tanh(0.5 * x) + 0.5


def _rms(x, g):
    return x * lax.rsqrt(jnp.mean(x * x, axis=-1, keepdims=True) + EPS) * g


def _mod_kernel(c_ref, w_ref, b_ref, o_ref):
    c = c_ref[...]
    o_ref[...] = _dot_f32(c * _sigmoid(c), w_ref[...], NN) + b_ref[...]


def _modulation(c_all, w_ada, b_ada):
    n = c_all.shape[0]
    return pl.pallas_call(
        _mod_kernel,
        grid=(6,),
        in_specs=[pl.BlockSpec((n, D_MODEL), lambda j: (0, 0)),
                  pl.BlockSpec((D_MODEL, D_MODEL), lambda j: (0, j)),
                  pl.BlockSpec((1, D_MODEL), lambda j: (0, j))],
        out_specs=pl.BlockSpec((n, D_MODEL), lambda j: (0, j)),
        out_shape=jax.ShapeDtypeStruct((n, 6 * D_MODEL), F32),
        compiler_params=pltpu.CompilerParams(vmem_limit_bytes=V7X_VMEM_LIMIT),
        name="adaln_mod",
    )(c_all, w_ada, b_ada.reshape(1, -1))


def _rotate(x, cos_t, sin_t):
    width = x.shape[-1]
    d = lax.broadcasted_iota(jnp.int32, x.shape, 1) % HEAD_DIM
    half = ROT_DIM // 2
    partner = jnp.where(d < half, pltpu.roll(x, width - half, axis=1), pltpu.roll(x, half, axis=1))
    return x * cos_t + partner * sin_t


def _inproj_kernel(x_ref, sh_ref, sc_ref, g_ref, w_ref, cos_ref, sin_ref,
                   qa_ref, ka_ref, va_ref, qh_ref, fh_ref, ih_ref, gh_ref):
    x = x_ref[...]
    sh = sh_ref[...].reshape(-1, D_MODEL)
    sc = sc_ref[...].reshape(-1, D_MODEL)
    hb = (_rms(x, g_ref[...]) * (1.0 + sc) + sh).astype(BF16)
    cos_k = cos_ref[...]
    sin_k = sin_ref[...]
    cos_q = jnp.concatenate([cos_k] * ATTN_GROUP, axis=1)
    sin_q = jnp.concatenate([sin_k] * ATTN_GROUP, axis=1)
    o = 0

    def project(width):
        nonlocal o
        z = _dg(hb, w_ref[:, o:o + width], NN)
        o += width
        return z

    qa = _rotate(project(ATTN_WIDTH), cos_q, sin_q)
    qa_ref[...] = (qa * (HEAD_DIM ** -0.5 * LOG2E)).astype(BF16)
    kv = project(2 * KV_WIDTH)
    ka_ref[...] = _rotate(kv[:, :KV_WIDTH], cos_k, sin_k)
    va_ref[...] = kv[:, KV_WIDTH:]
    qh_ref[...] = project(HG_WIDTH)
    fh_ref[...] = project(HG_WIDTH)
    ih_ref[...] = project(HG_WIDTH).astype(BF16)
    gh_ref[...] = project(HG_WIDTH)


def _rope_tables(positions):
    half = ROT_DIM // 2
    inv = ROPE_THETA ** (-(np.arange(half, dtype=np.float64) * 2.0 / ROT_DIM))
    ang = np.asarray(positions, np.float64)[:, None] * inv[None, :]
    cos_h = np.ones((len(positions), HEAD_DIM))
    sin_h = np.zeros((len(positions), HEAD_DIM))
    cos_h[:, :half] = np.cos(ang)
    cos_h[:, half:ROT_DIM] = np.cos(ang)
    sin_h[:, :half] = -np.sin(ang)
    sin_h[:, half:ROT_DIM] = np.sin(ang)
    cos_t = np.tile(cos_h, (1, ATTN_KV_HEADS)).astype(np.float32)
    sin_t = np.tile(sin_h, (1, ATTN_KV_HEADS)).astype(np.float32)
    return jnp.asarray(cos_t), jnp.asarray(sin_t)


def _in_projection(x, mod3, mod2, g_mix, w_in_bf16, cos_t, sin_t, tm):
    t = x.shape[0]
    nt = t // tm
    if mod3 is not None:
        per_b = (t // mod3.shape[0]) // tm
        sh_spec = pl.BlockSpec((1, 1, D_MODEL), lambda i: (i // per_b, 0, 0))
        sc_spec = pl.BlockSpec((1, 1, D_MODEL), lambda i: (i // per_b, 0, 1))
        mod = mod3
        ncs = cos_t.shape[0] // tm
        cs_spec = pl.BlockSpec((tm, KV_WIDTH), lambda i: (i % ncs, 0))
    else:
        sh_spec = pl.BlockSpec((tm, D_MODEL), lambda i: (i, 0))
        sc_spec = pl.BlockSpec((tm, D_MODEL), lambda i: (i, 1))
        mod = mod2
        cs_spec = pl.BlockSpec((tm, KV_WIDTH), lambda i: (i, 0))
    row = lambda w: pl.BlockSpec((tm, w), lambda i: (i, 0))
    return pl.pallas_call(
        _inproj_kernel,
        grid=(nt,),
        in_specs=[row(D_MODEL), sh_spec, sc_spec,
                  pl.BlockSpec((1, D_MODEL), lambda i: (0, 0)),
                  pl.BlockSpec((D_MODEL, IN_COLS), lambda i: (0, 0)),
                  cs_spec, cs_spec],
        out_specs=[row(ATTN_WIDTH), row(KV_WIDTH), row(KV_WIDTH),
                   row(HG_WIDTH), row(HG_WIDTH), row(HG_WIDTH), row(HG_WIDTH)],
        out_shape=[jax.ShapeDtypeStruct((t, ATTN_WIDTH), BF16),
                   jax.ShapeDtypeStruct((t, KV_WIDTH), F32),
                   jax.ShapeDtypeStruct((t, KV_WIDTH), F32),
                   jax.ShapeDtypeStruct((t, HG_WIDTH), F32),
                   jax.ShapeDtypeStruct((t, HG_WIDTH), F32),
                   jax.ShapeDtypeStruct((t, HG_WIDTH), BF16),
                   jax.ShapeDtypeStruct((t, HG_WIDTH), F32)],
        compiler_params=pltpu.CompilerParams(vmem_limit_bytes=V7X_VMEM_LIMIT),
        name="in_proj",
    )(x, mod, mod, g_mix.reshape(1, -1), w_in_bf16, cos_t, sin_t)


def _attn_prompt_kernel(sink_ref, q_ref, kc_ref, kp_ref, vc_ref, vp_ref, g_ref, o_ref):
    n = pl.program_id(1)
    blk = WINDOW
    nblk = q_ref.shape[0] // blk
    pair_w = 2 * HEAD_DIM
    low = lax.broadcasted_iota(jnp.int32, (1, pair_w), 1) < HEAD_DIM
    kall = jnp.concatenate([kp_ref[...], kc_ref[...]], axis=0)
    vall = jnp.concatenate([vp_ref[...], vc_ref[...]], axis=0)
    kroll = pltpu.roll(kall, HEAD_DIM, axis=1)
    vroll = pltpu.roll(vall, HEAD_DIM, axis=1)
    kdup = [jnp.where(low, kall, kroll).astype(BF16), jnp.where(low, kroll, kall).astype(BF16)]
    v_lo = [jnp.where(low, vall, 0.0).astype(BF16), jnp.where(low, vroll, 0.0).astype(BF16)]
    v_hi = [jnp.where(low, 0.0, vroll).astype(BF16), jnp.where(low, 0.0, vall).astype(BF16)]
    qi = lax.broadcasted_iota(jnp.int32, (blk, 2 * blk), 0)
    kj = lax.broadcasted_iota(jnp.int32, (blk, 2 * blk), 1)
    band = (kj >= qi) & (kj <= qi + blk)
    zero = jnp.zeros((), BF16)
    for bi in range(nblk):
        rows = slice(bi * blk, (bi + 1) * blk)
        krows = slice(bi * blk, (bi + 2) * blk)
        ok = band & ((kj >= blk) | (n > 0)) if bi == 0 else band
        heads = range(ATTN_HEADS)
        kv = [head // ATTN_GROUP for head in heads]
        sinks = [sink_ref[head] * LOG2E for head in heads]
        s = []
        for head in heads:
            qp = q_ref[rows, (head // 2) * pair_w:(head // 2 + 1) * pair_w]
            qm = jnp.where(low if head % 2 == 0 else jnp.logical_not(low), qp, zero)
            s.append(jnp.where(ok, _dg(qm, kdup[kv[head]][krows], NT), -jnp.inf))
        m = [jnp.maximum(jnp.max(s[head], axis=-1, keepdims=True), sinks[head]) for head in heads]
        p = [jnp.exp2(s[head] - m[head]) for head in heads]
        den = [jnp.sum(p[head], axis=-1, keepdims=True) + jnp.exp2(sinks[head] - m[head]) for head in heads]
        pv = [_dg(p[head].astype(BF16), (v_lo if head % 2 == 0 else v_hi)[kv[head]][krows], NN)
              for head in heads]
        pairs = [(pv[2 * j] + pv[2 * j + 1]) / jnp.where(low, den[2 * j], den[2 * j + 1])
                 for j in range(ATTN_HEADS // 2)]
        o_ref[rows, :] = _rms(jnp.concatenate(pairs, axis=1), g_ref[...]).astype(BF16)


def _attention_prompt(qa, ka, va, sinks, g_attn_out, batch):
    t = qa.shape[0]
    nb = t // batch // (WINDOW * ATTN_STEP_BLOCKS)
    cur = lambda w: pl.BlockSpec((WINDOW * ATTN_STEP_BLOCKS, w), lambda b, n: (b * nb + n, 0))
    prev = lambda w: pl.BlockSpec(
        (WINDOW, w), lambda b, n: (ATTN_STEP_BLOCKS * (b * nb + n) - jnp.minimum(n, 1), 0))
    return pl.pallas_call(
        _attn_prompt_kernel,
        grid=(batch, nb),
        in_specs=[pl.BlockSpec(memory_space=pltpu.SMEM),
                  cur(ATTN_WIDTH), cur(KV_WIDTH), prev(KV_WIDTH), cur(KV_WIDTH), prev(KV_WIDTH),
                  pl.BlockSpec((1, ATTN_WIDTH), lambda b, n: (0, 0))],
        out_specs=cur(ATTN_WIDTH),
        out_shape=jax.ShapeDtypeStruct((t, ATTN_WIDTH), BF16),
        name="attn_prompt",
    )(sinks, qa, ka, ka, va, va, g_attn_out.reshape(1, -1))


def _lower_bound(lb_logits_ref):
    lg = lb_logits_ref[...]
    e = jnp.exp(lg - jnp.max(lg, axis=0, keepdims=True))
    return e[0:1] / jnp.sum(e, axis=0, keepdims=True)


def _hgrn_prompt_kernel(q_ref, f_ref, i_ref, g_ref, lbl_ref, gn_ref, o_ref, sfin_ref, st_ref):
    step = pl.program_id(1)
    C = HG_CHUNK
    W = HG_WIDTH
    nsub = C // HG_SUB
    n_chunks = q_ref.shape[0] // C
    heads = [slice(hh * HG_DK, (hh + 1) * HG_DK) for hh in range(HG_HEADS)]

    @pl.when(step == 0)
    def _():
        st_ref[...] = jnp.zeros_like(st_ref)

    lb = _lower_bound(lbl_ref)
    r64 = lax.broadcasted_iota(jnp.int32, (C, C), 0)
    c64 = lax.broadcasted_iota(jnp.int32, (C, C), 1)
    tri = (r64 >= c64).astype(BF16)
    rsub = r64 % HG_SUB
    sr = lax.broadcasted_iota(jnp.int32, ((HG_SUB - 1) * C, C), 0)
    su = lax.broadcasted_iota(jnp.int32, ((HG_SUB - 1) * C, C), 1)
    sd, st_row = sr // C + 1, sr % C
    shifts = ((su == st_row - sd) & (st_row % HG_SUB >= sd)).astype(BF16)
    o_intra, q_state, upd, decay_last = {}, {}, {}, {}
    for ci in range(n_chunks):
        rows = slice(ci * C, (ci + 1) * C)
        f = lb + (1.0 - lb) * _sigmoid(f_ref[rows, :])
        kk = 1.0 - f
        logf = jnp.log2(f)
        q = q_ref[rows, :]
        v = i_ref[rows, :]
        b = _dot_exact_lhs(tri, logf, NN)
        k_shift = _dg(shifts, kk.astype(BF16), NN)
        prods = [q * kk]
        w = logf
        for d in range(1, HG_SUB):
            if d > 1:
                w = w + pltpu.roll(logf, d - 1, axis=0)
            prods.append(q * k_shift[(d - 1) * C:d * C] * jnp.exp2(w))
        ends = [b[j * HG_SUB + HG_SUB - 1:(j + 1) * HG_SUB] for j in range(nsub)]
        kt = kk * jnp.exp2(jnp.concatenate([jnp.broadcast_to(e, (HG_SUB, W)) for e in ends], axis=0) - b)
        lhs, rhs = [], []
        for j in range(nsub - 1):
            lo = (j + 1) * HG_SUB
            lhs.append(jnp.concatenate(
                [jnp.zeros((lo, W), F32), q[lo:] * jnp.exp2(b[lo:] - ends[j])], axis=0).astype(BF16))
            pieces = [kt[j * HG_SUB:lo]]
            if j > 0:
                pieces.insert(0, jnp.zeros((j * HG_SUB, W), F32))
            pieces.append(jnp.zeros((C - lo, W), F32))
            rhs.append(jnp.concatenate(pieces, axis=0).astype(BF16))
        q_state[ci] = (q * jnp.exp2(b)).astype(BF16)
        kd = (kk * jnp.exp2(ends[-1] - b)).astype(BF16)
        decay_last[ci] = jnp.exp2(ends[-1])
        a_off = [_dg(jnp.concatenate([x[:, sl] for x in lhs], axis=1),
                     jnp.concatenate([x[:, sl] for x in rhs], axis=1), NT) for sl in heads]
        for hh, sl in enumerate(heads):
            upd[ci, hh] = _dg(v[:, sl], kd[:, sl], TN)
        sums = [[jnp.sum(prods[d][:, sl], axis=-1, keepdims=True) for d in range(HG_SUB)] for sl in heads]
        for hh, sl in enumerate(heads):
            a = a_off[hh]
            for d in range(HG_SUB):
                a = jnp.where((c64 == r64 - d) & (rsub >= d), sums[hh][d], a)
            o_intra[ci, hh] = _dg(a.astype(BF16), v[:, sl], NN)
    finals = []
    outs = {}
    for hh, sl in enumerate(heads):
        st = st_ref[hh]
        for ci in range(n_chunks):
            outs[ci, hh] = o_intra[ci, hh] + _dg(q_state[ci][:, sl], st.astype(BF16), NT)
            st = st * decay_last[ci][:, sl] + upd[ci, hh]
        st_ref[hh] = st
        finals.append(st)
    for ci in range(n_chunks):
        rows = slice(ci * C, (ci + 1) * C)
        for hh, sl in enumerate(heads):
            g = g_ref[rows, sl]
            o_ref[rows, sl] = (_rms(outs[ci, hh], gn_ref[:, sl]) * (g * _sigmoid(g))).astype(BF16)

    @pl.when(step == pl.num_programs(1) - 1)
    def _():
        for hh in range(HG_HEADS):
            sfin_ref[0, hh] = finals[hh].T


def _hgrn_prompt(qh, fh, ih, gh, lb_logits, g_hg_out, batch):
    t = qh.shape[0]
    nc = t // batch // (HG_CHUNK * HG_STEP_CHUNKS)
    blk = pl.BlockSpec((HG_CHUNK * HG_STEP_CHUNKS, HG_WIDTH), lambda b, c: (b * nc + c, 0))
    const = lambda r: pl.BlockSpec((r, HG_WIDTH), lambda b, c: (0, 0))
    return pl.pallas_call(
        _hgrn_prompt_kernel,
        grid=(batch, nc),
        in_specs=[blk, blk, blk, blk, const(lb_logits.shape[0]), const(1)],
        out_specs=[blk, pl.BlockSpec((1, HG_HEADS, HG_DK, HG_DV), lambda b, c: (b, 0, 0, 0))],
        out_shape=[jax.ShapeDtypeStruct((t, HG_WIDTH), BF16),
                   jax.ShapeDtypeStruct((batch, HG_HEADS, HG_DK, HG_DV), F32)],
        scratch_shapes=[pltpu.VMEM((HG_HEADS, HG_DV, HG_DK), F32)],
        name="hgrn_prompt",
    )(qh, fh, ih, gh, lb_logits, g_hg_out.reshape(1, -1))


def _attn_sample_kernel(sink_ref, q_ref, kn_ref, vn_ref, kc_ref, vc_ref, g_ref,
                        o_ref, ko_ref, vo_ref):
    bt = q_ref.shape[0]
    win = kc_ref.shape[1]
    kc = kc_ref[...]
    vc = vc_ref[...]
    kn = kn_ref[...]
    vn = vn_ref[...]
    outs = []
    for h in range(ATTN_KV_HEADS):
        ls = slice(h * HEAD_DIM, (h + 1) * HEAD_DIM)
        q = q_ref[:, h * ATTN_GROUP:(h + 1) * ATTN_GROUP, :]
        s = jnp.einsum('bgd,bjd->bgj', q, kc[:, :, ls].astype(BF16), preferred_element_type=F32)
        s_new = jnp.sum(q.astype(F32) * kn[:, None, ls], axis=-1, keepdims=True)
        gi = lax.broadcasted_iota(jnp.int32, (1, ATTN_GROUP, 1), 1)
        sink = jnp.zeros((1, ATTN_GROUP, 1), F32)
        for g in range(ATTN_GROUP):
            sink = jnp.where(gi == g, sink_ref[h * ATTN_GROUP + g] * LOG2E, sink)
        m = jnp.maximum(jnp.maximum(jnp.max(s, axis=-1, keepdims=True), s_new), sink)
        p = jnp.exp2(s - m)
        p_new = jnp.exp2(s_new - m)
        den = jnp.sum(p, axis=-1, keepdims=True) + p_new + jnp.exp2(sink - m)
        o = jnp.einsum('bgj,bjd->bgd', p.astype(BF16), vc[:, :, ls].astype(BF16),
                       preferred_element_type=F32)
        o = (o + p_new * vn[:, None, ls]) / den
        outs.append(o)
    ssq = sum(jnp.sum(jnp.sum(o * o, axis=-1, keepdims=True), axis=1, keepdims=True) for o in outs)
    scale = lax.rsqrt(ssq / ATTN_WIDTH + EPS)
    for h in range(ATTN_KV_HEADS):
        o_ref[h] = (outs[h] * scale * g_ref[h][None]).astype(BF16)
    ri = lax.broadcasted_iota(jnp.int32, (win, KV_WIDTH), 0)
    for b in range(bt):
        ko_ref[b] = jnp.where(ri == win - 1, kn[b:b + 1], pltpu.roll(kc[b], win - 1, axis=0))
        vo_ref[b] = jnp.where(ri == win - 1, vn[b:b + 1], pltpu.roll(vc[b], win - 1, axis=0))


def _attention_sample(qa, ka, va, cache_k, cache_v, sinks, g_attn_out, bt=16):
    nb = qa.shape[0]
    win = cache_k.shape[1]
    q3 = qa.reshape(nb, ATTN_HEADS, HEAD_DIM)
    g3 = g_attn_out.reshape(ATTN_KV_HEADS, ATTN_GROUP, HEAD_DIM)
    row = lambda w: pl.BlockSpec((bt, w), lambda i: (i, 0))
    cache = pl.BlockSpec((bt, win, KV_WIDTH), lambda i: (i, 0, 0))
    o4, k_new, v_new = pl.pallas_call(
        _attn_sample_kernel,
        grid=(nb // bt,),
        in_specs=[pl.BlockSpec(memory_space=pltpu.SMEM),
                  pl.BlockSpec((bt, ATTN_HEADS, HEAD_DIM), lambda i: (i, 0, 0)),
                  row(KV_WIDTH), row(KV_WIDTH), cache, cache,
                  pl.BlockSpec((ATTN_KV_HEADS, ATTN_GROUP, HEAD_DIM), lambda i: (0, 0, 0))],
        out_specs=[pl.BlockSpec((ATTN_KV_HEADS, bt, ATTN_GROUP, HEAD_DIM), lambda i: (0, i, 0, 0)),
                   cache, cache],
        out_shape=[jax.ShapeDtypeStruct((ATTN_KV_HEADS, nb, ATTN_GROUP, HEAD_DIM), BF16),
                   jax.ShapeDtypeStruct(cache_k.shape, F32),
                   jax.ShapeDtypeStruct(cache_v.shape, F32)],
        name="attn_sample",
    )(sinks, q3, ka, va, cache_k, cache_v, g3)
    oa = jnp.transpose(o4, (1, 0, 2, 3)).reshape(nb, ATTN_WIDTH)
    return oa, k_new, v_new


def _hgrn_sample_kernel(q_ref, f_ref, i_ref, g_ref, lbl_ref, gn_ref, s0_ref, o_ref, s_ref):
    bt = q_ref.shape[0]
    lb = _lower_bound(lbl_ref)
    f = lb + (1.0 - lb) * _sigmoid(f_ref[...])
    kk = 1.0 - f
    q = q_ref[...]
    v = i_ref[...].astype(F32)
    g = g_ref[...]
    gate = g * _sigmoid(g)
    r = lax.broadcasted_iota(jnp.int32, (HG_DK, HG_DK), 0)
    cc = lax.broadcasted_iota(jnp.int32, (HG_DK, HG_DK), 1)
    diag = r == cc

    def column(x_row):
        return jnp.sum(jnp.where(diag, x_row, 0.0), axis=1, keepdims=True)

    units = [(b, hh, slice(hh * HG_DK, (hh + 1) * HG_DK)) for b in range(bt) for hh in range(HG_HEADS)]
    f_col = [column(f[b:b + 1, sl]) for b, hh, sl in units]
    k_col = [column(kk[b:b + 1, sl]) for b, hh, sl in units]
    q_col = [column(q[b:b + 1, sl]) for b, hh, sl in units]
    s_new = [f_col[u] * s0_ref[b, hh] + k_col[u] * v[b:b + 1, sl] for u, (b, hh, sl) in enumerate(units)]
    for u, (b, hh, sl) in enumerate(units):
        s_ref[b, hh] = s_new[u]
    outs = [_rms(jnp.sum(s_new[u] * q_col[u], axis=0, keepdims=True), gn_ref[:, sl])
            for u, (b, hh, sl) in enumerate(units)]
    for b in range(bt):
        row = jnp.concatenate(outs[b * HG_HEADS:(b + 1) * HG_HEADS], axis=1)
        o_ref[b:b + 1, :] = (row * gate[b:b + 1]).astype(BF16)


def _hgrn_sample(qh, fh, ih, gh, lb_logits, g_hg_out, state, bt=16):
    nb = qh.shape[0]
    row = pl.BlockSpec((bt, HG_WIDTH), lambda i: (i, 0))
    const = lambda r: pl.BlockSpec((r, HG_WIDTH), lambda i: (0, 0))
    st = pl.BlockSpec((bt, HG_HEADS, HG_DK, HG_DV), lambda i: (i, 0, 0, 0))
    return pl.pallas_call(
        _hgrn_sample_kernel,
        grid=(nb // bt,),
        in_specs=[row, row, row, row, const(lb_logits.shape[0]), const(1), st],
        out_specs=[row, st],
        out_shape=[jax.ShapeDtypeStruct((nb, HG_WIDTH), BF16),
                   jax.ShapeDtypeStruct(state.shape, F32)],
        name="hgrn_sample",
    )(qh, fh, ih, gh, lb_logits, g_hg_out.reshape(1, -1), state)


def _outproj_kernel(x_ref, oa_ref, oh_ref, gt_ref, sh_ref, sc_ref, g_ref, wo_ref, wr_ref, br_ref, *rest,
                    n_tiles):
    x1_ref, xs_ref, lp_ref, w_ref, pc_ref, hb_scr, lp_scr = rest[-7:]
    i = pl.program_id(0)
    tm = x_ref.shape[0]
    rows = xs_ref.shape[0]

    @pl.when(i == 0)
    def _():
        hb_scr[...] = jnp.zeros_like(hb_scr)
        lp_scr[...] = jnp.zeros_like(lp_scr)

    hb_prev = jnp.where(i <= n_tiles, hb_scr[...], jnp.zeros((), BF16))
    lp_prev = [lp_scr[k:k + 1, :] for k in range(TOP_K)]
    chunk = 256
    pending = list(range(0, rows, chunk))

    def place_next():
        if pending:
            r0 = pending.pop(0)
            si = lax.broadcasted_iota(jnp.int32, (chunk, tm), 0) + r0
            hit = (si == lp_prev[0]) | (si == lp_prev[1]) | (si == lp_prev[2]) | (si == lp_prev[3])
            xs_ref[r0:r0 + chunk, :] = _dg(jnp.where(hit, 1.0, 0.0).astype(BF16), hb_prev, NN).astype(BF16)

    place_next()
    gt = gt_ref[...].reshape(-1, D_MODEL)
    sh = sh_ref[...].reshape(-1, D_MODEL)
    sc = sc_ref[...].reshape(-1, D_MODEL)
    mix = _dg(oa_ref[...], wo_ref[0:ATTN_WIDTH, :], NN) + _dg(oh_ref[...], wo_ref[ATTN_WIDTH:, :], NN)
    x1 = x_ref[...] + gt * mix
    x1_ref[...] = x1
    place_next()
    h2 = _rms(x1, g_ref[...]) * (1.0 + sc) + sh
    place_next()
    logits = _dot_hilo(wr_ref[...], h2, NT) + br_ref[...]
    place_next()
    ei = lax.broadcasted_iota(jnp.int32, logits.shape, 0)
    vals, sel = [], []
    l = logits
    for _ in range(TOP_K):
        m = jnp.max(l, axis=0, keepdims=True)
        idx = jnp.min(jnp.where(l == m, ei, N_EXPERTS), axis=0, keepdims=True)
        pick = ei == idx
        vals.append(m)
        sel.append(pick)
        l = jnp.where(pick, -jnp.inf, l)
        place_next()
    ex = [jnp.exp(v - vals[0]) for v in vals]
    den = ex[0] + ex[1] + ex[2] + ex[3]
    for k in range(TOP_K):
        w_ref[k:k + 1, :] = ex[k] / den
    place_next()

    member = jnp.where(sel[0] | sel[1] | sel[2] | sel[3], 1.0, 0.0)
    tr = lax.broadcasted_iota(jnp.int32, (tm, tm), 0)
    tc = lax.broadcasted_iota(jnp.int32, (tm, tm), 1)
    rank = _dg(member.astype(BF16), (tr < tc).astype(BF16), NN)
    pieces = jnp.floor((jnp.sum(member, axis=1, keepdims=True) + (PIECE - 1)) * (1.0 / PIECE))
    er = lax.broadcasted_iota(jnp.int32, (N_EXPERTS, N_EXPERTS), 0)
    ec = lax.broadcasted_iota(jnp.int32, (N_EXPERTS, N_EXPERTS), 1)
    pieces_b = jnp.broadcast_to(pieces, (N_EXPERTS, 128))
    pc_ref[0] = pieces_b.astype(jnp.int32)
    base = _dg((ec < er).astype(BF16), pieces_b.astype(BF16), NN)[:, 0:1] * PIECE
    slot = base + rank
    for k in range(TOP_K):
        lp = jnp.sum(jnp.where(sel[k], slot, 0.0), axis=0, keepdims=True).astype(jnp.int32)
        lp_ref[k:k + 1, :] = lp
        lp_scr[k:k + 1, :] = lp
    while pending:
        place_next()
    hb_scr[...] = h2.astype(BF16)


def _out_projection(x, oa, oh, mod3, mod2, g_ffn, w_out_bf16, w_router_t, b_router, tm, slots, slot_block):
    t = x.shape[0]
    nt = t // tm
    rows = SLOT_ROWS[tm]
    real = lambda i: jnp.minimum(i, nt - 1)
    if mod3 is not None:
        per_b = (t // mod3.shape[0]) // tm
        mspec = lambda j: pl.BlockSpec((1, 1, D_MODEL), lambda i: (real(i) // per_b, 0, j))
        mod = mod3
    else:
        mspec = lambda j: pl.BlockSpec((tm, D_MODEL), lambda i: (real(i), j))
        mod = mod2
    row = lambda w: pl.BlockSpec((tm, w), lambda i: (real(i), 0))
    full = lambda a: pl.BlockSpec(a.shape, lambda i: (0,) * a.ndim)
    g2 = g_ffn.reshape(1, -1)
    br = b_router.reshape(-1, 1)
    args = [x, oa, oh, mod, mod, mod, g2, w_out_bf16, w_router_t, br]
    in_specs = [row(D_MODEL), row(ATTN_WIDTH), row(HG_WIDTH), mspec(2), mspec(3), mspec(4),
                full(g2), full(w_out_bf16), full(w_router_t), full(br)]
    aliases = {}
    n_fill = 0
    if slots is not None:
        args.append(slots)
        in_specs.append(pl.BlockSpec(memory_space=pl.ANY))
        aliases = {len(args) - 1: 1}
    else:
        n_fill = pl.cdiv(N_PIECES * PIECE - nt * rows, rows)
    return pl.pallas_call(
        functools.partial(_outproj_kernel, n_tiles=nt),
        grid=(nt + 1 + n_fill,),
        in_specs=in_specs,
        out_specs=[row(D_MODEL),
                   pl.BlockSpec((rows, D_MODEL), lambda i: (slot_block + jnp.maximum(i - 1, 0), 0)),
                   pl.BlockSpec((TOP_K, tm), lambda i: (0, real(i))),
                   pl.BlockSpec((TOP_K, tm), lambda i: (0, real(i))),
                   pl.BlockSpec((1, N_EXPERTS, 128), lambda i: (real(i), 0, 0))],
        out_shape=[jax.ShapeDtypeStruct((t, D_MODEL), F32),
                   jax.ShapeDtypeStruct((N_PIECES * PIECE, D_MODEL), BF16),
                   jax.ShapeDtypeStruct((TOP_K, t), jnp.int32),
                   jax.ShapeDtypeStruct((TOP_K, t), F32),
                   jax.ShapeDtypeStruct((nt, N_EXPERTS, 128), jnp.int32)],
        scratch_shapes=[pltpu.VMEM((tm, D_MODEL), BF16), pltpu.VMEM((8, tm), jnp.int32)],
        input_output_aliases=aliases,
        compiler_params=pltpu.CompilerParams(vmem_limit_bytes=V7X_VMEM_LIMIT),
        name="out_proj_router",
    )(*args)


def _segment_offsets_kernel(dest_ref, dd_ref, q_ref, o_ref):
    reached = dest_ref[...] <= q_ref[...]
    o_ref[...] = jnp.sum(jnp.where(reached, dd_ref[...], 0), axis=0, keepdims=True)


def _segment_offsets(dest, d_delta, queries):
    lanes = 512
    n_seg = -(-dest.shape[0] // 8) * 8
    n_q = -(-queries.shape[0] // lanes) * lanes
    never = jnp.iinfo(jnp.int32).max
    dest_c = jnp.pad(dest, (0, n_seg - dest.shape[0]), constant_values=never).reshape(n_seg, 1)
    dd_c = jnp.pad(d_delta, (0, n_seg - d_delta.shape[0])).reshape(n_seg, 1)
    q_r = jnp.pad(queries, (0, n_q - queries.shape[0])).reshape(1, n_q)
    seg = pl.BlockSpec((n_seg, 1), lambda i: (0, 0))
    out = pl.pallas_call(
        _segment_offsets_kernel,
        grid=(n_q // lanes,),
        in_specs=[seg, seg, pl.BlockSpec((1, lanes), lambda i: (0, i))],
        out_specs=pl.BlockSpec((1, lanes), lambda i: (0, i)),
        out_shape=jax.ShapeDtypeStruct((1, n_q), jnp.int32),
        name="segment_offsets",
    )(dest_c, dd_c, q_r)
    return out[0, :queries.shape[0]]


def _prefix_sums(x):
    i = jnp.arange(x.shape[-1])
    return jnp.sum(jnp.where(i[None, :] <= i[:, None], x[..., None, :], 0), axis=-1)


def _piece_tables(pieces_ie):
    cap = jnp.asarray(TILE_PIECE_CAP, jnp.int32)
    gbase = jnp.asarray(TILE_PIECE_BASE, jnp.int32)
    experts = jnp.arange(N_EXPERTS, dtype=jnp.int32)
    seg_src = gbase[:, None] + _prefix_sums(pieces_ie) - pieces_ie
    used_i = jnp.sum(pieces_ie, axis=1)
    tail_i = cap - used_i
    np_e = jnp.sum(pieces_ie, axis=0)
    rem = np_e % STEP_PIECES
    head_e = jnp.where(rem > 0, rem, jnp.minimum(np_e, STEP_PIECES))
    nt_e = np_e // STEP_PIECES + (rem > 0)
    tile_end = _prefix_sums(nt_e)
    tile_start = tile_end - nt_e
    n_comp = jnp.sum(nt_e)
    q_start_e = _prefix_sums(np_e) - np_e
    n_used = jnp.sum(np_e)
    tt = jnp.arange(MOE_MAX_STEPS, dtype=jnp.int32)
    e_t = jnp.minimum(jnp.sum(tt[:, None] >= tile_end[None, :], axis=1), N_EXPERTS - 1).astype(jnp.int32)
    of_step = lambda v: jnp.sum(jnp.where(e_t[:, None] == experts[None, :], v[None, :], 0), axis=1)
    is_comp = tt < n_comp
    k = tt - of_step(tile_start)
    head_t = of_step(head_e)
    q0_comp = of_step(q_start_e) + jnp.where(k == 0, 0, head_t + STEP_PIECES * (k - 1))
    live_comp = jnp.where(k == 0, head_t, STEP_PIECES)
    q0_fill = n_used + STEP_PIECES * (tt - n_comp)
    q0 = jnp.where(is_comp, q0_comp, q0_fill)
    live = jnp.where(is_comp, live_comp, jnp.clip(N_PIECES - q0_fill, 0, STEP_PIECES))
    n_busy = n_comp + (N_PIECES - n_used + STEP_PIECES - 1) // STEP_PIECES
    pieces_ei = pieces_ie.T
    dest_seg = q_start_e[:, None] + _prefix_sums(pieces_ei) - pieces_ei
    dest = jnp.concatenate([dest_seg.reshape(-1), n_used + _prefix_sums(tail_i) - tail_i])
    src = jnp.concatenate([seg_src.T.reshape(-1), gbase + used_i])
    delta = src - dest
    d_delta = delta - jnp.concatenate([jnp.zeros((1,), jnp.int32), delta[:-1]])
    lane = jnp.arange(STEP_PIECES, dtype=jnp.int32)
    qq = (q0[:, None] + lane[None, :]).reshape(-1)
    ok = lane[None, :] < live[:, None]
    piece = (qq + _segment_offsets(dest, d_delta, qq)).reshape(MOE_MAX_STEPS, STEP_PIECES)
    dump = N_PIECES + (tt % 2)[:, None] * STEP_PIECES + lane[None, :]
    tbl_out = jnp.where(ok, piece, dump)
    last = jnp.maximum(n_comp - 1, 0)
    tbl_in = jnp.where(ok & is_comp[:, None], piece, 0)
    in_last = jnp.sum(jnp.where((tt == last)[:, None], tbl_in, 0), axis=0)
    tbl_in = jnp.where(is_comp[:, None], tbl_in, in_last[None, :])
    texp = jnp.where(is_comp, e_t, jnp.sum(jnp.where(tt == last, e_t, 0)))
    first = is_comp & (k == 0)
    kind = jnp.where(is_comp, jnp.where(live <= HALF_PIECES, 1, 2), jnp.where(tt < n_busy, 0, 3))
    has_e = nt_e > 0
    order_e = _prefix_sums(has_e.astype(jnp.int32)) - 1
    later = has_e[None, :] & (experts[None, :] > experts[:, None])
    next_e = jnp.min(jnp.where(later, experts[None, :], N_EXPERTS), axis=1)
    next_e = jnp.where(next_e < N_EXPERTS, next_e, -1)
    i32 = lambda a: a.astype(jnp.int32)
    return (i32(tbl_in.reshape(-1)), i32(tbl_out.reshape(-1)), i32(texp), i32(kind), i32(first),
            i32(of_step(order_e) % 2), i32(of_step(next_e)))


def _moe_grouped_kernel(tin_ref, tout_ref, texp_ref, kind_ref, first_ref, wpar_ref, nexp_ref,
                        x_hbm, wu_hbm, wd_hbm, bg_ref, bl_ref, bd_ref, y_hbm,
                        wg_s, wl_s, wd_s, xbuf, ybuf, xsems, ysems, wu_buf, wd_buf, wsems):
    npc = STEP_PIECES
    t = pl.program_id(0)
    nsteps = pl.num_programs(0)
    kind = kind_ref[t]
    slot = t % 2

    def x_copy(step, buf, p):
        src = tin_ref[step * npc + p]
        return pltpu.make_async_copy(
            x_hbm.at[pl.ds(pl.multiple_of(src * PIECE, PIECE), PIECE), :],
            xbuf.at[buf, pl.ds(p * PIECE, PIECE), :],
            xsems.at[buf, p])

    def x_each(step, buf, fn):
        k = kind_ref[step]

        @pl.when(k == 2)
        def _():
            for p in range(STEP_PIECES):
                fn(x_copy(step, buf, p))

        @pl.when(k == 1)
        def _():
            for p in range(HALF_PIECES):
                fn(x_copy(step, buf, p))

    def y_copy(step, buf, p):
        dst = tout_ref[step * npc + p]
        return pltpu.make_async_copy(
            ybuf.at[buf, pl.ds(p * PIECE, PIECE), :],
            y_hbm.at[pl.ds(pl.multiple_of(dst * PIECE, PIECE), PIECE), :],
            ysems.at[buf, p])

    def y_wait(step, buf):
        for p in range(npc):
            y_copy(step, buf, p).wait()

    def w_copies(expert, buf):
        return (pltpu.make_async_copy(wu_hbm.at[expert], wu_buf.at[buf], wsems.at[buf, 0]),
                pltpu.make_async_copy(wd_hbm.at[expert], wd_buf.at[buf], wsems.at[buf, 1]))

    @pl.when(t == 0)
    def _():
        ybuf[...] = jnp.zeros_like(ybuf)
        x_each(0, 0, lambda cp: cp.start())
        for cp in w_copies(texp_ref[0], wpar_ref[0]):
            cp.start()

    @pl.when(t >= 2)
    def _():
        y_wait(t - 2, slot)

    @pl.when(t + 1 < nsteps)
    def _():
        x_each(t + 1, 1 - slot, lambda cp: cp.start())

    @pl.when((kind > 0) & (kind < 3) & (first_ref[t] == 1))
    def _():
        wb = wpar_ref[t]
        for cp in w_copies(texp_ref[t], wb):
            cp.wait()

        @pl.when(nexp_ref[t] >= 0)
        def _():
            for cp in w_copies(nexp_ref[t], 1 - wb):
                cp.start()

        cb = 256
        r = lax.broadcasted_iota(jnp.int32, (cb, cb), 0)
        c = lax.broadcasted_iota(jnp.int32, (cb, cb), 1)
        perm = jnp.where(r == jnp.where(c < cb // 2, 2 * c, 2 * (c - cb // 2) + 1), 1.0, 0.0).astype(BF16)
        for blk in range(2 * D_FF // cb):
            wp = _dg(wu_buf[wb, :, blk * cb:(blk + 1) * cb].astype(BF16), perm, NN).astype(BF16)
            wg_s[:, blk * (cb // 2):(blk + 1) * (cb // 2)] = wp[:, :cb // 2]
            wl_s[:, blk * (cb // 2):(blk + 1) * (cb // 2)] = wp[:, cb // 2:]
        wd_s[...] = wd_buf[wb].astype(BF16)

    x_each(t, slot, lambda cp: cp.wait())

    def expert(n_pieces):
        xb = xbuf[slot, 0:n_pieces * PIECE, :]
        groups = [slice(j * FF_GROUP, (j + 1) * FF_GROUP) for j in range(D_FF // FF_GROUP)]
        glu = [jnp.minimum(_dg(xb, wg_s[:, c], NN) + bg_ref[0, :, c], SWIGLU_LIMIT) for c in groups]
        lin = [jnp.clip(_dg(xb, wl_s[:, c], NN) + bl_ref[0, :, c], -SWIGLU_LIMIT, SWIGLU_LIMIT)
               for c in groups]
        act = [(g * _sigmoid(SWIGLU_ALPHA * g) * (l + 1.0)).astype(BF16) for g, l in zip(glu, lin)]
        y = bd_ref[0] + _dg(act[0], wd_s[groups[0], :], NN)
        for c, a in zip(groups[1:], act[1:]):
            y = y + _dg(a, wd_s[c, :], NN)
        ybuf[slot, 0:n_pieces * PIECE, :] = y.astype(BF16)

    @pl.when(kind == 2)
    def _():
        expert(STEP_PIECES)

    @pl.when(kind == 1)
    def _():
        expert(HALF_PIECES)

    @pl.when(kind == 0)
    def _():
        ybuf[slot] = jnp.zeros((npc * PIECE, D_MODEL), BF16)

    for p in range(npc):
        y_copy(t, slot, p).start()

    @pl.when(t == nsteps - 1)
    def _():
        y_wait(t - 1, 1 - slot)
        y_wait(t, slot)


def _moe_grouped(slots, tables, w_up, w_down, bg, bl, bd):
    npc = STEP_PIECES
    by_expert = lambda shape: pl.BlockSpec(
        shape, lambda t, tin, tout, texp, kind, first, wpar, nexp: (texp[t],) + (0,) * (len(shape) - 1))
    return pl.pallas_call(
        _moe_grouped_kernel,
        grid_spec=pltpu.PrefetchScalarGridSpec(
            num_scalar_prefetch=7,
            grid=(MOE_MAX_STEPS,),
            in_specs=[pl.BlockSpec(memory_space=pl.ANY), pl.BlockSpec(memory_space=pl.ANY),
                      pl.BlockSpec(memory_space=pl.ANY),
                      by_expert((1, 1, D_FF)), by_expert((1, 1, D_FF)), by_expert((1, 1, D_MODEL))],
            out_specs=pl.BlockSpec(memory_space=pl.ANY),
            scratch_shapes=[pltpu.VMEM((D_MODEL, D_FF), BF16), pltpu.VMEM((D_MODEL, D_FF), BF16),
                            pltpu.VMEM((D_FF, D_MODEL), BF16),
                            pltpu.VMEM((2, npc * PIECE, D_MODEL), BF16),
                            pltpu.VMEM((2, npc * PIECE, D_MODEL), BF16),
                            pltpu.SemaphoreType.DMA((2, npc)),
                            pltpu.SemaphoreType.DMA((2, npc)),
                            pltpu.VMEM((2, D_MODEL, 2 * D_FF), F32),
                            pltpu.VMEM((2, D_FF, D_MODEL), F32),
                            pltpu.SemaphoreType.DMA((2, 2))]),
        out_shape=jax.ShapeDtypeStruct(((N_PIECES + 2 * npc) * PIECE, D_MODEL), BF16),
        compiler_params=pltpu.CompilerParams(vmem_limit_bytes=V7X_VMEM_LIMIT),
        name="moe_grouped",
    )(*tables, slots, w_up, w_down, bg, bl, bd)


def _combine_kernel(y_ref, lp_ref, w_ref, x1_ref, gt_ref, gf_ref, o_ref):
    tm = x1_ref.shape[0]
    rows = y_ref.shape[0]
    r = lax.broadcasted_iota(jnp.int32, (tm, tm), 0)
    c = lax.broadcasted_iota(jnp.int32, (tm, tm), 1)
    diag = r == c

    def column(x_row):
        return jnp.sum(jnp.where(diag, x_row, 0.0), axis=1, keepdims=True)

    pos = [column(lp_ref[k:k + 1, :].astype(F32)).astype(jnp.int32) for k in range(TOP_K)]
    wgt = [column(w_ref[k:k + 1, :]) for k in range(TOP_K)]
    chunk = 512
    moe = jnp.zeros((tm, D_MODEL), F32)
    for r0 in range(0, rows, chunk):
        si = lax.broadcasted_iota(jnp.int32, (tm, chunk), 1) + r0
        wt = jnp.zeros((tm, chunk), F32)
        for k in range(TOP_K):
            wt = jnp.where(si == pos[k], wgt[k], wt)
        moe = moe + _dg(wt.astype(BF16), y_ref[r0:r0 + chunk, :], NN)
    gt = gt_ref[...].reshape(-1, D_MODEL)
    o_ref[...] = _rms(x1_ref[...] + gt * moe, gf_ref[...])


def _combine(yslots, lp, w, x1, mod3, mod2, g_final, tm, slot_block):
    t = x1.shape[0]
    nt = t // tm
    rows = SLOT_ROWS[tm]
    if mod3 is not None:
        per_b = (t // mod3.shape[0]) // tm
        gspec = pl.BlockSpec((1, 1, D_MODEL), lambda i: (i // per_b, 0, 5))
        mod = mod3
    else:
        gspec = pl.BlockSpec((tm, D_MODEL), lambda i: (i, 5))
        mod = mod2
    return pl.pallas_call(
        _combine_kernel,
        grid=(nt,),
        in_specs=[pl.BlockSpec((rows, D_MODEL), lambda i: (slot_block + i, 0)),
                  pl.BlockSpec((TOP_K, tm), lambda i: (0, i)),
                  pl.BlockSpec((TOP_K, tm), lambda i: (0, i)),
                  pl.BlockSpec((tm, D_MODEL), lambda i: (i, 0)),
                  gspec,
                  pl.BlockSpec((1, D_MODEL), lambda i: (0, 0))],
        out_specs=pl.BlockSpec((tm, D_MODEL), lambda i: (i, 0)),
        out_shape=jax.ShapeDtypeStruct((t, D_MODEL), F32),
        compiler_params=pltpu.CompilerParams(vmem_limit_bytes=V7X_VMEM_LIMIT),
        name="moe_combine",
    )(yslots, lp, w, x1, mod, g_final.reshape(1, -1))


def kernel(x_prompt, x_sample, c_prompt, c_sample, cache_k_win, cache_v_win, state_hgrn, w_ada, b_ada,
           g_mix, g_ffn, w_in, attn_sinks, g_attn_out, hg_lb_logits, g_hg_out, w_out, w_router, b_router,
           w_up, b_up, w_down, b_down, g_final):
    batch, seq, d = x_prompt.shape
    nsamp = x_sample.shape[0]
    win = cache_k_win.shape[2]
    layer = 0

    mod = _modulation(jnp.concatenate([c_prompt, c_sample], axis=0), w_ada[layer], b_ada[layer])
    mod_p = mod[:batch].reshape(batch, 1, 6 * d)
    mod_s = mod[batch:]

    assert (batch * seq, nsamp) == (N_PROMPT_TILES * TOK_TILE, SAMPLE_TILE)
    w_in_b = w_in[layer].astype(BF16)
    w_out_b = w_out[layer].astype(BF16)
    w_router_t = w_router[layer].T
    bg = b_up[layer][:, None, 0::2]
    bl = b_up[layer][:, None, 1::2]
    bd = b_down[layer][:, None, :]

    xp = x_prompt.reshape(batch * seq, d)
    cos_p, sin_p = _rope_tables(np.arange(seq))
    qa, ka, va, qh, fh, ih, gh = _in_projection(xp, mod_p, None, g_mix[layer], w_in_b, cos_p, sin_p,
                                                IN_PROJ_TILE)
    oa = _attention_prompt(qa, ka, va, attn_sinks[layer], g_attn_out[layer], batch)
    oh, s_prompt = _hgrn_prompt(qh, fh, ih, gh, hg_lb_logits, g_hg_out[layer], batch)
    x1_p, slots, lp_p, cw_p, pc_p = _out_projection(
        xp, oa, oh, mod_p, None, g_ffn[layer], w_out_b, w_router_t, b_router[layer], TOK_TILE, None, 0)
    k_win_p = ka.reshape(batch, seq, ATTN_KV_HEADS, HEAD_DIM)[:, seq - win:]
    v_win_p = va.reshape(batch, seq, ATTN_KV_HEADS, HEAD_DIM)[:, seq - win:]

    xs = x_sample.reshape(nsamp, d)
    cos_s, sin_s = _rope_tables(np.full((nsamp,), PAST_LEN))
    qa, ka, va, qh, fh, ih, gh = _in_projection(xs, None, mod_s, g_mix[layer], w_in_b, cos_s, sin_s, nsamp)
    oa, k_win_s, v_win_s = _attention_sample(
        qa, ka, va, cache_k_win[layer].reshape(nsamp, win, KV_WIDTH),
        cache_v_win[layer].reshape(nsamp, win, KV_WIDTH), attn_sinks[layer], g_attn_out[layer])
    oh, s_sample = _hgrn_sample(qh, fh, ih, gh, hg_lb_logits, g_hg_out[layer], state_hgrn[layer])
    sample_block = TILE_PIECE_BASE[-1] * PIECE // SLOT_ROWS[SAMPLE_TILE]
    x1_s, slots, lp_s, cw_s, pc_s = _out_projection(
        xs, oa, oh, None, mod_s, g_ffn[layer], w_out_b, w_router_t, b_router[layer], SAMPLE_TILE,
        slots, sample_block)

    tables = _piece_tables(jnp.concatenate([pc_p[:, :, 0], pc_s[:, :, 0]], axis=0))
    yslots = _moe_grouped(slots, tables, w_up[layer], w_down[layer], bg, bl, bd)
    y_prompt = _combine(yslots, lp_p, cw_p, x1_p, mod_p, None, g_final, TOK_TILE, 0)
    y_sample = _combine(yslots, lp_s, cw_s, x1_s, None, mod_s, g_final, SAMPLE_TILE, sample_block)

    kv_shape = (1, nsamp, win, ATTN_KV_HEADS, HEAD_DIM)
    return (y_prompt.reshape(batch, seq, d), y_sample.reshape(nsamp, 1, d),
            k_win_p[None], v_win_p[None], s_prompt[None],
            k_win_s.reshape(kv_shape), v_win_s.reshape(kv_shape), s_sample[None])
```

```python
import functools

import numpy as np
import jax
import jax.numpy as jnp
from jax import lax
from jax.experimental import pallas as pl
from jax.experimental.pallas import tpu as pltpu

F32 = jnp.float32
BF16 = jnp.bfloat16

D_MODEL = 1024
SEQ = 2048
PAST_LEN = 16384
ATTN_HEADS = 8
ATTN_KV_HEADS = 2
HEAD_DIM = 64
ATTN_GROUP = ATTN_HEADS // ATTN_KV_HEADS
ATTN_WIDTH = ATTN_HEADS * HEAD_DIM
KV_WIDTH = ATTN_KV_HEADS * HEAD_DIM
WINDOW = 128
ATTN_STEP_BLOCKS = 8
ROT_DIM = HEAD_DIM // 4
ROPE_THETA = 500000.0
HG_HEADS = 4
HG_DK = 128
HG_DV = 128
HG_WIDTH = HG_HEADS * HG_DV
HG_CHUNK = 64
HG_SUB = 8
HG_STEP_CHUNKS = 16
IN_COLS = ATTN_WIDTH + 2 * KV_WIDTH + 4 * HG_WIDTH
N_EXPERTS = 32
TOP_K = 4
D_FF = D_MODEL
SWIGLU_ALPHA = 1.702
SWIGLU_LIMIT = 7.0
EPS = 1e-5
LOG2E = float(np.log2(np.e))

V7X_VMEM_LIMIT = 56 * 1024 * 1024

PIECE = 16
TOK_TILE = 512
IN_PROJ_TILE = 1024
SAMPLE_TILE = 128
N_PROMPT_TILES = 32
SLOT_ROWS = {TOK_TILE: 2560, SAMPLE_TILE: 1024}
TILE_PIECE_CAP = [SLOT_ROWS[TOK_TILE] // PIECE] * N_PROMPT_TILES + [SLOT_ROWS[SAMPLE_TILE] // PIECE]
TILE_PIECE_BASE = [i * TILE_PIECE_CAP[0] for i in range(N_PROMPT_TILES + 1)]
N_PIECES = sum(TILE_PIECE_CAP)
STEP_PIECES = 32
FF_GROUP = 512
HALF_PIECES = 16
MOE_MAX_STEPS = N_PIECES // STEP_PIECES + N_EXPERTS + 1 + 2

NN = (((1,), (0,)), ((), ()))
NT = (((1,), (1,)), ((), ()))
TN = (((0,), (0,)), ((), ()))


def _dg(a, b, dims):
    return lax.dot_general(a, b, dims, preferred_element_type=F32)


def _split3(x):
    h = x.astype(BF16)
    r = x - h.astype(F32)
    m = r.astype(BF16)
    l = (r - m.astype(F32)).astype(BF16)
    return h, m, l


def _dot_f32(a, b, dims):
    ah, am, al = _split3(a)
    bh, bm, bl = _split3(b)
    return (_dg(ah, bh, dims) + (_dg(ah, bm, dims) + _dg(am, bh, dims))
            + (_dg(am, bm, dims) + _dg(ah, bl, dims) + _dg(al, bh, dims)))


def _split2(x):
    h = x.astype(BF16)
    return h, (x - h.astype(F32)).astype(BF16)


def _dot_hilo(a, b, dims):
    ah, al = _split2(a)
    bh, bl = _split2(b)
    return _dg(ah, bh, dims) + (_dg(ah, bl, dims) + _dg(al, bh, dims))


def _dot_exact_lhs(a_bf16, b, dims):
    bh, bm, bl = _split3(b)
    return _dg(a_bf16, bh, dims) + _dg(a_bf16, bm, dims) + _dg(a_bf16, bl, dims)


def _sigmoid(x):
    return 0.5 * jnp.tanh(0.5 * x) + 0.5


def _rms(x, g):
    return x * lax.rsqrt(jnp.mean(x * x, axis=-1, keepdims=True) + EPS) * g


def _mod_kernel(c_ref, w_ref, b_ref, o_ref):
    c = c_ref[...]
    o_ref[...] = _dot_f32(c * _sigmoid(c), w_ref[...], NN) + b_ref[...]


def _modulation(c_all, w_ada, b_ada):
    n = c_all.shape[0]
    return pl.pallas_call(
        _mod_kernel,
        grid=(6,),
        in_specs=[pl.BlockSpec((n, D_MODEL), lambda j: (0, 0)),
                  pl.BlockSpec((D_MODEL, D_MODEL), lambda j: (0, j)),
                  pl.BlockSpec((1, D_MODEL), lambda j: (0, j))],
        out_specs=pl.BlockSpec((n, D_MODEL), lambda j: (0, j)),
        out_shape=jax.ShapeDtypeStruct((n, 6 * D_MODEL), F32),
        compiler_params=pltpu.CompilerParams(vmem_limit_bytes=V7X_VMEM_LIMIT),
        name="adaln_mod",
    )(c_all, w_ada, b_ada.reshape(1, -1))


def _rotate(x, cos_t, sin_t):
    width = x.shape[-1]
    d = lax.broadcasted_iota(jnp.int32, x.shape, 1) % HEAD_DIM
    half = ROT_DIM // 2
    partner = jnp.where(d < half, pltpu.roll(x, width - half, axis=1), pltpu.roll(x, half, axis=1))
    return x * cos_t + partner * sin_t


def _inproj_kernel(x_ref, sh_ref, sc_ref, g_ref, w_ref, cos_ref, sin_ref,
                   qa_ref, ka_ref, va_ref, qh_ref, fh_ref, ih_ref, gh_ref):
    x = x_ref[...]
    sh = sh_ref[...].reshape(-1, D_MODEL)
    sc = sc_ref[...].reshape(-1, D_MODEL)
    hb = (_rms(x, g_ref[...]) * (1.0 + sc) + sh).astype(BF16)
    cos_k = cos_ref[...]
    sin_k = sin_ref[...]
    cos_q = jnp.concatenate([cos_k] * ATTN_GROUP, axis=1)
    sin_q = jnp.concatenate([sin_k] * ATTN_GROUP, axis=1)
    o = 0

    def project(width):
        nonlocal o
        z = _dg(hb, w_ref[:, o:o + width], NN)
        o += width
        return z

    qa = _rotate(project(ATTN_WIDTH), cos_q, sin_q)
    qa_ref[...] = (qa * (HEAD_DIM ** -0.5 * LOG2E)).astype(BF16)
    kv = project(2 * KV_WIDTH)
    ka_ref[...] = _rotate(kv[:, :KV_WIDTH], cos_k, sin_k)
    va_ref[...] = kv[:, KV_WIDTH:]
    qh_ref[...] = project(HG_WIDTH)
    fh_ref[...] = project(HG_WIDTH)
    ih_ref[...] = project(HG_WIDTH).astype(BF16)
    gh_ref[...] = project(HG_WIDTH)


def _rope_tables(positions):
    half = ROT_DIM // 2
    inv = ROPE_THETA ** (-(np.arange(half, dtype=np.float64) * 2.0 / ROT_DIM))
    ang = np.asarray(positions, np.float64)[:, None] * inv[None, :]
    cos_h = np.ones((len(positions), HEAD_DIM))
    sin_h = np.zeros((len(positions), HEAD_DIM))
    cos_h[:, :half] = np.cos(ang)
    cos_h[:, half:ROT_DIM] = np.cos(ang)
    sin_h[:, :half] = -np.sin(ang)
    sin_h[:, half:ROT_DIM] = np.sin(ang)
    cos_t = np.tile(cos_h, (1, ATTN_KV_HEADS)).astype(np.float32)
    sin_t = np.tile(sin_h, (1, ATTN_KV_HEADS)).astype(np.float32)
    return jnp.asarray(cos_t), jnp.asarray(sin_t)


def _in_projection(x, mod3, mod2, g_mix, w_in_bf16, cos_t, sin_t, tm):
    t = x.shape[0]
    nt = t // tm
    if mod3 is not None:
        per_b = (t // mod3.shape[0]) // tm
        sh_spec = pl.BlockSpec((1, 1, D_MODEL), lambda i: (i // per_b, 0, 0))
        sc_spec = pl.BlockSpec((1, 1, D_MODEL), lambda i: (i // per_b, 0, 1))
        mod = mod3
        ncs = cos_t.shape[0] // tm
        cs_spec = pl.BlockSpec((tm, KV_WIDTH), lambda i: (i % ncs, 0))
    else:
        sh_spec = pl.BlockSpec((tm, D_MODEL), lambda i: (i, 0))
        sc_spec = pl.BlockSpec((tm, D_MODEL), lambda i: (i, 1))
        mod = mod2
        cs_spec = pl.BlockSpec((tm, KV_WIDTH), lambda i: (i, 0))
    row = lambda w: pl.BlockSpec((tm, w), lambda i: (i, 0))
    return pl.pallas_call(
        _inproj_kernel,
        grid=(nt,),
        in_specs=[row(D_MODEL), sh_spec, sc_spec,
                  pl.BlockSpec((1, D_MODEL), lambda i: (0, 0)),
                  pl.BlockSpec((D_MODEL, IN_COLS), lambda i: (0, 0)),
                  cs_spec, cs_spec],
        out_specs=[row(ATTN_WIDTH), row(KV_WIDTH), row(KV_WIDTH),
                   row(HG_WIDTH), row(HG_WIDTH), row(HG_WIDTH), row(HG_WIDTH)],
        out_shape=[jax.ShapeDtypeStruct((t, ATTN_WIDTH), BF16),
                   jax.ShapeDtypeStruct((t, KV_WIDTH), F32),
                   jax.ShapeDtypeStruct((t, KV_WIDTH), F32),
                   jax.ShapeDtypeStruct((t, HG_WIDTH), F32),
                   jax.ShapeDtypeStruct((t, HG_WIDTH), F32),
                   jax.ShapeDtypeStruct((t, HG_WIDTH), BF16),
                   jax.ShapeDtypeStruct((t, HG_WIDTH), F32)],
        compiler_params=pltpu.CompilerParams(vmem_limit_bytes=V7X_VMEM_LIMIT),
        name="in_proj",
    )(x, mod, mod, g_mix.reshape(1, -1), w_in_bf16, cos_t, sin_t)


def _attn_prompt_kernel(sink_ref, q_ref, kc_ref, kp_ref, vc_ref, vp_ref, g_ref, o_ref):
    n = pl.program_id(1)
    blk = WINDOW
    nblk = q_ref.shape[0] // blk
    pair_w = 2 * HEAD_DIM
    low = lax.broadcasted_iota(jnp.int32, (1, pair_w), 1) < HEAD_DIM
    kall = jnp.concatenate([kp_ref[...], kc_ref[...]], axis=0)
    vall = jnp.concatenate([vp_ref[...], vc_ref[...]], axis=0)
    kroll = pltpu.roll(kall, HEAD_DIM, axis=1)
    vroll = pltpu.roll(vall, HEAD_DIM, axis=1)
    kdup = [jnp.where(low, kall, kroll).astype(BF16), jnp.where(low, kroll, kall).astype(BF16)]
    v_lo = [jnp.where(low, vall, 0.0).astype(BF16), jnp.where(low, vroll, 0.0).astype(BF16)]
    v_hi = [jnp.where(low, 0.0, vroll).astype(BF16), jnp.where(low, 0.0, vall).astype(BF16)]
    qi = lax.broadcasted_iota(jnp.int32, (blk, 2 * blk), 0)
    kj = lax.broadcasted_iota(jnp.int32, (blk, 2 * blk), 1)
    band = (kj >= qi) & (kj <= qi + blk)
    zero = jnp.zeros((), BF16)
    for bi in range(nblk):
        rows = slice(bi * blk, (bi + 1) * blk)
        krows = slice(bi * blk, (bi + 2) * blk)
        ok = band & ((kj >= blk) | (n > 0)) if bi == 0 else band
        heads = range(ATTN_HEADS)
        kv = [head // ATTN_GROUP for head in heads]
        sinks = [sink_ref[head] * LOG2E for head in heads]
        s = []
        for head in heads:
            qp = q_ref[rows, (head // 2) * pair_w:(head // 2 + 1) * pair_w]
            qm = jnp.where(low if head % 2 == 0 else jnp.logical_not(low), qp, zero)
            s.append(jnp.where(ok, _dg(qm, kdup[kv[head]][krows], NT), -jnp.inf))
        m = [jnp.maximum(jnp.max(s[head], axis=-1, keepdims=True), sinks[head]) for head in heads]
        p = [jnp.exp2(s[head] - m[head]) for head in heads]
        den = [jnp.sum(p[head], axis=-1, keepdims=True) + jnp.exp2(sinks[head] - m[head]) for head in heads]
        pv = [_dg(p[head].astype(BF16), (v_lo if head % 2 == 0 else v_hi)[kv[head]][krows], NN)
              for head in heads]
        pairs = [(pv[2 * j] + pv[2 * j + 1]) / jnp.where(low, den[2 * j], den[2 * j + 1])
                 for j in range(ATTN_HEADS // 2)]
        o_ref[rows, :] = _rms(jnp.concatenate(pairs, axis=1), g_ref[...]).astype(BF16)


def _attention_prompt(qa, ka, va, sinks, g_attn_out, batch):
    t = qa.shape[0]
    nb = t // batch // (WINDOW * ATTN_STEP_BLOCKS)
    cur = lambda w: pl.BlockSpec((WINDOW * ATTN_STEP_BLOCKS, w), lambda b, n: (b * nb + n, 0))
    prev = lambda w: pl.BlockSpec(
        (WINDOW, w), lambda b, n: (ATTN_STEP_BLOCKS * (b * nb + n) - jnp.minimum(n, 1), 0))
    return pl.pallas_call(
        _attn_prompt_kernel,
        grid=(batch, nb),
        in_specs=[pl.BlockSpec(memory_space=pltpu.SMEM),
                  cur(ATTN_WIDTH), cur(KV_WIDTH), prev(KV_WIDTH), cur(KV_WIDTH), prev(KV_WIDTH),
                  pl.BlockSpec((1, ATTN_WIDTH), lambda b, n: (0, 0))],
        out_specs=cur(ATTN_WIDTH),
        out_shape=jax.ShapeDtypeStruct((t, ATTN_WIDTH), BF16),
        name="attn_prompt",
    )(sinks, qa, ka, ka, va, va, g_attn_out.reshape(1, -1))


def _lower_bound(lb_logits_ref):
    lg = lb_logits_ref[...]
    e = jnp.exp(lg - jnp.max(lg, axis=0, keepdims=True))
    return e[0:1] / jnp.sum(e, axis=0, keepdims=True)


def _hgrn_prompt_kernel(q_ref, f_ref, i_ref, g_ref, lbl_ref, gn_ref, o_ref, sfin_ref, st_ref):
    step = pl.program_id(1)
    C = HG_CHUNK
    W = HG_WIDTH
    nsub = C // HG_SUB
    n_chunks = q_ref.shape[0] // C
    heads = [slice(hh * HG_DK, (hh + 1) * HG_DK) for hh in range(HG_HEADS)]

    @pl.when(step == 0)
    def _():
        st_ref[...] = jnp.zeros_like(st_ref)

    lb = _lower_bound(lbl_ref)
    r64 = lax.broadcasted_iota(jnp.int32, (C, C), 0)
    c64 = lax.broadcasted_iota(jnp.int32, (C, C), 1)
    tri = (r64 >= c64).astype(BF16)
    rsub = r64 % HG_SUB
    sr = lax.broadcasted_iota(jnp.int32, ((HG_SUB - 1) * C, C), 0)
    su = lax.broadcasted_iota(jnp.int32, ((HG_SUB - 1) * C, C), 1)
    sd, st_row = sr // C + 1, sr % C
    shifts = ((su == st_row - sd) & (st_row % HG_SUB >= sd)).astype(BF16)
    o_intra, q_state, upd, decay_last = {}, {}, {}, {}
    for ci in range(n_chunks):
        rows = slice(ci * C, (ci + 1) * C)
        f = lb + (1.0 - lb) * _sigmoid(f_ref[rows, :])
        kk = 1.0 - f
        logf = jnp.log2(f)
        q = q_ref[rows, :]
        v = i_ref[rows, :]
        b = _dot_exact_lhs(tri, logf, NN)
        k_shift = _dg(shifts, kk.astype(BF16), NN)
        prods = [q * kk]
        w = logf
        for d in range(1, HG_SUB):
            if d > 1:
                w = w + pltpu.roll(logf, d - 1, axis=0)
            prods.append(q * k_shift[(d - 1) * C:d * C] * jnp.exp2(w))
        ends = [b[j * HG_SUB + HG_SUB - 1:(j + 1) * HG_SUB] for j in range(nsub)]
        kt = kk * jnp.exp2(jnp.concatenate([jnp.broadcast_to(e, (HG_SUB, W)) for e in ends], axis=0) - b)
        lhs, rhs = [], []
        for j in range(nsub - 1):
            lo = (j + 1) * HG_SUB
            lhs.append(jnp.concatenate(
                [jnp.zeros((lo, W), F32), q[lo:] * jnp.exp2(b[lo:] - ends[j])], axis=0).astype(BF16))
            pieces = [kt[j * HG_SUB:lo]]
            if j > 0:
                pieces.insert(0, jnp.zeros((j * HG_SUB, W), F32))
            pieces.append(jnp.zeros((C - lo, W), F32))
            rhs.append(jnp.concatenate(pieces, axis=0).astype(BF16))
        q_state[ci] = (q * jnp.exp2(b)).astype(BF16)
        kd = (kk * jnp.exp2(ends[-1] - b)).astype(BF16)
        decay_last[ci] = jnp.exp2(ends[-1])
        a_off = [_dg(jnp.concatenate([x[:, sl] for x in lhs], axis=1),
                     jnp.concatenate([x[:, sl] for x in rhs], axis=1), NT) for sl in heads]
        for hh, sl in enumerate(heads):
            upd[ci, hh] = _dg(v[:, sl], kd[:, sl], TN)
        sums = [[jnp.sum(prods[d][:, sl], axis=-1, keepdims=True) for d in range(HG_SUB)] for sl in heads]
        for hh, sl in enumerate(heads):
            a = a_off[hh]
            for d in range(HG_SUB):
                a = jnp.where((c64 == r64 - d) & (rsub >= d), sums[hh][d], a)
            o_intra[ci, hh] = _dg(a.astype(BF16), v[:, sl], NN)
    finals = []
    outs = {}
    for hh, sl in enumerate(heads):
        st = st_ref[hh]
        for ci in range(n_chunks):
            outs[ci, hh] = o_intra[ci, hh] + _dg(q_state[ci][:, sl], st.astype(BF16), NT)
            st = st * decay_last[ci][:, sl] + upd[ci, hh]
        st_ref[hh] = st
        finals.append(st)
    for ci in range(n_chunks):
        rows = slice(ci * C, (ci + 1) * C)
        for hh, sl in enumerate(heads):
            g = g_ref[rows, sl]
            o_ref[rows, sl] = (_rms(outs[ci, hh], gn_ref[:, sl]) * (g * _sigmoid(g))).astype(BF16)

    @pl.when(step == pl.num_programs(1) - 1)
    def _():
        for hh in range(HG_HEADS):
            sfin_ref[0, hh] = finals[hh].T


def _hgrn_prompt(qh, fh, ih, gh, lb_logits, g_hg_out, batch):
    t = qh.shape[0]
    nc = t // batch // (HG_CHUNK * HG_STEP_CHUNKS)
    blk = pl.BlockSpec((HG_CHUNK * HG_STEP_CHUNKS, HG_WIDTH), lambda b, c: (b * nc + c, 0))
    const = lambda r: pl.BlockSpec((r, HG_WIDTH), lambda b, c: (0, 0))
    return pl.pallas_call(
        _hgrn_prompt_kernel,
        grid=(batch, nc),
        in_specs=[blk, blk, blk, blk, const(lb_logits.shape[0]), const(1)],
        out_specs=[blk, pl.BlockSpec((1, HG_HEADS, HG_DK, HG_DV), lambda b, c: (b, 0, 0, 0))],
        out_shape=[jax.ShapeDtypeStruct((t, HG_WIDTH), BF16),
                   jax.ShapeDtypeStruct((batch, HG_HEADS, HG_DK, HG_DV), F32)],
        scratch_shapes=[pltpu.VMEM((HG_HEADS, HG_DV, HG_DK), F32)],
        name="hgrn_prompt",
    )(qh, fh, ih, gh, lb_logits, g_hg_out.reshape(1, -1))


def _attn_sample_kernel(sink_ref, q_ref, kn_ref, vn_ref, kc_ref, vc_ref, g_ref,
                        o_ref, ko_ref, vo_ref):
    bt = q_ref.shape[0]
    win = kc_ref.shape[1]
    kc = kc_ref[...]
    vc = vc_ref[...]
    kn = kn_ref[...]
    vn = vn_ref[...]
    outs = []
    for h in range(ATTN_KV_HEADS):
        ls = slice(h * HEAD_DIM, (h + 1) * HEAD_DIM)
        q = q_ref[:, h * ATTN_GROUP:(h + 1) * ATTN_GROUP, :]
        s = jnp.einsum('bgd,bjd->bgj', q, kc[:, :, ls].astype(BF16), preferred_element_type=F32)
        s_new = jnp.sum(q.astype(F32) * kn[:, None, ls], axis=-1, keepdims=True)
        gi = lax.broadcasted_iota(jnp.int32, (1, ATTN_GROUP, 1), 1)
        sink = jnp.zeros((1, ATTN_GROUP, 1), F32)
        for g in range(ATTN_GROUP):
            sink = jnp.where(gi == g, sink_ref[h * ATTN_GROUP + g] * LOG2E, sink)
        m = jnp.maximum(jnp.maximum(jnp.max(s, axis=-1, keepdims=True), s_new), sink)
        p = jnp.exp2(s - m)
        p_new = jnp.exp2(s_new - m)
        den = jnp.sum(p, axis=-1, keepdims=True) + p_new + jnp.exp2(sink - m)
        o = jnp.einsum('bgj,bjd->bgd', p.astype(BF16), vc[:, :, ls].astype(BF16),
                       preferred_element_type=F32)
        o = (o + p_new * vn[:, None, ls]) / den
        outs.append(o)
    ssq = sum(jnp.sum(jnp.sum(o * o, axis=-1, keepdims=True), axis=1, keepdims=True) for o in outs)
    scale = lax.rsqrt(ssq / ATTN_WIDTH + EPS)
    for h in range(ATTN_KV_HEADS):
        o_ref[h] = (outs[h] * scale * g_ref[h][None]).astype(BF16)
    ri = lax.broadcasted_iota(jnp.int32, (win, KV_WIDTH), 0)
    for b in range(bt):
        ko_ref[b] = jnp.where(ri == win - 1, kn[b:b + 1], pltpu.roll(kc[b], win - 1, axis=0))
        vo_ref[b] = jnp.where(ri == win - 1, vn[b:b + 1], pltpu.roll(vc[b], win - 1, axis=0))


def _attention_sample(qa, ka, va, cache_k, cache_v, sinks, g_attn_out, bt=16):
    nb = qa.shape[0]
    win = cache_k.shape[1]
    q3 = qa.reshape(nb, ATTN_HEADS, HEAD_DIM)
    g3 = g_attn_out.reshape(ATTN_KV_HEADS, ATTN_GROUP, HEAD_DIM)
    row = lambda w: pl.BlockSpec((bt, w), lambda i: (i, 0))
    cache = pl.BlockSpec((bt, win, KV_WIDTH), lambda i: (i, 0, 0))
    o4, k_new, v_new = pl.pallas_call(
        _attn_sample_kernel,
        grid=(nb // bt,),
        in_specs=[pl.BlockSpec(memory_space=pltpu.SMEM),
                  pl.BlockSpec((bt, ATTN_HEADS, HEAD_DIM), lambda i: (i, 0, 0)),
                  row(KV_WIDTH), row(KV_WIDTH), cache, cache,
                  pl.BlockSpec((ATTN_KV_HEADS, ATTN_GROUP, HEAD_DIM), lambda i: (0, 0, 0))],
        out_specs=[pl.BlockSpec((ATTN_KV_HEADS, bt, ATTN_GROUP, HEAD_DIM), lambda i: (0, i, 0, 0)),
                   cache, cache],
        out_shape=[jax.ShapeDtypeStruct((ATTN_KV_HEADS, nb, ATTN_GROUP, HEAD_DIM), BF16),
                   jax.ShapeDtypeStruct(cache_k.shape, F32),
                   jax.ShapeDtypeStruct(cache_v.shape, F32)],
        name="attn_sample",
    )(sinks, q3, ka, va, cache_k, cache_v, g3)
    oa = jnp.transpose(o4, (1, 0, 2, 3)).reshape(nb, ATTN_WIDTH)
    return oa, k_new, v_new


def _hgrn_sample_kernel(q_ref, f_ref, i_ref, g_ref, lbl_ref, gn_ref, s0_ref, o_ref, s_ref):
    bt = q_ref.shape[0]
    lb = _lower_bound(lbl_ref)
    f = lb + (1.0 - lb) * _sigmoid(f_ref[...])
    kk = 1.0 - f
    q = q_ref[...]
    v = i_ref[...].astype(F32)
    g = g_ref[...]
    gate = g * _sigmoid(g)
    r = lax.broadcasted_iota(jnp.int32, (HG_DK, HG_DK), 0)
    cc = lax.broadcasted_iota(jnp.int32, (HG_DK, HG_DK), 1)
    diag = r == cc

    def column(x_row):
        return jnp.sum(jnp.where(diag, x_row, 0.0), axis=1, keepdims=True)

    units = [(b, hh, slice(hh * HG_DK, (hh + 1) * HG_DK)) for b in range(bt) for hh in range(HG_HEADS)]
    f_col = [column(f[b:b + 1, sl]) for b, hh, sl in units]
    k_col = [column(kk[b:b + 1, sl]) for b, hh, sl in units]
    q_col = [column(q[b:b + 1, sl]) for b, hh, sl in units]
    s_new = [f_col[u] * s0_ref[b, hh] + k_col[u] * v[b:b + 1, sl] for u, (b, hh, sl) in enumerate(units)]
    for u, (b, hh, sl) in enumerate(units):
        s_ref[b, hh] = s_new[u]
    outs = [_rms(jnp.sum(s_new[u] * q_col[u], axis=0, keepdims=True), gn_ref[:, sl])
            for u, (b, hh, sl) in enumerate(units)]
    for b in range(bt):
        row = jnp.concatenate(outs[b * HG_HEADS:(b + 1) * HG_HEADS], axis=1)
        o_ref[b:b + 1, :] = (row * gate[b:b + 1]).astype(BF16)


def _hgrn_sample(qh, fh, ih, gh, lb_logits, g_hg_out, state, bt=16):
    nb = qh.shape[0]
    row = pl.BlockSpec((bt, HG_WIDTH), lambda i: (i, 0))
    const = lambda r: pl.BlockSpec((r, HG_WIDTH), lambda i: (0, 0))
    st = pl.BlockSpec((bt, HG_HEADS, HG_DK, HG_DV), lambda i: (i, 0, 0, 0))
    return pl.pallas_call(
        _hgrn_sample_kernel,
        grid=(nb // bt,),
        in_specs=[row, row, row, row, const(lb_logits.shape[0]), const(1), st],
        out_specs=[row, st],
        out_shape=[jax.ShapeDtypeStruct((nb, HG_WIDTH), BF16),
                   jax.ShapeDtypeStruct(state.shape, F32)],
        name="hgrn_sample",
    )(qh, fh, ih, gh, lb_logits, g_hg_out.reshape(1, -1), state)


def _outproj_kernel(x_ref, oa_ref, oh_ref, gt_ref, sh_ref, sc_ref, g_ref, wo_ref, wr_ref, br_ref, *rest,
                    n_tiles):
    x1_ref, xs_ref, lp_ref, w_ref, pc_ref, hb_scr, lp_scr = rest[-7:]
    i = pl.program_id(0)
    tm = x_ref.shape[0]
    rows = xs_ref.shape[0]

    @pl.when(i == 0)
    def _():
        hb_scr[...] = jnp.zeros_like(hb_scr)
        lp_scr[...] = jnp.zeros_like(lp_scr)

    hb_prev = jnp.where(i <= n_tiles, hb_scr[...], jnp.zeros((), BF16))
    lp_prev = [lp_scr[k:k + 1, :] for k in range(TOP_K)]
    chunk = 256
    pending = list(range(0, rows, chunk))

    def place_next():
        if pending:
            r0 = pending.pop(0)
            si = lax.broadcasted_iota(jnp.int32, (chunk, tm), 0) + r0
            hit = (si == lp_prev[0]) | (si == lp_prev[1]) | (si == lp_prev[2]) | (si == lp_prev[3])
            xs_ref[r0:r0 + chunk, :] = _dg(jnp.where(hit, 1.0, 0.0).astype(BF16), hb_prev, NN).astype(BF16)

    place_next()
    gt = gt_ref[...].reshape(-1, D_MODEL)
    sh = sh_ref[...].reshape(-1, D_MODEL)
    sc = sc_ref[...].reshape(-1, D_MODEL)
    mix = _dg(oa_ref[...], wo_ref[0:ATTN_WIDTH, :], NN) + _dg(oh_ref[...], wo_ref[ATTN_WIDTH:, :], NN)
    x1 = x_ref[...] + gt * mix
    x1_ref[...] = x1
    place_next()
    h2 = _rms(x1, g_ref[...]) * (1.0 + sc) + sh
    place_next()
    logits = _dot_hilo(wr_ref[...], h2, NT) + br_ref[...]
    place_next()
    ei = lax.broadcasted_iota(jnp.int32, logits.shape, 0)
    vals, sel = [], []
    l = logits
    for _ in range(TOP_K):
        m = jnp.max(l, axis=0, keepdims=True)
        idx = jnp.min(jnp.where(l == m, ei, N_EXPERTS), axis=0, keepdims=True)
        pick = ei == idx
        vals.append(m)
        sel.append(pick)
        l = jnp.where(pick, -jnp.inf, l)
        place_next()
    ex = [jnp.exp(v - vals[0]) for v in vals]
    den = ex[0] + ex[1] + ex[2] + ex[3]
    for k in range(TOP_K):
        w_ref[k:k + 1, :] = ex[k] / den
    place_next()

    member = jnp.where(sel[0] | sel[1] | sel[2] | sel[3], 1.0, 0.0)
    tr = lax.broadcasted_iota(jnp.int32, (tm, tm), 0)
    tc = lax.broadcasted_iota(jnp.int32, (tm, tm), 1)
    rank = _dg(member.astype(BF16), (tr < tc).astype(BF16), NN)
    pieces = jnp.floor((jnp.sum(member, axis=1, keepdims=True) + (PIECE - 1)) * (1.0 / PIECE))
    er = lax.broadcasted_iota(jnp.int32, (N_EXPERTS, N_EXPERTS), 0)
    ec = lax.broadcasted_iota(jnp.int32, (N_EXPERTS, N_EXPERTS), 1)
    pieces_b = jnp.broadcast_to(pieces, (N_EXPERTS, 128))
    pc_ref[0] = pieces_b.astype(jnp.int32)
    base = _dg((ec < er).astype(BF16), pieces_b.astype(BF16), NN)[:, 0:1] * PIECE
    slot = base + rank
    for k in range(TOP_K):
        lp = jnp.sum(jnp.where(sel[k], slot, 0.0), axis=0, keepdims=True).astype(jnp.int32)
        lp_ref[k:k + 1, :] = lp
        lp_scr[k:k + 1, :] = lp
    while pending:
        place_next()
    hb_scr[...] = h2.astype(BF16)


def _out_projection(x, oa, oh, mod3, mod2, g_ffn, w_out_bf16, w_router_t, b_router, tm, slots, slot_block):
    t = x.shape[0]
    nt = t // tm
    rows = SLOT_ROWS[tm]
    real = lambda i: jnp.minimum(i, nt - 1)
    if mod3 is not None:
        per_b = (t // mod3.shape[0]) // tm
        mspec = lambda j: pl.BlockSpec((1, 1, D_MODEL), lambda i: (real(i) // per_b, 0, j))
        mod = mod3
    else:
        mspec = lambda j: pl.BlockSpec((tm, D_MODEL), lambda i: (real(i), j))
        mod = mod2
    row = lambda w: pl.BlockSpec((tm, w), lambda i: (real(i), 0))
    full = lambda a: pl.BlockSpec(a.shape, lambda i: (0,) * a.ndim)
    g2 = g_ffn.reshape(1, -1)
    br = b_router.reshape(-1, 1)
    args = [x, oa, oh, mod, mod, mod, g2, w_out_bf16, w_router_t, br]
    in_specs = [row(D_MODEL), row(ATTN_WIDTH), row(HG_WIDTH), mspec(2), mspec(3), mspec(4),
                full(g2), full(w_out_bf16), full(w_router_t), full(br)]
    aliases = {}
    n_fill = 0
    if slots is not None:
        args.append(slots)
        in_specs.append(pl.BlockSpec(memory_space=pl.ANY))
        aliases = {len(args) - 1: 1}
    else:
        n_fill = pl.cdiv(N_PIECES * PIECE - nt * rows, rows)
    return pl.pallas_call(
        functools.partial(_outproj_kernel, n_tiles=nt),
        grid=(nt + 1 + n_fill,),
        in_specs=in_specs,
        out_specs=[row(D_MODEL),
                   pl.BlockSpec((rows, D_MODEL), lambda i: (slot_block + jnp.maximum(i - 1, 0), 0)),
                   pl.BlockSpec((TOP_K, tm), lambda i: (0, real(i))),
                   pl.BlockSpec((TOP_K, tm), lambda i: (0, real(i))),
                   pl.BlockSpec((1, N_EXPERTS, 128), lambda i: (real(i), 0, 0))],
        out_shape=[jax.ShapeDtypeStruct((t, D_MODEL), F32),
                   jax.ShapeDtypeStruct((N_PIECES * PIECE, D_MODEL), BF16),
                   jax.ShapeDtypeStruct((TOP_K, t), jnp.int32),
                   jax.ShapeDtypeStruct((TOP_K, t), F32),
                   jax.ShapeDtypeStruct((nt, N_EXPERTS, 128), jnp.int32)],
        scratch_shapes=[pltpu.VMEM((tm, D_MODEL), BF16), pltpu.VMEM((8, tm), jnp.int32)],
        input_output_aliases=aliases,
        compiler_params=pltpu.CompilerParams(vmem_limit_bytes=V7X_VMEM_LIMIT),
        name="out_proj_router",
    )(*args)


def _segment_offsets_kernel(dest_ref, dd_ref, q_ref, o_ref):
    reached = dest_ref[...] <= q_ref[...]
    o_ref[...] = jnp.sum(jnp.where(reached, dd_ref[...], 0), axis=0, keepdims=True)


def _segment_offsets(dest, d_delta, queries):
    lanes = 512
    n_seg = -(-dest.shape[0] // 8) * 8
    n_q = -(-queries.shape[0] // lanes) * lanes
    never = jnp.iinfo(jnp.int32).max
    dest_c = jnp.pad(dest, (0, n_seg - dest.shape[0]), constant_values=never).reshape(n_seg, 1)
    dd_c = jnp.pad(d_delta, (0, n_seg - d_delta.shape[0])).reshape(n_seg, 1)
    q_r = jnp.pad(queries, (0, n_q - queries.shape[0])).reshape(1, n_q)
    seg = pl.BlockSpec((n_seg, 1), lambda i: (0, 0))
    out = pl.pallas_call(
        _segment_offsets_kernel,
        grid=(n_q // lanes,),
        in_specs=[seg, seg, pl.BlockSpec((1, lanes), lambda i: (0, i))],
        out_specs=pl.BlockSpec((1, lanes), lambda i: (0, i)),
        out_shape=jax.ShapeDtypeStruct((1, n_q), jnp.int32),
        name="segment_offsets",
    )(dest_c, dd_c, q_r)
    return out[0, :queries.shape[0]]


def _prefix_sums(x):
    i = jnp.arange(x.shape[-1])
    return jnp.sum(jnp.where(i[None, :] <= i[:, None], x[..., None, :], 0), axis=-1)


def _piece_tables(pieces_ie):
    cap = jnp.asarray(TILE_PIECE_CAP, jnp.int32)
    gbase = jnp.asarray(TILE_PIECE_BASE, jnp.int32)
    experts = jnp.arange(N_EXPERTS, dtype=jnp.int32)
    seg_src = gbase[:, None] + _prefix_sums(pieces_ie) - pieces_ie
    used_i = jnp.sum(pieces_ie, axis=1)
    tail_i = cap - used_i
    np_e = jnp.sum(pieces_ie, axis=0)
    rem = np_e % STEP_PIECES
    head_e = jnp.where(rem > 0, rem, jnp.minimum(np_e, STEP_PIECES))
    nt_e = np_e // STEP_PIECES + (rem > 0)
    tile_end = _prefix_sums(nt_e)
    tile_start = tile_end - nt_e
    n_comp = jnp.sum(nt_e)
    q_start_e = _prefix_sums(np_e) - np_e
    n_used = jnp.sum(np_e)
    tt = jnp.arange(MOE_MAX_STEPS, dtype=jnp.int32)
    e_t = jnp.minimum(jnp.sum(tt[:, None] >= tile_end[None, :], axis=1), N_EXPERTS - 1).astype(jnp.int32)
    of_step = lambda v: jnp.sum(jnp.where(e_t[:, None] == experts[None, :], v[None, :], 0), axis=1)
    is_comp = tt < n_comp
    k = tt - of_step(tile_start)
    head_t = of_step(head_e)
    q0_comp = of_step(q_start_e) + jnp.where(k == 0, 0, head_t + STEP_PIECES * (k - 1))
    live_comp = jnp.where(k == 0, head_t, STEP_PIECES)
    q0_fill = n_used + STEP_PIECES * (tt - n_comp)
    q0 = jnp.where(is_comp, q0_comp, q0_fill)
    live = jnp.where(is_comp, live_comp, jnp.clip(N_PIECES - q0_fill, 0, STEP_PIECES))
    n_busy = n_comp + (N_PIECES - n_used + STEP_PIECES - 1) // STEP_PIECES
    pieces_ei = pieces_ie.T
    dest_seg = q_start_e[:, None] + _prefix_sums(pieces_ei) - pieces_ei
    dest = jnp.concatenate([dest_seg.reshape(-1), n_used + _prefix_sums(tail_i) - tail_i])
    src = jnp.concatenate([seg_src.T.reshape(-1), gbase + used_i])
    delta = src - dest
    d_delta = delta - jnp.concatenate([jnp.zeros((1,), jnp.int32), delta[:-1]])
    lane = jnp.arange(STEP_PIECES, dtype=jnp.int32)
    qq = (q0[:, None] + lane[None, :]).reshape(-1)
    ok = lane[None, :] < live[:, None]
    piece = (qq + _segment_offsets(dest, d_delta, qq)).reshape(MOE_MAX_STEPS, STEP_PIECES)
    dump = N_PIECES + (tt % 2)[:, None] * STEP_PIECES + lane[None, :]
    tbl_out = jnp.where(ok, piece, dump)
    last = jnp.maximum(n_comp - 1, 0)
    tbl_in = jnp.where(ok & is_comp[:, None], piece, 0)
    in_last = jnp.sum(jnp.where((tt == last)[:, None], tbl_in, 0), axis=0)
    tbl_in = jnp.where(is_comp[:, None], tbl_in, in_last[None, :])
    texp = jnp.where(is_comp, e_t, jnp.sum(jnp.where(tt == last, e_t, 0)))
    first = is_comp & (k == 0)
    kind = jnp.where(is_comp, jnp.where(live <= HALF_PIECES, 1, 2), jnp.where(tt < n_busy, 0, 3))
    has_e = nt_e > 0
    order_e = _prefix_sums(has_e.astype(jnp.int32)) - 1
    later = has_e[None, :] & (experts[None, :] > experts[:, None])
    next_e = jnp.min(jnp.where(later, experts[None, :], N_EXPERTS), axis=1)
    next_e = jnp.where(next_e < N_EXPERTS, next_e, -1)
    i32 = lambda a: a.astype(jnp.int32)
    return (i32(tbl_in.reshape(-1)), i32(tbl_out.reshape(-1)), i32(texp), i32(kind), i32(first),
            i32(of_step(order_e) % 2), i32(of_step(next_e)))


def _moe_grouped_kernel(tin_ref, tout_ref, texp_ref, kind_ref, first_ref, wpar_ref, nexp_ref,
                        x_hbm, wu_hbm, wd_hbm, bg_ref, bl_ref, bd_ref, y_hbm,
                        wg_s, wl_s, wd_s, xbuf, ybuf, xsems, ysems, wu_buf, wd_buf, wsems):
    npc = STEP_PIECES
    t = pl.program_id(0)
    nsteps = pl.num_programs(0)
    kind = kind_ref[t]
    slot = t % 2

    def x_copy(step, buf, p):
        src = tin_ref[step * npc + p]
        return pltpu.make_async_copy(
            x_hbm.at[pl.ds(pl.multiple_of(src * PIECE, PIECE), PIECE), :],
            xbuf.at[buf, pl.ds(p * PIECE, PIECE), :],
            xsems.at[buf, p])

    def x_each(step, buf, fn):
        k = kind_ref[step]

        @pl.when(k == 2)
        def _():
            for p in range(STEP_PIECES):
                fn(x_copy(step, buf, p))

        @pl.when(k == 1)
        def _():
            for p in range(HALF_PIECES):
                fn(x_copy(step, buf, p))

    def y_copy(step, buf, p):
        dst = tout_ref[step * npc + p]
        return pltpu.make_async_copy(
            ybuf.at[buf, pl.ds(p * PIECE, PIECE), :],
            y_hbm.at[pl.ds(pl.multiple_of(dst * PIECE, PIECE), PIECE), :],
            ysems.at[buf, p])

    def y_wait(step, buf):
        for p in range(npc):
            y_copy(step, buf, p).wait()

    def w_copies(expert, buf):
        return (pltpu.make_async_copy(wu_hbm.at[expert], wu_buf.at[buf], wsems.at[buf, 0]),
                pltpu.make_async_copy(wd_hbm.at[expert], wd_buf.at[buf], wsems.at[buf, 1]))

    @pl.when(t == 0)
    def _():
        ybuf[...] = jnp.zeros_like(ybuf)
        x_each(0, 0, lambda cp: cp.start())
        for cp in w_copies(texp_ref[0], wpar_ref[0]):
            cp.start(priority=1)

    @pl.when(t >= 2)
    def _():
        y_wait(t - 2, slot)

    @pl.when(t + 1 < nsteps)
    def _():
        x_each(t + 1, 1 - slot, lambda cp: cp.start())

    @pl.when((kind > 0) & (kind < 3) & (first_ref[t] == 1))
    def _():
        wb = wpar_ref[t]
        for cp in w_copies(texp_ref[t], wb):
            cp.wait()

        @pl.when(nexp_ref[t] >= 0)
        def _():
            for cp in w_copies(nexp_ref[t], 1 - wb):
                cp.start(priority=1)

        cb = 256
        r = lax.broadcasted_iota(jnp.int32, (cb, cb), 0)
        c = lax.broadcasted_iota(jnp.int32, (cb, cb), 1)
        perm = jnp.where(r == jnp.where(c < cb // 2, 2 * c, 2 * (c - cb // 2) + 1), 1.0, 0.0).astype(BF16)
        for blk in range(2 * D_FF // cb):
            wp = _dg(wu_buf[wb, :, blk * cb:(blk + 1) * cb].astype(BF16), perm, NN).astype(BF16)
            wg_s[:, blk * (cb // 2):(blk + 1) * (cb // 2)] = wp[:, :cb // 2]
            wl_s[:, blk * (cb // 2):(blk + 1) * (cb // 2)] = wp[:, cb // 2:]
        wd_s[...] = wd_buf[wb].astype(BF16)

    x_each(t, slot, lambda cp: cp.wait())

    def expert(n_pieces):
        xb = xbuf[slot, 0:n_pieces * PIECE, :]
        groups = [slice(j * FF_GROUP, (j + 1) * FF_GROUP) for j in range(D_FF // FF_GROUP)]
        glu = [jnp.minimum(_dg(xb, wg_s[:, c], NN) + bg_ref[0, :, c], SWIGLU_LIMIT) for c in groups]
        lin = [jnp.clip(_dg(xb, wl_s[:, c], NN) + bl_ref[0, :, c], -SWIGLU_LIMIT, SWIGLU_LIMIT)
               for c in groups]
        act = [(g * _sigmoid(SWIGLU_ALPHA * g) * (l + 1.0)).astype(BF16) for g, l in zip(glu, lin)]
        y = bd_ref[0] + _dg(act[0], wd_s[groups[0], :], NN)
        for c, a in zip(groups[1:], act[1:]):
            y = y + _dg(a, wd_s[c, :], NN)
        ybuf[slot, 0:n_pieces * PIECE, :] = y.astype(BF16)

    @pl.when(kind == 2)
    def _():
        expert(STEP_PIECES)

    @pl.when(kind == 1)
    def _():
        expert(HALF_PIECES)

    @pl.when(kind == 0)
    def _():
        ybuf[slot] = jnp.zeros((npc * PIECE, D_MODEL), BF16)

    for p in range(npc):
        y_copy(t, slot, p).start(priority=p % 2)

    @pl.when(t == nsteps - 1)
    def _():
        y_wait(t - 1, 1 - slot)
        y_wait(t, slot)


def _moe_grouped(slots, tables, w_up, w_down, bg, bl, bd):
    npc = STEP_PIECES
    by_expert = lambda shape: pl.BlockSpec(
        shape, lambda t, tin, tout, texp, kind, first, wpar, nexp: (texp[t],) + (0,) * (len(shape) - 1))
    return pl.pallas_call(
        _moe_grouped_kernel,
        grid_spec=pltpu.PrefetchScalarGridSpec(
            num_scalar_prefetch=7,
            grid=(MOE_MAX_STEPS,),
            in_specs=[pl.BlockSpec(memory_space=pl.ANY), pl.BlockSpec(memory_space=pl.ANY),
                      pl.BlockSpec(memory_space=pl.ANY),
                      by_expert((1, 1, D_FF)), by_expert((1, 1, D_FF)), by_expert((1, 1, D_MODEL))],
            out_specs=pl.BlockSpec(memory_space=pl.ANY),
            scratch_shapes=[pltpu.VMEM((D_MODEL, D_FF), BF16), pltpu.VMEM((D_MODEL, D_FF), BF16),
                            pltpu.VMEM((D_FF, D_MODEL), BF16),
                            pltpu.VMEM((2, npc * PIECE, D_MODEL), BF16),
                            pltpu.VMEM((2, npc * PIECE, D_MODEL), BF16),
                            pltpu.SemaphoreType.DMA((2, npc)),
                            pltpu.SemaphoreType.DMA((2, npc)),
                            pltpu.VMEM((2, D_MODEL, 2 * D_FF), F32),
                            pltpu.VMEM((2, D_FF, D_MODEL), F32),
                            pltpu.SemaphoreType.DMA((2, 2))]),
        out_shape=jax.ShapeDtypeStruct(((N_PIECES + 2 * npc) * PIECE, D_MODEL), BF16),
        compiler_params=pltpu.CompilerParams(vmem_limit_bytes=V7X_VMEM_LIMIT),
        name="moe_grouped",
    )(*tables, slots, w_up, w_down, bg, bl, bd)


def _combine_kernel(y_ref, lp_ref, w_ref, x1_ref, gt_ref, gf_ref, o_ref):
    tm = x1_ref.shape[0]
    rows = y_ref.shape[0]
    r = lax.broadcasted_iota(jnp.int32, (tm, tm), 0)
    c = lax.broadcasted_iota(jnp.int32, (tm, tm), 1)
    diag = r == c

    def column(x_row):
        return jnp.sum(jnp.where(diag, x_row, 0.0), axis=1, keepdims=True)

    pos = [column(lp_ref[k:k + 1, :].astype(F32)).astype(jnp.int32) for k in range(TOP_K)]
    wgt = [column(w_ref[k:k + 1, :]) for k in range(TOP_K)]
    chunk = 512
    moe = jnp.zeros((tm, D_MODEL), F32)
    for r0 in range(0, rows, chunk):
        si = lax.broadcasted_iota(jnp.int32, (tm, chunk), 1) + r0
        wt = jnp.zeros((tm, chunk), F32)
        for k in range(TOP_K):
            wt = jnp.where(si == pos[k], wgt[k], wt)
        moe = moe + _dg(wt.astype(BF16), y_ref[r0:r0 + chunk, :], NN)
    gt = gt_ref[...].reshape(-1, D_MODEL)
    o_ref[...] = _rms(x1_ref[...] + gt * moe, gf_ref[...])


def _combine(yslots, lp, w, x1, mod3, mod2, g_final, tm, slot_block):
    t = x1.shape[0]
    nt = t // tm
    rows = SLOT_ROWS[tm]
    if mod3 is not None:
        per_b = (t // mod3.shape[0]) // tm
        gspec = pl.BlockSpec((1, 1, D_MODEL), lambda i: (i // per_b, 0, 5))
        mod = mod3
    else:
        gspec = pl.BlockSpec((tm, D_MODEL), lambda i: (i, 5))
        mod = mod2
    return pl.pallas_call(
        _combine_kernel,
        grid=(nt,),
        in_specs=[pl.BlockSpec((rows, D_MODEL), lambda i: (slot_block + i, 0)),
                  pl.BlockSpec((TOP_K, tm), lambda i: (0, i)),
                  pl.BlockSpec((TOP_K, tm), lambda i: (0, i)),
                  pl.BlockSpec((tm, D_MODEL), lambda i: (i, 0)),
                  gspec,
                  pl.BlockSpec((1, D_MODEL), lambda i: (0, 0))],
        out_specs=pl.BlockSpec((tm, D_MODEL), lambda i: (i, 0)),
        out_shape=jax.ShapeDtypeStruct((t, D_MODEL), F32),
        compiler_params=pltpu.CompilerParams(vmem_limit_bytes=V7X_VMEM_LIMIT),
        name="moe_combine",
    )(yslots, lp, w, x1, mod, g_final.reshape(1, -1))


def kernel(x_prompt, x_sample, c_prompt, c_sample, cache_k_win, cache_v_win, state_hgrn, w_ada, b_ada,
           g_mix, g_ffn, w_in, attn_sinks, g_attn_out, hg_lb_logits, g_hg_out, w_out, w_router, b_router,
           w_up, b_up, w_down, b_down, g_final):
    batch, seq, d = x_prompt.shape
    nsamp = x_sample.shape[0]
    win = cache_k_win.shape[2]
    layer = 0

    mod = _modulation(jnp.concatenate([c_prompt, c_sample], axis=0), w_ada[layer], b_ada[layer])
    mod_p = mod[:batch].reshape(batch, 1, 6 * d)
    mod_s = mod[batch:]

    assert (batch * seq, nsamp) == (N_PROMPT_TILES * TOK_TILE, SAMPLE_TILE)
    w_in_b = w_in[layer].astype(BF16)
    w_out_b = w_out[layer].astype(BF16)
    w_router_t = w_router[layer].T
    bg = b_up[layer][:, None, 0::2]
    bl = b_up[layer][:, None, 1::2]
    bd = b_down[layer][:, None, :]

    xp = x_prompt.reshape(batch * seq, d)
    cos_p, sin_p = _rope_tables(np.arange(seq))
    qa, ka, va, qh, fh, ih, gh = _in_projection(xp, mod_p, None, g_mix[layer], w_in_b, cos_p, sin_p,
                                                IN_PROJ_TILE)
    oa = _attention_prompt(qa, ka, va, attn_sinks[layer], g_attn_out[layer], batch)
    oh, s_prompt = _hgrn_prompt(qh, fh, ih, gh, hg_lb_logits, g_hg_out[layer], batch)
    x1_p, slots, lp_p, cw_p, pc_p = _out_projection(
        xp, oa, oh, mod_p, None, g_ffn[layer], w_out_b, w_router_t, b_router[layer], TOK_TILE, None, 0)
    k_win_p = ka.reshape(batch, seq, ATTN_KV_HEADS, HEAD_DIM)[:, seq - win:]
    v_win_p = va.reshape(batch, seq, ATTN_KV_HEADS, HEAD_DIM)[:, seq - win:]

    xs = x_sample.reshape(nsamp, d)
    cos_s, sin_s = _rope_tables(np.full((nsamp,), PAST_LEN))
    qa, ka, va, qh, fh, ih, gh = _in_projection(xs, None, mod_s, g_mix[layer], w_in_b, cos_s, sin_s, nsamp)
    oa, k_win_s, v_win_s = _attention_sample(
        qa, ka, va, cache_k_win[layer].reshape(nsamp, win, KV_WIDTH),
        cache_v_win[layer].reshape(nsamp, win, KV_WIDTH), attn_sinks[layer], g_attn_out[layer])
    oh, s_sample = _hgrn_sample(qh, fh, ih, gh, hg_lb_logits, g_hg_out[layer], state_hgrn[layer])
    sample_block = TILE_PIECE_BASE[-1] * PIECE // SLOT_ROWS[SAMPLE_TILE]
    x1_s, slots, lp_s, cw_s, pc_s = _out_projection(
        xs, oa, oh, None, mod_s, g_ffn[layer], w_out_b, w_router_t, b_router[layer], SAMPLE_TILE,
        slots, sample_block)

    tables = _piece_tables(jnp.concatenate([pc_p[:, :, 0], pc_s[:, :, 0]], axis=0))
    yslots = _moe_grouped(slots, tables, w_up[layer], w_down[layer], bg, bl, bd)
    y_prompt = _combine(yslots, lp_p, cw_p, x1_p, mod_p, None, g_final, TOK_TILE, 0)
    y_sample = _combine(yslots, lp_s, cw_s, x1_s, None, mod_s, g_final, SAMPLE_TILE, sample_block)

    kv_shape = (1, nsamp, win, ATTN_KV_HEADS, HEAD_DIM)
    return (y_prompt.reshape(batch, seq, d), y_sample.reshape(nsamp, 1, d),
            k_win_p[None], v_win_p[None], s_prompt[None],
            k_win_s.reshape(kv_shape), v_win_s.reshape(kv_shape), s_sample[None])
```
